```python
import jax, jax.numpy as jnp
from jax import lax
import numpy as np

D_MODEL = 1024
BATCH = 2
SEQ = 8192
DEPTH = 4
DEC_BATCH = 128
DEC_SEQ = 1
PAST_LEN = 8192
PAGE_SIZE = 128

N_META = 16
WINDOW = 128
HEAD_DIM = 64
ATT_WIDTH = D_MODEL // 2
ATT_HEADS = ATT_WIDTH // HEAD_DIM
KV_HEADS = ATT_HEADS // 4
GQA_GROUP = ATT_HEADS // KV_HEADS
HG_WIDTH = D_MODEL - ATT_WIDTH
HG_EXPAND = 128
HG_HEADS = HG_WIDTH // HG_EXPAND
HG_KDIM = HG_EXPAND
HG_VDIM = HG_WIDTH // HG_HEADS
HG_CHUNK = 64
MIX_WIDTH = ATT_WIDTH + HG_WIDTH
PROJ_WIDTH = 2 * ATT_WIDTH + 2 * KV_HEADS * HEAD_DIM + 2 * HG_HEADS * HG_KDIM + 2 * HG_WIDTH
EPS = 1e-6
NEG_BIG = -1e30
TINY = 1e-30

kernel_name = "hymba_swa_sink_hgrn2_decode_step"


def rms_norm(x, w):
    x32 = x.astype(jnp.float32)
    y = x32 * lax.rsqrt(jnp.mean(x32 * x32, axis=-1, keepdims=True) + EPS) * w.astype(jnp.float32)
    return y.astype(x.dtype)


def project(x, norm_w, w_in, q_norm_w, k_norm_w, lb):
    lead = x.shape[:-1]
    p = rms_norm(x, norm_w) @ w_in
    sizes = [ATT_WIDTH, KV_HEADS * HEAD_DIM, KV_HEADS * HEAD_DIM, ATT_WIDTH,
             HG_HEADS * HG_KDIM, HG_HEADS * HG_KDIM, HG_WIDTH, HG_WIDTH]
    qa, ka, va, ga, qh, fh, ih, gh = jnp.split(p, np.cumsum(sizes)[:-1].tolist(), axis=-1)
    qa = rms_norm(qa.reshape(*lead, KV_HEADS, GQA_GROUP, HEAD_DIM), q_norm_w)
    ka = rms_norm(ka.reshape(*lead, KV_HEADS, HEAD_DIM), k_norm_w)
    va = va.reshape(*lead, KV_HEADS, HEAD_DIM)
    hq = jax.nn.silu(qh.astype(jnp.float32)).reshape(*lead, HG_HEADS, HG_KDIM)
    z = fh.astype(jnp.float32).reshape(*lead, HG_HEADS, HG_KDIM)
    hk = (1.0 - lb) * jax.nn.sigmoid(-z)
    f = lb + (1.0 - lb) * jax.nn.sigmoid(z)
    hg = jnp.log(jnp.maximum(f, TINY))
    hv = ih.astype(jnp.float32).reshape(*lead, HG_HEADS, HG_VDIM)
    return qa, ka, va, ga, hq, hk, hv, hg, gh


def window_mask(qpos, kpos):
    qq = qpos[..., :, None]
    meta = jnp.arange(N_META) <= qq
    kk = kpos[..., None, :]
    band = (kk >= N_META) & (kk <= qq) & (qq - kk < WINDOW)
    return jnp.concatenate([meta, band], axis=-1)


def sink_attention(q, k, v, mask, sink):
    s = jnp.einsum('bnqhgd,bnchd->bnhgqc', q, k).astype(jnp.float32) * (HEAD_DIM ** -0.5)
    s = jnp.where(mask[None, :, None, None], s, NEG_BIG)
    sink_col = jnp.broadcast_to(sink.astype(jnp.float32).reshape(1, 1, KV_HEADS, GQA_GROUP, 1, 1), s.shape[:-1] + (1,))
    prob = jax.nn.softmax(jnp.concatenate([s, sink_col], axis=-1), axis=-1)[..., :-1]
    return jnp.einsum('bnhgqc,bnchd->bnqhgd', prob.astype(v.dtype), v)


def window_attention_prompt(q, k, v, sink):
    B, L = q.shape[:2]
    nb = -(-L // WINDOW)
    pad = nb * WINDOW - L
    padf = lambda a: jnp.pad(a, [(0, 0), (0, pad)] + [(0, 0)] * (a.ndim - 2))
    qb = padf(q).reshape(B, nb, WINDOW, KV_HEADS, GQA_GROUP, HEAD_DIM)
    kb = padf(k).reshape(B, nb, WINDOW, KV_HEADS, HEAD_DIM)
    vb = padf(v).reshape(B, nb, WINDOW, KV_HEADS, HEAD_DIM)
    band = lambda a: jnp.concatenate([jnp.pad(a, [(0, 0), (1, 0), (0, 0), (0, 0), (0, 0)])[:, :-1], a], axis=2)
    meta = lambda a: jnp.broadcast_to(a[:, None, :N_META], (B, nb, N_META, KV_HEADS, HEAD_DIM))
    keys = jnp.concatenate([meta(k), band(kb)], axis=2)
    vals = jnp.concatenate([meta(v), band(vb)], axis=2)
    blk = jnp.arange(nb)[:, None]
    qpos = blk * WINDOW + jnp.arange(WINDOW)[None]
    kpos = (blk - 1) * WINDOW + jnp.arange(2 * WINDOW)[None]
    o = sink_attention(qb, keys, vals, window_mask(qpos, kpos), sink)
    return o.reshape(B, nb * WINDOW, KV_HEADS, GQA_GROUP, HEAD_DIM)[:, :L]


def window_attention_sample(q, k, v, meta_k, meta_v, buf_k, buf_v, sink):
    T = q.shape[1]
    wb = buf_k.shape[1]
    qpos = PAST_LEN + jnp.arange(T)
    kpos = jnp.concatenate([PAST_LEN - wb + jnp.arange(wb), PAST_LEN + jnp.arange(T)])
    keys = jnp.concatenate([meta_k, buf_k, k], axis=1)[:, None]
    vals = jnp.concatenate([meta_v, buf_v, v], axis=1)[:, None]
    return sink_attention(q[:, None], keys, vals, window_mask(qpos, kpos)[None], sink)[:, 0]


def hgrn_chunk(S0, q, k, v, g):
    C = q.shape[1]
    G = jnp.cumsum(g, axis=1)
    o_inter = jnp.einsum('bthk,bhkv->bthv', q * jnp.exp(G), S0)
    causal = jnp.tril(jnp.ones((C, C), dtype=bool))[None, :, :, None, None]
    diff = G[:, :, None] - G[:, None, :]
    decay = jnp.where(causal, jnp.exp(jnp.where(causal, diff, 0.0)), 0.0)
    A = jnp.einsum('btshk,bthk,bshk->btsh', decay, q, k)
    o = o_inter + jnp.einsum('btsh,bshv->bthv', A, v)
    G_end = G[:, -1]
    S1 = jnp.exp(G_end)[..., None] * S0 + jnp.einsum('bshk,bshv->bhkv', k * jnp.exp(G_end[:, None] - G), v)
    return S1, o


def hgrn_prompt(q, k, v, g):
    B = q.shape[0]
    S0 = jnp.zeros((B, HG_HEADS, HG_KDIM, HG_VDIM), jnp.float32)
    S, o_meta = hgrn_chunk(S0, q[:, :N_META], k[:, :N_META], v[:, :N_META], g[:, :N_META])
    nc = (q.shape[1] - N_META) // HG_CHUNK
    to_chunks = lambda a: jnp.moveaxis(a[:, N_META:].reshape(B, nc, HG_CHUNK, *a.shape[2:]), 1, 0)
    S, o_real = lax.scan(lambda s, xs: hgrn_chunk(s, *xs), S, (to_chunks(q), to_chunks(k), to_chunks(v), to_chunks(g)))
    o_real = jnp.moveaxis(o_real, 0, 1).reshape(B, nc * HG_CHUNK, HG_HEADS, HG_VDIM)
    return jnp.concatenate([o_meta, o_real], axis=1), S


def merge(att_o, ga, hg_o, gh, hg_norm_w, w_out):
    lead = ga.shape[:-1]
    att = att_o.reshape(*lead, ATT_WIDTH) * jax.nn.silu(ga)
    hg = rms_norm(hg_o, hg_norm_w).reshape(*lead, HG_WIDTH).astype(gh.dtype) * jax.nn.silu(gh)
    return jnp.concatenate([att, hg], axis=-1) @ w_out


def setup_inputs(seed: int = 0) -> dict:
    key = jax.random.key(seed)
    ks = jax.random.split(key, 18)
    nrm = jax.random.normal
    w_buf = min(WINDOW, PAST_LEN)
    return {
        "x_prompt": nrm(ks[0], (BATCH, SEQ, D_MODEL), jnp.float32),
        "x_sample": nrm(ks[1], (DEC_BATCH, DEC_SEQ, D_MODEL), jnp.float32),
        "cache_win_k": nrm(ks[2], (DEC_BATCH, DEPTH, w_buf, KV_HEADS, HEAD_DIM), jnp.float32),
        "cache_win_v": nrm(ks[3], (DEC_BATCH, DEPTH, w_buf, KV_HEADS, HEAD_DIM), jnp.float32),
        "cache_meta_k": nrm(ks[4], (DEC_BATCH, DEPTH, N_META, KV_HEADS, HEAD_DIM), jnp.float32),
        "cache_meta_v": nrm(ks[5], (DEC_BATCH, DEPTH, N_META, KV_HEADS, HEAD_DIM), jnp.float32),
        "state_hgrn": 0.5 * nrm(ks[6], (DEC_BATCH, DEPTH, HG_HEADS, HG_KDIM, HG_VDIM), jnp.float32),
        "meta_tokens": nrm(ks[7], (N_META, D_MODEL), jnp.float32),
        "norm_w": 1.0 + 0.02 * nrm(ks[8], (DEPTH, D_MODEL), jnp.float32),
        "w_in": nrm(ks[9], (DEPTH, D_MODEL, PROJ_WIDTH), jnp.float32) * D_MODEL ** -0.5,
        "q_norm_w": 1.0 + 0.02 * nrm(ks[10], (DEPTH, HEAD_DIM), jnp.float32),
        "k_norm_w": 1.0 + 0.02 * nrm(ks[11], (DEPTH, HEAD_DIM), jnp.float32),
        "attn_sinks": 0.5 * nrm(ks[12], (DEPTH, ATT_HEADS), jnp.float32),
        "hg_lb_logits": 0.1 * nrm(ks[13], (DEPTH, HG_HEADS * HG_KDIM), jnp.float32),
        "hg_norm_w": 1.0 + 0.02 * nrm(ks[14], (DEPTH, HG_VDIM), jnp.float32),
        "w_out": nrm(ks[15], (DEPTH, MIX_WIDTH, D_MODEL), jnp.float32) * MIX_WIDTH ** -0.5,
    }


def reference(x_prompt, x_sample, cache_win_k, cache_win_v, cache_meta_k, cache_meta_v, state_hgrn,
              meta_tokens, norm_w, w_in, q_norm_w, k_norm_w, attn_sinks, hg_lb_logits, hg_norm_w, w_out):
    lb_p = jax.nn.softmax(hg_lb_logits.astype(jnp.float32), axis=0)
    lb_all = (jnp.cumsum(lb_p, axis=0) - lb_p[0]).reshape(DEPTH, HG_HEADS, HG_KDIM)
    B = x_prompt.shape[0]
    w_buf = cache_win_k.shape[2]
    xp = jnp.concatenate([jnp.broadcast_to(meta_tokens[None].astype(x_prompt.dtype), (B, N_META, D_MODEL)), x_prompt], axis=1)
    xs = x_sample
    wkp, wvp, mkp, mvp, hsp, wks, wvs, hss = [], [], [], [], [], [], [], []
    for l in range(DEPTH):
        qa, ka, va, ga, hq, hk, hv, hg, gh = project(xp, norm_w[l], w_in[l], q_norm_w[l], k_norm_w[l], lb_all[l])
        att = window_attention_prompt(qa, ka, va, attn_sinks[l])
        hgo, S_p = hgrn_prompt(hq, hk, hv, hg)
        xp = xp + merge(att, ga, hgo, gh, hg_norm_w[l], w_out[l])
        wkp.append(ka[:, -w_buf:]); wvp.append(va[:, -w_buf:])
        mkp.append(ka[:, :N_META]); mvp.append(va[:, :N_META])
        hsp.append(S_p)
        qa, ka, va, ga, hq, hk, hv, hg, gh = project(xs, norm_w[l], w_in[l], q_norm_w[l], k_norm_w[l], lb_all[l])
        att = window_attention_sample(qa, ka, va, cache_meta_k[:, l], cache_meta_v[:, l], cache_win_k[:, l], cache_win_v[:, l], attn_sinks[l])
        S_s, hgo = hgrn_chunk(state_hgrn[:, l].astype(jnp.float32), hq, hk, hv, hg)
        xs = xs + merge(att, ga, hgo, gh, hg_norm_w[l], w_out[l])
        wks.append(jnp.concatenate([cache_win_k[:, l], ka], axis=1)[:, -w_buf:])
        wvs.append(jnp.concatenate([cache_win_v[:, l], va], axis=1)[:, -w_buf:])
        hss.append(S_s)
    y_prompt = xp[:, N_META:]
    y_sample = xs
    return (y_prompt, y_sample, jnp.stack(wkp, 1), jnp.stack(wvp, 1), jnp.stack(mkp, 1), jnp.stack(mvp, 1), jnp.stack(hsp, 1),
            jnp.stack(wks, 1), jnp.stack(wvs, 1), jnp.stack(hss, 1))
```

```python
import functools

import numpy as np
import jax
import jax.numpy as jnp
from jax import lax
from jax.experimental import pallas as pl
from jax.experimental.pallas import tpu as pltpu

F32 = jnp.float32
BF16 = jnp.bfloat16

D_MODEL = 1024
N_META = 16
WINDOW = 128
HEAD_DIM = 64
ATT_WIDTH = 512
ATT_HEADS = 8
KV_HEADS = 2
GQA_GROUP = 4
KV_WIDTH = KV_HEADS * HEAD_DIM
HG_WIDTH = 512
HG_HEADS = 4
HG_KDIM = 128
HG_VDIM = 128
PROJ_WIDTH = 3328
EPS = 1e-6
NEG_BIG = -1e30
TINY = 1e-30

C_Q, C_K, C_V, C_GA, C_QH, C_FH, C_IH, C_GH = 0, 512, 640, 768, 1280, 1792, 2304, 2816

SUBLANES = 8
HG_CHUNK = 64
ROW_TILE = 512
HG_TILE = 256
SAMPLE_TILE = 16
VMEM_LIMIT = 48 * 1024 * 1024


def _dot(a, b):
    return jnp.dot(a, b, preferred_element_type=F32)


def _dot_nt(a, b):
    return lax.dot_general(a, b, (((1,), (1,)), ((), ())), preferred_element_type=F32)


def _dot_tn(a, b):
    return lax.dot_general(a, b, (((0,), (0,)), ((), ())), preferred_element_type=F32)


def _silu(x):
    return x * (1.0 / (1.0 + jnp.exp(-x)))


def _group_mean_sq(p, bd):
    sq = p * p
    hi = sq.astype(BF16)
    lo = (sq - hi.astype(F32)).astype(BF16)
    return _dot(hi, bd) + _dot(lo, bd)


def _proj_kernel(layer, x_ref, nw_ref, w_ref, qnw_ref, knw_ref, lbl_ref, bd_ref,
                 q_ref, kv_ref, sga_ref, hq_ref, hk_ref, hv_ref, g_ref, sgh_ref):
    x = x_ref[...]
    ms = jnp.mean(x * x, axis=-1, keepdims=True)
    h = (x * lax.rsqrt(ms + EPS) * nw_ref[...]).astype(BF16)

    def proj(lo, hi):
        return _dot(h, w_ref[:, lo:hi])

    bd = bd_ref[...]
    halves = []
    for c in range(2):
        pq = proj(C_Q + 256 * c, C_Q + 256 * (c + 1))
        halves.append(pq * lax.rsqrt(_group_mean_sq(pq, bd) + EPS))
    q = jnp.concatenate(halves, axis=-1) * qnw_ref[...]
    q_ref[...] = q.astype(q_ref.dtype)

    pk = proj(C_K, C_V)
    kn = pk * lax.rsqrt(_group_mean_sq(pk, bd[:KV_WIDTH, :KV_WIDTH]) + EPS) * knw_ref[...]
    kv_ref[:, 0:KV_WIDTH] = kn
    kv_ref[:, KV_WIDTH:2 * KV_WIDTH] = proj(C_V, C_GA)

    sga_ref[...] = _silu(proj(C_GA, C_QH)).astype(sga_ref.dtype)
    hq_ref[...] = _silu(proj(C_QH, C_FH)).astype(hq_ref.dtype)

    logits = lbl_ref[...]
    e = jnp.exp(logits - jnp.max(logits, axis=0, keepdims=True))
    p = e / jnp.sum(e, axis=0, keepdims=True)
    lb = jnp.zeros((1, HG_WIDTH), F32)
    for j in range(1, layer + 1):
        lb = lb + p[j:j + 1, :]

    z = proj(C_FH, C_IH)
    ez = jnp.exp(-jnp.abs(z))
    r = 1.0 / (1.0 + ez)
    pos = z >= 0.0
    sig_pos = jnp.where(pos, r, ez * r)
    sig_neg = jnp.where(pos, ez * r, r)
    hk_ref[...] = ((1.0 - lb) * sig_neg).astype(hk_ref.dtype)
    f = lb + (1.0 - lb) * sig_pos
    g_ref[...] = jnp.log(jnp.maximum(f, TINY))

    hv_ref[...] = proj(C_IH, C_GH).astype(hv_ref.dtype)
    sgh_ref[...] = _silu(proj(C_GH, PROJ_WIDTH)).astype(sgh_ref.dtype)


def _project(layer, x, norm_w, w_in, qnw, knw, lb_logits, bd, row_tile, act_dtype):
    n = x.shape[0]
    rows = lambda width: pl.BlockSpec((row_tile, width), lambda i: (i, 0))
    full = lambda shape: pl.BlockSpec(shape, lambda i: (0,) * len(shape))
    out = lambda width, dtype: jax.ShapeDtypeStruct((n, width), dtype)
    return pl.pallas_call(
        functools.partial(_proj_kernel, layer),
        grid=(n // row_tile,),
        in_specs=[rows(D_MODEL), full((1, D_MODEL)), full((D_MODEL, PROJ_WIDTH)), full((1, ATT_WIDTH)),
                  full((1, KV_WIDTH)), full(lb_logits.shape), full((256, 256))],
        out_specs=[rows(ATT_WIDTH), rows(2 * KV_WIDTH), rows(ATT_WIDTH), rows(HG_WIDTH), rows(HG_WIDTH),
                   rows(HG_WIDTH), rows(HG_WIDTH), rows(HG_WIDTH)],
        out_shape=[out(ATT_WIDTH, act_dtype), out(2 * KV_WIDTH, F32), out(ATT_WIDTH, act_dtype),
                   out(HG_WIDTH, act_dtype), out(HG_WIDTH, act_dtype), out(HG_WIDTH, act_dtype),
                   out(HG_WIDTH, F32), out(HG_WIDTH, act_dtype)],
        compiler_params=pltpu.CompilerParams(dimension_semantics=("arbitrary",), vmem_limit_bytes=VMEM_LIMIT),
        name="proj",
    )(x, norm_w, w_in, qnw, knw, lb_logits, bd)


def _out_kernel(att_ref, hg_ref, x_ref, w_ref, y_ref):
    y_ref[...] = (x_ref[...] + _dot(att_ref[...].astype(BF16), w_ref[0:ATT_WIDTH, :])
                  + _dot(hg_ref[...].astype(BF16), w_ref[ATT_WIDTH:, :]))


def _out_project(att, hg, x, w_out, row_tile):
    n = x.shape[0]
    rows = lambda width: pl.BlockSpec((row_tile, width), lambda i: (i, 0))
    return pl.pallas_call(
        _out_kernel,
        grid=(n // row_tile,),
        in_specs=[rows(ATT_WIDTH), rows(HG_WIDTH), rows(D_MODEL),
                  pl.BlockSpec((D_MODEL, D_MODEL), lambda i: (0, 0))],
        out_specs=rows(D_MODEL),
        out_shape=jax.ShapeDtypeStruct((n, D_MODEL), F32),
        compiler_params=pltpu.CompilerParams(dimension_semantics=("arbitrary",), vmem_limit_bytes=VMEM_LIMIT),
        name="out_proj",
    )(att, hg, x, w_out)


def _split_heads(k):
    first = lax.broadcasted_iota(jnp.int32, k.shape, 1) < HEAD_DIM
    return (jnp.where(first, k, 0.0).astype(BF16), jnp.where(first, 0.0, k).astype(BF16))


def _attend(q, key_sets, sink_of):
    m = q.shape[0]
    first = lax.broadcasted_iota(jnp.int32, (m, KV_WIDTH), 1) < HEAD_DIM
    blocks = []
    for g in range(GQA_GROUP):
        qg = q[:, g * KV_WIDTH:(g + 1) * KV_WIDTH]
        per_head = []
        for h in range(KV_HEADS):
            scores = []
            for k_heads, _, mask in key_sets:
                s = _dot_nt(qg, k_heads[h])
                if mask is not None:
                    s = jnp.where(mask, s, NEG_BIG)
                scores.append(s)
            sink = sink_of(h, g)
            mx = jnp.max(scores[0], axis=-1, keepdims=True)
            for s in scores[1:]:
                mx = jnp.maximum(mx, jnp.max(s, axis=-1, keepdims=True))
            mx = jnp.maximum(mx, sink)
            den = jnp.exp(sink - mx)
            acc = jnp.zeros((m, KV_WIDTH), F32)
            for s, (_, v, _) in zip(scores, key_sets):
                p = jnp.exp(s - mx)
                den = den + jnp.sum(p, axis=-1, keepdims=True)
                acc = acc + _dot(p.astype(BF16), v)
            per_head.append(acc * (1.0 / den))
        blocks.append(jnp.where(first, per_head[0], per_head[1]))
    return jnp.concatenate(blocks, axis=-1)


def _prompt_attn_kernel(sink_ref, q_ref, kvc_ref, kvp_ref, kvm_ref, sga_ref, o_ref):
    i = pl.program_id(1)
    kv2 = jnp.concatenate([kvp_ref[...], kvc_ref[...]], axis=0)
    kvm = kvm_ref[...]
    row = lax.broadcasted_iota(jnp.int32, (WINDOW, 2 * WINDOW), 0)
    col = lax.broadcasted_iota(jnp.int32, (WINDOW, 2 * WINDOW), 1)
    no_prev = jnp.where(i > 0, 0, 2 * WINDOW)
    band = jnp.where(col < WINDOW, col - row - 1 - no_prev, row - (col - WINDOW)) >= 0
    key_sets = [(_split_heads(kv2[:, :KV_WIDTH]), kv2[:, KV_WIDTH:].astype(BF16), band),
                (_split_heads(kvm[:, :KV_WIDTH]), kvm[:, KV_WIDTH:].astype(BF16), None)]
    att = _attend(q_ref[...], key_sets, lambda h, g: sink_ref[h * GQA_GROUP + g])
    o_ref[...] = (att * sga_ref[...].astype(F32)).astype(o_ref.dtype)


def _prompt_attention(sinks, q, kv, kv_meta, sga, batch, seq):
    nb = seq // WINDOW
    blk = lambda width: pl.BlockSpec((WINDOW, width), lambda b, i, s: (b * nb + i, 0))
    prev = pl.BlockSpec((WINDOW, 2 * KV_WIDTH), lambda b, i, s: (b * nb + jnp.maximum(i - 1, 0), 0))
    meta = pl.BlockSpec((N_META, 2 * KV_WIDTH), lambda b, i, s: (0, 0))
    return pl.pallas_call(
        _prompt_attn_kernel,
        grid_spec=pltpu.PrefetchScalarGridSpec(
            num_scalar_prefetch=1, grid=(batch, nb),
            in_specs=[blk(ATT_WIDTH), blk(2 * KV_WIDTH), prev, meta, blk(ATT_WIDTH)],
            out_specs=blk(ATT_WIDTH)),
        out_shape=jax.ShapeDtypeStruct((batch * seq, ATT_WIDTH), BF16),
        compiler_params=pltpu.CompilerParams(dimension_semantics=("arbitrary", "arbitrary"),
                                             vmem_limit_bytes=VMEM_LIMIT),
        name="prompt_attn",
    )(sinks, q, kv, kv, kv_meta, sga)


def _cumsum_rows(g, tri):
    g1 = g.astype(BF16)
    r1 = g - g1.astype(F32)
    g2 = r1.astype(BF16)
    g3 = (r1 - g2.astype(F32)).astype(BF16)
    return _dot(tri, g1) + _dot(tri, g2) + _dot(tri, g3)


def _lower_tri(c):
    row = lax.broadcasted_iota(jnp.int32, (c, c), 0)
    col = lax.broadcasted_iota(jnp.int32, (c, c), 1)
    return jnp.where(col <= row, 1.0, 0.0).astype(BF16)


def _hgrn_masks(c):
    row = lax.broadcasted_iota(jnp.int32, (c, c), 0)
    col = lax.broadcasted_iota(jnp.int32, (c, c), 1)
    levels = []
    bs = c // 2
    while bs >= SUBLANES:
        rb, cb = row // bs, col // bs
        levels.append((bs, jnp.where(jnp.logical_and(rb % 2 == 1, cb == rb - 1), 1.0, 0.0)))
        bs //= 2
    sub = lax.broadcasted_iota(jnp.int32, (c, HG_KDIM), 0) % SUBLANES
    lags = [sub >= d for d in range(SUBLANES)]
    return levels, lags


def _hgrn_chunk(q, k, v, gc, st, masks):
    c = q.shape[0]
    levels, lags = masks
    n_grp = c // SUBLANES
    grp = lambda a, i: a[i * SUBLANES:(i + 1) * SUBLANES, :]
    last_row = lambda i: jnp.broadcast_to(gc[i * SUBLANES + SUBLANES - 1:(i + 1) * SUBLANES, :], (SUBLANES, HG_KDIM))

    g_end = gc[c - 1:c, :]
    o = _dot_nt((q * jnp.exp(gc)).astype(BF16), st.astype(BF16))

    a = jnp.zeros((c, c), F32)
    for bs, mask in levels:
        per = bs // SUBLANES
        parts = []
        for i in range(n_grp):
            blk = i // per
            if blk % 2 == 1:
                parts.append(grp(q, i) * jnp.exp(grp(gc, i) - last_row(blk * per - 1)))
            else:
                parts.append(grp(k, i) * jnp.exp(last_row(blk * per + per - 1) - grp(gc, i)))
        u = jnp.concatenate(parts, axis=0).astype(BF16)
        a = a + _dot_nt(u, u) * mask
    o = o + _dot(a.astype(BF16), v.astype(BF16))

    for d in range(SUBLANES):
        if d == 0:
            w = jnp.sum(q * k, axis=-1, keepdims=True)
            o = o + w * v
        else:
            kd = pltpu.roll(k, d, 0)
            gd = pltpu.roll(gc, d, 0)
            vd = pltpu.roll(v, d, 0)
            w = jnp.sum(q * kd * jnp.exp(gc - gd), axis=-1, keepdims=True)
            o = o + jnp.where(lags[d], w, 0.0) * vd

    kdec = (k * jnp.exp(g_end - gc)).astype(BF16)
    st_new = st * jnp.exp(g_end) + _dot_tn(v.astype(BF16), kdec)
    return o, st_new


def _head_norm_gate(o, nw, gate):
    ms = jnp.mean(o * o, axis=-1, keepdims=True)
    return o * lax.rsqrt(ms + EPS) * nw * gate


def _prompt_hgrn_kernel(hq_ref, hk_ref, hv_ref, g_ref, sgh_ref, s0_ref, nw_ref, o_ref, sfin_ref, st_ref):
    j = pl.program_id(1)

    @pl.when(j == 0)
    def _():
        st_ref[...] = s0_ref[...]

    tri = _lower_tri(HG_CHUNK)
    masks = _hgrn_masks(HG_CHUNK)
    nw = nw_ref[...]

    def chunk(ci, carry):
        rows = pl.ds(pl.multiple_of(ci * HG_CHUNK, HG_CHUNK), HG_CHUNK)
        gc = _cumsum_rows(g_ref[rows, :], tri)
        for h in range(HG_HEADS):
            lanes = slice(h * HG_KDIM, (h + 1) * HG_KDIM)
            o, st = _hgrn_chunk(hq_ref[rows, lanes].astype(F32), hk_ref[rows, lanes].astype(F32),
                                hv_ref[rows, lanes].astype(F32), gc[:, lanes], st_ref[h], masks)
            st_ref[h] = st
            o_ref[rows, lanes] = _head_norm_gate(o, nw, sgh_ref[rows, lanes].astype(F32)).astype(o_ref.dtype)
        return carry

    lax.fori_loop(0, HG_TILE // HG_CHUNK, chunk, 0)

    @pl.when(j == pl.num_programs(1) - 1)
    def _():
        for h in range(HG_HEADS):
            sfin_ref[h] = st_ref[h].T


def _prompt_hgrn(hq, hk, hv, g, sgh, s0t, hg_nw, batch, seq):
    nt = seq // HG_TILE
    blk = pl.BlockSpec((HG_TILE, HG_WIDTH), lambda b, j: (b * nt + j, 0))
    state_shape = (HG_HEADS, HG_VDIM, HG_KDIM)
    return pl.pallas_call(
        _prompt_hgrn_kernel,
        grid=(batch, nt),
        in_specs=[blk, blk, blk, blk, blk, pl.BlockSpec(state_shape, lambda b, j: (0, 0, 0)),
                  pl.BlockSpec((1, HG_VDIM), lambda b, j: (0, 0))],
        out_specs=[blk, pl.BlockSpec((None,) + state_shape, lambda b, j: (b, 0, 0, 0))],
        out_shape=[jax.ShapeDtypeStruct((batch * seq, HG_WIDTH), BF16),
                   jax.ShapeDtypeStruct((batch,) + state_shape, F32)],
        scratch_shapes=[pltpu.VMEM(state_shape, F32)],
        compiler_params=pltpu.CompilerParams(dimension_semantics=("arbitrary", "arbitrary"),
                                             vmem_limit_bytes=VMEM_LIMIT),
        name="prompt_hgrn",
    )(hq, hk, hv, g, sgh, s0t, hg_nw)


def _meta_mix_kernel(sink_ref, q_ref, kv_ref, sga_ref, hq_ref, hk_ref, hv_ref, g_ref, sgh_ref, nw_ref,
                     att_ref, hg_ref, st_ref):
    kv = kv_ref[...]
    row = lax.broadcasted_iota(jnp.int32, (N_META, N_META), 0)
    col = lax.broadcasted_iota(jnp.int32, (N_META, N_META), 1)
    att = _attend(q_ref[...].astype(BF16),
                  [(_split_heads(kv[:, :KV_WIDTH]), kv[:, KV_WIDTH:].astype(BF16), col <= row)],
                  lambda h, g: sink_ref[h * GQA_GROUP + g])
    att_ref[...] = (att * sga_ref[...]).astype(att_ref.dtype)

    gc = _cumsum_rows(g_ref[...], _lower_tri(N_META))
    masks = _hgrn_masks(N_META)
    for h in range(HG_HEADS):
        lanes = slice(h * HG_KDIM, (h + 1) * HG_KDIM)
        o, st = _hgrn_chunk(hq_ref[:, lanes], hk_ref[:, lanes], hv_ref[:, lanes], gc[:, lanes],
                            jnp.zeros((HG_VDIM, HG_KDIM), F32), masks)
        st_ref[h] = st
        hg_ref[:, lanes] = _head_norm_gate(o, nw_ref[...], sgh_ref[:, lanes]).astype(hg_ref.dtype)


def _meta_mix(sinks, q, kv, sga, hq, hk, hv, g, sgh, hg_nw):
    blk_idx = q.shape[0] // N_META - 1
    blk = lambda width: pl.BlockSpec((N_META, width), lambda i, s: (blk_idx, 0))
    first = lambda width: pl.BlockSpec((N_META, width), lambda i, s: (0, 0))
    return pl.pallas_call(
        _meta_mix_kernel,
        grid_spec=pltpu.PrefetchScalarGridSpec(
            num_scalar_prefetch=1, grid=(1,),
            in_specs=[blk(ATT_WIDTH), blk(2 * KV_WIDTH), blk(ATT_WIDTH), blk(HG_WIDTH), blk(HG_WIDTH),
                      blk(HG_WIDTH), blk(HG_WIDTH), blk(HG_WIDTH), pl.BlockSpec((1, HG_VDIM), lambda i, s: (0, 0))],
            out_specs=[first(ATT_WIDTH), first(HG_WIDTH),
                       pl.BlockSpec((HG_HEADS, HG_VDIM, HG_KDIM), lambda i, s: (0, 0, 0))]),
        out_shape=[jax.ShapeDtypeStruct((N_META, ATT_WIDTH), F32),
                   jax.ShapeDtypeStruct((N_META, HG_WIDTH), F32),
                   jax.ShapeDtypeStruct((HG_HEADS, HG_VDIM, HG_KDIM), F32)],
        compiler_params=pltpu.CompilerParams(dimension_semantics=("arbitrary",)),
        name="meta_mix",
    )(sinks, q, kv, sga, hq, hk, hv, g, sgh, hg_nw)


def _sample_mix_kernel(q_ref, kv_ref, sga_ref, hq_ref, hk_ref, hv_ref, g_ref, sgh_ref, nw_ref,
                       sink_ref, seg_ref, exp_ref, ck_ref, cv_ref, mk_ref, mv_ref, s_ref,
                       att_ref, hg_ref, nk_ref, nv_ref, ns_ref):
    n_keys = WINDOW + N_META + SUBLANES
    key_row = lax.broadcasted_iota(jnp.int32, (n_keys, KV_WIDTH), 0)
    visible = jnp.logical_and(key_row >= 1, key_row <= WINDOW + N_META)
    win_row = lax.broadcasted_iota(jnp.int32, (WINDOW, KV_WIDTH), 0)
    sink = sink_ref[...]
    seg = seg_ref[...]
    expand = exp_ref[...]
    nw = nw_ref[...]

    def per_seq(b, carry):
        kv_new = kv_ref[b]
        k_new = kv_new[:, 0:KV_WIDTH]
        v_new = kv_new[:, KV_WIDTH:2 * KV_WIDTH]
        ck = ck_ref[b]
        cv = cv_ref[b]
        keys = jnp.concatenate([ck, mk_ref[b], jnp.broadcast_to(k_new, (SUBLANES, KV_WIDTH))], axis=0)
        vals = jnp.concatenate([cv, mv_ref[b], jnp.broadcast_to(v_new, (SUBLANES, KV_WIDTH))], axis=0)
        prod = jnp.concatenate([keys] * GQA_GROUP, axis=1) * q_ref[b]
        s = jnp.where(visible, _dot(prod.astype(BF16), seg), NEG_BIG)
        mx = jnp.maximum(jnp.max(s, axis=0, keepdims=True), sink)
        p = jnp.exp(s - mx)
        den = jnp.sum(p, axis=0, keepdims=True) + jnp.exp(sink - mx)
        p = p * (1.0 / den)
        wide = _dot(p.astype(BF16), expand)
        att = jnp.sum(wide * jnp.concatenate([vals] * GQA_GROUP, axis=1), axis=0, keepdims=True)
        att_ref[b] = att * sga_ref[b]

        nk_ref[b] = jnp.where(win_row == WINDOW - 1, k_new, pltpu.roll(ck, WINDOW - 1, 0))
        nv_ref[b] = jnp.where(win_row == WINDOW - 1, v_new, pltpu.roll(cv, WINDOW - 1, 0))

        g_row, hq_row, hk_row, hv_row, sgh_row = g_ref[b], hq_ref[b], hk_ref[b], hv_ref[b], sgh_ref[b]
        outs = []
        for h in range(HG_HEADS):
            lanes = slice(h * HG_KDIM, (h + 1) * HG_KDIM)
            col = lambda r: jnp.broadcast_to(r[:, lanes], (HG_KDIM, HG_KDIM)).T
            s1 = jnp.exp(col(g_row)) * s_ref[b, h] + col(hk_row) * hv_row[:, lanes]
            ns_ref[b, h] = s1
            o = jnp.sum(col(hq_row) * s1, axis=0, keepdims=True)
            outs.append(_head_norm_gate(o, nw, sgh_row[:, lanes]))
        hg_ref[b] = jnp.concatenate(outs, axis=-1)
        return carry

    lax.fori_loop(0, SAMPLE_TILE, per_seq, 0)


def _sample_mix(layer, q, kv, sga, hq, hk, hv, g, sgh, hg_nw, sink_row, seg, expand,
                cache_k, cache_v, meta_k, meta_v, state, n_seq):
    t = SAMPLE_TILE
    as_rows = lambda a: a[:n_seq].reshape(n_seq, 1, a.shape[-1])
    rows = lambda width: pl.BlockSpec((t, 1, width), lambda i: (i, 0, 0))
    full = lambda shape: pl.BlockSpec(shape, lambda i: (0,) * len(shape))
    cache = lambda n: pl.BlockSpec((t, None, n, KV_WIDTH), lambda i: (i, layer, 0, 0))
    state_in = pl.BlockSpec((t, None, HG_HEADS, HG_KDIM, HG_VDIM), lambda i: (i, layer, 0, 0, 0))
    att, hg, nk, nv, ns = pl.pallas_call(
        _sample_mix_kernel,
        grid=(n_seq // t,),
        in_specs=[rows(ATT_WIDTH), rows(2 * KV_WIDTH), rows(ATT_WIDTH), rows(HG_WIDTH), rows(HG_WIDTH),
                  rows(HG_WIDTH), rows(HG_WIDTH), rows(HG_WIDTH), full((1, HG_VDIM)),
                  full((1, KV_WIDTH)), full(seg.shape), full(expand.shape),
                  cache(WINDOW), cache(WINDOW), cache(N_META), cache(N_META), state_in],
        out_specs=[rows(ATT_WIDTH), rows(HG_WIDTH),
                   pl.BlockSpec((t, WINDOW, KV_WIDTH), lambda i: (i, 0, 0)),
                   pl.BlockSpec((t, WINDOW, KV_WIDTH), lambda i: (i, 0, 0)),
                   pl.BlockSpec((t, HG_HEADS, HG_KDIM, HG_VDIM), lambda i: (i, 0, 0, 0))],
        out_shape=[jax.ShapeDtypeStruct((n_seq, 1, ATT_WIDTH), F32),
                   jax.ShapeDtypeStruct((n_seq, 1, HG_WIDTH), F32),
                   jax.ShapeDtypeStruct((n_seq, WINDOW, KV_WIDTH), F32),
                   jax.ShapeDtypeStruct((n_seq, WINDOW, KV_WIDTH), F32),
                   jax.ShapeDtypeStruct((n_seq, HG_HEADS, HG_KDIM, HG_VDIM), F32)],
        compiler_params=pltpu.CompilerParams(dimension_semantics=("arbitrary",), vmem_limit_bytes=VMEM_LIMIT),
        name="sample_mix",
    )(*[as_rows(a) for a in (q, kv, sga, hq, hk, hv, g, sgh)], hg_nw, sink_row, seg, expand,
      cache_k, cache_v, meta_k, meta_v, state)
    return att.reshape(n_seq, ATT_WIDTH), hg.reshape(n_seq, HG_WIDTH), nk, nv, ns


def _g_major(a, axis):
    shape = a.shape
    a = a.reshape(shape[:axis] + (KV_HEADS, GQA_GROUP, HEAD_DIM) + shape[axis + 1:])
    a = jnp.swapaxes(a, axis, axis + 1)
    return a.reshape(shape)


def _constants():
    lane = np.arange(ATT_WIDTH)
    g_of, h_of = lane // KV_WIDTH, (lane % KV_WIDTH) // HEAD_DIM
    head = h_of * GQA_GROUP + g_of
    seg = (head[:, None] == np.arange(KV_WIDTH)[None, :]).astype(np.float32)
    grp = np.arange(256) // HEAD_DIM
    bd = (grp[:, None] == grp[None, :]).astype(np.float32) / HEAD_DIM
    return jnp.asarray(seg, BF16), jnp.asarray(seg.T, BF16), jnp.asarray(bd, BF16)


def kernel(x_prompt, x_sample, cache_win_k, cache_win_v, cache_meta_k, cache_meta_v, state_hgrn, meta_tokens,
           norm_w, w_in, q_norm_w, k_norm_w, attn_sinks, hg_lb_logits, hg_norm_w, w_out):
    batch, seq, _ = x_prompt.shape
    n_seq = x_sample.shape[0]
    depth = w_in.shape[0]
    w_buf = cache_win_k.shape[2]
    assert x_sample.shape[1] == 1 and w_buf == WINDOW and seq % ROW_TILE == 0 and n_seq % SAMPLE_TILE == 0

    seg, expand, bd = _constants()
    w_in_b = jnp.concatenate([_g_major(w_in[:, :, C_Q:C_K], 2), w_in[:, :, C_K:C_GA],
                              _g_major(w_in[:, :, C_GA:C_QH], 2), w_in[:, :, C_QH:]], axis=2).astype(BF16)
    w_out_b = jnp.concatenate([_g_major(w_out[:, :ATT_WIDTH], 1), w_out[:, ATT_WIDTH:]], axis=1).astype(BF16)
    qnw = jnp.tile(q_norm_w, (1, ATT_HEADS)) * (HEAD_DIM ** -0.5)
    knw = jnp.tile(k_norm_w, (1, KV_HEADS))
    lb_logits = hg_lb_logits.astype(F32)
    sink_rows = jnp.pad(attn_sinks.astype(F32), ((0, 0), (0, KV_WIDTH - ATT_HEADS)))

    ck = cache_win_k.reshape(n_seq, depth, w_buf, KV_WIDTH)
    cv = cache_win_v.reshape(n_seq, depth, w_buf, KV_WIDTH)
    mk = cache_meta_k.reshape(n_seq, depth, N_META, KV_WIDTH)
    mv = cache_meta_v.reshape(n_seq, depth, N_META, KV_WIDTH)

    xp = x_prompt.reshape(batch * seq, D_MODEL)
    xs = jnp.concatenate([x_sample.reshape(n_seq, D_MODEL), meta_tokens.astype(F32)], axis=0)
    n_small = xs.shape[0]

    outs = {k: [] for k in ("wkp", "wvp", "mk", "mv", "hsp", "wks", "wvs", "hss")}
    for l in range(depth):
        layer_w = (norm_w[l][None], w_in_b[l], qnw[l][None], knw[l][None], lb_logits, bd)
        sinks = attn_sinks[l].astype(F32)
        hg_nw = hg_norm_w[l][None]

        sm = _project(l, xs, *layer_w, row_tile=n_small, act_dtype=F32)
        att_m, hg_m, s0t = _meta_mix(sinks, *sm, hg_nw)
        att_s, hg_s, nk, nv, ns = _sample_mix(l, *sm, hg_nw, sink_rows[l][None], seg, expand,
                                              ck, cv, mk, mv, state_hgrn, n_seq)
        kv_meta = sm[1][n_seq:]
        xs = _out_project(jnp.concatenate([att_s, att_m], axis=0), jnp.concatenate([hg_s, hg_m], axis=0),
                          xs, w_out_b[l], n_small)

        q, kv, sga, hq, hk, hv, g, sgh = _project(l, xp, *layer_w, row_tile=ROW_TILE, act_dtype=BF16)
        att = _prompt_attention(sinks, q, kv, kv_meta, sga, batch, seq)
        hg, s_fin = _prompt_hgrn(hq, hk, hv, g, sgh, s0t, hg_nw, batch, seq)
        xp = _out_project(att, hg, xp, w_out_b[l], ROW_TILE)

        kv3 = kv.reshape(batch, seq, 2 * KV_WIDTH)
        outs["wkp"].append(kv3[:, seq - w_buf:, :KV_WIDTH])
        outs["wvp"].append(kv3[:, seq - w_buf:, KV_WIDTH:])
        outs["mk"].append(jnp.broadcast_to(kv_meta[None, :, :KV_WIDTH], (batch, N_META, KV_WIDTH)))
        outs["mv"].append(jnp.broadcast_to(kv_meta[None, :, KV_WIDTH:], (batch, N_META, KV_WIDTH)))
        outs["hsp"].append(s_fin)
        outs["wks"].append(nk)
        outs["wvs"].append(nv)
        outs["hss"].append(ns)

    stack = lambda name: jnp.stack(outs[name], axis=1)
    heads = lambda a: a.reshape(a.shape[:-1] + (KV_HEADS, HEAD_DIM))
    return (xp.reshape(batch, seq, D_MODEL), xs[:n_seq].reshape(n_seq, 1, D_MODEL),
            heads(stack("wkp")), heads(stack("wvp")), heads(stack("mk")), heads(stack("mv")), stack("hsp"),
            heads(stack("wks")), heads(stack("wvs")), stack("hss"))
```

```python
import functools

import numpy as np
import jax
import jax.numpy as jnp
from jax import lax
from jax.experimental import pallas as pl
from jax.experimental.pallas import tpu as pltpu

F32 = jnp.float32
BF16 = jnp.bfloat16

D_MODEL = 1024
N_META = 16
WINDOW = 128
HEAD_DIM = 64
ATT_WIDTH = 512
ATT_HEADS = 8
KV_HEADS = 2
GQA_GROUP = 4
KV_WIDTH = KV_HEADS * HEAD_DIM
HG_WIDTH = 512
HG_HEADS = 4
HG_KDIM = 128
HG_VDIM = 128
PROJ_WIDTH = 3328
EPS = 1e-6
NEG_BIG = -1e30
TINY = 1e-30

C_Q, C_K, C_V, C_GA, C_QH, C_FH, C_IH, C_GH = 0, 512, 640, 768, 1280, 1792, 2304, 2816

SUBLANES = 8
HG_CHUNK = 64
ROW_TILE = 512
HG_TILE = 256
SAMPLE_TILE = 16
VMEM_LIMIT = 48 * 1024 * 1024


def _dot(a, b):
    return jnp.dot(a, b, preferred_element_type=F32)


def _dot_nt(a, b):
    return lax.dot_general(a, b, (((1,), (1,)), ((), ())), preferred_element_type=F32)


def _dot_tn(a, b):
    return lax.dot_general(a, b, (((0,), (0,)), ((), ())), preferred_element_type=F32)


def _silu(x):
    return x * (1.0 / (1.0 + jnp.exp(-x)))


def _group_mean_sq(p, bd):
    sq = p * p
    hi = sq.astype(BF16)
    lo = (sq - hi.astype(F32)).astype(BF16)
    return _dot(hi, bd) + _dot(lo, bd)


def _proj_kernel(layer, x_ref, nw_ref, w_ref, qnw_ref, knw_ref, lbl_ref, bd_ref,
                 q_ref, kv_ref, sga_ref, hq_ref, hk_ref, hv_ref, g_ref, sgh_ref):
    x = x_ref[...]
    ms = jnp.mean(x * x, axis=-1, keepdims=True)
    h = (x * lax.rsqrt(ms + EPS) * nw_ref[...]).astype(BF16)

    def proj(lo, hi):
        return _dot(h, w_ref[:, lo:hi])

    bd = bd_ref[...]
    halves = []
    for c in range(2):
        pq = proj(C_Q + 256 * c, C_Q + 256 * (c + 1))
        halves.append(pq * lax.rsqrt(_group_mean_sq(pq, bd) + EPS))
    q = jnp.concatenate(halves, axis=-1) * qnw_ref[...]
    q_ref[...] = q.astype(q_ref.dtype)

    pk = proj(C_K, C_V)
    kn = pk * lax.rsqrt(_group_mean_sq(pk, bd[:KV_WIDTH, :KV_WIDTH]) + EPS) * knw_ref[...]
    kv_ref[:, 0:KV_WIDTH] = kn
    kv_ref[:, KV_WIDTH:2 * KV_WIDTH] = proj(C_V, C_GA)

    sga_ref[...] = _silu(proj(C_GA, C_QH)).astype(sga_ref.dtype)
    hq_ref[...] = _silu(proj(C_QH, C_FH)).astype(hq_ref.dtype)

    logits = lbl_ref[...]
    e = jnp.exp(logits - jnp.max(logits, axis=0, keepdims=True))
    p = e / jnp.sum(e, axis=0, keepdims=True)
    lb = jnp.zeros((1, HG_WIDTH), F32)
    for j in range(1, layer + 1):
        lb = lb + p[j:j + 1, :]

    z = proj(C_FH, C_IH)
    ez = jnp.exp(-jnp.abs(z))
    r = 1.0 / (1.0 + ez)
    pos = z >= 0.0
    sig_pos = jnp.where(pos, r, ez * r)
    sig_neg = jnp.where(pos, ez * r, r)
    hk_ref[...] = ((1.0 - lb) * sig_neg).astype(hk_ref.dtype)
    f = lb + (1.0 - lb) * sig_pos
    g_ref[...] = jnp.log(jnp.maximum(f, TINY))

    hv_ref[...] = proj(C_IH, C_GH).astype(hv_ref.dtype)
    sgh_ref[...] = _silu(proj(C_GH, PROJ_WIDTH)).astype(sgh_ref.dtype)


def _project(layer, x, norm_w, w_in, qnw, knw, lb_logits, bd, row_tile, act_dtype):
    n = x.shape[0]
    rows = lambda width: pl.BlockSpec((row_tile, width), lambda i: (i, 0))
    full = lambda shape: pl.BlockSpec(shape, lambda i: (0,) * len(shape))
    out = lambda width, dtype: jax.ShapeDtypeStruct((n, width), dtype)
    return pl.pallas_call(
        functools.partial(_proj_kernel, layer),
        grid=(n // row_tile,),
        in_specs=[rows(D_MODEL), full((1, D_MODEL)), full((D_MODEL, PROJ_WIDTH)), full((1, ATT_WIDTH)),
                  full((1, KV_WIDTH)), full(lb_logits.shape), full((256, 256))],
        out_specs=[rows(ATT_WIDTH), rows(2 * KV_WIDTH), rows(ATT_WIDTH), rows(HG_WIDTH), rows(HG_WIDTH),
                   rows(HG_WIDTH), rows(HG_WIDTH), rows(HG_WIDTH)],
        out_shape=[out(ATT_WIDTH, act_dtype), out(2 * KV_WIDTH, F32), out(ATT_WIDTH, act_dtype),
                   out(HG_WIDTH, act_dtype), out(HG_WIDTH, act_dtype), out(HG_WIDTH, act_dtype),
                   out(HG_WIDTH, F32), out(HG_WIDTH, act_dtype)],
        compiler_params=pltpu.CompilerParams(dimension_semantics=("arbitrary",), vmem_limit_bytes=VMEM_LIMIT),
        name="proj",
    )(x, norm_w, w_in, qnw, knw, lb_logits, bd)


def _out_kernel(att_ref, hg_ref, x_ref, w_ref, y_ref):
    y_ref[...] = (x_ref[...] + _dot(att_ref[...].astype(BF16), w_ref[0:ATT_WIDTH, :])
                  + _dot(hg_ref[...].astype(BF16), w_ref[ATT_WIDTH:, :]))


def _out_project(att, hg, x, w_out, row_tile):
    n = x.shape[0]
    rows = lambda width: pl.BlockSpec((row_tile, width), lambda i: (i, 0))
    return pl.pallas_call(
        _out_kernel,
        grid=(n // row_tile,),
        in_specs=[rows(ATT_WIDTH), rows(HG_WIDTH), rows(D_MODEL),
                  pl.BlockSpec((D_MODEL, D_MODEL), lambda i: (0, 0))],
        out_specs=rows(D_MODEL),
        out_shape=jax.ShapeDtypeStruct((n, D_MODEL), F32),
        compiler_params=pltpu.CompilerParams(dimension_semantics=("arbitrary",), vmem_limit_bytes=VMEM_LIMIT),
        name="out_proj",
    )(att, hg, x, w_out)


def _split_heads(k):
    first = lax.broadcasted_iota(jnp.int32, k.shape, 1) < HEAD_DIM
    return (jnp.where(first, k, 0.0).astype(BF16), jnp.where(first, 0.0, k).astype(BF16))


def _attend(q, key_sets, sink_of):
    m = q.shape[0]
    first = lax.broadcasted_iota(jnp.int32, (m, KV_WIDTH), 1) < HEAD_DIM
    blocks = []
    for g in range(GQA_GROUP):
        qg = q[:, g * KV_WIDTH:(g + 1) * KV_WIDTH]
        per_head = []
        for h in range(KV_HEADS):
            scores = []
            for k_heads, _, mask in key_sets:
                s = _dot_nt(qg, k_heads[h])
                if mask is not None:
                    s = jnp.where(mask, s, NEG_BIG)
                scores.append(s)
            sink = sink_of(h, g)
            mx = jnp.max(scores[0], axis=-1, keepdims=True)
            for s in scores[1:]:
                mx = jnp.maximum(mx, jnp.max(s, axis=-1, keepdims=True))
            mx = jnp.maximum(mx, sink)
            den = jnp.exp(sink - mx)
            acc = jnp.zeros((m, KV_WIDTH), F32)
            for s, (_, v, _) in zip(scores, key_sets):
                p = jnp.exp(s - mx)
                den = den + jnp.sum(p, axis=-1, keepdims=True)
                acc = acc + _dot(p.astype(BF16), v)
            per_head.append(acc * (1.0 / den))
        blocks.append(jnp.where(first, per_head[0], per_head[1]))
    return jnp.concatenate(blocks, axis=-1)


def _prompt_attn_kernel(sink_ref, q_ref, kvc_ref, kvp_ref, kvm_ref, sga_ref, o_ref):
    i = pl.program_id(1)
    kvp, kvc, kvm = kvp_ref[...], kvc_ref[...], kvm_ref[...]
    n_keys = 2 * WINDOW + N_META
    k_all = jnp.concatenate([kvp[:, :KV_WIDTH], kvc[:, :KV_WIDTH], kvm[:, :KV_WIDTH]], axis=0)
    kk = jnp.concatenate(_split_heads(k_all), axis=0)
    v_meta = jnp.concatenate([kvm[:, KV_WIDTH:], jnp.zeros((WINDOW - N_META, KV_WIDTH), F32)], axis=0)
    v_t = jnp.concatenate([kvp[:, KV_WIDTH:].T, kvc[:, KV_WIDTH:].T, v_meta.T], axis=1).astype(BF16)
    pad = jnp.zeros((3 * WINDOW - n_keys, WINDOW), BF16)

    key = lax.broadcasted_iota(jnp.int32, (n_keys, WINDOW), 0)
    qi = lax.broadcasted_iota(jnp.int32, (n_keys, WINDOW), 1)
    no_prev = jnp.where(i > 0, 0, 2 * WINDOW)
    visible = jnp.where(key < WINDOW, key - qi - 1 - no_prev,
                        jnp.where(key < 2 * WINDOW, qi - (key - WINDOW), 0)) >= 0
    head0_rows = lax.broadcasted_iota(jnp.int32, (KV_WIDTH, WINDOW), 0) < HEAD_DIM

    for g in range(GQA_GROUP):
        lanes = slice(g * KV_WIDTH, (g + 1) * KV_WIDTH)
        s_both = _dot_nt(kk, q_ref[:, lanes])
        per_head = []
        for h in range(KV_HEADS):
            sink = sink_ref[h * GQA_GROUP + g]
            s = jnp.where(visible, s_both[h * n_keys:(h + 1) * n_keys], NEG_BIG)
            mx = jnp.maximum(jnp.max(s, axis=0, keepdims=True), sink)
            p = jnp.exp(s - mx)
            den = jnp.sum(p, axis=0, keepdims=True) + jnp.exp(sink - mx)
            p_pad = jnp.concatenate([p.astype(BF16), pad], axis=0)
            per_head.append(_dot(v_t, p_pad) * (1.0 / den))
        o_t = jnp.where(head0_rows, per_head[0], per_head[1])
        o_ref[:, lanes] = (o_t.T * sga_ref[:, lanes].astype(F32)).astype(o_ref.dtype)


def _prompt_attention(sinks, q, kv, kv_meta, sga, batch, seq):
    nb = seq // WINDOW
    blk = lambda width: pl.BlockSpec((WINDOW, width), lambda b, i, s: (b * nb + i, 0))
    prev = pl.BlockSpec((WINDOW, 2 * KV_WIDTH), lambda b, i, s: (b * nb + jnp.maximum(i - 1, 0), 0))
    meta = pl.BlockSpec((N_META, 2 * KV_WIDTH), lambda b, i, s: (0, 0))
    return pl.pallas_call(
        _prompt_attn_kernel,
        grid_spec=pltpu.PrefetchScalarGridSpec(
            num_scalar_prefetch=1, grid=(batch, nb),
            in_specs=[blk(ATT_WIDTH), blk(2 * KV_WIDTH), prev, meta, blk(ATT_WIDTH)],
            out_specs=blk(ATT_WIDTH)),
        out_shape=jax.ShapeDtypeStruct((batch * seq, ATT_WIDTH), BF16),
        compiler_params=pltpu.CompilerParams(dimension_semantics=("arbitrary", "arbitrary"),
                                             vmem_limit_bytes=VMEM_LIMIT),
        name="prompt_attn",
    )(sinks, q, kv, kv, kv_meta, sga)


def _level_sizes(c):
    return [c >> (i + 1) for i in range(c.bit_length() - 1)]


def _decay_sum_matrix(c):
    t = np.arange(c)[:, None]
    r = np.arange(c)[None, :]
    mats = [r <= t, r > t]
    for bs in _level_sizes(c):
        a = (t // (2 * bs)) * (2 * bs) + bs - 1
        mats.append(((r > t) & (r <= a)) | ((r > a) & (r <= t)))
    return np.tile(np.concatenate(mats, axis=0).astype(np.float32), (1, 3))


def _decay_exponents(g, dmat):
    g1 = g.astype(BF16)
    r1 = g - g1.astype(F32)
    g2 = r1.astype(BF16)
    g3 = (r1 - g2.astype(F32)).astype(BF16)
    return _dot(dmat, jnp.concatenate([g1, g2, g3], axis=0))


def _hgrn_masks(c):
    row = lax.broadcasted_iota(jnp.int32, (c, c), 0)
    col = lax.broadcasted_iota(jnp.int32, (c, c), 1)
    row_k = lax.broadcasted_iota(jnp.int32, (c, HG_KDIM), 0)
    levels = []
    for bs in _level_sizes(c):
        rb, cb = row // bs, col // bs
        pairs = ((rb % 2) * (1 - jnp.abs(cb - rb + 1))) > 0
        levels.append((bs, (row_k // bs) % 2 == 1, pairs))
    return levels, row == col


def _hgrn_chunk(q, k, v, ex, st, masks):
    c = q.shape[0]
    levels, diag = masks
    o = _dot_nt((q * jnp.exp(ex(0))).astype(BF16), st.astype(BF16))

    a = jnp.where(diag, jnp.sum(q * k, axis=-1, keepdims=True), 0.0)
    for i, (bs, q_side, pairs) in enumerate(levels):
        if bs >= SUBLANES:
            side = jnp.concatenate([(q if (j // bs) % 2 == 1 else k)[j:j + SUBLANES] for j in range(0, c, SUBLANES)],
                                   axis=0)
        else:
            side = jnp.where(q_side, q, k)
        u = (side * jnp.exp(ex(2 + i))).astype(BF16)
        a = jnp.where(pairs, _dot_nt(u, u), a)
    o = o + _dot(a.astype(BF16), v.astype(BF16))

    kdec = (k * jnp.exp(ex(1))).astype(BF16)
    g_end = ex(0)[c - 1:c, :]
    st_new = st * jnp.exp(g_end) + _dot_tn(v.astype(BF16), kdec)
    return o, st_new


def _head_norm_gate(o, nw, gate):
    ms = jnp.mean(o * o, axis=-1, keepdims=True)
    return o * lax.rsqrt(ms + EPS) * nw * gate


def _prompt_hgrn_kernel(hq_ref, hk_ref, hv_ref, g_ref, sgh_ref, s0_ref, nw_ref, dmat_ref, o_ref, sfin_ref, st_ref):
    j = pl.program_id(1)

    @pl.when(j == 0)
    def _():
        st_ref[...] = s0_ref[...]

    masks = _hgrn_masks(HG_CHUNK)
    nw = nw_ref[...]
    dmat = dmat_ref[...]

    states = [st_ref[h] for h in range(HG_HEADS)]
    for ci in range(HG_TILE // HG_CHUNK):
        rows = slice(ci * HG_CHUNK, (ci + 1) * HG_CHUNK)
        ex = _decay_exponents(g_ref[rows, :], dmat)
        for h in range(HG_HEADS):
            lanes = slice(h * HG_KDIM, (h + 1) * HG_KDIM)
            o, states[h] = _hgrn_chunk(hq_ref[rows, lanes].astype(F32), hk_ref[rows, lanes].astype(F32),
                                       hv_ref[rows, lanes].astype(F32),
                                       lambda i: ex[i * HG_CHUNK:(i + 1) * HG_CHUNK, lanes], states[h], masks)
            o_ref[rows, lanes] = _head_norm_gate(o, nw, sgh_ref[rows, lanes].astype(F32)).astype(o_ref.dtype)
    for h in range(HG_HEADS):
        st_ref[h] = states[h]

    @pl.when(j == pl.num_programs(1) - 1)
    def _():
        for h in range(HG_HEADS):
            sfin_ref[h] = st_ref[h].T


def _prompt_hgrn(hq, hk, hv, g, sgh, s0t, hg_nw, batch, seq):
    nt = seq // HG_TILE
    blk = pl.BlockSpec((HG_TILE, HG_WIDTH), lambda b, j: (b * nt + j, 0))
    state_shape = (HG_HEADS, HG_VDIM, HG_KDIM)
    dmat = jnp.asarray(_decay_sum_matrix(HG_CHUNK), BF16)
    return pl.pallas_call(
        _prompt_hgrn_kernel,
        grid=(batch, nt),
        in_specs=[blk, blk, blk, blk, blk, pl.BlockSpec(state_shape, lambda b, j: (0, 0, 0)),
                  pl.BlockSpec((1, HG_VDIM), lambda b, j: (0, 0)), pl.BlockSpec(dmat.shape, lambda b, j: (0, 0))],
        out_specs=[blk, pl.BlockSpec((None,) + state_shape, lambda b, j: (b, 0, 0, 0))],
        out_shape=[jax.ShapeDtypeStruct((batch * seq, HG_WIDTH), BF16),
                   jax.ShapeDtypeStruct((batch,) + state_shape, F32)],
        scratch_shapes=[pltpu.VMEM(state_shape, F32)],
        compiler_params=pltpu.CompilerParams(dimension_semantics=("arbitrary", "arbitrary"),
                                             vmem_limit_bytes=VMEM_LIMIT),
        name="prompt_hgrn",
    )(hq, hk, hv, g, sgh, s0t, hg_nw, dmat)


def _meta_mix_kernel(sink_ref, q_ref, kv_ref, sga_ref, hq_ref, hk_ref, hv_ref, g_ref, sgh_ref, nw_ref, dmat_ref,
                     att_ref, hg_ref, st_ref):
    kv = kv_ref[...]
    row = lax.broadcasted_iota(jnp.int32, (N_META, N_META), 0)
    col = lax.broadcasted_iota(jnp.int32, (N_META, N_META), 1)
    att = _attend(q_ref[...].astype(BF16),
                  [(_split_heads(kv[:, :KV_WIDTH]), kv[:, KV_WIDTH:].astype(BF16), col <= row)],
                  lambda h, g: sink_ref[h * GQA_GROUP + g])
    att_ref[...] = (att * sga_ref[...]).astype(att_ref.dtype)

    ex = _decay_exponents(g_ref[...], dmat_ref[...])
    masks = _hgrn_masks(N_META)
    for h in range(HG_HEADS):
        lanes = slice(h * HG_KDIM, (h + 1) * HG_KDIM)
        o, st = _hgrn_chunk(hq_ref[:, lanes], hk_ref[:, lanes], hv_ref[:, lanes],
                            lambda i: ex[i * N_META:(i + 1) * N_META, lanes],
                            jnp.zeros((HG_VDIM, HG_KDIM), F32), masks)
        st_ref[h] = st
        hg_ref[:, lanes] = _head_norm_gate(o, nw_ref[...], sgh_ref[:, lanes]).astype(hg_ref.dtype)


def _meta_mix(sinks, q, kv, sga, hq, hk, hv, g, sgh, hg_nw):
    blk_idx = q.shape[0] // N_META - 1
    blk = lambda width: pl.BlockSpec((N_META, width), lambda i, s: (blk_idx, 0))
    first = lambda width: pl.BlockSpec((N_META, width), lambda i, s: (0, 0))
    dmat = jnp.asarray(_decay_sum_matrix(N_META), BF16)
    return pl.pallas_call(
        _meta_mix_kernel,
        grid_spec=pltpu.PrefetchScalarGridSpec(
            num_scalar_prefetch=1, grid=(1,),
            in_specs=[blk(ATT_WIDTH), blk(2 * KV_WIDTH), blk(ATT_WIDTH), blk(HG_WIDTH), blk(HG_WIDTH),
                      blk(HG_WIDTH), blk(HG_WIDTH), blk(HG_WIDTH), pl.BlockSpec((1, HG_VDIM), lambda i, s: (0, 0)),
                      pl.BlockSpec(dmat.shape, lambda i, s: (0, 0))],
            out_specs=[first(ATT_WIDTH), first(HG_WIDTH),
                       pl.BlockSpec((HG_HEADS, HG_VDIM, HG_KDIM), lambda i, s: (0, 0, 0))]),
        out_shape=[jax.ShapeDtypeStruct((N_META, ATT_WIDTH), F32),
                   jax.ShapeDtypeStruct((N_META, HG_WIDTH), F32),
                   jax.ShapeDtypeStruct((HG_HEADS, HG_VDIM, HG_KDIM), F32)],
        compiler_params=pltpu.CompilerParams(dimension_semantics=("arbitrary",)),
        name="meta_mix",
    )(sinks, q, kv, sga, hq, hk, hv, g, sgh, hg_nw, dmat)


def _sample_mix_kernel(q_ref, kv_ref, sga_ref, hq_ref, hk_ref, hv_ref, g_ref, sgh_ref, nw_ref,
                       sink_ref, seg_ref, exp_ref, ck_ref, cv_ref, mk_ref, mv_ref, s_ref,
                       att_ref, hg_ref, nk_ref, nv_ref, ns_ref):
    n_keys = WINDOW + N_META + SUBLANES
    key_row = lax.broadcasted_iota(jnp.int32, (n_keys, KV_WIDTH), 0)
    visible = jnp.logical_and(key_row >= 1, key_row <= WINDOW + N_META)
    win_row = lax.broadcasted_iota(jnp.int32, (WINDOW, KV_WIDTH), 0)
    sink = sink_ref[...]
    seg = seg_ref[...]
    expand = exp_ref[...]
    nw = nw_ref[...]

    def per_seq(b, carry):
        kv_new = kv_ref[b]
        k_new = kv_new[:, 0:KV_WIDTH]
        v_new = kv_new[:, KV_WIDTH:2 * KV_WIDTH]
        ck = ck_ref[b]
        cv = cv_ref[b]
        keys = jnp.concatenate([ck, mk_ref[b], jnp.broadcast_to(k_new, (SUBLANES, KV_WIDTH))], axis=0)
        vals = jnp.concatenate([cv, mv_ref[b], jnp.broadcast_to(v_new, (SUBLANES, KV_WIDTH))], axis=0)
        prod = jnp.concatenate([keys] * GQA_GROUP, axis=1) * q_ref[b]
        s = jnp.where(visible, _dot(prod.astype(BF16), seg), NEG_BIG)
        mx = jnp.maximum(jnp.max(s, axis=0, keepdims=True), sink)
        p = jnp.exp(s - mx)
        den = jnp.sum(p, axis=0, keepdims=True) + jnp.exp(sink - mx)
        p = p * (1.0 / den)
        wide = _dot(p.astype(BF16), expand)
        att = jnp.sum(wide * jnp.concatenate([vals] * GQA_GROUP, axis=1), axis=0, keepdims=True)
        att_ref[b] = att * sga_ref[b]

        nk_ref[b] = jnp.where(win_row == WINDOW - 1, k_new, pltpu.roll(ck, WINDOW - 1, 0))
        nv_ref[b] = jnp.where(win_row == WINDOW - 1, v_new, pltpu.roll(cv, WINDOW - 1, 0))

        g_row, hq_row, hk_row, hv_row, sgh_row = g_ref[b], hq_ref[b], hk_ref[b], hv_ref[b], sgh_ref[b]
        outs = []
        for h in range(HG_HEADS):
            lanes = slice(h * HG_KDIM, (h + 1) * HG_KDIM)
            col = lambda r: jnp.broadcast_to(r[:, lanes], (HG_KDIM, HG_KDIM)).T
            s1 = jnp.exp(col(g_row)) * s_ref[b, h] + col(hk_row) * hv_row[:, lanes]
            ns_ref[b, h] = s1
            o = jnp.sum(col(hq_row) * s1, axis=0, keepdims=True)
            outs.append(_head_norm_gate(o, nw, sgh_row[:, lanes]))
        hg_ref[b] = jnp.concatenate(outs, axis=-1)
        return carry

    lax.fori_loop(0, SAMPLE_TILE, per_seq, 0)


def _sample_mix(layer, q, kv, sga, hq, hk, hv, g, sgh, hg_nw, sink_row, seg, expand,
                cache_k, cache_v, meta_k, meta_v, state, n_seq):
    t = SAMPLE_TILE
    as_rows = lambda a: a[:n_seq].reshape(n_seq, 1, a.shape[-1])
    rows = lambda width: pl.BlockSpec((t, 1, width), lambda i: (i, 0, 0))
    full = lambda shape: pl.BlockSpec(shape, lambda i: (0,) * len(shape))
    cache = lambda n: pl.BlockSpec((t, None, n, KV_WIDTH), lambda i: (i, layer, 0, 0))
    state_in = pl.BlockSpec((t, None, HG_HEADS, HG_KDIM, HG_VDIM), lambda i: (i, layer, 0, 0, 0))
    att, hg, nk, nv, ns = pl.pallas_call(
        _sample_mix_kernel,
        grid=(n_seq // t,),
        in_specs=[rows(ATT_WIDTH), rows(2 * KV_WIDTH), rows(ATT_WIDTH), rows(HG_WIDTH), rows(HG_WIDTH),
                  rows(HG_WIDTH), rows(HG_WIDTH), rows(HG_WIDTH), full((1, HG_VDIM)),
                  full((1, KV_WIDTH)), full(seg.shape), full(expand.shape),
                  cache(WINDOW), cache(WINDOW), cache(N_META), cache(N_META), state_in],
        out_specs=[rows(ATT_WIDTH), rows(HG_WIDTH),
                   pl.BlockSpec((t, WINDOW, KV_WIDTH), lambda i: (i, 0, 0)),
                   pl.BlockSpec((t, WINDOW, KV_WIDTH), lambda i: (i, 0, 0)),
                   pl.BlockSpec((t, HG_HEADS, HG_KDIM, HG_VDIM), lambda i: (i, 0, 0, 0))],
        out_shape=[jax.ShapeDtypeStruct((n_seq, 1, ATT_WIDTH), F32),
                   jax.ShapeDtypeStruct((n_seq, 1, HG_WIDTH), F32),
                   jax.ShapeDtypeStruct((n_seq, WINDOW, KV_WIDTH), F32),
                   jax.ShapeDtypeStruct((n_seq, WINDOW, KV_WIDTH), F32),
                   jax.ShapeDtypeStruct((n_seq, HG_HEADS, HG_KDIM, HG_VDIM), F32)],
        compiler_params=pltpu.CompilerParams(dimension_semantics=("arbitrary",), vmem_limit_bytes=VMEM_LIMIT),
        name="sample_mix",
    )(*[as_rows(a) for a in (q, kv, sga, hq, hk, hv, g, sgh)], hg_nw, sink_row, seg, expand,
      cache_k, cache_v, meta_k, meta_v, state)
    return att.reshape(n_seq, ATT_WIDTH), hg.reshape(n_seq, HG_WIDTH), nk, nv, ns


def _g_major(a, axis):
    shape = a.shape
    a = a.reshape(shape[:axis] + (KV_HEADS, GQA_GROUP, HEAD_DIM) + shape[axis + 1:])
    a = jnp.swapaxes(a, axis, axis + 1)
    return a.reshape(shape)


def _constants():
    lane = np.arange(ATT_WIDTH)
    g_of, h_of = lane // KV_WIDTH, (lane % KV_WIDTH) // HEAD_DIM
    head = h_of * GQA_GROUP + g_of
    seg = (head[:, None] == np.arange(KV_WIDTH)[None, :]).astype(np.float32)
    grp = np.arange(256) // HEAD_DIM
    bd = (grp[:, None] == grp[None, :]).astype(np.float32) / HEAD_DIM
    return jnp.asarray(seg, BF16), jnp.asarray(seg.T, BF16), jnp.asarray(bd, BF16)


def kernel(x_prompt, x_sample, cache_win_k, cache_win_v, cache_meta_k, cache_meta_v, state_hgrn, meta_tokens,
           norm_w, w_in, q_norm_w, k_norm_w, attn_sinks, hg_lb_logits, hg_norm_w, w_out):
    batch, seq, _ = x_prompt.shape
    n_seq = x_sample.shape[0]
    depth = w_in.shape[0]
    w_buf = cache_win_k.shape[2]
    assert x_sample.shape[1] == 1 and w_buf == WINDOW and seq % ROW_TILE == 0 and n_seq % SAMPLE_TILE == 0

    seg, expand, bd = _constants()
    w_in_b = jnp.concatenate([_g_major(w_in[:, :, C_Q:C_K], 2), w_in[:, :, C_K:C_GA],
                              _g_major(w_in[:, :, C_GA:C_QH], 2), w_in[:, :, C_QH:]], axis=2).astype(BF16)
    w_out_b = jnp.concatenate([_g_major(w_out[:, :ATT_WIDTH], 1), w_out[:, ATT_WIDTH:]], axis=1).astype(BF16)
    qnw = jnp.tile(q_norm_w, (1, ATT_HEADS)) * (HEAD_DIM ** -0.5)
    knw = jnp.tile(k_norm_w, (1, KV_HEADS))
    lb_logits = hg_lb_logits.astype(F32)
    sink_rows = jnp.pad(attn_sinks.astype(F32), ((0, 0), (0, KV_WIDTH - ATT_HEADS)))

    ck = cache_win_k.reshape(n_seq, depth, w_buf, KV_WIDTH)
    cv = cache_win_v.reshape(n_seq, depth, w_buf, KV_WIDTH)
    mk = cache_meta_k.reshape(n_seq, depth, N_META, KV_WIDTH)
    mv = cache_meta_v.reshape(n_seq, depth, N_META, KV_WIDTH)

    xp = x_prompt.reshape(batch * seq, D_MODEL)
    xs = jnp.concatenate([x_sample.reshape(n_seq, D_MODEL), meta_tokens.astype(F32)], axis=0)
    n_small = xs.shape[0]

    outs = {k: [] for k in ("wkp", "wvp", "mk", "mv", "hsp", "wks", "wvs", "hss")}
    for l in range(depth):
        layer_w = (norm_w[l][None], w_in_b[l], qnw[l][None], knw[l][None], lb_logits, bd)
        sinks = attn_sinks[l].astype(F32)
        hg_nw = hg_norm_w[l][None]

        sm = _project(l, xs, *layer_w, row_tile=n_small, act_dtype=F32)
        att_m, hg_m, s0t = _meta_mix(sinks, *sm, hg_nw)
        att_s, hg_s, nk, nv, ns = _sample_mix(l, *sm, hg_nw, sink_rows[l][None], seg, expand,
                                              ck, cv, mk, mv, state_hgrn, n_seq)
        kv_meta = sm[1][n_seq:]
        xs = _out_project(jnp.concatenate([att_s, att_m], axis=0), jnp.concatenate([hg_s, hg_m], axis=0),
                          xs, w_out_b[l], n_small)

        q, kv, sga, hq, hk, hv, g, sgh = _project(l, xp, *layer_w, row_tile=ROW_TILE, act_dtype=BF16)
        att = _prompt_attention(sinks, q, kv, kv_meta, sga, batch, seq)
        hg, s_fin = _prompt_hgrn(hq, hk, hv, g, sgh, s0t, hg_nw, batch, seq)
        xp = _out_project(att, hg, xp, w_out_b[l], ROW_TILE)

        kv3 = kv.reshape(batch, seq, 2 * KV_WIDTH)
        outs["wkp"].append(kv3[:, seq - w_buf:, :KV_WIDTH])
        outs["wvp"].append(kv3[:, seq - w_buf:, KV_WIDTH:])
        outs["mk"].append(jnp.broadcast_to(kv_meta[None, :, :KV_WIDTH], (batch, N_META, KV_WIDTH)))
        outs["mv"].append(jnp.broadcast_to(kv_meta[None, :, KV_WIDTH:], (batch, N_META, KV_WIDTH)))
        outs["hsp"].append(s_fin)
        outs["wks"].append(nk)
        outs["wvs"].append(nv)
        outs["hss"].append(ns)

    stack = lambda name: jnp.stack(outs[name], axis=1)
    heads = lambda a: a.reshape(a.shape[:-1] + (KV_HEADS, HEAD_DIM))
    return (xp.reshape(batch, seq, D_MODEL), xs[:n_seq].reshape(n_seq, 1, D_MODEL),
            heads(stack("wkp")), heads(stack("wvp")), heads(stack("mk")), heads(stack("mv")), stack("hsp"),
            heads(stack("wks")), heads(stack("wvs")), stack("hss"))
```

```python
import functools

import numpy as np
import jax
import jax.numpy as jnp
from jax import lax
from jax.experimental import pallas as pl
from jax.experimental.pallas import tpu as pltpu

F32 = jnp.float32
BF16 = jnp.bfloat16

D_MODEL = 1024
N_META = 16
WINDOW = 128
HEAD_DIM = 64
ATT_WIDTH = 512
ATT_HEADS = 8
KV_HEADS = 2
GQA_GROUP = 4
KV_WIDTH = KV_HEADS * HEAD_DIM
HG_WIDTH = 512
HG_HEADS = 4
HG_KDIM = 128
HG_VDIM = 128
PROJ_WIDTH = 3328
EPS = 1e-6
NEG_BIG = -1e30
TINY = 1e-30

C_Q, C_K, C_V, C_GA, C_QH, C_FH, C_IH, C_GH = 0, 512, 640, 768, 1280, 1792, 2304, 2816

SUBLANES = 8
HG_CHUNK = 64
ROW_TILE = 512
HG_TILE = 256
SAMPLE_TILE = 16
VMEM_LIMIT = 48 * 1024 * 1024


def _dot(a, b):
    return jnp.dot(a, b, preferred_element_type=F32)


def _dot_nt(a, b):
    return lax.dot_general(a, b, (((1,), (1,)), ((), ())), preferred_element_type=F32)


def _dot_tn(a, b):
    return lax.dot_general(a, b, (((0,), (0,)), ((), ())), preferred_element_type=F32)


def _silu(x):
    return x * (1.0 / (1.0 + jnp.exp(-x)))


def _group_mean_sq(p, bd):
    sq = p * p
    hi = sq.astype(BF16)
    lo = (sq - hi.astype(F32)).astype(BF16)
    return _dot(hi, bd) + _dot(lo, bd)


def _proj_kernel(layer, x_ref, nw_ref, w_ref, qnw_ref, knw_ref, lbl_ref, bd_ref,
                 q_ref, kv_ref, sga_ref, hq_ref, hk_ref, hv_ref, g_ref, sgh_ref):
    x = x_ref[...]
    ms = jnp.mean(x * x, axis=-1, keepdims=True)
    h = (x * lax.rsqrt(ms + EPS) * nw_ref[...]).astype(BF16)

    def proj(lo, hi):
        return _dot(h, w_ref[:, lo:hi])

    bd = bd_ref[...]
    halves = []
    for c in range(2):
        pq = proj(C_Q + 256 * c, C_Q + 256 * (c + 1))
        halves.append(pq * lax.rsqrt(_group_mean_sq(pq, bd) + EPS))
    q = jnp.concatenate(halves, axis=-1) * qnw_ref[...]
    q_ref[...] = q.astype(q_ref.dtype)

    pk = proj(C_K, C_V)
    kn = pk * lax.rsqrt(_group_mean_sq(pk, bd[:KV_WIDTH, :KV_WIDTH]) + EPS) * knw_ref[...]
    kv_ref[:, 0:KV_WIDTH] = kn
    kv_ref[:, KV_WIDTH:2 * KV_WIDTH] = proj(C_V, C_GA)

    sga_ref[...] = _silu(proj(C_GA, C_QH)).astype(sga_ref.dtype)
    hq_ref[...] = _silu(proj(C_QH, C_FH)).astype(hq_ref.dtype)

    logits = lbl_ref[...]
    e = jnp.exp(logits - jnp.max(logits, axis=0, keepdims=True))
    p = e / jnp.sum(e, axis=0, keepdims=True)
    lb = jnp.zeros((1, HG_WIDTH), F32)
    for j in range(1, layer + 1):
        lb = lb + p[j:j + 1, :]

    z = proj(C_FH, C_IH)
    ez = jnp.exp(-jnp.abs(z))
    r = 1.0 / (1.0 + ez)
    pos = z >= 0.0
    sig_pos = jnp.where(pos, r, ez * r)
    sig_neg = jnp.where(pos, ez * r, r)
    hk_ref[...] = ((1.0 - lb) * sig_neg).astype(hk_ref.dtype)
    f = lb + (1.0 - lb) * sig_pos
    g_ref[...] = jnp.log(jnp.maximum(f, TINY))

    hv_ref[...] = proj(C_IH, C_GH).astype(hv_ref.dtype)
    sgh_ref[...] = _silu(proj(C_GH, PROJ_WIDTH)).astype(sgh_ref.dtype)


def _project(layer, x, norm_w, w_in, qnw, knw, lb_logits, bd, row_tile, act_dtype):
    n = x.shape[0]
    rows = lambda width: pl.BlockSpec((row_tile, width), lambda i: (i, 0))
    full = lambda shape: pl.BlockSpec(shape, lambda i: (0,) * len(shape))
    out = lambda width, dtype: jax.ShapeDtypeStruct((n, width), dtype)
    return pl.pallas_call(
        functools.partial(_proj_kernel, layer),
        grid=(n // row_tile,),
        in_specs=[rows(D_MODEL), full((1, D_MODEL)), full((D_MODEL, PROJ_WIDTH)), full((1, ATT_WIDTH)),
                  full((1, KV_WIDTH)), full(lb_logits.shape), full((256, 256))],
        out_specs=[rows(ATT_WIDTH), rows(2 * KV_WIDTH), rows(ATT_WIDTH), rows(HG_WIDTH), rows(HG_WIDTH),
                   rows(HG_WIDTH), rows(HG_WIDTH), rows(HG_WIDTH)],
        out_shape=[out(ATT_WIDTH, act_dtype), out(2 * KV_WIDTH, F32), out(ATT_WIDTH, act_dtype),
                   out(HG_WIDTH, act_dtype), out(HG_WIDTH, act_dtype), out(HG_WIDTH, act_dtype),
                   out(HG_WIDTH, F32), out(HG_WIDTH, act_dtype)],
        compiler_params=pltpu.CompilerParams(dimension_semantics=("arbitrary",), vmem_limit_bytes=VMEM_LIMIT),
        name="proj",
    )(x, norm_w, w_in, qnw, knw, lb_logits, bd)


def _out_kernel(att_ref, hg_ref, x_ref, w_ref, y_ref):
    y_ref[...] = (x_ref[...] + _dot(att_ref[...].astype(BF16), w_ref[0:ATT_WIDTH, :])
                  + _dot(hg_ref[...].astype(BF16), w_ref[ATT_WIDTH:, :]))


def _out_project(att, hg, x, w_out, row_tile):
    n = x.shape[0]
    rows = lambda width: pl.BlockSpec((row_tile, width), lambda i: (i, 0))
    return pl.pallas_call(
        _out_kernel,
        grid=(n // row_tile,),
        in_specs=[rows(ATT_WIDTH), rows(HG_WIDTH), rows(D_MODEL),
                  pl.BlockSpec((D_MODEL, D_MODEL), lambda i: (0, 0))],
        out_specs=rows(D_MODEL),
        out_shape=jax.ShapeDtypeStruct((n, D_MODEL), F32),
        compiler_params=pltpu.CompilerParams(dimension_semantics=("arbitrary",), vmem_limit_bytes=VMEM_LIMIT),
        name="out_proj",
    )(att, hg, x, w_out)


def _split_heads(k):
    first = lax.broadcasted_iota(jnp.int32, k.shape, 1) < HEAD_DIM
    return (jnp.where(first, k, 0.0).astype(BF16), jnp.where(first, 0.0, k).astype(BF16))


def _attend(q, key_sets, sink_of):
    m = q.shape[0]
    first = lax.broadcasted_iota(jnp.int32, (m, KV_WIDTH), 1) < HEAD_DIM
    blocks = []
    for g in range(GQA_GROUP):
        qg = q[:, g * KV_WIDTH:(g + 1) * KV_WIDTH]
        per_head = []
        for h in range(KV_HEADS):
            scores = []
            for k_heads, _, mask in key_sets:
                s = _dot_nt(qg, k_heads[h])
                if mask is not None:
                    s = jnp.where(mask, s, NEG_BIG)
                scores.append(s)
            sink = sink_of(h, g)
            mx = jnp.max(scores[0], axis=-1, keepdims=True)
            for s in scores[1:]:
                mx = jnp.maximum(mx, jnp.max(s, axis=-1, keepdims=True))
            mx = jnp.maximum(mx, sink)
            den = jnp.exp(sink - mx)
            acc = jnp.zeros((m, KV_WIDTH), F32)
            for s, (_, v, _) in zip(scores, key_sets):
                p = jnp.exp(s - mx)
                den = den + jnp.sum(p, axis=-1, keepdims=True)
                acc = acc + _dot(p.astype(BF16), v)
            per_head.append(acc * (1.0 / den))
        blocks.append(jnp.where(first, per_head[0], per_head[1]))
    return jnp.concatenate(blocks, axis=-1)


def _prompt_attn_kernel(sink_ref, q_ref, kvc_ref, kvp_ref, kvm_ref, sga_ref, o_ref):
    i = pl.program_id(1)
    kvp, kvc, kvm = kvp_ref[...], kvc_ref[...], kvm_ref[...]
    n_keys = 2 * WINDOW + N_META
    k_all = jnp.concatenate([kvp[:, :KV_WIDTH], kvc[:, :KV_WIDTH], kvm[:, :KV_WIDTH]], axis=0)
    kk = jnp.concatenate(_split_heads(k_all), axis=0)
    v_meta = jnp.concatenate([kvm[:, KV_WIDTH:], jnp.zeros((WINDOW - N_META, KV_WIDTH), F32)], axis=0)
    v_t = jnp.concatenate([kvp[:, KV_WIDTH:].T, kvc[:, KV_WIDTH:].T, v_meta.T], axis=1).astype(BF16)
    pad = jnp.zeros((3 * WINDOW - n_keys, WINDOW), BF16)

    key = lax.broadcasted_iota(jnp.int32, (n_keys, WINDOW), 0)
    qi = lax.broadcasted_iota(jnp.int32, (n_keys, WINDOW), 1)
    no_prev = jnp.where(i > 0, 0, 2 * WINDOW)
    visible = jnp.where(key < WINDOW, key - qi - 1 - no_prev,
                        jnp.where(key < 2 * WINDOW, qi - (key - WINDOW), 0)) >= 0
    head0_rows = lax.broadcasted_iota(jnp.int32, (KV_WIDTH, WINDOW), 0) < HEAD_DIM

    group_lanes = [slice(g * KV_WIDTH, (g + 1) * KV_WIDTH) for g in range(GQA_GROUP)]
    s_both = [_dot_nt(kk, q_ref[:, lanes]) for lanes in group_lanes]
    probs = []
    for g in range(GQA_GROUP):
        for h in range(KV_HEADS):
            sink = sink_ref[h * GQA_GROUP + g]
            s = jnp.where(visible, s_both[g][h * n_keys:(h + 1) * n_keys], NEG_BIG)
            mx = jnp.maximum(jnp.max(s, axis=0, keepdims=True), sink)
            p = jnp.exp(s - mx)
            den = jnp.sum(p, axis=0, keepdims=True) + jnp.exp(sink - mx)
            probs.append((jnp.concatenate([p.astype(BF16), pad], axis=0), 1.0 / den))
    outs = [_dot(v_t, p_pad) * inv for p_pad, inv in probs]
    for g, lanes in enumerate(group_lanes):
        o_t = jnp.where(head0_rows, outs[g * KV_HEADS], outs[g * KV_HEADS + 1])
        o_ref[:, lanes] = (o_t.T * sga_ref[:, lanes].astype(F32)).astype(o_ref.dtype)


def _prompt_attention(sinks, q, kv, kv_meta, sga, batch, seq):
    nb = seq // WINDOW
    blk = lambda width: pl.BlockSpec((WINDOW, width), lambda b, i, s: (b * nb + i, 0))
    prev = pl.BlockSpec((WINDOW, 2 * KV_WIDTH), lambda b, i, s: (b * nb + jnp.maximum(i - 1, 0), 0))
    meta = pl.BlockSpec((N_META, 2 * KV_WIDTH), lambda b, i, s: (0, 0))
    return pl.pallas_call(
        _prompt_attn_kernel,
        grid_spec=pltpu.PrefetchScalarGridSpec(
            num_scalar_prefetch=1, grid=(batch, nb),
            in_specs=[blk(ATT_WIDTH), blk(2 * KV_WIDTH), prev, meta, blk(ATT_WIDTH)],
            out_specs=blk(ATT_WIDTH)),
        out_shape=jax.ShapeDtypeStruct((batch * seq, ATT_WIDTH), BF16),
        compiler_params=pltpu.CompilerParams(dimension_semantics=("arbitrary", "arbitrary"),
                                             vmem_limit_bytes=VMEM_LIMIT),
        name="prompt_attn",
    )(sinks, q, kv, kv, kv_meta, sga)


def _level_sizes(c):
    return [c >> (i + 1) for i in range(c.bit_length() - 1)]


def _decay_sum_matrix(c):
    t = np.arange(c)[:, None]
    r = np.arange(c)[None, :]
    mats = [r <= t, r > t]
    for bs in _level_sizes(c):
        a = (t // (2 * bs)) * (2 * bs) + bs - 1
        mats.append(((r > t) & (r <= a)) | ((r > a) & (r <= t)))
    return np.tile(np.concatenate(mats, axis=0).astype(np.float32), (1, 3))


def _decay_exponents(g, dmat):
    g1 = g.astype(BF16)
    r1 = g - g1.astype(F32)
    g2 = r1.astype(BF16)
    g3 = (r1 - g2.astype(F32)).astype(BF16)
    return _dot(dmat, jnp.concatenate([g1, g2, g3], axis=0))


def _hgrn_masks(c):
    row = lax.broadcasted_iota(jnp.int32, (c, c), 0)
    col = lax.broadcasted_iota(jnp.int32, (c, c), 1)
    row_k = lax.broadcasted_iota(jnp.int32, (c, HG_KDIM), 0)
    levels = []
    for bs in _level_sizes(c):
        rb, cb = row // bs, col // bs
        pairs = ((rb % 2) * (1 - jnp.abs(cb - rb + 1))) > 0
        levels.append((bs, (row_k // bs) % 2 == 1, pairs))
    return levels, row == col


def _hgrn_chunks(items, states, masks):
    levels, diag = masks
    c = items[0][1].shape[0]

    stage1 = []
    for _, q, k, v, ex in items:
        pair_scores = []
        for i, (bs, q_side, _) in enumerate(levels):
            if bs >= SUBLANES:
                side = jnp.concatenate(
                    [(q if (j // bs) % 2 == 1 else k)[j:j + SUBLANES] for j in range(0, c, SUBLANES)], axis=0)
            else:
                side = jnp.where(q_side, q, k)
            u = (side * jnp.exp(ex(2 + i))).astype(BF16)
            pair_scores.append(_dot_nt(u, u))
        kdec = (k * jnp.exp(ex(1))).astype(BF16)
        stage1.append((pair_scores, _dot_tn(v.astype(BF16), kdec)))

    intra = []
    for (_, q, k, v, _), (pair_scores, _) in zip(items, stage1):
        a = jnp.where(diag, jnp.sum(q * k, axis=-1, keepdims=True), 0.0)
        for (_, _, pairs), scores in zip(levels, pair_scores):
            a = jnp.where(pairs, scores, a)
        intra.append(_dot(a.astype(BF16), v.astype(BF16)))

    outs = []
    for (head, q, _, _, ex), (_, increment), o_intra in zip(items, stage1, intra):
        st = states[head]
        outs.append(o_intra + _dot_nt((q * jnp.exp(ex(0))).astype(BF16), st.astype(BF16)))
        states[head] = st * jnp.exp(ex(0)[c - 1:c, :]) + increment
    return outs


def _head_norm_gate(o, nw, gate):
    ms = jnp.mean(o * o, axis=-1, keepdims=True)
    return o * lax.rsqrt(ms + EPS) * nw * gate


def _prompt_hgrn_kernel(hq_ref, hk_ref, hv_ref, g_ref, sgh_ref, s0_ref, nw_ref, dmat_ref, o_ref, sfin_ref, st_ref):
    j = pl.program_id(1)

    @pl.when(j == 0)
    def _():
        st_ref[...] = s0_ref[...]

    masks = _hgrn_masks(HG_CHUNK)
    nw = nw_ref[...]
    dmat = dmat_ref[...]

    def block_of(ex, lanes):
        return lambda i: ex[i * HG_CHUNK:(i + 1) * HG_CHUNK, lanes]

    items, where = [], []
    for ci in range(HG_TILE // HG_CHUNK):
        rows = slice(ci * HG_CHUNK, (ci + 1) * HG_CHUNK)
        ex = _decay_exponents(g_ref[rows, :], dmat)
        for h in range(HG_HEADS):
            lanes = slice(h * HG_KDIM, (h + 1) * HG_KDIM)
            items.append((h, hq_ref[rows, lanes].astype(F32), hk_ref[rows, lanes].astype(F32),
                          hv_ref[rows, lanes].astype(F32), block_of(ex, lanes)))
            where.append((rows, lanes))
    states = [st_ref[h] for h in range(HG_HEADS)]
    outs = _hgrn_chunks(items, states, masks)
    for (rows, lanes), o in zip(where, outs):
        o_ref[rows, lanes] = _head_norm_gate(o, nw, sgh_ref[rows, lanes].astype(F32)).astype(o_ref.dtype)
    for h in range(HG_HEADS):
        st_ref[h] = states[h]

    @pl.when(j == pl.num_programs(1) - 1)
    def _():
        for h in range(HG_HEADS):
            sfin_ref[h] = st_ref[h].T


def _prompt_hgrn(hq, hk, hv, g, sgh, s0t, hg_nw, batch, seq):
    nt = seq // HG_TILE
    blk = pl.BlockSpec((HG_TILE, HG_WIDTH), lambda b, j: (b * nt + j, 0))
    state_shape = (HG_HEADS, HG_VDIM, HG_KDIM)
    dmat = jnp.asarray(_decay_sum_matrix(HG_CHUNK), BF16)
    return pl.pallas_call(
        _prompt_hgrn_kernel,
        grid=(batch, nt),
        in_specs=[blk, blk, blk, blk, blk, pl.BlockSpec(state_shape, lambda b, j: (0, 0, 0)),
                  pl.BlockSpec((1, HG_VDIM), lambda b, j: (0, 0)), pl.BlockSpec(dmat.shape, lambda b, j: (0, 0))],
        out_specs=[blk, pl.BlockSpec((None,) + state_shape, lambda b, j: (b, 0, 0, 0))],
        out_shape=[jax.ShapeDtypeStruct((batch * seq, HG_WIDTH), BF16),
                   jax.ShapeDtypeStruct((batch,) + state_shape, F32)],
        scratch_shapes=[pltpu.VMEM(state_shape, F32)],
        compiler_params=pltpu.CompilerParams(dimension_semantics=("arbitrary", "arbitrary"),
                                             vmem_limit_bytes=VMEM_LIMIT),
        name="prompt_hgrn",
    )(hq, hk, hv, g, sgh, s0t, hg_nw, dmat)


def _meta_mix_kernel(sink_ref, q_ref, kv_ref, sga_ref, hq_ref, hk_ref, hv_ref, g_ref, sgh_ref, nw_ref, dmat_ref,
                     att_ref, hg_ref, st_ref):
    kv = kv_ref[...]
    row = lax.broadcasted_iota(jnp.int32, (N_META, N_META), 0)
    col = lax.broadcasted_iota(jnp.int32, (N_META, N_META), 1)
    att = _attend(q_ref[...].astype(BF16),
                  [(_split_heads(kv[:, :KV_WIDTH]), kv[:, KV_WIDTH:].astype(BF16), col <= row)],
                  lambda h, g: sink_ref[h * GQA_GROUP + g])
    att_ref[...] = (att * sga_ref[...]).astype(att_ref.dtype)

    ex = _decay_exponents(g_ref[...], dmat_ref[...])
    masks = _hgrn_masks(N_META)
    head_lanes = [slice(h * HG_KDIM, (h + 1) * HG_KDIM) for h in range(HG_HEADS)]
    items = [(h, hq_ref[:, lanes], hk_ref[:, lanes], hv_ref[:, lanes],
              (lambda lanes: lambda i: ex[i * N_META:(i + 1) * N_META, lanes])(lanes))
             for h, lanes in enumerate(head_lanes)]
    states = [jnp.zeros((HG_VDIM, HG_KDIM), F32) for _ in range(HG_HEADS)]
    outs = _hgrn_chunks(items, states, masks)
    for h, lanes in enumerate(head_lanes):
        st_ref[h] = states[h]
        hg_ref[:, lanes] = _head_norm_gate(outs[h], nw_ref[...], sgh_ref[:, lanes]).astype(hg_ref.dtype)


def _meta_mix(sinks, q, kv, sga, hq, hk, hv, g, sgh, hg_nw):
    blk_idx = q.shape[0] // N_META - 1
    blk = lambda width: pl.BlockSpec((N_META, width), lambda i, s: (blk_idx, 0))
    first = lambda width: pl.BlockSpec((N_META, width), lambda i, s: (0, 0))
    dmat = jnp.asarray(_decay_sum_matrix(N_META), BF16)
    return pl.pallas_call(
        _meta_mix_kernel,
        grid_spec=pltpu.PrefetchScalarGridSpec(
            num_scalar_prefetch=1, grid=(1,),
            in_specs=[blk(ATT_WIDTH), blk(2 * KV_WIDTH), blk(ATT_WIDTH), blk(HG_WIDTH), blk(HG_WIDTH),
                      blk(HG_WIDTH), blk(HG_WIDTH), blk(HG_WIDTH), pl.BlockSpec((1, HG_VDIM), lambda i, s: (0, 0)),
                      pl.BlockSpec(dmat.shape, lambda i, s: (0, 0))],
            out_specs=[first(ATT_WIDTH), first(HG_WIDTH),
                       pl.BlockSpec((HG_HEADS, HG_VDIM, HG_KDIM), lambda i, s: (0, 0, 0))]),
        out_shape=[jax.ShapeDtypeStruct((N_META, ATT_WIDTH), F32),
                   jax.ShapeDtypeStruct((N_META, HG_WIDTH), F32),
                   jax.ShapeDtypeStruct((HG_HEADS, HG_VDIM, HG_KDIM), F32)],
        compiler_params=pltpu.CompilerParams(dimension_semantics=("arbitrary",)),
        name="meta_mix",
    )(sinks, q, kv, sga, hq, hk, hv, g, sgh, hg_nw, dmat)


def _sample_mix_kernel(q_ref, kv_ref, sga_ref, hq_ref, hk_ref, hv_ref, g_ref, sgh_ref, nw_ref,
                       sink_ref, seg_ref, exp_ref, ck_ref, cv_ref, mk_ref, mv_ref, s_ref,
                       att_ref, hg_ref, nk_ref, nv_ref, ns_ref):
    n_keys = WINDOW + N_META + SUBLANES
    key_row = lax.broadcasted_iota(jnp.int32, (n_keys, KV_WIDTH), 0)
    visible = jnp.logical_and(key_row >= 1, key_row <= WINDOW + N_META)
    win_row = lax.broadcasted_iota(jnp.int32, (WINDOW, KV_WIDTH), 0)
    sink = sink_ref[...]
    seg = seg_ref[...]
    expand = exp_ref[...]
    nw = nw_ref[...]

    def per_seq(b, carry):
        kv_new = kv_ref[b]
        k_new = kv_new[:, 0:KV_WIDTH]
        v_new = kv_new[:, KV_WIDTH:2 * KV_WIDTH]
        ck = ck_ref[b]
        cv = cv_ref[b]
        keys = jnp.concatenate([ck, mk_ref[b], jnp.broadcast_to(k_new, (SUBLANES, KV_WIDTH))], axis=0)
        vals = jnp.concatenate([cv, mv_ref[b], jnp.broadcast_to(v_new, (SUBLANES, KV_WIDTH))], axis=0)
        prod = jnp.concatenate([keys] * GQA_GROUP, axis=1) * q_ref[b]
        s = jnp.where(visible, _dot(prod.astype(BF16), seg), NEG_BIG)
        mx = jnp.maximum(jnp.max(s, axis=0, keepdims=True), sink)
        p = jnp.exp(s - mx)
        den = jnp.sum(p, axis=0, keepdims=True) + jnp.exp(sink - mx)
        p = p * (1.0 / den)
        wide = _dot(p.astype(BF16), expand)
        att = jnp.sum(wide * jnp.concatenate([vals] * GQA_GROUP, axis=1), axis=0, keepdims=True)
        att_ref[b] = att * sga_ref[b]

        nk_ref[b] = jnp.where(win_row == WINDOW - 1, k_new, pltpu.roll(ck, WINDOW - 1, 0))
        nv_ref[b] = jnp.where(win_row == WINDOW - 1, v_new, pltpu.roll(cv, WINDOW - 1, 0))

        g_row, hq_row, hk_row, hv_row, sgh_row = g_ref[b], hq_ref[b], hk_ref[b], hv_ref[b], sgh_ref[b]
        outs = []
        for h in range(HG_HEADS):
            lanes = slice(h * HG_KDIM, (h + 1) * HG_KDIM)
            col = lambda r: jnp.broadcast_to(r[:, lanes], (HG_KDIM, HG_KDIM)).T
            s1 = jnp.exp(col(g_row)) * s_ref[b, h] + col(hk_row) * hv_row[:, lanes]
            ns_ref[b, h] = s1
            o = jnp.sum(col(hq_row) * s1, axis=0, keepdims=True)
            outs.append(_head_norm_gate(o, nw, sgh_row[:, lanes]))
        hg_ref[b] = jnp.concatenate(outs, axis=-1)
        return carry

    lax.fori_loop(0, SAMPLE_TILE, per_seq, 0)


def _sample_mix(layer, q, kv, sga, hq, hk, hv, g, sgh, hg_nw, sink_row, seg, expand,
                cache_k, cache_v, meta_k, meta_v, state, n_seq, stacked):
    t = SAMPLE_TILE
    depth = state.shape[1]
    as_rows = lambda a: a[:n_seq].reshape(n_seq, 1, a.shape[-1])
    rows = lambda width: pl.BlockSpec((t, 1, width), lambda i: (i, 0, 0))
    full = lambda shape: pl.BlockSpec(shape, lambda i: (0,) * len(shape))
    cache = lambda n: pl.BlockSpec((t, None, n, KV_WIDTH), lambda i: (i, layer, 0, 0))
    state_blk = pl.BlockSpec((t, None, HG_HEADS, HG_KDIM, HG_VDIM), lambda i: (i, layer, 0, 0, 0))
    operands = [as_rows(a) for a in (q, kv, sga, hq, hk, hv, g, sgh)] + [
        hg_nw, sink_row, seg, expand, cache_k, cache_v, meta_k, meta_v, state]
    in_specs = [rows(ATT_WIDTH), rows(2 * KV_WIDTH), rows(ATT_WIDTH), rows(HG_WIDTH), rows(HG_WIDTH),
                rows(HG_WIDTH), rows(HG_WIDTH), rows(HG_WIDTH), full((1, HG_VDIM)),
                full((1, KV_WIDTH)), full(seg.shape), full(expand.shape),
                cache(WINDOW), cache(WINDOW), cache(N_META), cache(N_META), state_blk]
    aliases = {}
    kernel_fn = _sample_mix_kernel
    if stacked is not None:
        aliases = {len(operands) + i: 2 + i for i in range(3)}
        operands = operands + list(stacked)
        in_specs = in_specs + [pl.BlockSpec(memory_space=pl.ANY)] * 3
        kernel_fn = _sample_mix_kernel_aliased
    att, hg, nk, nv, ns = pl.pallas_call(
        kernel_fn,
        grid=(n_seq // t,),
        in_specs=in_specs,
        out_specs=[rows(ATT_WIDTH), rows(HG_WIDTH), cache(WINDOW), cache(WINDOW), state_blk],
        out_shape=[jax.ShapeDtypeStruct((n_seq, 1, ATT_WIDTH), F32),
                   jax.ShapeDtypeStruct((n_seq, 1, HG_WIDTH), F32),
                   jax.ShapeDtypeStruct((n_seq, depth, WINDOW, KV_WIDTH), F32),
                   jax.ShapeDtypeStruct((n_seq, depth, WINDOW, KV_WIDTH), F32),
                   jax.ShapeDtypeStruct((n_seq, depth, HG_HEADS, HG_KDIM, HG_VDIM), F32)],
        input_output_aliases=aliases,
        compiler_params=pltpu.CompilerParams(dimension_semantics=("arbitrary",), vmem_limit_bytes=VMEM_LIMIT),
        name="sample_mix",
    )(*operands)
    return att.reshape(n_seq, ATT_WIDTH), hg.reshape(n_seq, HG_WIDTH), (nk, nv, ns)


def _sample_mix_kernel_aliased(*refs):
    n_in = 17
    _sample_mix_kernel(*refs[:n_in], *refs[n_in + 3:])


def _g_major(a, axis):
    shape = a.shape
    a = a.reshape(shape[:axis] + (KV_HEADS, GQA_GROUP, HEAD_DIM) + shape[axis + 1:])
    a = jnp.swapaxes(a, axis, axis + 1)
    return a.reshape(shape)


def _constants():
    lane = np.arange(ATT_WIDTH)
    g_of, h_of = lane // KV_WIDTH, (lane % KV_WIDTH) // HEAD_DIM
    head = h_of * GQA_GROUP + g_of
    seg = (head[:, None] == np.arange(KV_WIDTH)[None, :]).astype(np.float32)
    grp = np.arange(256) // HEAD_DIM
    bd = (grp[:, None] == grp[None, :]).astype(np.float32) / HEAD_DIM
    return jnp.asarray(seg, BF16), jnp.asarray(seg.T, BF16), jnp.asarray(bd, BF16)


def kernel(x_prompt, x_sample, cache_win_k, cache_win_v, cache_meta_k, cache_meta_v, state_hgrn, meta_tokens,
           norm_w, w_in, q_norm_w, k_norm_w, attn_sinks, hg_lb_logits, hg_norm_w, w_out):
    batch, seq, _ = x_prompt.shape
    n_seq = x_sample.shape[0]
    depth = w_in.shape[0]
    w_buf = cache_win_k.shape[2]
    assert x_sample.shape[1] == 1 and w_buf == WINDOW and seq % ROW_TILE == 0 and n_seq % SAMPLE_TILE == 0

    seg, expand, bd = _constants()
    w_in_b = jnp.concatenate([_g_major(w_in[:, :, C_Q:C_K], 2), w_in[:, :, C_K:C_GA],
                              _g_major(w_in[:, :, C_GA:C_QH], 2), w_in[:, :, C_QH:]], axis=2).astype(BF16)
    w_out_b = jnp.concatenate([_g_major(w_out[:, :ATT_WIDTH], 1), w_out[:, ATT_WIDTH:]], axis=1).astype(BF16)
    qnw = jnp.tile(q_norm_w, (1, ATT_HEADS)) * (HEAD_DIM ** -0.5)
    knw = jnp.tile(k_norm_w, (1, KV_HEADS))
    lb_logits = hg_lb_logits.astype(F32)
    sink_rows = jnp.pad(attn_sinks.astype(F32), ((0, 0), (0, KV_WIDTH - ATT_HEADS)))

    ck = cache_win_k.reshape(n_seq, depth, w_buf, KV_WIDTH)
    cv = cache_win_v.reshape(n_seq, depth, w_buf, KV_WIDTH)
    mk = cache_meta_k.reshape(n_seq, depth, N_META, KV_WIDTH)
    mv = cache_meta_v.reshape(n_seq, depth, N_META, KV_WIDTH)

    xp = x_prompt.reshape(batch * seq, D_MODEL)
    xs = jnp.concatenate([x_sample.reshape(n_seq, D_MODEL), meta_tokens.astype(F32)], axis=0)
    n_small = xs.shape[0]

    outs = {k: [] for k in ("wkp", "wvp", "mk", "mv", "hsp")}
    sample_stacked = None
    for l in range(depth):
        layer_w = (norm_w[l][None], w_in_b[l], qnw[l][None], knw[l][None], lb_logits, bd)
        sinks = attn_sinks[l].astype(F32)
        hg_nw = hg_norm_w[l][None]

        sm = _project(l, xs, *layer_w, row_tile=n_small, act_dtype=F32)
        att_m, hg_m, s0t = _meta_mix(sinks, *sm, hg_nw)
        att_s, hg_s, sample_stacked = _sample_mix(l, *sm, hg_nw, sink_rows[l][None], seg, expand,
                                                  ck, cv, mk, mv, state_hgrn, n_seq, sample_stacked)
        kv_meta = sm[1][n_seq:]
        xs = _out_project(jnp.concatenate([att_s, att_m], axis=0), jnp.concatenate([hg_s, hg_m], axis=0),
                          xs, w_out_b[l], n_small)

        q, kv, sga, hq, hk, hv, g, sgh = _project(l, xp, *layer_w, row_tile=ROW_TILE, act_dtype=BF16)
        att = _prompt_attention(sinks, q, kv, kv_meta, sga, batch, seq)
        hg, s_fin = _prompt_hgrn(hq, hk, hv, g, sgh, s0t, hg_nw, batch, seq)
        xp = _out_project(att, hg, xp, w_out_b[l], ROW_TILE)

        kv3 = kv.reshape(batch, seq, 2 * KV_WIDTH)
        outs["wkp"].append(kv3[:, seq - w_buf:, :KV_WIDTH])
        outs["wvp"].append(kv3[:, seq - w_buf:, KV_WIDTH:])
        outs["mk"].append(jnp.broadcast_to(kv_meta[None, :, :KV_WIDTH], (batch, N_META, KV_WIDTH)))
        outs["mv"].append(jnp.broadcast_to(kv_meta[None, :, KV_WIDTH:], (batch, N_META, KV_WIDTH)))
        outs["hsp"].append(s_fin)

    stack = lambda name: jnp.stack(outs[name], axis=1)
    heads = lambda a: a.reshape(a.shape[:-1] + (KV_HEADS, HEAD_DIM))
    new_k, new_v, new_state = sample_stacked
    return (xp.reshape(batch, seq, D_MODEL), xs[:n_seq].reshape(n_seq, 1, D_MODEL),
            heads(stack("wkp")), heads(stack("wvp")), heads(stack("mk")), heads(stack("mv")), stack("hsp"),
            heads(new_k), heads(new_v), new_state)
```

```python
import functools

import numpy as np
import jax
import jax.numpy as jnp
from jax import lax
from jax.experimental import pallas as pl
from jax.experimental.pallas import tpu as pltpu

F32 = jnp.float32
BF16 = jnp.bfloat16

D_MODEL = 1024
N_META = 16
WINDOW = 128
HEAD_DIM = 64
ATT_WIDTH = 512
ATT_HEADS = 8
KV_HEADS = 2
GQA_GROUP = 4
KV_WIDTH = KV_HEADS * HEAD_DIM
HG_WIDTH = 512
HG_HEADS = 4
HG_KDIM = 128
HG_VDIM = 128
PROJ_WIDTH = 3328
EPS = 1e-6
NEG_BIG = -1e30
TINY = 1e-30

C_Q, C_K, C_V, C_GA, C_QH, C_FH, C_IH, C_GH = 0, 512, 640, 768, 1280, 1792, 2304, 2816

SUBLANES = 8
HG_CHUNK = 64
ROW_TILE = 512
HG_TILE = 256
SAMPLE_TILE = 16
VMEM_LIMIT = 48 * 1024 * 1024


def _dot(a, b):
    return jnp.dot(a, b, preferred_element_type=F32)


def _dot_nt(a, b):
    return lax.dot_general(a, b, (((1,), (1,)), ((), ())), preferred_element_type=F32)


def _dot_tn(a, b):
    return lax.dot_general(a, b, (((0,), (0,)), ((), ())), preferred_element_type=F32)


def _silu(x):
    return x * (1.0 / (1.0 + jnp.exp(-x)))


def _row_parts(n_rows, n_parts):
    step = n_rows // n_parts
    return [slice(i * step, (i + 1) * step) for i in range(n_parts)]


def _mix_proj_kernel(layer, n_parts, att_ref, hg_ref, wo_ref, x_ref, *rest):
    parts = _row_parts(x_ref.shape[0], n_parts)
    xnew_ref = rest[6]
    xs = []
    for rs in parts:
        x = (x_ref[rs, :] + _dot(att_ref[rs, :].astype(BF16), wo_ref[0:ATT_WIDTH, :])
             + _dot(hg_ref[rs, :].astype(BF16), wo_ref[ATT_WIDTH:, :]))
        xnew_ref[rs, :] = x
        xs.append(x)
    _proj_body(layer, parts, xs, *rest[:6], *rest[7:])


def _proj_kernel(layer, n_parts, x_ref, *rest):
    parts = _row_parts(x_ref.shape[0], n_parts)
    _proj_body(layer, parts, [x_ref[rs, :] for rs in parts], *rest)


def _proj_body(layer, parts, xs, nw_ref, w_ref, qnw_ref, knw_ref, lbl_ref, bd_ref,
               q_ref, kv_ref, sga_ref, hq_ref, hk_ref, hv_ref, g_ref, sgh_ref):
    nw = nw_ref[...]
    hs = []
    for x in xs:
        ms = jnp.mean(x * x, axis=-1, keepdims=True)
        hs.append((x * lax.rsqrt(ms + EPS) * nw).astype(BF16))

    def proj(h, lo, hi):
        return _dot(h, w_ref[:, lo:hi])

    pq = [[proj(h, C_Q + 256 * c, C_Q + 256 * (c + 1)) for c in range(2)] for h in hs]
    pk = [proj(h, C_K, C_V) for h in hs]
    sq = [[(p * p).astype(BF16) for p in pqs + [pks]] for pqs, pks in zip(pq, pk)]

    logits = lbl_ref[...]
    e = jnp.exp(logits - jnp.max(logits, axis=0, keepdims=True))
    p = e / jnp.sum(e, axis=0, keepdims=True)
    lb = jnp.zeros((1, HG_WIDTH), F32)
    for j in range(1, layer + 1):
        lb = lb + p[j:j + 1, :]

    for rs, h in zip(parts, hs):
        kv_ref[rs, KV_WIDTH:2 * KV_WIDTH] = proj(h, C_V, C_GA)
        sga_ref[rs, :] = _silu(proj(h, C_GA, C_QH)).astype(sga_ref.dtype)
        hq_ref[rs, :] = _silu(proj(h, C_QH, C_FH)).astype(hq_ref.dtype)

        z = proj(h, C_FH, C_IH)
        ez = jnp.exp(-jnp.abs(z))
        r = 1.0 / (1.0 + ez)
        pos = z >= 0.0
        sig_pos = jnp.where(pos, r, ez * r)
        sig_neg = jnp.where(pos, ez * r, r)
        hk_ref[rs, :] = ((1.0 - lb) * sig_neg).astype(hk_ref.dtype)
        f = lb + (1.0 - lb) * sig_pos
        g_ref[rs, :] = jnp.log(jnp.maximum(f, TINY))

        hv_ref[rs, :] = proj(h, C_IH, C_GH).astype(hv_ref.dtype)
        sgh_ref[rs, :] = _silu(proj(h, C_GH, PROJ_WIDTH)).astype(sgh_ref.dtype)

    bd = bd_ref[...]
    for rs, pqs, pks, sqs in zip(parts, pq, pk, sq):
        q = jnp.concatenate([p * lax.rsqrt(_dot(s, bd) + EPS) for p, s in zip(pqs, sqs[:2])], axis=-1)
        q_ref[rs, :] = (q * qnw_ref[...]).astype(q_ref.dtype)
        kv_ref[rs, 0:KV_WIDTH] = pks * lax.rsqrt(_dot(sqs[2], bd[:KV_WIDTH, :KV_WIDTH]) + EPS) * knw_ref[...]


def _project(layer, x, norm_w, w_in, qnw, knw, lb_logits, bd, row_tile, act_dtype, mix=None):
    n = x.shape[0]
    rows = lambda width: pl.BlockSpec((row_tile, width), lambda i: (i, 0))
    full = lambda shape: pl.BlockSpec(shape, lambda i: (0,) * len(shape))
    out = lambda width, dtype: jax.ShapeDtypeStruct((n, width), dtype)
    operands = [x, norm_w, w_in, qnw, knw, lb_logits, bd]
    in_specs = [rows(D_MODEL), full((1, D_MODEL)), full((D_MODEL, PROJ_WIDTH)), full((1, ATT_WIDTH)),
                full((1, KV_WIDTH)), full(lb_logits.shape), full((256, 256))]
    out_specs = [rows(ATT_WIDTH), rows(2 * KV_WIDTH), rows(ATT_WIDTH), rows(HG_WIDTH), rows(HG_WIDTH),
                 rows(HG_WIDTH), rows(HG_WIDTH), rows(HG_WIDTH)]
    out_shape = [out(ATT_WIDTH, act_dtype), out(2 * KV_WIDTH, F32), out(ATT_WIDTH, act_dtype),
                 out(HG_WIDTH, act_dtype), out(HG_WIDTH, act_dtype), out(HG_WIDTH, act_dtype),
                 out(HG_WIDTH, F32), out(HG_WIDTH, act_dtype)]
    body = _proj_kernel
    if mix is not None:
        operands = list(mix) + operands
        in_specs = [rows(ATT_WIDTH), rows(HG_WIDTH), full((D_MODEL, D_MODEL))] + in_specs
        out_specs = [rows(D_MODEL)] + out_specs
        out_shape = [out(D_MODEL, F32)] + out_shape
        body = _mix_proj_kernel
    n_parts = 2 if row_tile % (2 * 128) == 0 else 1
    return pl.pallas_call(
        functools.partial(body, layer, n_parts),
        grid=(n // row_tile,),
        in_specs=in_specs, out_specs=out_specs, out_shape=out_shape,
        compiler_params=pltpu.CompilerParams(dimension_semantics=("arbitrary",), vmem_limit_bytes=VMEM_LIMIT),
        name="proj",
    )(*operands)


def _out_kernel(att_ref, hg_ref, x_ref, w_ref, y_ref):
    y_ref[...] = (x_ref[...] + _dot(att_ref[...].astype(BF16), w_ref[0:ATT_WIDTH, :])
                  + _dot(hg_ref[...].astype(BF16), w_ref[ATT_WIDTH:, :]))


def _out_project(att, hg, x, w_out, row_tile):
    n = x.shape[0]
    rows = lambda width: pl.BlockSpec((row_tile, width), lambda i: (i, 0))
    return pl.pallas_call(
        _out_kernel,
        grid=(n // row_tile,),
        in_specs=[rows(ATT_WIDTH), rows(HG_WIDTH), rows(D_MODEL),
                  pl.BlockSpec((D_MODEL, D_MODEL), lambda i: (0, 0))],
        out_specs=rows(D_MODEL),
        out_shape=jax.ShapeDtypeStruct((n, D_MODEL), F32),
        compiler_params=pltpu.CompilerParams(dimension_semantics=("arbitrary",), vmem_limit_bytes=VMEM_LIMIT),
        name="out_proj",
    )(att, hg, x, w_out)


def _split_heads(k):
    first = lax.broadcasted_iota(jnp.int32, k.shape, 1) < HEAD_DIM
    return (jnp.where(first, k, 0.0).astype(BF16), jnp.where(first, 0.0, k).astype(BF16))


def _attend(q, key_sets, sink_of):
    m = q.shape[0]
    first = lax.broadcasted_iota(jnp.int32, (m, KV_WIDTH), 1) < HEAD_DIM
    blocks = []
    for g in range(GQA_GROUP):
        qg = q[:, g * KV_WIDTH:(g + 1) * KV_WIDTH]
        per_head = []
        for h in range(KV_HEADS):
            scores = []
            for k_heads, _, mask in key_sets:
                s = _dot_nt(qg, k_heads[h])
                if mask is not None:
                    s = jnp.where(mask, s, NEG_BIG)
                scores.append(s)
            sink = sink_of(h, g)
            mx = jnp.max(scores[0], axis=-1, keepdims=True)
            for s in scores[1:]:
                mx = jnp.maximum(mx, jnp.max(s, axis=-1, keepdims=True))
            mx = jnp.maximum(mx, sink)
            den = jnp.exp(sink - mx)
            acc = jnp.zeros((m, KV_WIDTH), F32)
            for s, (_, v, _) in zip(scores, key_sets):
                p = jnp.exp(s - mx)
                den = den + jnp.sum(p, axis=-1, keepdims=True)
                acc = acc + _dot(p.astype(BF16), v)
            per_head.append(acc * (1.0 / den))
        blocks.append(jnp.where(first, per_head[0], per_head[1]))
    return jnp.concatenate(blocks, axis=-1)


def _prompt_attn_kernel(sink_ref, q_ref, kvc_ref, kvp_ref, kvm_ref, sga_ref, o_ref):
    i = pl.program_id(1)
    kvp, kvc, kvm = kvp_ref[...], kvc_ref[...], kvm_ref[...]
    n_keys = 2 * WINDOW + N_META
    k_all = jnp.concatenate([kvp[:, :KV_WIDTH], kvc[:, :KV_WIDTH], kvm[:, :KV_WIDTH]], axis=0)
    kk = jnp.concatenate(_split_heads(k_all), axis=0)
    v_meta = jnp.concatenate([kvm[:, KV_WIDTH:], jnp.zeros((WINDOW - N_META, KV_WIDTH), F32)], axis=0)
    v_t = jnp.concatenate([kvp[:, KV_WIDTH:].T, kvc[:, KV_WIDTH:].T, v_meta.T], axis=1).astype(BF16)
    pad = jnp.zeros((3 * WINDOW - n_keys, WINDOW), BF16)

    key = lax.broadcasted_iota(jnp.int32, (n_keys, WINDOW), 0)
    qi = lax.broadcasted_iota(jnp.int32, (n_keys, WINDOW), 1)
    no_prev = jnp.where(i > 0, 0, 2 * WINDOW)
    visible = jnp.where(key < WINDOW, key - qi - 1 - no_prev,
                        jnp.where(key < 2 * WINDOW, qi - (key - WINDOW), 0)) >= 0
    head0_rows = lax.broadcasted_iota(jnp.int32, (KV_WIDTH, WINDOW), 0) < HEAD_DIM

    group_lanes = [slice(g * KV_WIDTH, (g + 1) * KV_WIDTH) for g in range(GQA_GROUP)]
    s_both = [_dot_nt(kk, q_ref[:, lanes]) for lanes in group_lanes]
    probs = []
    for g in range(GQA_GROUP):
        for h in range(KV_HEADS):
            sink = sink_ref[h * GQA_GROUP + g]
            s = jnp.where(visible, s_both[g][h * n_keys:(h + 1) * n_keys], NEG_BIG)
            mx = jnp.maximum(jnp.max(s, axis=0, keepdims=True), sink)
            p = jnp.exp(s - mx)
            den = jnp.sum(p, axis=0, keepdims=True) + jnp.exp(sink - mx)
            probs.append((jnp.concatenate([p.astype(BF16), pad], axis=0), 1.0 / den))
    outs = [_dot(v_t, p_pad) * inv for p_pad, inv in probs]
    for g, lanes in enumerate(group_lanes):
        o_t = jnp.where(head0_rows, outs[g * KV_HEADS], outs[g * KV_HEADS + 1])
        o_ref[:, lanes] = (o_t.T * sga_ref[:, lanes].astype(F32)).astype(o_ref.dtype)


def _prompt_attention(sinks, q, kv, kv_meta, sga, batch, seq):
    nb = seq // WINDOW
    blk = lambda width: pl.BlockSpec((WINDOW, width), lambda b, i, s: (b * nb + i, 0))
    prev = pl.BlockSpec((WINDOW, 2 * KV_WIDTH), lambda b, i, s: (b * nb + jnp.maximum(i - 1, 0), 0))
    meta = pl.BlockSpec((N_META, 2 * KV_WIDTH), lambda b, i, s: (0, 0))
    return pl.pallas_call(
        _prompt_attn_kernel,
        grid_spec=pltpu.PrefetchScalarGridSpec(
            num_scalar_prefetch=1, grid=(batch, nb),
            in_specs=[blk(ATT_WIDTH), blk(2 * KV_WIDTH), prev, meta, blk(ATT_WIDTH)],
            out_specs=blk(ATT_WIDTH)),
        out_shape=jax.ShapeDtypeStruct((batch * seq, ATT_WIDTH), BF16),
        compiler_params=pltpu.CompilerParams(dimension_semantics=("arbitrary", "arbitrary"),
                                             vmem_limit_bytes=VMEM_LIMIT),
        name="prompt_attn",
    )(sinks, q, kv, kv, kv_meta, sga)


def _level_sizes(c):
    return [c >> (i + 1) for i in range(c.bit_length() - 1)]


def _decay_sum_matrix(c):
    t = np.arange(c)[:, None]
    r = np.arange(c)[None, :]
    mats = [r <= t, r > t]
    for bs in _level_sizes(c):
        a = (t // (2 * bs)) * (2 * bs) + bs - 1
        mats.append(((r > t) & (r <= a)) | ((r > a) & (r <= t)))
    return np.tile(np.concatenate(mats, axis=0).astype(np.float32), (1, 3))


def _decay_exponents(g, dmat):
    g1 = g.astype(BF16)
    r1 = g - g1.astype(F32)
    g2 = r1.astype(BF16)
    g3 = (r1 - g2.astype(F32)).astype(BF16)
    return _dot(dmat, jnp.concatenate([g1, g2, g3], axis=0))


def _hgrn_masks(c):
    row = lax.broadcasted_iota(jnp.int32, (c, c), 0)
    col = lax.broadcasted_iota(jnp.int32, (c, c), 1)
    row_k = lax.broadcasted_iota(jnp.int32, (c, HG_KDIM), 0)
    levels = []
    for bs in _level_sizes(c):
        rb, cb = row // bs, col // bs
        pairs = ((rb % 2) * (1 - jnp.abs(cb - rb + 1))) > 0
        levels.append((bs, (row_k // bs) % 2 == 1, pairs))
    return levels, row == col


def _hgrn_chunks(items, states, masks):
    levels, diag = masks
    c = items[0][1].shape[0]

    stage1 = []
    for _, q, k, v, ex in items:
        pair_scores = []
        for i, (bs, q_side, _) in enumerate(levels):
            if bs >= SUBLANES:
                side = jnp.concatenate(
                    [(q if (j // bs) % 2 == 1 else k)[j:j + SUBLANES] for j in range(0, c, SUBLANES)], axis=0)
            else:
                side = jnp.where(q_side, q, k)
            u = (side * jnp.exp(ex(2 + i))).astype(BF16)
            pair_scores.append(_dot_nt(u, u))
        kdec = (k * jnp.exp(ex(1))).astype(BF16)
        stage1.append((pair_scores, _dot_tn(v.astype(BF16), kdec)))

    intra = []
    for (_, q, k, v, _), (pair_scores, _) in zip(items, stage1):
        a = jnp.where(diag, jnp.sum(q * k, axis=-1, keepdims=True), 0.0)
        for (_, _, pairs), scores in zip(levels, pair_scores):
            a = jnp.where(pairs, scores, a)
        intra.append(_dot(a.astype(BF16), v.astype(BF16)))

    outs = []
    for (head, q, _, _, ex), (_, increment), o_intra in zip(items, stage1, intra):
        st = states[head]
        outs.append(o_intra + _dot_nt((q * jnp.exp(ex(0))).astype(BF16), st.astype(BF16)))
        states[head] = st * jnp.exp(ex(0)[c - 1:c, :]) + increment
    return outs


def _head_norm_gate(o, nw, gate):
    ms = jnp.mean(o * o, axis=-1, keepdims=True)
    return o * lax.rsqrt(ms + EPS) * nw * gate


def _prompt_hgrn_kernel(hq_ref, hk_ref, hv_ref, g_ref, sgh_ref, s0_ref, nw_ref, dmat_ref, o_ref, sfin_ref, st_ref):
    j = pl.program_id(1)

    @pl.when(j == 0)
    def _():
        st_ref[...] = s0_ref[...]

    masks = _hgrn_masks(HG_CHUNK)
    nw = nw_ref[...]
    dmat = dmat_ref[...]

    def block_of(ex, lanes):
        return lambda i: ex[i * HG_CHUNK:(i + 1) * HG_CHUNK, lanes]

    items, where = [], []
    for ci in range(HG_TILE // HG_CHUNK):
        rows = slice(ci * HG_CHUNK, (ci + 1) * HG_CHUNK)
        ex = _decay_exponents(g_ref[rows, :], dmat)
        for h in range(HG_HEADS):
            lanes = slice(h * HG_KDIM, (h + 1) * HG_KDIM)
            items.append((h, hq_ref[rows, lanes].astype(F32), hk_ref[rows, lanes].astype(F32),
                          hv_ref[rows, lanes].astype(F32), block_of(ex, lanes)))
            where.append((rows, lanes))
    states = [st_ref[h] for h in range(HG_HEADS)]
    outs = _hgrn_chunks(items, states, masks)
    for (rows, lanes), o in zip(where, outs):
        o_ref[rows, lanes] = _head_norm_gate(o, nw, sgh_ref[rows, lanes].astype(F32)).astype(o_ref.dtype)
    for h in range(HG_HEADS):
        st_ref[h] = states[h]

    @pl.when(j == pl.num_programs(1) - 1)
    def _():
        for h in range(HG_HEADS):
            sfin_ref[h] = st_ref[h].T


def _prompt_hgrn(hq, hk, hv, g, sgh, s0t, hg_nw, batch, seq):
    nt = seq // HG_TILE
    blk = pl.BlockSpec((HG_TILE, HG_WIDTH), lambda b, j: (b * nt + j, 0))
    state_shape = (HG_HEADS, HG_VDIM, HG_KDIM)
    dmat = jnp.asarray(_decay_sum_matrix(HG_CHUNK), BF16)
    return pl.pallas_call(
        _prompt_hgrn_kernel,
        grid=(batch, nt),
        in_specs=[blk, blk, blk, blk, blk, pl.BlockSpec(state_shape, lambda b, j: (0, 0, 0)),
                  pl.BlockSpec((1, HG_VDIM), lambda b, j: (0, 0)), pl.BlockSpec(dmat.shape, lambda b, j: (0, 0))],
        out_specs=[blk, pl.BlockSpec((None,) + state_shape, lambda b, j: (b, 0, 0, 0))],
        out_shape=[jax.ShapeDtypeStruct((batch * seq, HG_WIDTH), BF16),
                   jax.ShapeDtypeStruct((batch,) + state_shape, F32)],
        scratch_shapes=[pltpu.VMEM(state_shape, F32)],
        compiler_params=pltpu.CompilerParams(dimension_semantics=("arbitrary", "arbitrary"),
                                             vmem_limit_bytes=VMEM_LIMIT),
        name="prompt_hgrn",
    )(hq, hk, hv, g, sgh, s0t, hg_nw, dmat)


def _meta_mix_kernel(sink_ref, q_ref, kv_ref, sga_ref, hq_ref, hk_ref, hv_ref, g_ref, sgh_ref, nw_ref, dmat_ref,
                     att_ref, hg_ref, st_ref):
    kv = kv_ref[...]
    row = lax.broadcasted_iota(jnp.int32, (N_META, N_META), 0)
    col = lax.broadcasted_iota(jnp.int32, (N_META, N_META), 1)
    att = _attend(q_ref[...].astype(BF16),
                  [(_split_heads(kv[:, :KV_WIDTH]), kv[:, KV_WIDTH:].astype(BF16), col <= row)],
                  lambda h, g: sink_ref[h * GQA_GROUP + g])
    att_ref[...] = (att * sga_ref[...]).astype(att_ref.dtype)

    ex = _decay_exponents(g_ref[...], dmat_ref[...])
    masks = _hgrn_masks(N_META)
    head_lanes = [slice(h * HG_KDIM, (h + 1) * HG_KDIM) for h in range(HG_HEADS)]
    items = [(h, hq_ref[:, lanes], hk_ref[:, lanes], hv_ref[:, lanes],
              (lambda lanes: lambda i: ex[i * N_META:(i + 1) * N_META, lanes])(lanes))
             for h, lanes in enumerate(head_lanes)]
    states = [jnp.zeros((HG_VDIM, HG_KDIM), F32) for _ in range(HG_HEADS)]
    outs = _hgrn_chunks(items, states, masks)
    for h, lanes in enumerate(head_lanes):
        st_ref[h] = states[h]
        hg_ref[:, lanes] = _head_norm_gate(outs[h], nw_ref[...], sgh_ref[:, lanes]).astype(hg_ref.dtype)


def _meta_mix(sinks, q, kv, sga, hq, hk, hv, g, sgh, hg_nw):
    blk_idx = q.shape[0] // N_META - 1
    blk = lambda width: pl.BlockSpec((N_META, width), lambda i, s: (blk_idx, 0))
    first = lambda width: pl.BlockSpec((N_META, width), lambda i, s: (0, 0))
    dmat = jnp.asarray(_decay_sum_matrix(N_META), BF16)
    return pl.pallas_call(
        _meta_mix_kernel,
        grid_spec=pltpu.PrefetchScalarGridSpec(
            num_scalar_prefetch=1, grid=(1,),
            in_specs=[blk(ATT_WIDTH), blk(2 * KV_WIDTH), blk(ATT_WIDTH), blk(HG_WIDTH), blk(HG_WIDTH),
                      blk(HG_WIDTH), blk(HG_WIDTH), blk(HG_WIDTH), pl.BlockSpec((1, HG_VDIM), lambda i, s: (0, 0)),
                      pl.BlockSpec(dmat.shape, lambda i, s: (0, 0))],
            out_specs=[first(ATT_WIDTH), first(HG_WIDTH),
                       pl.BlockSpec((HG_HEADS, HG_VDIM, HG_KDIM), lambda i, s: (0, 0, 0))]),
        out_shape=[jax.ShapeDtypeStruct((N_META, ATT_WIDTH), F32),
                   jax.ShapeDtypeStruct((N_META, HG_WIDTH), F32),
                   jax.ShapeDtypeStruct((HG_HEADS, HG_VDIM, HG_KDIM), F32)],
        compiler_params=pltpu.CompilerParams(dimension_semantics=("arbitrary",)),
        name="meta_mix",
    )(sinks, q, kv, sga, hq, hk, hv, g, sgh, hg_nw, dmat)


def _sample_mix_kernel(q_ref, kv_ref, sga_ref, hq_ref, hk_ref, hv_ref, g_ref, sgh_ref, nw_ref,
                       sink_ref, seg_ref, exp_ref, ck_ref, cv_ref, mk_ref, mv_ref, s_ref,
                       att_ref, hg_ref, nk_ref, nv_ref, ns_ref):
    n_keys = WINDOW + N_META + SUBLANES
    key_row = lax.broadcasted_iota(jnp.int32, (n_keys, KV_WIDTH), 0)
    visible = jnp.logical_and(key_row >= 1, key_row <= WINDOW + N_META)
    win_row = lax.broadcasted_iota(jnp.int32, (WINDOW, KV_WIDTH), 0)
    sink = sink_ref[...]
    seg = seg_ref[...]
    expand = exp_ref[...]
    nw = nw_ref[...]

    def per_seq(b, carry):
        kv_new = kv_ref[b]
        k_new = kv_new[:, 0:KV_WIDTH]
        v_new = kv_new[:, KV_WIDTH:2 * KV_WIDTH]
        ck = ck_ref[b]
        cv = cv_ref[b]
        keys = jnp.concatenate([ck, mk_ref[b], jnp.broadcast_to(k_new, (SUBLANES, KV_WIDTH))], axis=0)
        vals = jnp.concatenate([cv, mv_ref[b], jnp.broadcast_to(v_new, (SUBLANES, KV_WIDTH))], axis=0)
        prod = jnp.concatenate([keys] * GQA_GROUP, axis=1) * q_ref[b]
        s = jnp.where(visible, _dot(prod.astype(BF16), seg), NEG_BIG)
        mx = jnp.maximum(jnp.max(s, axis=0, keepdims=True), sink)
        p = jnp.exp(s - mx)
        den = jnp.sum(p, axis=0, keepdims=True) + jnp.exp(sink - mx)
        p = p * (1.0 / den)
        wide = _dot(p.astype(BF16), expand)
        att = jnp.sum(wide * jnp.concatenate([vals] * GQA_GROUP, axis=1), axis=0, keepdims=True)
        att_ref[b] = att * sga_ref[b]

        nk_ref[b] = jnp.where(win_row == WINDOW - 1, k_new, pltpu.roll(ck, WINDOW - 1, 0))
        nv_ref[b] = jnp.where(win_row == WINDOW - 1, v_new, pltpu.roll(cv, WINDOW - 1, 0))

        g_row, hq_row, hk_row, hv_row, sgh_row = g_ref[b], hq_ref[b], hk_ref[b], hv_ref[b], sgh_ref[b]
        outs = []
        for h in range(HG_HEADS):
            lanes = slice(h * HG_KDIM, (h + 1) * HG_KDIM)
            col = lambda r: jnp.broadcast_to(r[:, lanes], (HG_KDIM, HG_KDIM)).T
            s1 = jnp.exp(col(g_row)) * s_ref[b, h] + col(hk_row) * hv_row[:, lanes]
            ns_ref[b, h] = s1
            o = jnp.sum(col(hq_row) * s1, axis=0, keepdims=True)
            outs.append(_head_norm_gate(o, nw, sgh_row[:, lanes]))
        hg_ref[b] = jnp.concatenate(outs, axis=-1)
        return carry

    lax.fori_loop(0, SAMPLE_TILE, per_seq, 0)


def _sample_mix(layer, q, kv, sga, hq, hk, hv, g, sgh, hg_nw, sink_row, seg, expand,
                cache_k, cache_v, meta_k, meta_v, state, n_seq, stacked):
    t = SAMPLE_TILE
    depth = state.shape[1]
    as_rows = lambda a: a[:n_seq].reshape(n_seq, 1, a.shape[-1])
    rows = lambda width: pl.BlockSpec((t, 1, width), lambda i: (i, 0, 0))
    full = lambda shape: pl.BlockSpec(shape, lambda i: (0,) * len(shape))
    cache = lambda n: pl.BlockSpec((t, None, n, KV_WIDTH), lambda i: (i, layer, 0, 0))
    state_blk = pl.BlockSpec((t, None, HG_HEADS, HG_KDIM, HG_VDIM), lambda i: (i, layer, 0, 0, 0))
    operands = [as_rows(a) for a in (q, kv, sga, hq, hk, hv, g, sgh)] + [
        hg_nw, sink_row, seg, expand, cache_k, cache_v, meta_k, meta_v, state]
    in_specs = [rows(ATT_WIDTH), rows(2 * KV_WIDTH), rows(ATT_WIDTH), rows(HG_WIDTH), rows(HG_WIDTH),
                rows(HG_WIDTH), rows(HG_WIDTH), rows(HG_WIDTH), full((1, HG_VDIM)),
                full((1, KV_WIDTH)), full(seg.shape), full(expand.shape),
                cache(WINDOW), cache(WINDOW), cache(N_META), cache(N_META), state_blk]
    aliases = {}
    kernel_fn = _sample_mix_kernel
    if stacked is not None:
        aliases = {len(operands) + i: 2 + i for i in range(3)}
        operands = operands + list(stacked)
        in_specs = in_specs + [pl.BlockSpec(memory_space=pl.ANY)] * 3
        kernel_fn = _sample_mix_kernel_aliased
    att, hg, nk, nv, ns = pl.pallas_call(
        kernel_fn,
        grid=(n_seq // t,),
        in_specs=in_specs,
        out_specs=[rows(ATT_WIDTH), rows(HG_WIDTH), cache(WINDOW), cache(WINDOW), state_blk],
        out_shape=[jax.ShapeDtypeStruct((n_seq, 1, ATT_WIDTH), F32),
                   jax.ShapeDtypeStruct((n_seq, 1, HG_WIDTH), F32),
                   jax.ShapeDtypeStruct((n_seq, depth, WINDOW, KV_WIDTH), F32),
                   jax.ShapeDtypeStruct((n_seq, depth, WINDOW, KV_WIDTH), F32),
                   jax.ShapeDtypeStruct((n_seq, depth, HG_HEADS, HG_KDIM, HG_VDIM), F32)],
        input_output_aliases=aliases,
        compiler_params=pltpu.CompilerParams(dimension_semantics=("arbitrary",), vmem_limit_bytes=VMEM_LIMIT),
        name="sample_mix",
    )(*operands)
    return att.reshape(n_seq, ATT_WIDTH), hg.reshape(n_seq, HG_WIDTH), (nk, nv, ns)


def _sample_mix_kernel_aliased(*refs):
    n_in = 17
    _sample_mix_kernel(*refs[:n_in], *refs[n_in + 3:])


def _g_major(a, axis):
    shape = a.shape
    a = a.reshape(shape[:axis] + (KV_HEADS, GQA_GROUP, HEAD_DIM) + shape[axis + 1:])
    a = jnp.swapaxes(a, axis, axis + 1)
    return a.reshape(shape)


def _constants():
    lane = np.arange(ATT_WIDTH)
    g_of, h_of = lane // KV_WIDTH, (lane % KV_WIDTH) // HEAD_DIM
    head = h_of * GQA_GROUP + g_of
    seg = (head[:, None] == np.arange(KV_WIDTH)[None, :]).astype(np.float32)
    grp = np.arange(256) // HEAD_DIM
    bd = (grp[:, None] == grp[None, :]).astype(np.float32) / HEAD_DIM
    return jnp.asarray(seg, BF16), jnp.asarray(seg.T, BF16), jnp.asarray(bd, BF16)


def kernel(x_prompt, x_sample, cache_win_k, cache_win_v, cache_meta_k, cache_meta_v, state_hgrn, meta_tokens,
           norm_w, w_in, q_norm_w, k_norm_w, attn_sinks, hg_lb_logits, hg_norm_w, w_out):
    batch, seq, _ = x_prompt.shape
    n_seq = x_sample.shape[0]
    depth = w_in.shape[0]
    w_buf = cache_win_k.shape[2]
    assert x_sample.shape[1] == 1 and w_buf == WINDOW and seq % ROW_TILE == 0 and n_seq % SAMPLE_TILE == 0

    seg, expand, bd = _constants()
    w_in_b = jnp.concatenate([_g_major(w_in[:, :, C_Q:C_K], 2), w_in[:, :, C_K:C_GA],
                              _g_major(w_in[:, :, C_GA:C_QH], 2), w_in[:, :, C_QH:]], axis=2).astype(BF16)
    w_out_b = jnp.concatenate([_g_major(w_out[:, :ATT_WIDTH], 1), w_out[:, ATT_WIDTH:]], axis=1).astype(BF16)
    qnw = jnp.tile(q_norm_w, (1, ATT_HEADS)) * (HEAD_DIM ** -0.5)
    knw = jnp.tile(k_norm_w, (1, KV_HEADS))
    lb_logits = hg_lb_logits.astype(F32)
    sink_rows = jnp.pad(attn_sinks.astype(F32), ((0, 0), (0, KV_WIDTH - ATT_HEADS)))

    ck = cache_win_k.reshape(n_seq, depth, w_buf, KV_WIDTH)
    cv = cache_win_v.reshape(n_seq, depth, w_buf, KV_WIDTH)
    mk = cache_meta_k.reshape(n_seq, depth, N_META, KV_WIDTH)
    mv = cache_meta_v.reshape(n_seq, depth, N_META, KV_WIDTH)

    xp = x_prompt.reshape(batch * seq, D_MODEL)
    xs = jnp.concatenate([x_sample.reshape(n_seq, D_MODEL), meta_tokens.astype(F32)], axis=0)
    n_small = xs.shape[0]

    outs = {k: [] for k in ("wkp", "wvp", "mk", "mv", "hsp")}
    sample_stacked = None
    mix_s = mix_p = None
    for l in range(depth):
        layer_w = (norm_w[l][None], w_in_b[l], qnw[l][None], knw[l][None], lb_logits, bd)
        sinks = attn_sinks[l].astype(F32)
        hg_nw = hg_norm_w[l][None]

        sm = _project(l, xs, *layer_w, row_tile=n_small, act_dtype=F32, mix=mix_s)
        if mix_s is not None:
            xs, sm = sm[0], sm[1:]
        att_m, hg_m, s0t = _meta_mix(sinks, *sm, hg_nw)
        att_s, hg_s, sample_stacked = _sample_mix(l, *sm, hg_nw, sink_rows[l][None], seg, expand,
                                                  ck, cv, mk, mv, state_hgrn, n_seq, sample_stacked)
        kv_meta = sm[1][n_seq:]
        mix_s = (jnp.concatenate([att_s, att_m], axis=0), jnp.concatenate([hg_s, hg_m], axis=0), w_out_b[l])

        pr = _project(l, xp, *layer_w, row_tile=ROW_TILE, act_dtype=BF16, mix=mix_p)
        if mix_p is not None:
            xp, pr = pr[0], pr[1:]
        q, kv, sga, hq, hk, hv, g, sgh = pr
        att = _prompt_attention(sinks, q, kv, kv_meta, sga, batch, seq)
        hg, s_fin = _prompt_hgrn(hq, hk, hv, g, sgh, s0t, hg_nw, batch, seq)
        mix_p = (att, hg, w_out_b[l])

        kv3 = kv.reshape(batch, seq, 2 * KV_WIDTH)
        outs["wkp"].append(kv3[:, seq - w_buf:, :KV_WIDTH])
        outs["wvp"].append(kv3[:, seq - w_buf:, KV_WIDTH:])
        outs["mk"].append(jnp.broadcast_to(kv_meta[None, :, :KV_WIDTH], (batch, N_META, KV_WIDTH)))
        outs["mv"].append(jnp.broadcast_to(kv_meta[None, :, KV_WIDTH:], (batch, N_META, KV_WIDTH)))
        outs["hsp"].append(s_fin)
    xs = _out_project(mix_s[0], mix_s[1], xs, mix_s[2], n_small)
    xp = _out_project(mix_p[0], mix_p[1], xp, mix_p[2], ROW_TILE)

    stack = lambda name: jnp.stack(outs[name], axis=1)
    heads = lambda a: a.reshape(a.shape[:-1] + (KV_HEADS, HEAD_DIM))
    new_k, new_v, new_state = sample_stacked
    return (xp.reshape(batch, seq, D_MODEL), xs[:n_seq].reshape(n_seq, 1, D_MODEL),
            heads(stack("wkp")), heads(stack("wvp")), heads(stack("mk")), heads(stack("mv")), stack("hsp"),
            heads(new_k), heads(new_v), new_state)
```

```python
import functools

import numpy as np
import jax
import jax.numpy as jnp
from jax import lax
from jax.experimental import pallas as pl
from jax.experimental.pallas import tpu as pltpu

F32 = jnp.float32
BF16 = jnp.bfloat16

D_MODEL = 1024
N_META = 16
WINDOW = 128
HEAD_DIM = 64
ATT_WIDTH = 512
ATT_HEADS = 8
KV_HEADS = 2
GQA_GROUP = 4
KV_WIDTH = KV_HEADS * HEAD_DIM
HG_WIDTH = 512
HG_HEADS = 4
HG_KDIM = 128
HG_VDIM = 128
PROJ_WIDTH = 3328
EPS = 1e-6
NEG_BIG = -1e30
TINY = 1e-30

C_Q, C_K, C_V, C_GA, C_QH, C_FH, C_IH, C_GH = 0, 512, 640, 768, 1280, 1792, 2304, 2816

SUBLANES = 8
HG_CHUNK = 64
ROW_TILE = 512
HG_TILE = 512
ATT_TILE = 512
SAMPLE_TILE = 16
VMEM_LIMIT = 48 * 1024 * 1024


def _dot(a, b):
    return jnp.dot(a, b, preferred_element_type=F32)


def _dot_nt(a, b):
    return lax.dot_general(a, b, (((1,), (1,)), ((), ())), preferred_element_type=F32)


def _dot_tn(a, b):
    return lax.dot_general(a, b, (((0,), (0,)), ((), ())), preferred_element_type=F32)


def _silu(x):
    return x * (1.0 / (1.0 + jnp.exp(-x)))


def _row_parts(n_rows, n_parts):
    step = n_rows // n_parts
    return [slice(i * step, (i + 1) * step) for i in range(n_parts)]


def _mix_proj_kernel(layer, n_parts, att_ref, hg_ref, wo_ref, x_ref, *rest):
    parts = _row_parts(x_ref.shape[0], n_parts)
    xnew_ref = rest[6]
    xs = []
    for rs in parts:
        x = (x_ref[rs, :] + _dot(att_ref[rs, :].astype(BF16), wo_ref[0:ATT_WIDTH, :])
             + _dot(hg_ref[rs, :].astype(BF16), wo_ref[ATT_WIDTH:, :]))
        xnew_ref[rs, :] = x
        xs.append(x)
    _proj_body(layer, parts, xs, *rest[:6], *rest[7:])


def _proj_kernel(layer, n_parts, x_ref, *rest):
    parts = _row_parts(x_ref.shape[0], n_parts)
    _proj_body(layer, parts, [x_ref[rs, :] for rs in parts], *rest)


def _proj_body(layer, parts, xs, nw_ref, w_ref, qnw_ref, knw_ref, lbl_ref, bd_ref,
               q_ref, kv_ref, sga_ref, hq_ref, hk_ref, hv_ref, g_ref, sgh_ref):
    nw = nw_ref[...]
    hs = []
    for x in xs:
        ms = jnp.mean(x * x, axis=-1, keepdims=True)
        hs.append((x * lax.rsqrt(ms + EPS) * nw).astype(BF16))

    def proj(h, lo, hi):
        return _dot(h, w_ref[:, lo:hi])

    pq = [[proj(h, C_Q + 256 * c, C_Q + 256 * (c + 1)) for c in range(2)] for h in hs]
    pk = [proj(h, C_K, C_V) for h in hs]
    sq = [[(p * p).astype(BF16) for p in pqs + [pks]] for pqs, pks in zip(pq, pk)]

    logits = lbl_ref[...]
    e = jnp.exp(logits - jnp.max(logits, axis=0, keepdims=True))
    p = e / jnp.sum(e, axis=0, keepdims=True)
    lb = jnp.zeros((1, HG_WIDTH), F32)
    for j in range(1, layer + 1):
        lb = lb + p[j:j + 1, :]

    for rs, h in zip(parts, hs):
        kv_ref[rs, KV_WIDTH:2 * KV_WIDTH] = proj(h, C_V, C_GA)
        sga_ref[rs, :] = _silu(proj(h, C_GA, C_QH)).astype(sga_ref.dtype)
        hq_ref[rs, :] = _silu(proj(h, C_QH, C_FH)).astype(hq_ref.dtype)

        z = proj(h, C_FH, C_IH)
        ez = jnp.exp(-jnp.abs(z))
        r = 1.0 / (1.0 + ez)
        pos = z >= 0.0
        sig_pos = jnp.where(pos, r, ez * r)
        sig_neg = jnp.where(pos, ez * r, r)
        hk_ref[rs, :] = ((1.0 - lb) * sig_neg).astype(hk_ref.dtype)
        f = lb + (1.0 - lb) * sig_pos
        g_ref[rs, :] = jnp.log(jnp.maximum(f, TINY))

        hv_ref[rs, :] = proj(h, C_IH, C_GH).astype(hv_ref.dtype)
        sgh_ref[rs, :] = _silu(proj(h, C_GH, PROJ_WIDTH)).astype(sgh_ref.dtype)

    bd = bd_ref[...]
    for rs, pqs, pks, sqs in zip(parts, pq, pk, sq):
        q = jnp.concatenate([p * lax.rsqrt(_dot(s, bd) + EPS) for p, s in zip(pqs, sqs[:2])], axis=-1)
        q_ref[rs, :] = (q * qnw_ref[...]).astype(q_ref.dtype)
        kv_ref[rs, 0:KV_WIDTH] = pks * lax.rsqrt(_dot(sqs[2], bd[:KV_WIDTH, :KV_WIDTH]) + EPS) * knw_ref[...]


def _project(layer, x, norm_w, w_in, qnw, knw, lb_logits, bd, row_tile, act_dtype, mix=None):
    n = x.shape[0]
    rows = lambda width: pl.BlockSpec((row_tile, width), lambda i: (i, 0))
    full = lambda shape: pl.BlockSpec(shape, lambda i: (0,) * len(shape))
    out = lambda width, dtype: jax.ShapeDtypeStruct((n, width), dtype)
    operands = [x, norm_w, w_in, qnw, knw, lb_logits, bd]
    in_specs = [rows(D_MODEL), full((1, D_MODEL)), full((D_MODEL, PROJ_WIDTH)), full((1, ATT_WIDTH)),
                full((1, KV_WIDTH)), full(lb_logits.shape), full((256, 256))]
    out_specs = [rows(ATT_WIDTH), rows(2 * KV_WIDTH), rows(ATT_WIDTH), rows(HG_WIDTH), rows(HG_WIDTH),
                 rows(HG_WIDTH), rows(HG_WIDTH), rows(HG_WIDTH)]
    out_shape = [out(ATT_WIDTH, act_dtype), out(2 * KV_WIDTH, F32), out(ATT_WIDTH, act_dtype),
                 out(HG_WIDTH, act_dtype), out(HG_WIDTH, act_dtype), out(HG_WIDTH, act_dtype),
                 out(HG_WIDTH, F32), out(HG_WIDTH, act_dtype)]
    body = _proj_kernel
    if mix is not None:
        operands = list(mix) + operands
        in_specs = [rows(ATT_WIDTH), rows(HG_WIDTH), full((D_MODEL, D_MODEL))] + in_specs
        out_specs = [rows(D_MODEL)] + out_specs
        out_shape = [out(D_MODEL, F32)] + out_shape
        body = _mix_proj_kernel
    n_parts = 2 if row_tile % (2 * 128) == 0 else 1
    return pl.pallas_call(
        functools.partial(body, layer, n_parts),
        grid=(n // row_tile,),
        in_specs=in_specs, out_specs=out_specs, out_shape=out_shape,
        compiler_params=pltpu.CompilerParams(dimension_semantics=("arbitrary",), vmem_limit_bytes=VMEM_LIMIT),
        name="proj",
    )(*operands)


def _out_kernel(att_ref, hg_ref, x_ref, w_ref, y_ref):
    y_ref[...] = (x_ref[...] + _dot(att_ref[...].astype(BF16), w_ref[0:ATT_WIDTH, :])
                  + _dot(hg_ref[...].astype(BF16), w_ref[ATT_WIDTH:, :]))


def _out_project(att, hg, x, w_out, row_tile):
    n = x.shape[0]
    rows = lambda width: pl.BlockSpec((row_tile, width), lambda i: (i, 0))
    return pl.pallas_call(
        _out_kernel,
        grid=(n // row_tile,),
        in_specs=[rows(ATT_WIDTH), rows(HG_WIDTH), rows(D_MODEL),
                  pl.BlockSpec((D_MODEL, D_MODEL), lambda i: (0, 0))],
        out_specs=rows(D_MODEL),
        out_shape=jax.ShapeDtypeStruct((n, D_MODEL), F32),
        compiler_params=pltpu.CompilerParams(dimension_semantics=("arbitrary",), vmem_limit_bytes=VMEM_LIMIT),
        name="out_proj",
    )(att, hg, x, w_out)


def _split_heads(k):
    first = lax.broadcasted_iota(jnp.int32, k.shape, 1) < HEAD_DIM
    return (jnp.where(first, k, 0.0).astype(BF16), jnp.where(first, 0.0, k).astype(BF16))


def _attend(q, key_sets, sink_of):
    m = q.shape[0]
    first = lax.broadcasted_iota(jnp.int32, (m, KV_WIDTH), 1) < HEAD_DIM
    blocks = []
    for g in range(GQA_GROUP):
        qg = q[:, g * KV_WIDTH:(g + 1) * KV_WIDTH]
        per_head = []
        for h in range(KV_HEADS):
            scores = []
            for k_heads, _, mask in key_sets:
                s = _dot_nt(qg, k_heads[h])
                if mask is not None:
                    s = jnp.where(mask, s, NEG_BIG)
                scores.append(s)
            sink = sink_of(h, g)
            mx = jnp.max(scores[0], axis=-1, keepdims=True)
            for s in scores[1:]:
                mx = jnp.maximum(mx, jnp.max(s, axis=-1, keepdims=True))
            mx = jnp.maximum(mx, sink)
            den = jnp.exp(sink - mx)
            acc = jnp.zeros((m, KV_WIDTH), F32)
            for s, (_, v, _) in zip(scores, key_sets):
                p = jnp.exp(s - mx)
                den = den + jnp.sum(p, axis=-1, keepdims=True)
                acc = acc + _dot(p.astype(BF16), v)
            per_head.append(acc * (1.0 / den))
        blocks.append(jnp.where(first, per_head[0], per_head[1]))
    return jnp.concatenate(blocks, axis=-1)


def _prompt_attn_kernel(sink_ref, q_ref, kvc_ref, kvp_ref, kvm_ref, sga_ref, o_ref):
    i = pl.program_id(1)
    n_blocks = q_ref.shape[0] // WINDOW
    n_keys = 2 * WINDOW + N_META
    kvm = kvm_ref[...]
    k_blocks = [_split_heads(kvp_ref[:, :KV_WIDTH])]
    vt_blocks = [kvp_ref[:, KV_WIDTH:].T.astype(BF16)]
    for j in range(n_blocks):
        rows = slice(j * WINDOW, (j + 1) * WINDOW)
        k_blocks.append(_split_heads(kvc_ref[rows, :KV_WIDTH]))
        vt_blocks.append(kvc_ref[rows, KV_WIDTH:].T.astype(BF16))
    k_meta = _split_heads(kvm[:, :KV_WIDTH])
    v_meta = jnp.concatenate([kvm[:, KV_WIDTH:], jnp.zeros((WINDOW - N_META, KV_WIDTH), F32)], axis=0)
    vt_meta = v_meta.T.astype(BF16)
    pad = jnp.zeros((3 * WINDOW - n_keys, WINDOW), BF16)

    key = lax.broadcasted_iota(jnp.int32, (n_keys, WINDOW), 0)
    qi = lax.broadcasted_iota(jnp.int32, (n_keys, WINDOW), 1)
    band = jnp.where(key < WINDOW, key - qi - 1, jnp.where(key < 2 * WINDOW, qi - (key - WINDOW), 0))
    visible = band >= 0
    no_prev = jnp.where(i > 0, 0, 2 * WINDOW)
    visible_first = jnp.where(key < WINDOW, band - no_prev, band) >= 0
    head0_rows = lax.broadcasted_iota(jnp.int32, (KV_WIDTH, WINDOW), 0) < HEAD_DIM
    group_lanes = [slice(g * KV_WIDTH, (g + 1) * KV_WIDTH) for g in range(GQA_GROUP)]

    s_both = []
    for j in range(n_blocks):
        rows = slice(j * WINDOW, (j + 1) * WINDOW)
        kk = jnp.concatenate([k_blocks[j][0], k_blocks[j + 1][0], k_meta[0],
                              k_blocks[j][1], k_blocks[j + 1][1], k_meta[1]], axis=0)
        s_both.append([_dot_nt(kk, q_ref[rows, lanes]) for lanes in group_lanes])
    probs = []
    for j in range(n_blocks):
        vis = visible_first if j == 0 else visible
        for g in range(GQA_GROUP):
            for h in range(KV_HEADS):
                sink = sink_ref[h * GQA_GROUP + g]
                s = jnp.where(vis, s_both[j][g][h * n_keys:(h + 1) * n_keys], NEG_BIG)
                mx = jnp.maximum(jnp.max(s, axis=0, keepdims=True), sink)
                p = jnp.exp(s - mx)
                den = jnp.sum(p, axis=0, keepdims=True) + jnp.exp(sink - mx)
                probs.append((jnp.concatenate([p.astype(BF16), pad], axis=0), 1.0 / den))
    outs = []
    for j in range(n_blocks):
        v_t = jnp.concatenate([vt_blocks[j], vt_blocks[j + 1], vt_meta], axis=1)
        for p_pad, inv in probs[j * ATT_HEADS:(j + 1) * ATT_HEADS]:
            outs.append(_dot(v_t, p_pad) * inv)
    for j in range(n_blocks):
        rows = slice(j * WINDOW, (j + 1) * WINDOW)
        for g, lanes in enumerate(group_lanes):
            pair = outs[j * ATT_HEADS + g * KV_HEADS:j * ATT_HEADS + (g + 1) * KV_HEADS]
            o_t = jnp.where(head0_rows, pair[0], pair[1])
            o_ref[rows, lanes] = (o_t.T * sga_ref[rows, lanes].astype(F32)).astype(o_ref.dtype)


def _prompt_attention(sinks, q, kv, kv_meta, sga, batch, seq):
    nb = seq // ATT_TILE
    per = ATT_TILE // WINDOW
    blk = lambda width: pl.BlockSpec((ATT_TILE, width), lambda b, i, s: (b * nb + i, 0))
    prev = pl.BlockSpec((WINDOW, 2 * KV_WIDTH), lambda b, i, s: ((b * nb + i) * per - jnp.minimum(i, 1), 0))
    meta = pl.BlockSpec((N_META, 2 * KV_WIDTH), lambda b, i, s: (0, 0))
    return pl.pallas_call(
        _prompt_attn_kernel,
        grid_spec=pltpu.PrefetchScalarGridSpec(
            num_scalar_prefetch=1, grid=(batch, nb),
            in_specs=[blk(ATT_WIDTH), blk(2 * KV_WIDTH), prev, meta, blk(ATT_WIDTH)],
            out_specs=blk(ATT_WIDTH)),
        out_shape=jax.ShapeDtypeStruct((batch * seq, ATT_WIDTH), BF16),
        compiler_params=pltpu.CompilerParams(dimension_semantics=("arbitrary", "arbitrary"),
                                             vmem_limit_bytes=VMEM_LIMIT),
        name="prompt_attn",
    )(sinks, q, kv, kv, kv_meta, sga)


def _level_sizes(c):
    return [c >> (i + 1) for i in range(c.bit_length() - 1)]


def _decay_sum_matrix(c):
    t = np.arange(c)[:, None]
    r = np.arange(c)[None, :]
    mats = [r <= t, r > t]
    for bs in _level_sizes(c):
        a = (t // (2 * bs)) * (2 * bs) + bs - 1
        mats.append(((r > t) & (r <= a)) | ((r > a) & (r <= t)))
    return np.tile(np.concatenate(mats, axis=0).astype(np.float32), (1, 3))


def _decay_exponents(g, dmat):
    g1 = g.astype(BF16)
    r1 = g - g1.astype(F32)
    g2 = r1.astype(BF16)
    g3 = (r1 - g2.astype(F32)).astype(BF16)
    return _dot(dmat, jnp.concatenate([g1, g2, g3], axis=0))


def _hgrn_masks(c):
    row = lax.broadcasted_iota(jnp.int32, (c, c), 0)
    col = lax.broadcasted_iota(jnp.int32, (c, c), 1)
    row_k = lax.broadcasted_iota(jnp.int32, (c, HG_KDIM), 0)
    levels = []
    for bs in _level_sizes(c):
        rb, cb = row // bs, col // bs
        pairs = ((rb % 2) * (1 - jnp.abs(cb - rb + 1))) > 0
        levels.append((bs, (row_k // bs) % 2 == 1, pairs))
    return levels, row == col


def _hgrn_chunks(items, states, masks):
    levels, diag = masks
    c = items[0][1].shape[0]

    stage1 = []
    for _, q, k, v, ex in items:
        pair_scores = []
        for i, (bs, q_side, _) in enumerate(levels):
            if bs >= SUBLANES:
                side = jnp.concatenate(
                    [(q if (j // bs) % 2 == 1 else k)[j:j + SUBLANES] for j in range(0, c, SUBLANES)], axis=0)
            else:
                side = jnp.where(q_side, q, k)
            u = (side * jnp.exp(ex(2 + i))).astype(BF16)
            pair_scores.append(_dot_nt(u, u))
        kdec = (k * jnp.exp(ex(1))).astype(BF16)
        stage1.append((pair_scores, _dot_tn(v.astype(BF16), kdec)))

    intra = []
    for (_, q, k, v, _), (pair_scores, _) in zip(items, stage1):
        a = jnp.where(diag, jnp.sum(q * k, axis=-1, keepdims=True), 0.0)
        for (_, _, pairs), scores in zip(levels, pair_scores):
            a = jnp.where(pairs, scores, a)
        intra.append(_dot(a.astype(BF16), v.astype(BF16)))

    outs = []
    for (head, q, _, _, ex), (_, increment), o_intra in zip(items, stage1, intra):
        st = states[head]
        outs.append(o_intra + _dot_nt((q * jnp.exp(ex(0))).astype(BF16), st.astype(BF16)))
        states[head] = st * jnp.exp(ex(0)[c - 1:c, :]) + increment
    return outs


def _head_norm_gate(o, nw, gate):
    ms = jnp.mean(o * o, axis=-1, keepdims=True)
    return o * lax.rsqrt(ms + EPS) * nw * gate


def _prompt_hgrn_kernel(hq_ref, hk_ref, hv_ref, g_ref, sgh_ref, s0_ref, nw_ref, dmat_ref, o_ref, sfin_ref, st_ref):
    j = pl.program_id(1)

    @pl.when(j == 0)
    def _():
        st_ref[...] = s0_ref[...]

    masks = _hgrn_masks(HG_CHUNK)
    nw = nw_ref[...]
    dmat = dmat_ref[...]

    def block_of(ex, lanes):
        return lambda i: ex[i * HG_CHUNK:(i + 1) * HG_CHUNK, lanes]

    items, where = [], []
    for ci in range(HG_TILE // HG_CHUNK):
        rows = slice(ci * HG_CHUNK, (ci + 1) * HG_CHUNK)
        ex = _decay_exponents(g_ref[rows, :], dmat)
        for h in range(HG_HEADS):
            lanes = slice(h * HG_KDIM, (h + 1) * HG_KDIM)
            items.append((h, hq_ref[rows, lanes].astype(F32), hk_ref[rows, lanes].astype(F32),
                          hv_ref[rows, lanes].astype(F32), block_of(ex, lanes)))
            where.append((rows, lanes))
    states = [st_ref[h] for h in range(HG_HEADS)]
    outs = _hgrn_chunks(items, states, masks)
    for (rows, lanes), o in zip(where, outs):
        o_ref[rows, lanes] = _head_norm_gate(o, nw, sgh_ref[rows, lanes].astype(F32)).astype(o_ref.dtype)
    for h in range(HG_HEADS):
        st_ref[h] = states[h]

    @pl.when(j == pl.num_programs(1) - 1)
    def _():
        for h in range(HG_HEADS):
            sfin_ref[h] = st_ref[h].T


def _prompt_hgrn(hq, hk, hv, g, sgh, s0t, hg_nw, batch, seq):
    nt = seq // HG_TILE
    blk = pl.BlockSpec((HG_TILE, HG_WIDTH), lambda b, j: (b * nt + j, 0))
    state_shape = (HG_HEADS, HG_VDIM, HG_KDIM)
    dmat = jnp.asarray(_decay_sum_matrix(HG_CHUNK), BF16)
    return pl.pallas_call(
        _prompt_hgrn_kernel,
        grid=(batch, nt),
        in_specs=[blk, blk, blk, blk, blk, pl.BlockSpec(state_shape, lambda b, j: (0, 0, 0)),
                  pl.BlockSpec((1, HG_VDIM), lambda b, j: (0, 0)), pl.BlockSpec(dmat.shape, lambda b, j: (0, 0))],
        out_specs=[blk, pl.BlockSpec((None,) + state_shape, lambda b, j: (b, 0, 0, 0))],
        out_shape=[jax.ShapeDtypeStruct((batch * seq, HG_WIDTH), BF16),
                   jax.ShapeDtypeStruct((batch,) + state_shape, F32)],
        scratch_shapes=[pltpu.VMEM(state_shape, F32)],
        compiler_params=pltpu.CompilerParams(dimension_semantics=("arbitrary", "arbitrary"),
                                             vmem_limit_bytes=VMEM_LIMIT),
        name="prompt_hgrn",
    )(hq, hk, hv, g, sgh, s0t, hg_nw, dmat)


def _meta_mix_kernel(sink_ref, q_ref, kv_ref, sga_ref, hq_ref, hk_ref, hv_ref, g_ref, sgh_ref, nw_ref, dmat_ref,
                     att_ref, hg_ref, st_ref):
    kv = kv_ref[...]
    row = lax.broadcasted_iota(jnp.int32, (N_META, N_META), 0)
    col = lax.broadcasted_iota(jnp.int32, (N_META, N_META), 1)
    att = _attend(q_ref[...].astype(BF16),
                  [(_split_heads(kv[:, :KV_WIDTH]), kv[:, KV_WIDTH:].astype(BF16), col <= row)],
                  lambda h, g: sink_ref[h * GQA_GROUP + g])
    att_ref[...] = (att * sga_ref[...]).astype(att_ref.dtype)

    ex = _decay_exponents(g_ref[...], dmat_ref[...])
    masks = _hgrn_masks(N_META)
    head_lanes = [slice(h * HG_KDIM, (h + 1) * HG_KDIM) for h in range(HG_HEADS)]
    items = [(h, hq_ref[:, lanes], hk_ref[:, lanes], hv_ref[:, lanes],
              (lambda lanes: lambda i: ex[i * N_META:(i + 1) * N_META, lanes])(lanes))
             for h, lanes in enumerate(head_lanes)]
    states = [jnp.zeros((HG_VDIM, HG_KDIM), F32) for _ in range(HG_HEADS)]
    outs = _hgrn_chunks(items, states, masks)
    for h, lanes in enumerate(head_lanes):
        st_ref[h] = states[h]
        hg_ref[:, lanes] = _head_norm_gate(outs[h], nw_ref[...], sgh_ref[:, lanes]).astype(hg_ref.dtype)


def _meta_mix(sinks, q, kv, sga, hq, hk, hv, g, sgh, hg_nw):
    blk_idx = q.shape[0] // N_META - 1
    blk = lambda width: pl.BlockSpec((N_META, width), lambda i, s: (blk_idx, 0))
    first = lambda width: pl.BlockSpec((N_META, width), lambda i, s: (0, 0))
    dmat = jnp.asarray(_decay_sum_matrix(N_META), BF16)
    return pl.pallas_call(
        _meta_mix_kernel,
        grid_spec=pltpu.PrefetchScalarGridSpec(
            num_scalar_prefetch=1, grid=(1,),
            in_specs=[blk(ATT_WIDTH), blk(2 * KV_WIDTH), blk(ATT_WIDTH), blk(HG_WIDTH), blk(HG_WIDTH),
                      blk(HG_WIDTH), blk(HG_WIDTH), blk(HG_WIDTH), pl.BlockSpec((1, HG_VDIM), lambda i, s: (0, 0)),
                      pl.BlockSpec(dmat.shape, lambda i, s: (0, 0))],
            out_specs=[first(ATT_WIDTH), first(HG_WIDTH),
                       pl.BlockSpec((HG_HEADS, HG_VDIM, HG_KDIM), lambda i, s: (0, 0, 0))]),
        out_shape=[jax.ShapeDtypeStruct((N_META, ATT_WIDTH), F32),
                   jax.ShapeDtypeStruct((N_META, HG_WIDTH), F32),
                   jax.ShapeDtypeStruct((HG_HEADS, HG_VDIM, HG_KDIM), F32)],
        compiler_params=pltpu.CompilerParams(dimension_semantics=("arbitrary",)),
        name="meta_mix",
    )(sinks, q, kv, sga, hq, hk, hv, g, sgh, hg_nw, dmat)


def _sample_mix_kernel(q_ref, kv_ref, sga_ref, hq_ref, hk_ref, hv_ref, g_ref, sgh_ref, nw_ref,
                       sink_ref, seg_ref, exp_ref, ck_ref, cv_ref, mk_ref, mv_ref, s_ref,
                       att_ref, hg_ref, nk_ref, nv_ref, ns_ref):
    n_keys = WINDOW + N_META + SUBLANES
    key_row = lax.broadcasted_iota(jnp.int32, (n_keys, KV_WIDTH), 0)
    visible = jnp.logical_and(key_row >= 1, key_row <= WINDOW + N_META)
    win_row = lax.broadcasted_iota(jnp.int32, (WINDOW, KV_WIDTH), 0)
    sink = sink_ref[...]
    seg = seg_ref[...]
    expand = exp_ref[...]
    nw = nw_ref[...]

    def per_seq(b, carry):
        kv_new = kv_ref[b]
        k_new = kv_new[:, 0:KV_WIDTH]
        v_new = kv_new[:, KV_WIDTH:2 * KV_WIDTH]
        ck = ck_ref[b]
        cv = cv_ref[b]
        keys = jnp.concatenate([ck, mk_ref[b], jnp.broadcast_to(k_new, (SUBLANES, KV_WIDTH))], axis=0)
        vals = jnp.concatenate([cv, mv_ref[b], jnp.broadcast_to(v_new, (SUBLANES, KV_WIDTH))], axis=0)
        prod = jnp.concatenate([keys] * GQA_GROUP, axis=1) * q_ref[b]
        s = jnp.where(visible, _dot(prod.astype(BF16), seg), NEG_BIG)
        mx = jnp.maximum(jnp.max(s, axis=0, keepdims=True), sink)
        p = jnp.exp(s - mx)
        den = jnp.sum(p, axis=0, keepdims=True) + jnp.exp(sink - mx)
        p = p * (1.0 / den)
        wide = _dot(p.astype(BF16), expand)
        att = jnp.sum(wide * jnp.concatenate([vals] * GQA_GROUP, axis=1), axis=0, keepdims=True)
        att_ref[b] = att * sga_ref[b]

        nk_ref[b] = jnp.where(win_row == WINDOW - 1, k_new, pltpu.roll(ck, WINDOW - 1, 0))
        nv_ref[b] = jnp.where(win_row == WINDOW - 1, v_new, pltpu.roll(cv, WINDOW - 1, 0))

        g_row, hq_row, hk_row, hv_row, sgh_row = g_ref[b], hq_ref[b], hk_ref[b], hv_ref[b], sgh_ref[b]
        outs = []
        for h in range(HG_HEADS):
            lanes = slice(h * HG_KDIM, (h + 1) * HG_KDIM)
            col = lambda r: jnp.broadcast_to(r[:, lanes], (HG_KDIM, HG_KDIM)).T
            s1 = jnp.exp(col(g_row)) * s_ref[b, h] + col(hk_row) * hv_row[:, lanes]
            ns_ref[b, h] = s1
            o = jnp.sum(col(hq_row) * s1, axis=0, keepdims=True)
            outs.append(_head_norm_gate(o, nw, sgh_row[:, lanes]))
        hg_ref[b] = jnp.concatenate(outs, axis=-1)
        return carry

    lax.fori_loop(0, SAMPLE_TILE, per_seq, 0)


def _sample_mix(layer, q, kv, sga, hq, hk, hv, g, sgh, hg_nw, sink_row, seg, expand,
                cache_k, cache_v, meta_k, meta_v, state, n_seq, stacked):
    t = SAMPLE_TILE
    depth = state.shape[1]
    as_rows = lambda a: a[:n_seq].reshape(n_seq, 1, a.shape[-1])
    rows = lambda width: pl.BlockSpec((t, 1, width), lambda i: (i, 0, 0))
    full = lambda shape: pl.BlockSpec(shape, lambda i: (0,) * len(shape))
    cache = lambda n: pl.BlockSpec((t, None, n, KV_WIDTH), lambda i: (i, layer, 0, 0))
    state_blk = pl.BlockSpec((t, None, HG_HEADS, HG_KDIM, HG_VDIM), lambda i: (i, layer, 0, 0, 0))
    operands = [as_rows(a) for a in (q, kv, sga, hq, hk, hv, g, sgh)] + [
        hg_nw, sink_row, seg, expand, cache_k, cache_v, meta_k, meta_v, state]
    in_specs = [rows(ATT_WIDTH), rows(2 * KV_WIDTH), rows(ATT_WIDTH), rows(HG_WIDTH), rows(HG_WIDTH),
                rows(HG_WIDTH), rows(HG_WIDTH), rows(HG_WIDTH), full((1, HG_VDIM)),
                full((1, KV_WIDTH)), full(seg.shape), full(expand.shape),
                cache(WINDOW), cache(WINDOW), cache(N_META), cache(N_META), state_blk]
    aliases = {}
    kernel_fn = _sample_mix_kernel
    if stacked is not None:
        aliases = {len(operands) + i: 2 + i for i in range(3)}
        operands = operands + list(stacked)
        in_specs = in_specs + [pl.BlockSpec(memory_space=pl.ANY)] * 3
        kernel_fn = _sample_mix_kernel_aliased
    att, hg, nk, nv, ns = pl.pallas_call(
        kernel_fn,
        grid=(n_seq // t,),
        in_specs=in_specs,
        out_specs=[rows(ATT_WIDTH), rows(HG_WIDTH), cache(WINDOW), cache(WINDOW), state_blk],
        out_shape=[jax.ShapeDtypeStruct((n_seq, 1, ATT_WIDTH), F32),
                   jax.ShapeDtypeStruct((n_seq, 1, HG_WIDTH), F32),
                   jax.ShapeDtypeStruct((n_seq, depth, WINDOW, KV_WIDTH), F32),
                   jax.ShapeDtypeStruct((n_seq, depth, WINDOW, KV_WIDTH), F32),
                   jax.ShapeDtypeStruct((n_seq, depth, HG_HEADS, HG_KDIM, HG_VDIM), F32)],
        input_output_aliases=aliases,
        compiler_params=pltpu.CompilerParams(dimension_semantics=("arbitrary",), vmem_limit_bytes=VMEM_LIMIT),
        name="sample_mix",
    )(*operands)
    return att.reshape(n_seq, ATT_WIDTH), hg.reshape(n_seq, HG_WIDTH), (nk, nv, ns)


def _sample_mix_kernel_aliased(*refs):
    n_in = 17
    _sample_mix_kernel(*refs[:n_in], *refs[n_in + 3:])


def _g_major(a, axis):
    shape = a.shape
    a = a.reshape(shape[:axis] + (KV_HEADS, GQA_GROUP, HEAD_DIM) + shape[axis + 1:])
    a = jnp.swapaxes(a, axis, axis + 1)
    return a.reshape(shape)


def _constants():
    lane = np.arange(ATT_WIDTH)
    g_of, h_of = lane // KV_WIDTH, (lane % KV_WIDTH) // HEAD_DIM
    head = h_of * GQA_GROUP + g_of
    seg = (head[:, None] == np.arange(KV_WIDTH)[None, :]).astype(np.float32)
    grp = np.arange(256) // HEAD_DIM
    bd = (grp[:, None] == grp[None, :]).astype(np.float32) / HEAD_DIM
    return jnp.asarray(seg, BF16), jnp.asarray(seg.T, BF16), jnp.asarray(bd, BF16)


def kernel(x_prompt, x_sample, cache_win_k, cache_win_v, cache_meta_k, cache_meta_v, state_hgrn, meta_tokens,
           norm_w, w_in, q_norm_w, k_norm_w, attn_sinks, hg_lb_logits, hg_norm_w, w_out):
    batch, seq, _ = x_prompt.shape
    n_seq = x_sample.shape[0]
    depth = w_in.shape[0]
    w_buf = cache_win_k.shape[2]
    assert x_sample.shape[1] == 1 and w_buf == WINDOW and seq % ROW_TILE == 0 and n_seq % SAMPLE_TILE == 0

    seg, expand, bd = _constants()
    w_in_b = jnp.concatenate([_g_major(w_in[:, :, C_Q:C_K], 2), w_in[:, :, C_K:C_GA],
                              _g_major(w_in[:, :, C_GA:C_QH], 2), w_in[:, :, C_QH:]], axis=2).astype(BF16)
    w_out_b = jnp.concatenate([_g_major(w_out[:, :ATT_WIDTH], 1), w_out[:, ATT_WIDTH:]], axis=1).astype(BF16)
    qnw = jnp.tile(q_norm_w, (1, ATT_HEADS)) * (HEAD_DIM ** -0.5)
    knw = jnp.tile(k_norm_w, (1, KV_HEADS))
    lb_logits = hg_lb_logits.astype(F32)
    sink_rows = jnp.pad(attn_sinks.astype(F32), ((0, 0), (0, KV_WIDTH - ATT_HEADS)))

    ck = cache_win_k.reshape(n_seq, depth, w_buf, KV_WIDTH)
    cv = cache_win_v.reshape(n_seq, depth, w_buf, KV_WIDTH)
    mk = cache_meta_k.reshape(n_seq, depth, N_META, KV_WIDTH)
    mv = cache_meta_v.reshape(n_seq, depth, N_META, KV_WIDTH)

    xp = x_prompt.reshape(batch * seq, D_MODEL)
    xs = jnp.concatenate([x_sample.reshape(n_seq, D_MODEL), meta_tokens.astype(F32)], axis=0)
    n_small = xs.shape[0]

    outs = {k: [] for k in ("wkp", "wvp", "mk", "mv", "hsp")}
    sample_stacked = None
    mix_s = mix_p = None
    for l in range(depth):
        layer_w = (norm_w[l][None], w_in_b[l], qnw[l][None], knw[l][None], lb_logits, bd)
        sinks = attn_sinks[l].astype(F32)
        hg_nw = hg_norm_w[l][None]

        sm = _project(l, xs, *layer_w, row_tile=n_small, act_dtype=F32, mix=mix_s)
        if mix_s is not None:
            xs, sm = sm[0], sm[1:]
        att_m, hg_m, s0t = _meta_mix(sinks, *sm, hg_nw)
        att_s, hg_s, sample_stacked = _sample_mix(l, *sm, hg_nw, sink_rows[l][None], seg, expand,
                                                  ck, cv, mk, mv, state_hgrn, n_seq, sample_stacked)
        kv_meta = sm[1][n_seq:]
        mix_s = (jnp.concatenate([att_s, att_m], axis=0), jnp.concatenate([hg_s, hg_m], axis=0), w_out_b[l])

        pr = _project(l, xp, *layer_w, row_tile=ROW_TILE, act_dtype=BF16, mix=mix_p)
        if mix_p is not None:
            xp, pr = pr[0], pr[1:]
        q, kv, sga, hq, hk, hv, g, sgh = pr
        att = _prompt_attention(sinks, q, kv, kv_meta, sga, batch, seq)
        hg, s_fin = _prompt_hgrn(hq, hk, hv, g, sgh, s0t, hg_nw, batch, seq)
        mix_p = (att, hg, w_out_b[l])

        kv3 = kv.reshape(batch, seq, 2 * KV_WIDTH)
        outs["wkp"].append(kv3[:, seq - w_buf:, :KV_WIDTH])
        outs["wvp"].append(kv3[:, seq - w_buf:, KV_WIDTH:])
        outs["mk"].append(jnp.broadcast_to(kv_meta[None, :, :KV_WIDTH], (batch, N_META, KV_WIDTH)))
        outs["mv"].append(jnp.broadcast_to(kv_meta[None, :, KV_WIDTH:], (batch, N_META, KV_WIDTH)))
        outs["hsp"].append(s_fin)
    xs = _out_project(mix_s[0], mix_s[1], xs, mix_s[2], n_small)
    xp = _out_project(mix_p[0], mix_p[1], xp, mix_p[2], ROW_TILE)

    stack = lambda name: jnp.stack(outs[name], axis=1)
    heads = lambda a: a.reshape(a.shape[:-1] + (KV_HEADS, HEAD_DIM))
    new_k, new_v, new_state = sample_stacked
    return (xp.reshape(batch, seq, D_MODEL), xs[:n_seq].reshape(n_seq, 1, D_MODEL),
            heads(stack("wkp")), heads(stack("wvp")), heads(stack("mk")), heads(stack("mv")), stack("hsp"),
            heads(new_k), heads(new_v), new_state)
```

```python
import functools

import numpy as np
import jax
import jax.numpy as jnp
from jax import lax
from jax.experimental import pallas as pl
from jax.experimental.pallas import tpu as pltpu

F32 = jnp.float32
BF16 = jnp.bfloat16

D_MODEL = 1024
N_META = 16
WINDOW = 128
HEAD_DIM = 64
ATT_WIDTH = 512
ATT_HEADS = 8
KV_HEADS = 2
GQA_GROUP = 4
KV_WIDTH = KV_HEADS * HEAD_DIM
HG_WIDTH = 512
HG_HEADS = 4
HG_KDIM = 128
HG_VDIM = 128
PROJ_WIDTH = 3328
EPS = 1e-6
NEG_BIG = -1e30
TINY = 1e-30
LOG2E = 1.4426950408889634

C_Q, C_K, C_V, C_GA, C_QH, C_FH, C_IH, C_GH = 0, 512, 640, 768, 1280, 1792, 2304, 2816

SUBLANES = 8
HG_CHUNK = 64
ROW_TILE = 512
HG_TILE = 512
ATT_TILE = 512
SAMPLE_TILE = 16
VMEM_LIMIT = 48 * 1024 * 1024


def _dot(a, b):
    return jnp.dot(a, b, preferred_element_type=F32)


def _dot_nt(a, b):
    return lax.dot_general(a, b, (((1,), (1,)), ((), ())), preferred_element_type=F32)


def _dot_tn(a, b):
    return lax.dot_general(a, b, (((0,), (0,)), ((), ())), preferred_element_type=F32)


def _silu(x):
    return x * (1.0 / (1.0 + jnp.exp(-x)))


def _row_parts(n_rows, n_parts):
    step = n_rows // n_parts
    return [slice(i * step, (i + 1) * step) for i in range(n_parts)]


def _mix_proj_kernel(layer, n_parts, att_ref, hg_ref, wo_ref, x_ref, *rest):
    parts = _row_parts(x_ref.shape[0], n_parts)
    xnew_ref = rest[6]
    xs = []
    for rs in parts:
        x = (x_ref[rs, :] + _dot(att_ref[rs, :].astype(BF16), wo_ref[0:ATT_WIDTH, :])
             + _dot(hg_ref[rs, :].astype(BF16), wo_ref[ATT_WIDTH:, :]))
        xnew_ref[rs, :] = x
        xs.append(x)
    _proj_body(layer, parts, xs, *rest[:6], *rest[7:])


def _proj_kernel(layer, n_parts, x_ref, *rest):
    parts = _row_parts(x_ref.shape[0], n_parts)
    _proj_body(layer, parts, [x_ref[rs, :] for rs in parts], *rest)


def _proj_body(layer, parts, xs, nw_ref, w_ref, qnw_ref, knw_ref, lbl_ref, bd_ref,
               q_ref, kv_ref, sga_ref, hq_ref, hk_ref, hv_ref, g_ref, sgh_ref):
    nw = nw_ref[...]
    hs = []
    for x in xs:
        ms = jnp.mean(x * x, axis=-1, keepdims=True)
        hs.append((x * lax.rsqrt(ms + EPS) * nw).astype(BF16))

    def proj(h, lo, hi):
        return _dot(h, w_ref[:, lo:hi])

    pq = [[proj(h, C_Q + 256 * c, C_Q + 256 * (c + 1)) for c in range(2)] for h in hs]
    pk = [proj(h, C_K, C_V) for h in hs]
    sq = [[(p * p).astype(BF16) for p in pqs + [pks]] for pqs, pks in zip(pq, pk)]

    logits = lbl_ref[...]
    e = jnp.exp(logits - jnp.max(logits, axis=0, keepdims=True))
    p = e / jnp.sum(e, axis=0, keepdims=True)
    lb = jnp.zeros((1, HG_WIDTH), F32)
    for j in range(1, layer + 1):
        lb = lb + p[j:j + 1, :]

    for rs, h in zip(parts, hs):
        kv_ref[rs, KV_WIDTH:2 * KV_WIDTH] = proj(h, C_V, C_GA)
        sga_ref[rs, :] = _silu(proj(h, C_GA, C_QH)).astype(sga_ref.dtype)
        hq_ref[rs, :] = _silu(proj(h, C_QH, C_FH)).astype(hq_ref.dtype)

        z = proj(h, C_FH, C_IH)
        ez = jnp.exp(-jnp.abs(z))
        r = 1.0 / (1.0 + ez)
        pos = z >= 0.0
        sig_pos = jnp.where(pos, r, ez * r)
        sig_neg = jnp.where(pos, ez * r, r)
        hk_ref[rs, :] = ((1.0 - lb) * sig_neg).astype(hk_ref.dtype)
        f = lb + (1.0 - lb) * sig_pos
        g_ref[rs, :] = jnp.log(jnp.maximum(f, TINY))

        hv_ref[rs, :] = proj(h, C_IH, C_GH).astype(hv_ref.dtype)
        sgh_ref[rs, :] = _silu(proj(h, C_GH, PROJ_WIDTH)).astype(sgh_ref.dtype)

    bd = bd_ref[...]
    for rs, pqs, pks, sqs in zip(parts, pq, pk, sq):
        q = jnp.concatenate([p * lax.rsqrt(_dot(s, bd) + EPS) for p, s in zip(pqs, sqs[:2])], axis=-1)
        q_ref[rs, :] = (q * qnw_ref[...]).astype(q_ref.dtype)
        kv_ref[rs, 0:KV_WIDTH] = pks * lax.rsqrt(_dot(sqs[2], bd[:KV_WIDTH, :KV_WIDTH]) + EPS) * knw_ref[...]


def _project(layer, x, norm_w, w_in, qnw, knw, lb_logits, bd, row_tile, act_dtype, mix=None):
    n = x.shape[0]
    rows = lambda width: pl.BlockSpec((row_tile, width), lambda i: (i, 0))
    full = lambda shape: pl.BlockSpec(shape, lambda i: (0,) * len(shape))
    out = lambda width, dtype: jax.ShapeDtypeStruct((n, width), dtype)
    operands = [x, norm_w, w_in, qnw, knw, lb_logits, bd]
    in_specs = [rows(D_MODEL), full((1, D_MODEL)), full((D_MODEL, PROJ_WIDTH)), full((1, ATT_WIDTH)),
                full((1, KV_WIDTH)), full(lb_logits.shape), full((256, 256))]
    out_specs = [rows(ATT_WIDTH), rows(2 * KV_WIDTH), rows(ATT_WIDTH), rows(HG_WIDTH), rows(HG_WIDTH),
                 rows(HG_WIDTH), rows(HG_WIDTH), rows(HG_WIDTH)]
    out_shape = [out(ATT_WIDTH, act_dtype), out(2 * KV_WIDTH, F32), out(ATT_WIDTH, act_dtype),
                 out(HG_WIDTH, act_dtype), out(HG_WIDTH, act_dtype), out(HG_WIDTH, act_dtype),
                 out(HG_WIDTH, F32), out(HG_WIDTH, act_dtype)]
    body = _proj_kernel
    if mix is not None:
        operands = list(mix) + operands
        in_specs = [rows(ATT_WIDTH), rows(HG_WIDTH), full((D_MODEL, D_MODEL))] + in_specs
        out_specs = [rows(D_MODEL)] + out_specs
        out_shape = [out(D_MODEL, F32)] + out_shape
        body = _mix_proj_kernel
    n_parts = 2 if row_tile % (2 * 128) == 0 else 1
    return pl.pallas_call(
        functools.partial(body, layer, n_parts),
        grid=(n // row_tile,),
        in_specs=in_specs, out_specs=out_specs, out_shape=out_shape,
        compiler_params=pltpu.CompilerParams(dimension_semantics=("arbitrary",), vmem_limit_bytes=VMEM_LIMIT),
        name="proj",
    )(*operands)


def _out_kernel(att_ref, hg_ref, x_ref, w_ref, y_ref):
    y_ref[...] = (x_ref[...] + _dot(att_ref[...].astype(BF16), w_ref[0:ATT_WIDTH, :])
                  + _dot(hg_ref[...].astype(BF16), w_ref[ATT_WIDTH:, :]))


def _out_project(att, hg, x, w_out, row_tile):
    n = x.shape[0]
    rows = lambda width: pl.BlockSpec((row_tile, width), lambda i: (i, 0))
    return pl.pallas_call(
        _out_kernel,
        grid=(n // row_tile,),
        in_specs=[rows(ATT_WIDTH), rows(HG_WIDTH), rows(D_MODEL),
                  pl.BlockSpec((D_MODEL, D_MODEL), lambda i: (0, 0))],
        out_specs=rows(D_MODEL),
        out_shape=jax.ShapeDtypeStruct((n, D_MODEL), F32),
        compiler_params=pltpu.CompilerParams(dimension_semantics=("arbitrary",), vmem_limit_bytes=VMEM_LIMIT),
        name="out_proj",
    )(att, hg, x, w_out)


def _split_heads(k):
    first = lax.broadcasted_iota(jnp.int32, k.shape, 1) < HEAD_DIM
    return (jnp.where(first, k, 0.0).astype(BF16), jnp.where(first, 0.0, k).astype(BF16))


def _attend(q, key_sets, sink_of):
    m = q.shape[0]
    first = lax.broadcasted_iota(jnp.int32, (m, KV_WIDTH), 1) < HEAD_DIM
    blocks = []
    for g in range(GQA_GROUP):
        qg = q[:, g * KV_WIDTH:(g + 1) * KV_WIDTH]
        per_head = []
        for h in range(KV_HEADS):
            scores = []
            for k_heads, _, mask in key_sets:
                s = _dot_nt(qg, k_heads[h])
                if mask is not None:
                    s = jnp.where(mask, s, NEG_BIG)
                scores.append(s)
            sink = sink_of(h, g)
            mx = jnp.max(scores[0], axis=-1, keepdims=True)
            for s in scores[1:]:
                mx = jnp.maximum(mx, jnp.max(s, axis=-1, keepdims=True))
            mx = jnp.maximum(mx, sink)
            den = jnp.exp(sink - mx)
            acc = jnp.zeros((m, KV_WIDTH), F32)
            for s, (_, v, _) in zip(scores, key_sets):
                p = jnp.exp(s - mx)
                den = den + jnp.sum(p, axis=-1, keepdims=True)
                acc = acc + _dot(p.astype(BF16), v)
            per_head.append(acc * (1.0 / den))
        blocks.append(jnp.where(first, per_head[0], per_head[1]))
    return jnp.concatenate(blocks, axis=-1)


def _prompt_attn_kernel(sink_ref, q_ref, kvc_ref, kvp_ref, kvm_ref, sga_ref, o_ref):
    i = pl.program_id(1)
    n_blocks = q_ref.shape[0] // WINDOW
    n_keys = 2 * WINDOW + N_META
    kvm = kvm_ref[...]
    k_blocks = [_split_heads(kvp_ref[:, :KV_WIDTH])]
    vt_blocks = [kvp_ref[:, KV_WIDTH:].T.astype(BF16)]
    for j in range(n_blocks):
        rows = slice(j * WINDOW, (j + 1) * WINDOW)
        k_blocks.append(_split_heads(kvc_ref[rows, :KV_WIDTH]))
        vt_blocks.append(kvc_ref[rows, KV_WIDTH:].T.astype(BF16))
    k_meta = _split_heads(kvm[:, :KV_WIDTH])
    v_meta = jnp.concatenate([kvm[:, KV_WIDTH:], jnp.zeros((WINDOW - N_META, KV_WIDTH), F32)], axis=0)
    vt_meta = v_meta.T.astype(BF16)
    pad = jnp.zeros((3 * WINDOW - n_keys, WINDOW), BF16)

    key = lax.broadcasted_iota(jnp.int32, (n_keys, WINDOW), 0)
    qi = lax.broadcasted_iota(jnp.int32, (n_keys, WINDOW), 1)
    band = jnp.where(key < WINDOW, key - qi - 1, jnp.where(key < 2 * WINDOW, qi - (key - WINDOW), 0))
    visible = band >= 0
    no_prev = jnp.where(i > 0, 0, 2 * WINDOW)
    visible_first = jnp.where(key < WINDOW, band - no_prev, band) >= 0
    head0_rows = lax.broadcasted_iota(jnp.int32, (KV_WIDTH, WINDOW), 0) < HEAD_DIM
    group_lanes = [slice(g * KV_WIDTH, (g + 1) * KV_WIDTH) for g in range(GQA_GROUP)]

    s_both = []
    for j in range(n_blocks):
        rows = slice(j * WINDOW, (j + 1) * WINDOW)
        kk = jnp.concatenate([k_blocks[j][0], k_blocks[j + 1][0], k_meta[0],
                              k_blocks[j][1], k_blocks[j + 1][1], k_meta[1]], axis=0)
        s_both.append([_dot_nt(kk, q_ref[rows, lanes]) for lanes in group_lanes])
    probs = []
    for j in range(n_blocks):
        vis = visible_first if j == 0 else visible
        for g in range(GQA_GROUP):
            for h in range(KV_HEADS):
                sink = sink_ref[h * GQA_GROUP + g]
                s = jnp.where(vis, s_both[j][g][h * n_keys:(h + 1) * n_keys], NEG_BIG)
                mx = jnp.maximum(jnp.max(s, axis=0, keepdims=True), sink)
                p = jnp.exp(s - mx)
                den = jnp.sum(p, axis=0, keepdims=True) + jnp.exp(sink - mx)
                probs.append((jnp.concatenate([p.astype(BF16), pad], axis=0), 1.0 / den))
    outs = []
    for j in range(n_blocks):
        v_t = jnp.concatenate([vt_blocks[j], vt_blocks[j + 1], vt_meta], axis=1)
        for p_pad, inv in probs[j * ATT_HEADS:(j + 1) * ATT_HEADS]:
            outs.append(_dot(v_t, p_pad) * inv)
    for j in range(n_blocks):
        rows = slice(j * WINDOW, (j + 1) * WINDOW)
        for g, lanes in enumerate(group_lanes):
            pair = outs[j * ATT_HEADS + g * KV_HEADS:j * ATT_HEADS + (g + 1) * KV_HEADS]
            o_t = jnp.where(head0_rows, pair[0], pair[1])
            o_ref[rows, lanes] = (o_t.T * sga_ref[rows, lanes].astype(F32)).astype(o_ref.dtype)


def _prompt_attention(sinks, q, kv, kv_meta, sga, batch, seq):
    nb = seq // ATT_TILE
    per = ATT_TILE // WINDOW
    blk = lambda width: pl.BlockSpec((ATT_TILE, width), lambda b, i, s: (b * nb + i, 0))
    prev = pl.BlockSpec((WINDOW, 2 * KV_WIDTH), lambda b, i, s: ((b * nb + i) * per - jnp.minimum(i, 1), 0))
    meta = pl.BlockSpec((N_META, 2 * KV_WIDTH), lambda b, i, s: (0, 0))
    return pl.pallas_call(
        _prompt_attn_kernel,
        grid_spec=pltpu.PrefetchScalarGridSpec(
            num_scalar_prefetch=1, grid=(batch, nb),
            in_specs=[blk(ATT_WIDTH), blk(2 * KV_WIDTH), prev, meta, blk(ATT_WIDTH)],
            out_specs=blk(ATT_WIDTH)),
        out_shape=jax.ShapeDtypeStruct((batch * seq, ATT_WIDTH), BF16),
        compiler_params=pltpu.CompilerParams(dimension_semantics=("arbitrary", "arbitrary"),
                                             vmem_limit_bytes=VMEM_LIMIT),
        name="prompt_attn",
    )(sinks, q, kv, kv, kv_meta, sga)


def _level_sizes(c):
    return [c >> (i + 1) for i in range(c.bit_length() - 1)]


def _decay_sum_matrix(c):
    t = np.arange(c)[:, None]
    r = np.arange(c)[None, :]
    mats = [r <= t]
    for bs in _level_sizes(c):
        if bs < SUBLANES:
            a = (t // (2 * bs)) * (2 * bs) + bs - 1
            mats.append(((r > t) & (r <= a)) | ((r > a) & (r <= t)))
    return np.tile(np.concatenate(mats, axis=0).astype(np.float32), (1, 3))


def _decay_sums(g, dmat):
    g = g * LOG2E
    g1 = g.astype(BF16)
    r1 = g - g1.astype(F32)
    g2 = r1.astype(BF16)
    g3 = (r1 - g2.astype(F32)).astype(BF16)
    return _dot(dmat, jnp.concatenate([g1, g2, g3], axis=0))


def _chunk_exponents(block, c):
    start = block(0)
    grp = lambda j: start[j * SUBLANES:(j + 1) * SUBLANES, :]
    lasts = {}

    def last(j):
        if j not in lasts:
            lasts[j] = jnp.broadcast_to(start[j * SUBLANES + SUBLANES - 1:(j + 1) * SUBLANES, :], (SUBLANES, HG_KDIM))
        return lasts[j]

    n_grp = c // SUBLANES
    end = jnp.concatenate([last(n_grp - 1) - grp(j) for j in range(n_grp)], axis=0)
    levels, fine = [], 1
    for bs in _level_sizes(c):
        if bs >= SUBLANES:
            per = bs // SUBLANES
            parts = []
            for j in range(n_grp):
                blk = j // per
                anchor = last((blk // 2) * 2 * per + per - 1)
                parts.append(grp(j) - anchor if blk % 2 == 1 else anchor - grp(j))
            levels.append(jnp.concatenate(parts, axis=0))
        else:
            levels.append(block(fine))
            fine += 1
    return start, end, levels


def _hgrn_masks(c):
    row = lax.broadcasted_iota(jnp.int32, (c, c), 0)
    col = lax.broadcasted_iota(jnp.int32, (c, c), 1)
    row_k = lax.broadcasted_iota(jnp.int32, (c, HG_KDIM), 0)
    levels = []
    for bs in _level_sizes(c):
        rb, cb = row // bs, col // bs
        pairs = ((rb % 2) * (1 - jnp.abs(cb - rb + 1))) > 0
        levels.append((bs, (row_k // bs) % 2 == 1, pairs))
    return levels, row == col


def _hgrn_chunks(items, states, masks):
    levels, diag = masks
    c = items[0][1].shape[0]

    stage1 = []
    for _, q, k, v, (_, to_end, level_ex) in items:
        pair_scores = []
        for (bs, q_side, _), lex in zip(levels, level_ex):
            if bs >= SUBLANES:
                side = jnp.concatenate(
                    [(q if (j // bs) % 2 == 1 else k)[j:j + SUBLANES] for j in range(0, c, SUBLANES)], axis=0)
            else:
                side = jnp.where(q_side, q, k)
            u = (side * jnp.exp2(lex)).astype(BF16)
            pair_scores.append(_dot_nt(u, u))
        kdec = (k * jnp.exp2(to_end)).astype(BF16)
        stage1.append((pair_scores, _dot_tn(v, kdec)))

    intra = []
    for (_, q, k, v, _), (pair_scores, _) in zip(items, stage1):
        a = jnp.where(diag, jnp.sum(q * k, axis=-1, keepdims=True), 0.0)
        for (_, _, pairs), scores in zip(levels, pair_scores):
            a = jnp.where(pairs, scores, a)
        intra.append(_dot(a.astype(BF16), v))

    outs = []
    for (head, q, _, _, (from_start, _, _)), (_, increment), o_intra in zip(items, stage1, intra):
        st = states[head]
        decay = jnp.exp2(from_start)
        outs.append(o_intra + _dot_nt((q * decay).astype(BF16), st.astype(BF16)))
        states[head] = st * decay[c - 1:c, :] + increment
    return outs


def _head_norm_gate(o, nw, gate):
    ms = jnp.mean(o * o, axis=-1, keepdims=True)
    return o * lax.rsqrt(ms + EPS) * nw * gate


def _prompt_hgrn_kernel(hq_ref, hk_ref, hv_ref, g_ref, sgh_ref, s0_ref, nw_ref, dmat_ref, o_ref, sfin_ref, st_ref):
    j = pl.program_id(1)

    @pl.when(j == 0)
    def _():
        st_ref[...] = s0_ref[...]

    masks = _hgrn_masks(HG_CHUNK)
    nw = nw_ref[...]
    dmat = dmat_ref[...]

    def block_of(sums, lanes):
        return lambda i: sums[i * HG_CHUNK:(i + 1) * HG_CHUNK, lanes]

    items, where = [], []
    for ci in range(HG_TILE // HG_CHUNK):
        rows = slice(ci * HG_CHUNK, (ci + 1) * HG_CHUNK)
        sums = _decay_sums(g_ref[rows, :], dmat)
        for h in range(HG_HEADS):
            lanes = slice(h * HG_KDIM, (h + 1) * HG_KDIM)
            items.append((h, hq_ref[rows, lanes].astype(F32), hk_ref[rows, lanes].astype(F32),
                          hv_ref[rows, lanes], _chunk_exponents(block_of(sums, lanes), HG_CHUNK)))
            where.append((rows, lanes))
    states = [st_ref[h] for h in range(HG_HEADS)]
    outs = _hgrn_chunks(items, states, masks)
    for (rows, lanes), o in zip(where, outs):
        o_ref[rows, lanes] = _head_norm_gate(o, nw, sgh_ref[rows, lanes].astype(F32)).astype(o_ref.dtype)
    for h in range(HG_HEADS):
        st_ref[h] = states[h]

    @pl.when(j == pl.num_programs(1) - 1)
    def _():
        for h in range(HG_HEADS):
            sfin_ref[h] = st_ref[h].T


def _prompt_hgrn(hq, hk, hv, g, sgh, s0t, hg_nw, batch, seq):
    nt = seq // HG_TILE
    blk = pl.BlockSpec((HG_TILE, HG_WIDTH), lambda b, j: (b * nt + j, 0))
    state_shape = (HG_HEADS, HG_VDIM, HG_KDIM)
    dmat = jnp.asarray(_decay_sum_matrix(HG_CHUNK), BF16)
    return pl.pallas_call(
        _prompt_hgrn_kernel,
        grid=(batch, nt),
        in_specs=[blk, blk, blk, blk, blk, pl.BlockSpec(state_shape, lambda b, j: (0, 0, 0)),
                  pl.BlockSpec((1, HG_VDIM), lambda b, j: (0, 0)), pl.BlockSpec(dmat.shape, lambda b, j: (0, 0))],
        out_specs=[blk, pl.BlockSpec((None,) + state_shape, lambda b, j: (b, 0, 0, 0))],
        out_shape=[jax.ShapeDtypeStruct((batch * seq, HG_WIDTH), BF16),
                   jax.ShapeDtypeStruct((batch,) + state_shape, F32)],
        scratch_shapes=[pltpu.VMEM(state_shape, F32)],
        compiler_params=pltpu.CompilerParams(dimension_semantics=("arbitrary", "arbitrary"),
                                             vmem_limit_bytes=VMEM_LIMIT),
        name="prompt_hgrn",
    )(hq, hk, hv, g, sgh, s0t, hg_nw, dmat)


def _meta_mix_kernel(sink_ref, q_ref, kv_ref, sga_ref, hq_ref, hk_ref, hv_ref, g_ref, sgh_ref, nw_ref, dmat_ref,
                     att_ref, hg_ref, st_ref):
    kv = kv_ref[...]
    row = lax.broadcasted_iota(jnp.int32, (N_META, N_META), 0)
    col = lax.broadcasted_iota(jnp.int32, (N_META, N_META), 1)
    att = _attend(q_ref[...].astype(BF16),
                  [(_split_heads(kv[:, :KV_WIDTH]), kv[:, KV_WIDTH:].astype(BF16), col <= row)],
                  lambda h, g: sink_ref[h * GQA_GROUP + g])
    att_ref[...] = (att * sga_ref[...]).astype(att_ref.dtype)

    sums = _decay_sums(g_ref[...], dmat_ref[...])
    masks = _hgrn_masks(N_META)
    head_lanes = [slice(h * HG_KDIM, (h + 1) * HG_KDIM) for h in range(HG_HEADS)]
    items = [(h, hq_ref[:, lanes], hk_ref[:, lanes], hv_ref[:, lanes].astype(BF16),
              _chunk_exponents((lambda lanes: lambda i: sums[i * N_META:(i + 1) * N_META, lanes])(lanes), N_META))
             for h, lanes in enumerate(head_lanes)]
    states = [jnp.zeros((HG_VDIM, HG_KDIM), F32) for _ in range(HG_HEADS)]
    outs = _hgrn_chunks(items, states, masks)
    for h, lanes in enumerate(head_lanes):
        st_ref[h] = states[h]
        hg_ref[:, lanes] = _head_norm_gate(outs[h], nw_ref[...], sgh_ref[:, lanes]).astype(hg_ref.dtype)


def _meta_mix(sinks, q, kv, sga, hq, hk, hv, g, sgh, hg_nw):
    blk_idx = q.shape[0] // N_META - 1
    blk = lambda width: pl.BlockSpec((N_META, width), lambda i, s: (blk_idx, 0))
    first = lambda width: pl.BlockSpec((N_META, width), lambda i, s: (0, 0))
    dmat = jnp.asarray(_decay_sum_matrix(N_META), BF16)
    return pl.pallas_call(
        _meta_mix_kernel,
        grid_spec=pltpu.PrefetchScalarGridSpec(
            num_scalar_prefetch=1, grid=(1,),
            in_specs=[blk(ATT_WIDTH), blk(2 * KV_WIDTH), blk(ATT_WIDTH), blk(HG_WIDTH), blk(HG_WIDTH),
                      blk(HG_WIDTH), blk(HG_WIDTH), blk(HG_WIDTH), pl.BlockSpec((1, HG_VDIM), lambda i, s: (0, 0)),
                      pl.BlockSpec(dmat.shape, lambda i, s: (0, 0))],
            out_specs=[first(ATT_WIDTH), first(HG_WIDTH),
                       pl.BlockSpec((HG_HEADS, HG_VDIM, HG_KDIM), lambda i, s: (0, 0, 0))]),
        out_shape=[jax.ShapeDtypeStruct((N_META, ATT_WIDTH), F32),
                   jax.ShapeDtypeStruct((N_META, HG_WIDTH), F32),
                   jax.ShapeDtypeStruct((HG_HEADS, HG_VDIM, HG_KDIM), F32)],
        compiler_params=pltpu.CompilerParams(dimension_semantics=("arbitrary",)),
        name="meta_mix",
    )(sinks, q, kv, sga, hq, hk, hv, g, sgh, hg_nw, dmat)


def _sample_mix_kernel(q_ref, kv_ref, sga_ref, hq_ref, hk_ref, hv_ref, g_ref, sgh_ref, nw_ref,
                       sink_ref, seg_ref, exp_ref, ck_ref, cv_ref, mk_ref, mv_ref, s_ref,
                       att_ref, hg_ref, nk_ref, nv_ref, ns_ref):
    n_keys = WINDOW + N_META + SUBLANES
    key_row = lax.broadcasted_iota(jnp.int32, (n_keys, KV_WIDTH), 0)
    visible = jnp.logical_and(key_row >= 1, key_row <= WINDOW + N_META)
    win_row = lax.broadcasted_iota(jnp.int32, (WINDOW, KV_WIDTH), 0)
    sink = sink_ref[...]
    seg = seg_ref[...]
    expand = exp_ref[...]
    nw = nw_ref[...]

    def per_seq(b, carry):
        kv_new = kv_ref[b]
        k_new = kv_new[:, 0:KV_WIDTH]
        v_new = kv_new[:, KV_WIDTH:2 * KV_WIDTH]
        ck = ck_ref[b]
        cv = cv_ref[b]
        keys = jnp.concatenate([ck, mk_ref[b], jnp.broadcast_to(k_new, (SUBLANES, KV_WIDTH))], axis=0)
        vals = jnp.concatenate([cv, mv_ref[b], jnp.broadcast_to(v_new, (SUBLANES, KV_WIDTH))], axis=0)
        prod = jnp.concatenate([keys] * GQA_GROUP, axis=1) * q_ref[b]
        s = jnp.where(visible, _dot(prod.astype(BF16), seg), NEG_BIG)
        mx = jnp.maximum(jnp.max(s, axis=0, keepdims=True), sink)
        p = jnp.exp(s - mx)
        den = jnp.sum(p, axis=0, keepdims=True) + jnp.exp(sink - mx)
        p = p * (1.0 / den)
        wide = _dot(p.astype(BF16), expand)
        att = jnp.sum(wide * jnp.concatenate([vals] * GQA_GROUP, axis=1), axis=0, keepdims=True)
        att_ref[b] = att * sga_ref[b]

        nk_ref[b] = jnp.where(win_row == WINDOW - 1, k_new, pltpu.roll(ck, WINDOW - 1, 0))
        nv_ref[b] = jnp.where(win_row == WINDOW - 1, v_new, pltpu.roll(cv, WINDOW - 1, 0))

        g_row, hq_row, hk_row, hv_row, sgh_row = g_ref[b], hq_ref[b], hk_ref[b], hv_ref[b], sgh_ref[b]
        outs = []
        for h in range(HG_HEADS):
            lanes = slice(h * HG_KDIM, (h + 1) * HG_KDIM)
            col = lambda r: jnp.broadcast_to(r[:, lanes], (HG_KDIM, HG_KDIM)).T
            s1 = jnp.exp(col(g_row)) * s_ref[b, h] + col(hk_row) * hv_row[:, lanes]
            ns_ref[b, h] = s1
            o = jnp.sum(col(hq_row) * s1, axis=0, keepdims=True)
            outs.append(_head_norm_gate(o, nw, sgh_row[:, lanes]))
        hg_ref[b] = jnp.concatenate(outs, axis=-1)
        return carry

    lax.fori_loop(0, SAMPLE_TILE, per_seq, 0)


def _sample_mix(layer, q, kv, sga, hq, hk, hv, g, sgh, hg_nw, sink_row, seg, expand,
                cache_k, cache_v, meta_k, meta_v, state, n_seq, stacked):
    t = SAMPLE_TILE
    depth = state.shape[1]
    as_rows = lambda a: a[:n_seq].reshape(n_seq, 1, a.shape[-1])
    rows = lambda width: pl.BlockSpec((t, 1, width), lambda i: (i, 0, 0))
    full = lambda shape: pl.BlockSpec(shape, lambda i: (0,) * len(shape))
    cache = lambda n: pl.BlockSpec((t, None, n, KV_WIDTH), lambda i: (i, layer, 0, 0))
    state_blk = pl.BlockSpec((t, None, HG_HEADS, HG_KDIM, HG_VDIM), lambda i: (i, layer, 0, 0, 0))
    operands = [as_rows(a) for a in (q, kv, sga, hq, hk, hv, g, sgh)] + [
        hg_nw, sink_row, seg, expand, cache_k, cache_v, meta_k, meta_v, state]
    in_specs = [rows(ATT_WIDTH), rows(2 * KV_WIDTH), rows(ATT_WIDTH), rows(HG_WIDTH), rows(HG_WIDTH),
                rows(HG_WIDTH), rows(HG_WIDTH), rows(HG_WIDTH), full((1, HG_VDIM)),
                full((1, KV_WIDTH)), full(seg.shape), full(expand.shape),
                cache(WINDOW), cache(WINDOW), cache(N_META), cache(N_META), state_blk]
    aliases = {}
    kernel_fn = _sample_mix_kernel
    if stacked is not None:
        aliases = {len(operands) + i: 2 + i for i in range(3)}
        operands = operands + list(stacked)
        in_specs = in_specs + [pl.BlockSpec(memory_space=pl.ANY)] * 3
        kernel_fn = _sample_mix_kernel_aliased
    att, hg, nk, nv, ns = pl.pallas_call(
        kernel_fn,
        grid=(n_seq // t,),
        in_specs=in_specs,
        out_specs=[rows(ATT_WIDTH), rows(HG_WIDTH), cache(WINDOW), cache(WINDOW), state_blk],
        out_shape=[jax.ShapeDtypeStruct((n_seq, 1, ATT_WIDTH), F32),
                   jax.ShapeDtypeStruct((n_seq, 1, HG_WIDTH), F32),
                   jax.ShapeDtypeStruct((n_seq, depth, WINDOW, KV_WIDTH), F32),
                   jax.ShapeDtypeStruct((n_seq, depth, WINDOW, KV_WIDTH), F32),
                   jax.ShapeDtypeStruct((n_seq, depth, HG_HEADS, HG_KDIM, HG_VDIM), F32)],
        input_output_aliases=aliases,
        compiler_params=pltpu.CompilerParams(dimension_semantics=("arbitrary",), vmem_limit_bytes=VMEM_LIMIT),
        name="sample_mix",
    )(*operands)
    return att.reshape(n_seq, ATT_WIDTH), hg.reshape(n_seq, HG_WIDTH), (nk, nv, ns)


def _sample_mix_kernel_aliased(*refs):
    n_in = 17
    _sample_mix_kernel(*refs[:n_in], *refs[n_in + 3:])


def _g_major(a, axis):
    shape = a.shape
    a = a.reshape(shape[:axis] + (KV_HEADS, GQA_GROUP, HEAD_DIM) + shape[axis + 1:])
    a = jnp.swapaxes(a, axis, axis + 1)
    return a.reshape(shape)


def _constants():
    lane = np.arange(ATT_WIDTH)
    g_of, h_of = lane // KV_WIDTH, (lane % KV_WIDTH) // HEAD_DIM
    head = h_of * GQA_GROUP + g_of
    seg = (head[:, None] == np.arange(KV_WIDTH)[None, :]).astype(np.float32)
    grp = np.arange(256) // HEAD_DIM
    bd = (grp[:, None] == grp[None, :]).astype(np.float32) / HEAD_DIM
    return jnp.asarray(seg, BF16), jnp.asarray(seg.T, BF16), jnp.asarray(bd, BF16)


def kernel(x_prompt, x_sample, cache_win_k, cache_win_v, cache_meta_k, cache_meta_v, state_hgrn, meta_tokens,
           norm_w, w_in, q_norm_w, k_norm_w, attn_sinks, hg_lb_logits, hg_norm_w, w_out):
    batch, seq, _ = x_prompt.shape
    n_seq = x_sample.shape[0]
    depth = w_in.shape[0]
    w_buf = cache_win_k.shape[2]
    assert x_sample.shape[1] == 1 and w_buf == WINDOW and seq % ROW_TILE == 0 and n_seq % SAMPLE_TILE == 0

    seg, expand, bd = _constants()
    w_in_b = jnp.concatenate([_g_major(w_in[:, :, C_Q:C_K], 2), w_in[:, :, C_K:C_GA],
                              _g_major(w_in[:, :, C_GA:C_QH], 2), w_in[:, :, C_QH:]], axis=2).astype(BF16)
    w_out_b = jnp.concatenate([_g_major(w_out[:, :ATT_WIDTH], 1), w_out[:, ATT_WIDTH:]], axis=1).astype(BF16)
    qnw = jnp.tile(q_norm_w, (1, ATT_HEADS)) * (HEAD_DIM ** -0.5)
    knw = jnp.tile(k_norm_w, (1, KV_HEADS))
    lb_logits = hg_lb_logits.astype(F32)
    sink_rows = jnp.pad(attn_sinks.astype(F32), ((0, 0), (0, KV_WIDTH - ATT_HEADS)))

    ck = cache_win_k.reshape(n_seq, depth, w_buf, KV_WIDTH)
    cv = cache_win_v.reshape(n_seq, depth, w_buf, KV_WIDTH)
    mk = cache_meta_k.reshape(n_seq, depth, N_META, KV_WIDTH)
    mv = cache_meta_v.reshape(n_seq, depth, N_META, KV_WIDTH)

    xp = x_prompt.reshape(batch * seq, D_MODEL)
    xs = jnp.concatenate([x_sample.reshape(n_seq, D_MODEL), meta_tokens.astype(F32)], axis=0)
    n_small = xs.shape[0]

    outs = {k: [] for k in ("wkp", "wvp", "mk", "mv", "hsp")}
    sample_stacked = None
    mix_s = mix_p = None
    for l in range(depth):
        layer_w = (norm_w[l][None], w_in_b[l], qnw[l][None], knw[l][None], lb_logits, bd)
        sinks = attn_sinks[l].astype(F32)
        hg_nw = hg_norm_w[l][None]

        sm = _project(l, xs, *layer_w, row_tile=n_small, act_dtype=F32, mix=mix_s)
        if mix_s is not None:
            xs, sm = sm[0], sm[1:]
        att_m, hg_m, s0t = _meta_mix(sinks, *sm, hg_nw)
        att_s, hg_s, sample_stacked = _sample_mix(l, *sm, hg_nw, sink_rows[l][None], seg, expand,
                                                  ck, cv, mk, mv, state_hgrn, n_seq, sample_stacked)
        kv_meta = sm[1][n_seq:]
        mix_s = (jnp.concatenate([att_s, att_m], axis=0), jnp.concatenate([hg_s, hg_m], axis=0), w_out_b[l])

        pr = _project(l, xp, *layer_w, row_tile=ROW_TILE, act_dtype=BF16, mix=mix_p)
        if mix_p is not None:
            xp, pr = pr[0], pr[1:]
        q, kv, sga, hq, hk, hv, g, sgh = pr
        att = _prompt_attention(sinks, q, kv, kv_meta, sga, batch, seq)
        hg, s_fin = _prompt_hgrn(hq, hk, hv, g, sgh, s0t, hg_nw, batch, seq)
        mix_p = (att, hg, w_out_b[l])

        kv3 = kv.reshape(batch, seq, 2 * KV_WIDTH)
        outs["wkp"].append(kv3[:, seq - w_buf:, :KV_WIDTH])
        outs["wvp"].append(kv3[:, seq - w_buf:, KV_WIDTH:])
        outs["mk"].append(jnp.broadcast_to(kv_meta[None, :, :KV_WIDTH], (batch, N_META, KV_WIDTH)))
        outs["mv"].append(jnp.broadcast_to(kv_meta[None, :, KV_WIDTH:], (batch, N_META, KV_WIDTH)))
        outs["hsp"].append(s_fin)
    xs = _out_project(mix_s[0], mix_s[1], xs, mix_s[2], n_small)
    xp = _out_project(mix_p[0], mix_p[1], xp, mix_p[2], ROW_TILE)

    stack = lambda name: jnp.stack(outs[name], axis=1)
    heads = lambda a: a.reshape(a.shape[:-1] + (KV_HEADS, HEAD_DIM))
    new_k, new_v, new_state = sample_stacked
    return (xp.reshape(batch, seq, D_MODEL), xs[:n_seq].reshape(n_seq, 1, D_MODEL),
            heads(stack("wkp")), heads(stack("wvp")), heads(stack("mk")), heads(stack("mv")), stack("hsp"),
            heads(new_k), heads(new_v), new_state)
```

```python
import functools

import numpy as np
import jax
import jax.numpy as jnp
from jax import lax
from jax.experimental import pallas as pl
from jax.experimental.pallas import tpu as pltpu

F32 = jnp.float32
BF16 = jnp.bfloat16

D_MODEL = 1024
N_META = 16
WINDOW = 128
HEAD_DIM = 64
ATT_WIDTH = 512
ATT_HEADS = 8
KV_HEADS = 2
GQA_GROUP = 4
KV_WIDTH = KV_HEADS * HEAD_DIM
HG_WIDTH = 512
HG_HEADS = 4
HG_KDIM = 128
HG_VDIM = 128
PROJ_WIDTH = 3328
EPS = 1e-6
NEG_BIG = -1e30
TINY = 1e-30
LOG2E = 1.4426950408889634

C_Q, C_K, C_V, C_GA, C_QH, C_FH, C_IH, C_GH = 0, 512, 640, 768, 1280, 1792, 2304, 2816

SUBLANES = 8
HG_CHUNK = 64
ROW_TILE = 512
HG_TILE = 512
ATT_TILE = 512
SAMPLE_TILE = 8
VMEM_LIMIT = 48 * 1024 * 1024


def _dot(a, b):
    return jnp.dot(a, b, preferred_element_type=F32)


def _dot_nt(a, b):
    return lax.dot_general(a, b, (((1,), (1,)), ((), ())), preferred_element_type=F32)


def _dot_tn(a, b):
    return lax.dot_general(a, b, (((0,), (0,)), ((), ())), preferred_element_type=F32)


def _silu(x):
    return x * (1.0 / (1.0 + jnp.exp(-x)))


def _row_parts(n_rows, n_parts):
    step = n_rows // n_parts
    return [slice(i * step, (i + 1) * step) for i in range(n_parts)]


def _mix_proj_kernel(layer, n_parts, att_ref, hg_ref, wo_ref, x_ref, *rest):
    parts = _row_parts(x_ref.shape[0], n_parts)
    xnew_ref = rest[6]
    xs = []
    for rs in parts:
        x = (x_ref[rs, :] + _dot(att_ref[rs, :].astype(BF16), wo_ref[0:ATT_WIDTH, :])
             + _dot(hg_ref[rs, :].astype(BF16), wo_ref[ATT_WIDTH:, :]))
        xnew_ref[rs, :] = x
        xs.append(x)
    _proj_body(layer, parts, xs, *rest[:6], *rest[7:])


def _proj_kernel(layer, n_parts, x_ref, *rest):
    parts = _row_parts(x_ref.shape[0], n_parts)
    _proj_body(layer, parts, [x_ref[rs, :] for rs in parts], *rest)


def _proj_body(layer, parts, xs, nw_ref, w_ref, qnw_ref, knw_ref, lbl_ref, bd_ref,
               q_ref, kv_ref, sga_ref, hq_ref, hk_ref, hv_ref, g_ref, sgh_ref):
    nw = nw_ref[...]
    hs = []
    for x in xs:
        ms = jnp.mean(x * x, axis=-1, keepdims=True)
        hs.append((x * lax.rsqrt(ms + EPS) * nw).astype(BF16))

    def proj(h, lo, hi):
        return _dot(h, w_ref[:, lo:hi])

    pq = [[proj(h, C_Q + 256 * c, C_Q + 256 * (c + 1)) for c in range(2)] for h in hs]
    pk = [proj(h, C_K, C_V) for h in hs]
    sq = [[(p * p).astype(BF16) for p in pqs + [pks]] for pqs, pks in zip(pq, pk)]

    logits = lbl_ref[...]
    e = jnp.exp(logits - jnp.max(logits, axis=0, keepdims=True))
    p = e / jnp.sum(e, axis=0, keepdims=True)
    depth_row = lax.broadcasted_iota(jnp.int32, logits.shape, 0)
    in_range = jnp.where(depth_row >= 1, layer - depth_row, -1) >= 0
    lb = jnp.sum(jnp.where(in_range, p, 0.0), axis=0, keepdims=True)

    for rs, h in zip(parts, hs):
        kv_ref[rs, KV_WIDTH:2 * KV_WIDTH] = proj(h, C_V, C_GA)
        sga_ref[rs, :] = _silu(proj(h, C_GA, C_QH)).astype(sga_ref.dtype)
        hq_ref[rs, :] = _silu(proj(h, C_QH, C_FH)).astype(hq_ref.dtype)

        z = proj(h, C_FH, C_IH)
        ez = jnp.exp(-jnp.abs(z))
        r = 1.0 / (1.0 + ez)
        pos = z >= 0.0
        sig_pos = jnp.where(pos, r, ez * r)
        sig_neg = jnp.where(pos, ez * r, r)
        hk_ref[rs, :] = ((1.0 - lb) * sig_neg).astype(hk_ref.dtype)
        f = lb + (1.0 - lb) * sig_pos
        g_ref[rs, :] = jnp.log(jnp.maximum(f, TINY))

        hv_ref[rs, :] = proj(h, C_IH, C_GH).astype(hv_ref.dtype)
        sgh_ref[rs, :] = _silu(proj(h, C_GH, PROJ_WIDTH)).astype(sgh_ref.dtype)

    bd = bd_ref[...]
    for rs, pqs, pks, sqs in zip(parts, pq, pk, sq):
        q = jnp.concatenate([p * lax.rsqrt(_dot(s, bd) + EPS) for p, s in zip(pqs, sqs[:2])], axis=-1)
        q_ref[rs, :] = (q * qnw_ref[...]).astype(q_ref.dtype)
        kv_ref[rs, 0:KV_WIDTH] = pks * lax.rsqrt(_dot(sqs[2], bd[:KV_WIDTH, :KV_WIDTH]) + EPS) * knw_ref[...]


def _project(layer, x, norm_w, w_in, qnw, knw, lb_logits, bd, row_tile, act_dtype, mix=None):
    n = x.shape[0]
    rows = lambda width: pl.BlockSpec((row_tile, width), lambda i: (i, 0))
    full = lambda shape: pl.BlockSpec(shape, lambda i: (0,) * len(shape))
    out = lambda width, dtype: jax.ShapeDtypeStruct((n, width), dtype)
    operands = [x, norm_w, w_in, qnw, knw, lb_logits, bd]
    in_specs = [rows(D_MODEL), full((1, D_MODEL)), full((D_MODEL, PROJ_WIDTH)), full((1, ATT_WIDTH)),
                full((1, KV_WIDTH)), full(lb_logits.shape), full((256, 256))]
    out_specs = [rows(ATT_WIDTH), rows(2 * KV_WIDTH), rows(ATT_WIDTH), rows(HG_WIDTH), rows(HG_WIDTH),
                 rows(HG_WIDTH), rows(HG_WIDTH), rows(HG_WIDTH)]
    out_shape = [out(ATT_WIDTH, act_dtype), out(2 * KV_WIDTH, F32), out(ATT_WIDTH, act_dtype),
                 out(HG_WIDTH, act_dtype), out(HG_WIDTH, act_dtype), out(HG_WIDTH, act_dtype),
                 out(HG_WIDTH, F32), out(HG_WIDTH, act_dtype)]
    body = _proj_kernel
    if mix is not None:
        operands = list(mix) + operands
        in_specs = [rows(ATT_WIDTH), rows(HG_WIDTH), full((D_MODEL, D_MODEL))] + in_specs
        out_specs = [rows(D_MODEL)] + out_specs
        out_shape = [out(D_MODEL, F32)] + out_shape
        body = _mix_proj_kernel
    n_parts = 2 if row_tile % (2 * 128) == 0 else 1
    return pl.pallas_call(
        functools.partial(body, layer, n_parts),
        grid=(n // row_tile,),
        in_specs=in_specs, out_specs=out_specs, out_shape=out_shape,
        compiler_params=pltpu.CompilerParams(dimension_semantics=("arbitrary",), vmem_limit_bytes=VMEM_LIMIT),
        name="proj",
    )(*operands)


def _out_kernel(att_ref, hg_ref, x_ref, w_ref, y_ref):
    y_ref[...] = (x_ref[...] + _dot(att_ref[...].astype(BF16), w_ref[0:ATT_WIDTH, :])
                  + _dot(hg_ref[...].astype(BF16), w_ref[ATT_WIDTH:, :]))


def _out_project(att, hg, x, w_out, row_tile):
    n = x.shape[0]
    rows = lambda width: pl.BlockSpec((row_tile, width), lambda i: (i, 0))
    return pl.pallas_call(
        _out_kernel,
        grid=(n // row_tile,),
        in_specs=[rows(ATT_WIDTH), rows(HG_WIDTH), rows(D_MODEL),
                  pl.BlockSpec((D_MODEL, D_MODEL), lambda i: (0, 0))],
        out_specs=rows(D_MODEL),
        out_shape=jax.ShapeDtypeStruct((n, D_MODEL), F32),
        compiler_params=pltpu.CompilerParams(dimension_semantics=("arbitrary",), vmem_limit_bytes=VMEM_LIMIT),
        name="out_proj",
    )(att, hg, x, w_out)


def _split_heads(k):
    first = lax.broadcasted_iota(jnp.int32, k.shape, 1) < HEAD_DIM
    return (jnp.where(first, k, 0.0).astype(BF16), jnp.where(first, 0.0, k).astype(BF16))


def _attend(q, key_sets, sink_of):
    m = q.shape[0]
    first = lax.broadcasted_iota(jnp.int32, (m, KV_WIDTH), 1) < HEAD_DIM
    blocks = []
    for g in range(GQA_GROUP):
        qg = q[:, g * KV_WIDTH:(g + 1) * KV_WIDTH]
        per_head = []
        for h in range(KV_HEADS):
            scores = []
            for k_heads, _, mask in key_sets:
                s = _dot_nt(qg, k_heads[h])
                if mask is not None:
                    s = jnp.where(mask, s, NEG_BIG)
                scores.append(s)
            sink = sink_of(h, g)
            mx = jnp.max(scores[0], axis=-1, keepdims=True)
            for s in scores[1:]:
                mx = jnp.maximum(mx, jnp.max(s, axis=-1, keepdims=True))
            mx = jnp.maximum(mx, sink)
            den = jnp.exp(sink - mx)
            acc = jnp.zeros((m, KV_WIDTH), F32)
            for s, (_, v, _) in zip(scores, key_sets):
                p = jnp.exp(s - mx)
                den = den + jnp.sum(p, axis=-1, keepdims=True)
                acc = acc + _dot(p.astype(BF16), v)
            per_head.append(acc * (1.0 / den))
        blocks.append(jnp.where(first, per_head[0], per_head[1]))
    return jnp.concatenate(blocks, axis=-1)


def _prompt_attn_kernel(sink_ref, q_ref, kvc_ref, kvp_ref, kvm_ref, sga_ref, o_ref):
    i = pl.program_id(1)
    n_blocks = q_ref.shape[0] // WINDOW
    n_keys = 2 * WINDOW + N_META
    kvm = kvm_ref[...]
    k_blocks = [_split_heads(kvp_ref[:, :KV_WIDTH])]
    vt_blocks = [kvp_ref[:, KV_WIDTH:].T.astype(BF16)]
    for j in range(n_blocks):
        rows = slice(j * WINDOW, (j + 1) * WINDOW)
        k_blocks.append(_split_heads(kvc_ref[rows, :KV_WIDTH]))
        vt_blocks.append(kvc_ref[rows, KV_WIDTH:].T.astype(BF16))
    k_meta = _split_heads(kvm[:, :KV_WIDTH])
    v_meta = jnp.concatenate([kvm[:, KV_WIDTH:], jnp.zeros((WINDOW - N_META, KV_WIDTH), F32)], axis=0)
    vt_meta = v_meta.T.astype(BF16)
    pad = jnp.zeros((3 * WINDOW - n_keys, WINDOW), BF16)

    key = lax.broadcasted_iota(jnp.int32, (n_keys, WINDOW), 0)
    qi = lax.broadcasted_iota(jnp.int32, (n_keys, WINDOW), 1)
    band = jnp.where(key < WINDOW, key - qi - 1, jnp.where(key < 2 * WINDOW, qi - (key - WINDOW), 0))
    visible = band >= 0
    no_prev = jnp.where(i > 0, 0, 2 * WINDOW)
    visible_first = jnp.where(key < WINDOW, band - no_prev, band) >= 0
    head0_rows = lax.broadcasted_iota(jnp.int32, (KV_WIDTH, WINDOW), 0) < HEAD_DIM
    group_lanes = [slice(g * KV_WIDTH, (g + 1) * KV_WIDTH) for g in range(GQA_GROUP)]

    s_both = []
    for j in range(n_blocks):
        rows = slice(j * WINDOW, (j + 1) * WINDOW)
        kk = jnp.concatenate([k_blocks[j][0], k_blocks[j + 1][0], k_meta[0],
                              k_blocks[j][1], k_blocks[j + 1][1], k_meta[1]], axis=0)
        s_both.append([_dot_nt(kk, q_ref[rows, lanes]) for lanes in group_lanes])
    probs = []
    for j in range(n_blocks):
        vis = visible_first if j == 0 else visible
        for g in range(GQA_GROUP):
            for h in range(KV_HEADS):
                sink = sink_ref[h * GQA_GROUP + g]
                s = jnp.where(vis, s_both[j][g][h * n_keys:(h + 1) * n_keys], NEG_BIG)
                mx = jnp.maximum(jnp.max(s, axis=0, keepdims=True), sink)
                p = jnp.exp(s - mx)
                den = jnp.sum(p, axis=0, keepdims=True) + jnp.exp(sink - mx)
                probs.append((jnp.concatenate([p.astype(BF16), pad], axis=0), 1.0 / den))
    outs = []
    for j in range(n_blocks):
        v_t = jnp.concatenate([vt_blocks[j], vt_blocks[j + 1], vt_meta], axis=1)
        for p_pad, inv in probs[j * ATT_HEADS:(j + 1) * ATT_HEADS]:
            outs.append(_dot(v_t, p_pad) * inv)
    for j in range(n_blocks):
        rows = slice(j * WINDOW, (j + 1) * WINDOW)
        for g, lanes in enumerate(group_lanes):
            pair = outs[j * ATT_HEADS + g * KV_HEADS:j * ATT_HEADS + (g + 1) * KV_HEADS]
            o_t = jnp.where(head0_rows, pair[0], pair[1])
            o_ref[rows, lanes] = (o_t.T * sga_ref[rows, lanes].astype(F32)).astype(o_ref.dtype)


def _prompt_attention(sinks, q, kv, kv_meta, sga, batch, seq):
    nb = seq // ATT_TILE
    per = ATT_TILE // WINDOW
    blk = lambda width: pl.BlockSpec((ATT_TILE, width), lambda b, i, s: (b * nb + i, 0))
    prev = pl.BlockSpec((WINDOW, 2 * KV_WIDTH), lambda b, i, s: ((b * nb + i) * per - jnp.minimum(i, 1), 0))
    meta = pl.BlockSpec((N_META, 2 * KV_WIDTH), lambda b, i, s: (0, 0))
    return pl.pallas_call(
        _prompt_attn_kernel,
        grid_spec=pltpu.PrefetchScalarGridSpec(
            num_scalar_prefetch=1, grid=(batch, nb),
            in_specs=[blk(ATT_WIDTH), blk(2 * KV_WIDTH), prev, meta, blk(ATT_WIDTH)],
            out_specs=blk(ATT_WIDTH)),
        out_shape=jax.ShapeDtypeStruct((batch * seq, ATT_WIDTH), BF16),
        compiler_params=pltpu.CompilerParams(dimension_semantics=("arbitrary", "arbitrary"),
                                             vmem_limit_bytes=VMEM_LIMIT),
        name="prompt_attn",
    )(sinks, q, kv, kv, kv_meta, sga)


def _level_sizes(c):
    return [c >> (i + 1) for i in range(c.bit_length() - 1)]


def _decay_sum_matrix(c):
    t = np.arange(c)[:, None]
    r = np.arange(c)[None, :]
    mats = [r <= t]
    for bs in _level_sizes(c):
        if bs < SUBLANES:
            a = (t // (2 * bs)) * (2 * bs) + bs - 1
            mats.append(((r > t) & (r <= a)) | ((r > a) & (r <= t)))
    return np.tile(np.concatenate(mats, axis=0).astype(np.float32), (1, 3))


def _decay_sums(g, dmat):
    g = g * LOG2E
    g1 = g.astype(BF16)
    r1 = g - g1.astype(F32)
    g2 = r1.astype(BF16)
    g3 = (r1 - g2.astype(F32)).astype(BF16)
    return _dot(dmat, jnp.concatenate([g1, g2, g3], axis=0))


def _chunk_exponents(block, c):
    start = block(0)
    grp = lambda j: start[j * SUBLANES:(j + 1) * SUBLANES, :]
    lasts = {}

    def last(j):
        if j not in lasts:
            lasts[j] = jnp.broadcast_to(start[j * SUBLANES + SUBLANES - 1:(j + 1) * SUBLANES, :], (SUBLANES, HG_KDIM))
        return lasts[j]

    n_grp = c // SUBLANES
    end = jnp.concatenate([last(n_grp - 1) - grp(j) for j in range(n_grp)], axis=0)
    levels, fine = [], 1
    for bs in _level_sizes(c):
        if bs >= SUBLANES:
            per = bs // SUBLANES
            parts = []
            for j in range(n_grp):
                blk = j // per
                anchor = last((blk // 2) * 2 * per + per - 1)
                parts.append(grp(j) - anchor if blk % 2 == 1 else anchor - grp(j))
            levels.append(jnp.concatenate(parts, axis=0))
        else:
            levels.append(block(fine))
            fine += 1
    return start, end, levels


def _hgrn_masks(c):
    row = lax.broadcasted_iota(jnp.int32, (c, c), 0)
    col = lax.broadcasted_iota(jnp.int32, (c, c), 1)
    row_k = lax.broadcasted_iota(jnp.int32, (c, HG_KDIM), 0)
    levels = []
    for bs in _level_sizes(c):
        rb, cb = row // bs, col // bs
        pairs = ((rb % 2) * (1 - jnp.abs(cb - rb + 1))) > 0
        levels.append((bs, (row_k // bs) % 2 == 1, pairs))
    return levels, row == col


def _hgrn_chunks(items, states, masks):
    levels, diag = masks
    c = items[0][1].shape[0]

    stage1 = []
    for _, q, k, v, (_, to_end, level_ex) in items:
        pair_scores = []
        for (bs, q_side, _), lex in zip(levels, level_ex):
            if bs >= SUBLANES:
                side = jnp.concatenate(
                    [(q if (j // bs) % 2 == 1 else k)[j:j + SUBLANES] for j in range(0, c, SUBLANES)], axis=0)
            else:
                side = jnp.where(q_side, q, k)
            u = (side * jnp.exp2(lex)).astype(BF16)
            pair_scores.append(_dot_nt(u, u))
        kdec = (k * jnp.exp2(to_end)).astype(BF16)
        stage1.append((pair_scores, _dot_tn(v, kdec)))

    intra = []
    for (_, q, k, v, _), (pair_scores, _) in zip(items, stage1):
        a = jnp.where(diag, jnp.sum(q * k, axis=-1, keepdims=True), 0.0)
        for (_, _, pairs), scores in zip(levels, pair_scores):
            a = jnp.where(pairs, scores, a)
        intra.append(_dot(a.astype(BF16), v))

    outs = []
    for (head, q, _, _, (from_start, _, _)), (_, increment), o_intra in zip(items, stage1, intra):
        st = states[head]
        decay = jnp.exp2(from_start)
        outs.append(o_intra + _dot_nt((q * decay).astype(BF16), st.astype(BF16)))
        states[head] = st * decay[c - 1:c, :] + increment
    return outs


def _head_norm_gate(o, nw, gate):
    ms = jnp.mean(o * o, axis=-1, keepdims=True)
    return o * lax.rsqrt(ms + EPS) * nw * gate


def _prompt_hgrn_kernel(hq_ref, hk_ref, hv_ref, g_ref, sgh_ref, s0_ref, nw_ref, dmat_ref, o_ref, sfin_ref, st_ref):
    j = pl.program_id(1)

    @pl.when(j == 0)
    def _():
        st_ref[...] = s0_ref[...]

    masks = _hgrn_masks(HG_CHUNK)
    nw = nw_ref[...]
    dmat = dmat_ref[...]

    def block_of(sums, lanes):
        return lambda i: sums[i * HG_CHUNK:(i + 1) * HG_CHUNK, lanes]

    items, where = [], []
    for ci in range(HG_TILE // HG_CHUNK):
        rows = slice(ci * HG_CHUNK, (ci + 1) * HG_CHUNK)
        sums = _decay_sums(g_ref[rows, :], dmat)
        for h in range(HG_HEADS):
            lanes = slice(h * HG_KDIM, (h + 1) * HG_KDIM)
            items.append((h, hq_ref[rows, lanes].astype(F32), hk_ref[rows, lanes].astype(F32),
                          hv_ref[rows, lanes], _chunk_exponents(block_of(sums, lanes), HG_CHUNK)))
            where.append((rows, lanes))
    states = [st_ref[h] for h in range(HG_HEADS)]
    outs = _hgrn_chunks(items, states, masks)
    for (rows, lanes), o in zip(where, outs):
        o_ref[rows, lanes] = _head_norm_gate(o, nw, sgh_ref[rows, lanes].astype(F32)).astype(o_ref.dtype)
    for h in range(HG_HEADS):
        st_ref[h] = states[h]

    @pl.when(j == pl.num_programs(1) - 1)
    def _():
        for h in range(HG_HEADS):
            sfin_ref[h] = st_ref[h].T


def _prompt_hgrn(hq, hk, hv, g, sgh, s0t, hg_nw, batch, seq):
    nt = seq // HG_TILE
    blk = pl.BlockSpec((HG_TILE, HG_WIDTH), lambda b, j: (b * nt + j, 0))
    state_shape = (HG_HEADS, HG_VDIM, HG_KDIM)
    dmat = jnp.asarray(_decay_sum_matrix(HG_CHUNK), BF16)
    return pl.pallas_call(
        _prompt_hgrn_kernel,
        grid=(batch, nt),
        in_specs=[blk, blk, blk, blk, blk, pl.BlockSpec(state_shape, lambda b, j: (0, 0, 0)),
                  pl.BlockSpec((1, HG_VDIM), lambda b, j: (0, 0)), pl.BlockSpec(dmat.shape, lambda b, j: (0, 0))],
        out_specs=[blk, pl.BlockSpec((None,) + state_shape, lambda b, j: (b, 0, 0, 0))],
        out_shape=[jax.ShapeDtypeStruct((batch * seq, HG_WIDTH), BF16),
                   jax.ShapeDtypeStruct((batch,) + state_shape, F32)],
        scratch_shapes=[pltpu.VMEM(state_shape, F32)],
        compiler_params=pltpu.CompilerParams(dimension_semantics=("arbitrary", "arbitrary"),
                                             vmem_limit_bytes=VMEM_LIMIT),
        name="prompt_hgrn",
    )(hq, hk, hv, g, sgh, s0t, hg_nw, dmat)


def _meta_mix(sink_of, q_ref, kv_ref, sga_ref, hq_ref, hk_ref, hv_ref, g_ref, sgh_ref, nw_ref, dmat_ref,
              att_ref, hg_ref, st_ref):
    kv = kv_ref[...]
    row = lax.broadcasted_iota(jnp.int32, (N_META, N_META), 0)
    col = lax.broadcasted_iota(jnp.int32, (N_META, N_META), 1)
    att = _attend(q_ref[...].astype(BF16),
                  [(_split_heads(kv[:, :KV_WIDTH]), kv[:, KV_WIDTH:].astype(BF16), col <= row)], sink_of)
    att_ref[...] = (att * sga_ref[...]).astype(att_ref.dtype)

    sums = _decay_sums(g_ref[...], dmat_ref[...])
    masks = _hgrn_masks(N_META)
    head_lanes = [slice(h * HG_KDIM, (h + 1) * HG_KDIM) for h in range(HG_HEADS)]
    items = [(h, hq_ref[:, lanes], hk_ref[:, lanes], hv_ref[:, lanes].astype(BF16),
              _chunk_exponents((lambda lanes: lambda i: sums[i * N_META:(i + 1) * N_META, lanes])(lanes), N_META))
             for h, lanes in enumerate(head_lanes)]
    states = [jnp.zeros((HG_VDIM, HG_KDIM), F32) for _ in range(HG_HEADS)]
    outs = _hgrn_chunks(items, states, masks)
    for h, lanes in enumerate(head_lanes):
        st_ref[h] = states[h]
        hg_ref[:, lanes] = _head_norm_gate(outs[h], nw_ref[...], sgh_ref[:, lanes]).astype(hg_ref.dtype)


def _sample_mix(q_ref, kv_ref, sga_ref, hq_ref, hk_ref, hv_ref, g_ref, sgh_ref, nw_ref,
                sink_ref, seg_ref, exp_ref, ck_ref, cv_ref, mk_ref, mv_ref, s_ref,
                att_ref, hg_ref, nk_ref, nv_ref, ns_ref):
    n_keys = WINDOW + N_META + SUBLANES
    key_row = lax.broadcasted_iota(jnp.int32, (n_keys, KV_WIDTH), 0)
    visible = jnp.logical_and(key_row >= 1, key_row <= WINDOW + N_META)
    win_row = lax.broadcasted_iota(jnp.int32, (WINDOW, KV_WIDTH), 0)
    sink = sink_ref[...]
    seg = seg_ref[...]
    expand = exp_ref[...]
    nw = nw_ref[...]

    def per_seq(b, carry):
        kv_new = kv_ref[b]
        k_new = kv_new[:, 0:KV_WIDTH]
        v_new = kv_new[:, KV_WIDTH:2 * KV_WIDTH]
        ck = ck_ref[b]
        cv = cv_ref[b]
        keys = jnp.concatenate([ck, mk_ref[b], jnp.broadcast_to(k_new, (SUBLANES, KV_WIDTH))], axis=0)
        vals = jnp.concatenate([cv, mv_ref[b], jnp.broadcast_to(v_new, (SUBLANES, KV_WIDTH))], axis=0)
        prod = jnp.concatenate([keys] * GQA_GROUP, axis=1) * q_ref[b]
        s = jnp.where(visible, _dot(prod.astype(BF16), seg), NEG_BIG)
        mx = jnp.maximum(jnp.max(s, axis=0, keepdims=True), sink)
        p = jnp.exp(s - mx)
        den = jnp.sum(p, axis=0, keepdims=True) + jnp.exp(sink - mx)
        p = p * (1.0 / den)
        wide = _dot(p.astype(BF16), expand)
        att = jnp.sum(wide * jnp.concatenate([vals] * GQA_GROUP, axis=1), axis=0, keepdims=True)
        att_ref[b] = att * sga_ref[b]

        nk_ref[b] = jnp.where(win_row == WINDOW - 1, k_new, pltpu.roll(ck, WINDOW - 1, 0))
        nv_ref[b] = jnp.where(win_row == WINDOW - 1, v_new, pltpu.roll(cv, WINDOW - 1, 0))

        g_row, hq_row, hk_row, hv_row, sgh_row = g_ref[b], hq_ref[b], hk_ref[b], hv_ref[b], sgh_ref[b]
        outs = []
        for h in range(HG_HEADS):
            lanes = slice(h * HG_KDIM, (h + 1) * HG_KDIM)
            col = lambda r: jnp.broadcast_to(r[:, lanes], (HG_KDIM, HG_KDIM)).T
            s1 = jnp.exp(col(g_row)) * s_ref[b, h] + col(hk_row) * hv_row[:, lanes]
            ns_ref[b, h] = s1
            o = jnp.sum(col(hq_row) * s1, axis=0, keepdims=True)
            outs.append(_head_norm_gate(o, nw, sgh_row[:, lanes]))
        hg_ref[b] = jnp.concatenate(outs, axis=-1)
        return carry

    lax.fori_loop(0, q_ref.shape[0], per_seq, 0)


def _small_path_kernel(sinks_ref, x0_ref, nw_ref, w_ref, qnw_ref, knw_ref, lbl_ref, bd_ref, wo_ref, hgnw_ref,
                       sinkrow_ref, seg_ref, exp_ref, dmat_ref, ck_ref, cv_ref, mk_ref, mv_ref, s_ref,
                       y_ref, kvm_ref, s0t_ref, nk_ref, nv_ref, ns_ref,
                       x_scr, att_scr, hg_scr, *scr):
    proj_scr, staged, staged_att, staged_hg = scr[:8], scr[8:16], scr[16], scr[17]
    layer, tile = pl.program_id(0), pl.program_id(1)
    n_rows = x_scr.shape[0]
    n_seq = n_rows - N_META
    t = ck_ref.shape[0]

    @pl.when(jnp.logical_and(layer == 0, tile == 0))
    def _():
        x_scr[...] = x0_ref[...]

    @pl.when(tile == 0)
    def _():
        _proj_body(layer, [slice(0, n_rows)], [x_scr[...]], nw_ref, w_ref, qnw_ref, knw_ref, lbl_ref, bd_ref,
                   *proj_scr)
        meta = lambda ref: ref.at[n_seq:n_rows]
        _meta_mix(lambda h, g: sinks_ref[layer, h * GQA_GROUP + g], *[meta(r) for r in proj_scr], hgnw_ref,
                  dmat_ref, meta(att_scr), meta(hg_scr), s0t_ref)
        kvm_ref[...] = proj_scr[1][n_seq:n_rows, :]

    rows = pl.ds(pl.multiple_of(tile * t, t), t)
    for src, dst in zip(proj_scr, staged):
        block = src[rows, :]
        for b in range(t):
            dst[b] = block[b:b + 1, :]
    _sample_mix(*staged, hgnw_ref, sinkrow_ref, seg_ref, exp_ref, ck_ref, cv_ref, mk_ref, mv_ref, s_ref,
                staged_att, staged_hg, nk_ref, nv_ref, ns_ref)
    att_scr[rows, :] = jnp.concatenate([staged_att[b] for b in range(t)], axis=0)
    hg_scr[rows, :] = jnp.concatenate([staged_hg[b] for b in range(t)], axis=0)

    @pl.when(tile == pl.num_programs(1) - 1)
    def _():
        x = (x_scr[...] + _dot(att_scr[...].astype(BF16), wo_ref[0:ATT_WIDTH, :])
             + _dot(hg_scr[...].astype(BF16), wo_ref[ATT_WIDTH:, :]))
        x_scr[...] = x
        y_ref[...] = x


def _small_path(sinks, x0, norm_w, w_in, qnw, knw, lb_logits, bd, w_out, hg_nw, sink_rows, seg, expand,
                cache_k, cache_v, meta_k, meta_v, state):
    n_rows = x0.shape[0]
    n_seq, depth = state.shape[:2]
    t = SAMPLE_TILE
    dmat = jnp.asarray(_decay_sum_matrix(N_META), BF16)
    full = lambda a: pl.BlockSpec(a.shape, lambda l, i, s: (0,) * a.ndim)
    per_layer = lambda a: pl.BlockSpec((None,) + a.shape[1:], lambda l, i, s: (l,) + (0,) * (a.ndim - 1))
    cache = lambda a: pl.BlockSpec((t, None) + a.shape[2:], lambda l, i, s: (i, l) + (0,) * (a.ndim - 2))
    as_rows = lambda a: a[:, None, :]
    state_t = (HG_HEADS, HG_VDIM, HG_KDIM)
    widths = (ATT_WIDTH, 2 * KV_WIDTH, ATT_WIDTH, HG_WIDTH, HG_WIDTH, HG_WIDTH, HG_WIDTH, HG_WIDTH)
    operands = [x0, as_rows(norm_w), w_in, as_rows(qnw), as_rows(knw), lb_logits, bd, w_out, as_rows(hg_nw),
                as_rows(sink_rows), seg, expand, dmat, cache_k, cache_v, meta_k, meta_v, state]
    in_specs = [full(x0), per_layer(operands[1]), per_layer(w_in), per_layer(operands[3]), per_layer(operands[4]),
                full(lb_logits), full(bd), per_layer(w_out), per_layer(operands[8]), per_layer(operands[9]),
                full(seg), full(expand), full(dmat), cache(cache_k), cache(cache_v), cache(meta_k), cache(meta_v),
                cache(state)]
    out_shape = [jax.ShapeDtypeStruct((n_rows, D_MODEL), F32),
                 jax.ShapeDtypeStruct((depth, N_META, 2 * KV_WIDTH), F32),
                 jax.ShapeDtypeStruct((depth,) + state_t, F32),
                 jax.ShapeDtypeStruct(cache_k.shape, F32), jax.ShapeDtypeStruct(cache_v.shape, F32),
                 jax.ShapeDtypeStruct(state.shape, F32)]
    out_specs = [pl.BlockSpec((n_rows, D_MODEL), lambda l, i, s: (0, 0)),
                 pl.BlockSpec((None, N_META, 2 * KV_WIDTH), lambda l, i, s: (l, 0, 0)),
                 pl.BlockSpec((None,) + state_t, lambda l, i, s: (l, 0, 0, 0)),
                 cache(cache_k), cache(cache_v), cache(state)]
    scratch = ([pltpu.VMEM((n_rows, D_MODEL), F32), pltpu.VMEM((n_rows, ATT_WIDTH), F32),
                pltpu.VMEM((n_rows, HG_WIDTH), F32)]
               + [pltpu.VMEM((n_rows, w), F32) for w in widths]
               + [pltpu.VMEM((t, 1, w), F32) for w in widths]
               + [pltpu.VMEM((t, 1, ATT_WIDTH), F32), pltpu.VMEM((t, 1, HG_WIDTH), F32)])
    return pl.pallas_call(
        _small_path_kernel,
        grid_spec=pltpu.PrefetchScalarGridSpec(
            num_scalar_prefetch=1, grid=(depth, n_seq // t),
            in_specs=in_specs, out_specs=out_specs, scratch_shapes=scratch),
        out_shape=out_shape,
        compiler_params=pltpu.CompilerParams(dimension_semantics=("arbitrary", "arbitrary"),
                                             vmem_limit_bytes=VMEM_LIMIT),
        name="small_path",
    )(sinks, *operands)


def _g_major(a, axis):
    shape = a.shape
    a = a.reshape(shape[:axis] + (KV_HEADS, GQA_GROUP, HEAD_DIM) + shape[axis + 1:])
    a = jnp.swapaxes(a, axis, axis + 1)
    return a.reshape(shape)


def _constants():
    lane = np.arange(ATT_WIDTH)
    g_of, h_of = lane // KV_WIDTH, (lane % KV_WIDTH) // HEAD_DIM
    head = h_of * GQA_GROUP + g_of
    seg = (head[:, None] == np.arange(KV_WIDTH)[None, :]).astype(np.float32)
    grp = np.arange(256) // HEAD_DIM
    bd = (grp[:, None] == grp[None, :]).astype(np.float32) / HEAD_DIM
    return jnp.asarray(seg, BF16), jnp.asarray(seg.T, BF16), jnp.asarray(bd, BF16)


def kernel(x_prompt, x_sample, cache_win_k, cache_win_v, cache_meta_k, cache_meta_v, state_hgrn, meta_tokens,
           norm_w, w_in, q_norm_w, k_norm_w, attn_sinks, hg_lb_logits, hg_norm_w, w_out):
    batch, seq, _ = x_prompt.shape
    n_seq = x_sample.shape[0]
    depth = w_in.shape[0]
    w_buf = cache_win_k.shape[2]
    assert x_sample.shape[1] == 1 and w_buf == WINDOW and seq % ROW_TILE == 0 and n_seq % SAMPLE_TILE == 0

    seg, expand, bd = _constants()
    w_in_b = jnp.concatenate([_g_major(w_in[:, :, C_Q:C_K], 2), w_in[:, :, C_K:C_GA],
                              _g_major(w_in[:, :, C_GA:C_QH], 2), w_in[:, :, C_QH:]], axis=2).astype(BF16)
    w_out_b = jnp.concatenate([_g_major(w_out[:, :ATT_WIDTH], 1), w_out[:, ATT_WIDTH:]], axis=1).astype(BF16)
    qnw = jnp.tile(q_norm_w, (1, ATT_HEADS)) * (HEAD_DIM ** -0.5)
    knw = jnp.tile(k_norm_w, (1, KV_HEADS))
    lb_logits = hg_lb_logits.astype(F32)
    sink_rows = jnp.pad(attn_sinks.astype(F32), ((0, 0), (0, KV_WIDTH - ATT_HEADS)))

    ck = cache_win_k.reshape(n_seq, depth, w_buf, KV_WIDTH)
    cv = cache_win_v.reshape(n_seq, depth, w_buf, KV_WIDTH)
    mk = cache_meta_k.reshape(n_seq, depth, N_META, KV_WIDTH)
    mv = cache_meta_v.reshape(n_seq, depth, N_META, KV_WIDTH)

    x_small = jnp.concatenate([x_sample.reshape(n_seq, D_MODEL), meta_tokens.astype(F32)], axis=0)
    y_small, kv_meta, s0t, new_k, new_v, new_state = _small_path(
        attn_sinks.astype(F32), x_small, norm_w, w_in_b, qnw, knw, lb_logits, bd, w_out_b, hg_norm_w, sink_rows,
        seg, expand, ck, cv, mk, mv, state_hgrn)

    xp = x_prompt.reshape(batch * seq, D_MODEL)
    outs = {k: [] for k in ("wkp", "wvp", "hsp")}
    mix = None
    for l in range(depth):
        pr = _project(l, xp, norm_w[l][None], w_in_b[l], qnw[l][None], knw[l][None], lb_logits, bd,
                      row_tile=ROW_TILE, act_dtype=BF16, mix=mix)
        if mix is not None:
            xp, pr = pr[0], pr[1:]
        q, kv, sga, hq, hk, hv, g, sgh = pr
        att = _prompt_attention(attn_sinks[l].astype(F32), q, kv, kv_meta[l], sga, batch, seq)
        hg, s_fin = _prompt_hgrn(hq, hk, hv, g, sgh, s0t[l], hg_norm_w[l][None], batch, seq)
        mix = (att, hg, w_out_b[l])

        kv3 = kv.reshape(batch, seq, 2 * KV_WIDTH)
        outs["wkp"].append(kv3[:, seq - w_buf:, :KV_WIDTH])
        outs["wvp"].append(kv3[:, seq - w_buf:, KV_WIDTH:])
        outs["hsp"].append(s_fin)
    xp = _out_project(mix[0], mix[1], xp, mix[2], ROW_TILE)

    stack = lambda name: jnp.stack(outs[name], axis=1)
    heads = lambda a: a.reshape(a.shape[:-1] + (KV_HEADS, HEAD_DIM))
    meta_rows = lambda a: jnp.broadcast_to(a[None], (batch,) + a.shape)
    return (xp.reshape(batch, seq, D_MODEL), y_small[:n_seq].reshape(n_seq, 1, D_MODEL),
            heads(stack("wkp")), heads(stack("wvp")),
            heads(meta_rows(kv_meta[:, :, :KV_WIDTH])), heads(meta_rows(kv_meta[:, :, KV_WIDTH:])), stack("hsp"),
            heads(new_k), heads(new_v), new_state)
```

```python
import functools

import numpy as np
import jax
import jax.numpy as jnp
from jax import lax
from jax.experimental import pallas as pl
from jax.experimental.pallas import tpu as pltpu

F32 = jnp.float32
BF16 = jnp.bfloat16

D_MODEL = 1024
N_META = 16
WINDOW = 128
HEAD_DIM = 64
ATT_WIDTH = 512
ATT_HEADS = 8
KV_HEADS = 2
GQA_GROUP = 4
KV_WIDTH = KV_HEADS * HEAD_DIM
HG_WIDTH = 512
HG_HEADS = 4
HG_KDIM = 128
HG_VDIM = 128
PROJ_WIDTH = 3328
EPS = 1e-6
NEG_BIG = -1e30
TINY = 1e-30
LOG2E = 1.4426950408889634

C_Q, C_K, C_V, C_GA, C_QH, C_FH, C_IH, C_GH = 0, 512, 640, 768, 1280, 1792, 2304, 2816

SUBLANES = 8
HG_CHUNK = 64
ROW_TILE = 512
HG_TILE = 512
ATT_TILE = 512
SAMPLE_TILE = 8
VMEM_LIMIT = 48 * 1024 * 1024


def _dot(a, b):
    return jnp.dot(a, b, preferred_element_type=F32)


def _dot_nt(a, b):
    return lax.dot_general(a, b, (((1,), (1,)), ((), ())), preferred_element_type=F32)


def _dot_tn(a, b):
    return lax.dot_general(a, b, (((0,), (0,)), ((), ())), preferred_element_type=F32)


def _silu(x):
    return x * (1.0 / (1.0 + jnp.exp(-x)))


def _group_major(x):
    first = lax.broadcasted_iota(jnp.int32, (x.shape[0], KV_WIDTH), 1) < HEAD_DIM
    blocks = [x[:, j * KV_WIDTH:(j + 1) * KV_WIDTH] for j in range(GQA_GROUP)]
    swapped = [pltpu.roll(b, HEAD_DIM, 1) for b in blocks]
    out = []
    for g in range(GQA_GROUP):
        a, b = g // 2, GQA_GROUP // 2 + g // 2
        out.append(jnp.where(first, blocks[a], swapped[b]) if g % 2 == 0 else jnp.where(first, swapped[a], blocks[b]))
    return jnp.concatenate(out, axis=-1)


def _row_parts(n_rows, n_parts):
    step = n_rows // n_parts
    return [slice(i * step, (i + 1) * step) for i in range(n_parts)]


def _mix_proj_kernel(layer, n_parts, att_ref, hg_ref, wo_ref, x_ref, *rest):
    parts = _row_parts(x_ref.shape[0], n_parts)
    xnew_ref = rest[6]
    xs = []
    for rs in parts:
        x = (x_ref[rs, :] + _dot(att_ref[rs, :].astype(BF16), wo_ref[0:ATT_WIDTH, :])
             + _dot(hg_ref[rs, :].astype(BF16), wo_ref[ATT_WIDTH:, :]))
        xnew_ref[rs, :] = x
        xs.append(x)
    _proj_body(layer, parts, xs, *rest[:6], *rest[7:])


def _proj_kernel(layer, n_parts, x_ref, *rest):
    parts = _row_parts(x_ref.shape[0], n_parts)
    _proj_body(layer, parts, [x_ref[rs, :] for rs in parts], *rest)


def _proj_body(layer, parts, xs, nw_ref, w_ref, qnw_ref, knw_ref, lbl_ref, bd_ref,
               q_ref, kv_ref, sga_ref, hq_ref, hk_ref, hv_ref, g_ref, sgh_ref):
    nw = nw_ref[...]
    hs = []
    for x in xs:
        ms = jnp.mean(x * x, axis=-1, keepdims=True)
        hs.append((x * lax.rsqrt(ms + EPS) * nw).astype(BF16))

    def proj(h, lo, hi):
        return _dot(h, w_ref[:, lo:hi])

    pq = [[proj(h, C_Q + 256 * c, C_Q + 256 * (c + 1)) for c in range(2)] for h in hs]
    pk = [proj(h, C_K, C_V) for h in hs]
    sq = [[(p * p).astype(BF16) for p in pqs + [pks]] for pqs, pks in zip(pq, pk)]

    logits = lbl_ref[...]
    e = jnp.exp(logits - jnp.max(logits, axis=0, keepdims=True))
    p = e / jnp.sum(e, axis=0, keepdims=True)
    depth_row = lax.broadcasted_iota(jnp.int32, logits.shape, 0)
    in_range = jnp.where(depth_row >= 1, layer - depth_row, -1) >= 0
    lb = jnp.sum(jnp.where(in_range, p, 0.0), axis=0, keepdims=True)

    for rs, h in zip(parts, hs):
        kv_ref[rs, KV_WIDTH:2 * KV_WIDTH] = proj(h, C_V, C_GA)
        sga_ref[rs, :] = _group_major(_silu(proj(h, C_GA, C_QH))).astype(sga_ref.dtype)
        hq_ref[rs, :] = _silu(proj(h, C_QH, C_FH)).astype(hq_ref.dtype)

        z = proj(h, C_FH, C_IH)
        ez = jnp.exp(-jnp.abs(z))
        r = 1.0 / (1.0 + ez)
        pos = z >= 0.0
        sig_pos = jnp.where(pos, r, ez * r)
        sig_neg = jnp.where(pos, ez * r, r)
        hk_ref[rs, :] = ((1.0 - lb) * sig_neg).astype(hk_ref.dtype)
        f = lb + (1.0 - lb) * sig_pos
        g_ref[rs, :] = jnp.log(jnp.maximum(f, TINY))

        hv_ref[rs, :] = proj(h, C_IH, C_GH).astype(hv_ref.dtype)
        sgh_ref[rs, :] = _silu(proj(h, C_GH, PROJ_WIDTH)).astype(sgh_ref.dtype)

    bd = bd_ref[...]
    for rs, pqs, pks, sqs in zip(parts, pq, pk, sq):
        q = jnp.concatenate([p * lax.rsqrt(_dot(s, bd) + EPS) for p, s in zip(pqs, sqs[:2])], axis=-1)
        q_ref[rs, :] = _group_major(q * qnw_ref[...]).astype(q_ref.dtype)
        kv_ref[rs, 0:KV_WIDTH] = pks * lax.rsqrt(_dot(sqs[2], bd[:KV_WIDTH, :KV_WIDTH]) + EPS) * knw_ref[...]


def _project(layer, x, norm_w, w_in, qnw, knw, lb_logits, bd, row_tile, act_dtype, mix=None):
    n = x.shape[0]
    rows = lambda width: pl.BlockSpec((row_tile, width), lambda i: (i, 0))
    full = lambda shape: pl.BlockSpec(shape, lambda i: (0,) * len(shape))
    out = lambda width, dtype: jax.ShapeDtypeStruct((n, width), dtype)
    operands = [x, norm_w, w_in, qnw, knw, lb_logits, bd]
    in_specs = [rows(D_MODEL), full((1, D_MODEL)), full((D_MODEL, PROJ_WIDTH)), full((1, ATT_WIDTH)),
                full((1, KV_WIDTH)), full(lb_logits.shape), full((256, 256))]
    out_specs = [rows(ATT_WIDTH), rows(2 * KV_WIDTH), rows(ATT_WIDTH), rows(HG_WIDTH), rows(HG_WIDTH),
                 rows(HG_WIDTH), rows(HG_WIDTH), rows(HG_WIDTH)]
    out_shape = [out(ATT_WIDTH, act_dtype), out(2 * KV_WIDTH, F32), out(ATT_WIDTH, act_dtype),
                 out(HG_WIDTH, act_dtype), out(HG_WIDTH, act_dtype), out(HG_WIDTH, act_dtype),
                 out(HG_WIDTH, F32), out(HG_WIDTH, act_dtype)]
    body = _proj_kernel
    if mix is not None:
        operands = list(mix) + operands
        in_specs = [rows(ATT_WIDTH), rows(HG_WIDTH), full((D_MODEL, D_MODEL))] + in_specs
        out_specs = [rows(D_MODEL)] + out_specs
        out_shape = [out(D_MODEL, F32)] + out_shape
        body = _mix_proj_kernel
    n_parts = 2 if row_tile % (2 * 128) == 0 else 1
    return pl.pallas_call(
        functools.partial(body, layer, n_parts),
        grid=(n // row_tile,),
        in_specs=in_specs, out_specs=out_specs, out_shape=out_shape,
        compiler_params=pltpu.CompilerParams(dimension_semantics=("arbitrary",), vmem_limit_bytes=VMEM_LIMIT),
        name="proj",
    )(*operands)


def _out_kernel(att_ref, hg_ref, x_ref, w_ref, y_ref):
    y_ref[...] = (x_ref[...] + _dot(att_ref[...].astype(BF16), w_ref[0:ATT_WIDTH, :])
                  + _dot(hg_ref[...].astype(BF16), w_ref[ATT_WIDTH:, :]))


def _out_project(att, hg, x, w_out, row_tile):
    n = x.shape[0]
    rows = lambda width: pl.BlockSpec((row_tile, width), lambda i: (i, 0))
    return pl.pallas_call(
        _out_kernel,
        grid=(n // row_tile,),
        in_specs=[rows(ATT_WIDTH), rows(HG_WIDTH), rows(D_MODEL),
                  pl.BlockSpec((D_MODEL, D_MODEL), lambda i: (0, 0))],
        out_specs=rows(D_MODEL),
        out_shape=jax.ShapeDtypeStruct((n, D_MODEL), F32),
        compiler_params=pltpu.CompilerParams(dimension_semantics=("arbitrary",), vmem_limit_bytes=VMEM_LIMIT),
        name="out_proj",
    )(att, hg, x, w_out)


def _split_heads(k):
    first = lax.broadcasted_iota(jnp.int32, k.shape, 1) < HEAD_DIM
    return (jnp.where(first, k, 0.0).astype(BF16), jnp.where(first, 0.0, k).astype(BF16))


def _attend(q, key_sets, sink_of):
    m = q.shape[0]
    first = lax.broadcasted_iota(jnp.int32, (m, KV_WIDTH), 1) < HEAD_DIM
    blocks = []
    for g in range(GQA_GROUP):
        qg = q[:, g * KV_WIDTH:(g + 1) * KV_WIDTH]
        per_head = []
        for h in range(KV_HEADS):
            scores = []
            for k_heads, _, mask in key_sets:
                s = _dot_nt(qg, k_heads[h])
                if mask is not None:
                    s = jnp.where(mask, s, NEG_BIG)
                scores.append(s)
            sink = sink_of(h, g)
            mx = jnp.max(scores[0], axis=-1, keepdims=True)
            for s in scores[1:]:
                mx = jnp.maximum(mx, jnp.max(s, axis=-1, keepdims=True))
            mx = jnp.maximum(mx, sink)
            den = jnp.exp2(sink - mx)
            acc = jnp.zeros((m, KV_WIDTH), F32)
            for s, (_, v, _) in zip(scores, key_sets):
                p = jnp.exp2(s - mx)
                den = den + jnp.sum(p, axis=-1, keepdims=True)
                acc = acc + _dot(p.astype(BF16), v)
            per_head.append(acc * (1.0 / den))
        blocks.append(jnp.where(first, per_head[0], per_head[1]))
    return jnp.concatenate(blocks, axis=-1)


def _prompt_attn_kernel(sink_ref, q_ref, kvc_ref, kvp_ref, kvm_ref, sga_ref, o_ref):
    i = pl.program_id(1)
    n_blocks = q_ref.shape[0] // WINDOW
    n_keys = 2 * WINDOW + N_META
    kvm = kvm_ref[...]
    k_blocks = [_split_heads(kvp_ref[:, :KV_WIDTH])]
    vt_blocks = [kvp_ref[:, KV_WIDTH:].T.astype(BF16)]
    for j in range(n_blocks):
        rows = slice(j * WINDOW, (j + 1) * WINDOW)
        k_blocks.append(_split_heads(kvc_ref[rows, :KV_WIDTH]))
        vt_blocks.append(kvc_ref[rows, KV_WIDTH:].T.astype(BF16))
    k_meta = _split_heads(kvm[:, :KV_WIDTH])
    v_meta = jnp.concatenate([kvm[:, KV_WIDTH:], jnp.zeros((WINDOW - N_META, KV_WIDTH), F32)], axis=0)
    vt_meta = v_meta.T.astype(BF16)
    pad = jnp.zeros((3 * WINDOW - n_keys, WINDOW), BF16)

    key = lax.broadcasted_iota(jnp.int32, (n_keys, WINDOW), 0)
    qi = lax.broadcasted_iota(jnp.int32, (n_keys, WINDOW), 1)
    band = jnp.where(key < WINDOW, key - qi - 1, jnp.where(key < 2 * WINDOW, qi - (key - WINDOW), 0))
    visible = band >= 0
    no_prev = jnp.where(i > 0, 0, 2 * WINDOW)
    visible_first = jnp.where(key < WINDOW, band - no_prev, band) >= 0
    head0_rows = lax.broadcasted_iota(jnp.int32, (KV_WIDTH, WINDOW), 0) < HEAD_DIM
    group_lanes = [slice(g * KV_WIDTH, (g + 1) * KV_WIDTH) for g in range(GQA_GROUP)]

    s_both = []
    for j in range(n_blocks):
        rows = slice(j * WINDOW, (j + 1) * WINDOW)
        kk = jnp.concatenate([k_blocks[j][0], k_blocks[j + 1][0], k_meta[0],
                              k_blocks[j][1], k_blocks[j + 1][1], k_meta[1]], axis=0)
        s_both.append([_dot_nt(kk, q_ref[rows, lanes]) for lanes in group_lanes])
    probs = []
    for j in range(n_blocks):
        vis = visible_first if j == 0 else visible
        for g in range(GQA_GROUP):
            for h in range(KV_HEADS):
                sink = sink_ref[h * GQA_GROUP + g]
                s = jnp.where(vis, s_both[j][g][h * n_keys:(h + 1) * n_keys], NEG_BIG)
                mx = jnp.maximum(jnp.max(s, axis=0, keepdims=True), sink)
                p = jnp.exp2(s - mx)
                den = jnp.sum(p, axis=0, keepdims=True) + jnp.exp2(sink - mx)
                probs.append((jnp.concatenate([p.astype(BF16), pad], axis=0), 1.0 / den))
    outs = []
    for j in range(n_blocks):
        v_t = jnp.concatenate([vt_blocks[j], vt_blocks[j + 1], vt_meta], axis=1)
        for p_pad, inv in probs[j * ATT_HEADS:(j + 1) * ATT_HEADS]:
            outs.append(_dot(v_t, p_pad) * inv)
    for j in range(n_blocks):
        rows = slice(j * WINDOW, (j + 1) * WINDOW)
        for g, lanes in enumerate(group_lanes):
            pair = outs[j * ATT_HEADS + g * KV_HEADS:j * ATT_HEADS + (g + 1) * KV_HEADS]
            o_t = jnp.where(head0_rows, pair[0], pair[1])
            o_ref[rows, lanes] = (o_t.T * sga_ref[rows, lanes].astype(F32)).astype(o_ref.dtype)


def _prompt_attention(sinks, q, kv, kv_meta, sga, batch, seq):
    nb = seq // ATT_TILE
    per = ATT_TILE // WINDOW
    blk = lambda width: pl.BlockSpec((ATT_TILE, width), lambda b, i, s: (b * nb + i, 0))
    prev = pl.BlockSpec((WINDOW, 2 * KV_WIDTH), lambda b, i, s: ((b * nb + i) * per - jnp.minimum(i, 1), 0))
    meta = pl.BlockSpec((N_META, 2 * KV_WIDTH), lambda b, i, s: (0, 0))
    return pl.pallas_call(
        _prompt_attn_kernel,
        grid_spec=pltpu.PrefetchScalarGridSpec(
            num_scalar_prefetch=1, grid=(batch, nb),
            in_specs=[blk(ATT_WIDTH), blk(2 * KV_WIDTH), prev, meta, blk(ATT_WIDTH)],
            out_specs=blk(ATT_WIDTH)),
        out_shape=jax.ShapeDtypeStruct((batch * seq, ATT_WIDTH), BF16),
        compiler_params=pltpu.CompilerParams(dimension_semantics=("arbitrary", "arbitrary"),
                                             vmem_limit_bytes=VMEM_LIMIT),
        name="prompt_attn",
    )(sinks, q, kv, kv, kv_meta, sga)


def _level_sizes(c):
    return [c >> (i + 1) for i in range(c.bit_length() - 1)]


def _decay_sum_matrix(c):
    t = np.arange(c)[:, None]
    r = np.arange(c)[None, :]
    mats = [r <= t]
    for bs in _level_sizes(c):
        if bs < SUBLANES:
            a = (t // (2 * bs)) * (2 * bs) + bs - 1
            mats.append(((r > t) & (r <= a)) | ((r > a) & (r <= t)))
    return np.tile(np.concatenate(mats, axis=0).astype(np.float32), (1, 3))


def _decay_sums(g, dmat):
    g = g * LOG2E
    g1 = g.astype(BF16)
    r1 = g - g1.astype(F32)
    g2 = r1.astype(BF16)
    g3 = (r1 - g2.astype(F32)).astype(BF16)
    return _dot(dmat, jnp.concatenate([g1, g2, g3], axis=0))


def _chunk_exponents(block, c):
    start = block(0)
    grp = lambda j: start[j * SUBLANES:(j + 1) * SUBLANES, :]
    lasts = {}

    def last(j):
        if j not in lasts:
            lasts[j] = jnp.broadcast_to(start[j * SUBLANES + SUBLANES - 1:(j + 1) * SUBLANES, :], (SUBLANES, HG_KDIM))
        return lasts[j]

    n_grp = c // SUBLANES
    end = jnp.concatenate([last(n_grp - 1) - grp(j) for j in range(n_grp)], axis=0)
    levels, fine = [], 1
    for bs in _level_sizes(c):
        if bs >= SUBLANES:
            per = bs // SUBLANES
            parts = []
            for j in range(n_grp):
                blk = j // per
                anchor = last((blk // 2) * 2 * per + per - 1)
                parts.append(grp(j) - anchor if blk % 2 == 1 else anchor - grp(j))
            levels.append(jnp.concatenate(parts, axis=0))
        else:
            levels.append(block(fine))
            fine += 1
    return start, end, levels


def _hgrn_masks(c):
    row = lax.broadcasted_iota(jnp.int32, (c, c), 0)
    col = lax.broadcasted_iota(jnp.int32, (c, c), 1)
    row_k = lax.broadcasted_iota(jnp.int32, (c, HG_KDIM), 0)
    levels = []
    for bs in _level_sizes(c):
        rb, cb = row // bs, col // bs
        pairs = ((rb % 2) * (1 - jnp.abs(cb - rb + 1))) > 0
        levels.append((bs, (row_k // bs) % 2 == 1, pairs))
    return levels, row == col


def _hgrn_chunks(items, states, masks):
    levels, diag = masks
    c = items[0][1].shape[0]

    stage1 = []
    for _, q, k, v, (_, to_end, level_ex) in items:
        pair_scores = []
        for (bs, q_side, _), lex in zip(levels, level_ex):
            if bs >= SUBLANES:
                side = jnp.concatenate(
                    [(q if (j // bs) % 2 == 1 else k)[j:j + SUBLANES] for j in range(0, c, SUBLANES)], axis=0)
            else:
                side = jnp.where(q_side, q, k)
            u = (side * jnp.exp2(lex)).astype(BF16)
            pair_scores.append(_dot_nt(u, u))
        kdec = (k * jnp.exp2(to_end)).astype(BF16)
        stage1.append((pair_scores, _dot_tn(v, kdec)))

    intra = []
    for (_, q, k, v, _), (pair_scores, _) in zip(items, stage1):
        a = jnp.where(diag, jnp.sum(q * k, axis=-1, keepdims=True), 0.0)
        for (_, _, pairs), scores in zip(levels, pair_scores):
            a = jnp.where(pairs, scores, a)
        intra.append(_dot(a.astype(BF16), v))

    outs = []
    for (head, q, _, _, (from_start, _, _)), (_, increment), o_intra in zip(items, stage1, intra):
        st = states[head]
        decay = jnp.exp2(from_start)
        outs.append(o_intra + _dot_nt((q * decay).astype(BF16), st.astype(BF16)))
        states[head] = st * decay[c - 1:c, :] + increment
    return outs


def _head_norm_gate(o, nw, gate):
    ms = jnp.mean(o * o, axis=-1, keepdims=True)
    return o * lax.rsqrt(ms + EPS) * nw * gate


def _prompt_hgrn_kernel(hq_ref, hk_ref, hv_ref, g_ref, sgh_ref, s0_ref, nw_ref, dmat_ref, o_ref, sfin_ref, st_ref):
    j = pl.program_id(1)

    @pl.when(j == 0)
    def _():
        st_ref[...] = s0_ref[...]

    masks = _hgrn_masks(HG_CHUNK)
    nw = nw_ref[...]
    dmat = dmat_ref[...]

    def block_of(sums, lanes):
        return lambda i: sums[i * HG_CHUNK:(i + 1) * HG_CHUNK, lanes]

    items, where = [], []
    for ci in range(HG_TILE // HG_CHUNK):
        rows = slice(ci * HG_CHUNK, (ci + 1) * HG_CHUNK)
        sums = _decay_sums(g_ref[rows, :], dmat)
        for h in range(HG_HEADS):
            lanes = slice(h * HG_KDIM, (h + 1) * HG_KDIM)
            items.append((h, hq_ref[rows, lanes].astype(F32), hk_ref[rows, lanes].astype(F32),
                          hv_ref[rows, lanes], _chunk_exponents(block_of(sums, lanes), HG_CHUNK)))
            where.append((rows, lanes))
    states = [st_ref[h] for h in range(HG_HEADS)]
    outs = _hgrn_chunks(items, states, masks)
    for (rows, lanes), o in zip(where, outs):
        o_ref[rows, lanes] = _head_norm_gate(o, nw, sgh_ref[rows, lanes].astype(F32)).astype(o_ref.dtype)
    for h in range(HG_HEADS):
        st_ref[h] = states[h]

    @pl.when(j == pl.num_programs(1) - 1)
    def _():
        for h in range(HG_HEADS):
            sfin_ref[h] = st_ref[h].T


def _prompt_hgrn(hq, hk, hv, g, sgh, s0t, hg_nw, batch, seq):
    nt = seq // HG_TILE
    blk = pl.BlockSpec((HG_TILE, HG_WIDTH), lambda b, j: (b * nt + j, 0))
    state_shape = (HG_HEADS, HG_VDIM, HG_KDIM)
    dmat = jnp.asarray(_decay_sum_matrix(HG_CHUNK), BF16)
    return pl.pallas_call(
        _prompt_hgrn_kernel,
        grid=(batch, nt),
        in_specs=[blk, blk, blk, blk, blk, pl.BlockSpec(state_shape, lambda b, j: (0, 0, 0)),
                  pl.BlockSpec((1, HG_VDIM), lambda b, j: (0, 0)), pl.BlockSpec(dmat.shape, lambda b, j: (0, 0))],
        out_specs=[blk, pl.BlockSpec((None,) + state_shape, lambda b, j: (b, 0, 0, 0))],
        out_shape=[jax.ShapeDtypeStruct((batch * seq, HG_WIDTH), BF16),
                   jax.ShapeDtypeStruct((batch,) + state_shape, F32)],
        scratch_shapes=[pltpu.VMEM(state_shape, F32)],
        compiler_params=pltpu.CompilerParams(dimension_semantics=("arbitrary", "arbitrary"),
                                             vmem_limit_bytes=VMEM_LIMIT),
        name="prompt_hgrn",
    )(hq, hk, hv, g, sgh, s0t, hg_nw, dmat)


def _meta_mix(sink_of, q_ref, kv_ref, sga_ref, hq_ref, hk_ref, hv_ref, g_ref, sgh_ref, nw_ref, dmat_ref,
              att_ref, hg_ref, st_ref):
    kv = kv_ref[...]
    row = lax.broadcasted_iota(jnp.int32, (N_META, N_META), 0)
    col = lax.broadcasted_iota(jnp.int32, (N_META, N_META), 1)
    att = _attend(q_ref[...].astype(BF16),
                  [(_split_heads(kv[:, :KV_WIDTH]), kv[:, KV_WIDTH:].astype(BF16), col <= row)], sink_of)
    att_ref[...] = (att * sga_ref[...]).astype(att_ref.dtype)

    sums = _decay_sums(g_ref[...], dmat_ref[...])
    masks = _hgrn_masks(N_META)
    head_lanes = [slice(h * HG_KDIM, (h + 1) * HG_KDIM) for h in range(HG_HEADS)]
    items = [(h, hq_ref[:, lanes], hk_ref[:, lanes], hv_ref[:, lanes].astype(BF16),
              _chunk_exponents((lambda lanes: lambda i: sums[i * N_META:(i + 1) * N_META, lanes])(lanes), N_META))
             for h, lanes in enumerate(head_lanes)]
    states = [jnp.zeros((HG_VDIM, HG_KDIM), F32) for _ in range(HG_HEADS)]
    outs = _hgrn_chunks(items, states, masks)
    for h, lanes in enumerate(head_lanes):
        st_ref[h] = states[h]
        hg_ref[:, lanes] = _head_norm_gate(outs[h], nw_ref[...], sgh_ref[:, lanes]).astype(hg_ref.dtype)


def _sample_mix(q_ref, kv_ref, sga_ref, hq_ref, hk_ref, hv_ref, g_ref, sgh_ref, nw_ref,
                sink_ref, seg_ref, exp_ref, ck_ref, cv_ref, mk_ref, mv_ref, s_ref,
                att_ref, hg_ref, nk_ref, nv_ref, ns_ref):
    n_keys = WINDOW + N_META + SUBLANES
    key_row = lax.broadcasted_iota(jnp.int32, (n_keys, KV_WIDTH), 0)
    visible = jnp.logical_and(key_row >= 1, key_row <= WINDOW + N_META)
    win_row = lax.broadcasted_iota(jnp.int32, (WINDOW, KV_WIDTH), 0)
    sink = sink_ref[...]
    seg = seg_ref[...]
    expand = exp_ref[...]
    nw = nw_ref[...]

    def per_seq(b, carry):
        kv_new = kv_ref[b]
        k_new = kv_new[:, 0:KV_WIDTH]
        v_new = kv_new[:, KV_WIDTH:2 * KV_WIDTH]
        ck = ck_ref[b]
        cv = cv_ref[b]
        keys = jnp.concatenate([ck, mk_ref[b], jnp.broadcast_to(k_new, (SUBLANES, KV_WIDTH))], axis=0)
        vals = jnp.concatenate([cv, mv_ref[b], jnp.broadcast_to(v_new, (SUBLANES, KV_WIDTH))], axis=0)
        prod = jnp.concatenate([keys] * GQA_GROUP, axis=1) * q_ref[b]
        s = jnp.where(visible, _dot(prod.astype(BF16), seg), NEG_BIG)
        mx = jnp.maximum(jnp.max(s, axis=0, keepdims=True), sink)
        p = jnp.exp2(s - mx)
        den = jnp.sum(p, axis=0, keepdims=True) + jnp.exp2(sink - mx)
        p = p * (1.0 / den)
        wide = _dot(p.astype(BF16), expand)
        att = jnp.sum(wide * jnp.concatenate([vals] * GQA_GROUP, axis=1), axis=0, keepdims=True)
        att_ref[b] = att * sga_ref[b]

        nk_ref[b] = jnp.where(win_row == WINDOW - 1, k_new, pltpu.roll(ck, WINDOW - 1, 0))
        nv_ref[b] = jnp.where(win_row == WINDOW - 1, v_new, pltpu.roll(cv, WINDOW - 1, 0))

        g_row, hq_row, hk_row, hv_row, sgh_row = g_ref[b], hq_ref[b], hk_ref[b], hv_ref[b], sgh_ref[b]
        outs = []
        for h in range(HG_HEADS):
            lanes = slice(h * HG_KDIM, (h + 1) * HG_KDIM)
            col = lambda r: jnp.broadcast_to(r[:, lanes], (HG_KDIM, HG_KDIM)).T
            s1 = jnp.exp(col(g_row)) * s_ref[b, h] + col(hk_row) * hv_row[:, lanes]
            ns_ref[b, h] = s1
            o = jnp.sum(col(hq_row) * s1, axis=0, keepdims=True)
            outs.append(_head_norm_gate(o, nw, sgh_row[:, lanes]))
        hg_ref[b] = jnp.concatenate(outs, axis=-1)
        return carry

    lax.fori_loop(0, q_ref.shape[0], per_seq, 0, unroll=4)


def _small_path_kernel(sinks_ref, x0_ref, nw_ref, w_ref, qnw_ref, knw_ref, lbl_ref, bd_ref, wo_ref, hgnw_ref,
                       sinkrow_ref, seg_ref, exp_ref, dmat_ref, ck_ref, cv_ref, mk_ref, mv_ref, s_ref,
                       y_ref, kvm_ref, s0t_ref, nk_ref, nv_ref, ns_ref,
                       x_scr, att_scr, hg_scr, *scr):
    proj_scr, staged, staged_att, staged_hg = scr[:8], scr[8:16], scr[16], scr[17]
    layer, tile = pl.program_id(0), pl.program_id(1)
    n_rows = x_scr.shape[0]
    n_seq = n_rows - N_META
    t = ck_ref.shape[0]

    @pl.when(jnp.logical_and(layer == 0, tile == 0))
    def _():
        x_scr[...] = x0_ref[...]

    @pl.when(tile == 0)
    def _():
        _proj_body(layer, [slice(0, n_rows)], [x_scr[...]], nw_ref, w_ref, qnw_ref, knw_ref, lbl_ref, bd_ref,
                   *proj_scr)
        meta = lambda ref: ref.at[n_seq:n_rows]
        _meta_mix(lambda h, g: sinks_ref[layer, h * GQA_GROUP + g], *[meta(r) for r in proj_scr], hgnw_ref,
                  dmat_ref, meta(att_scr), meta(hg_scr), s0t_ref)
        kvm_ref[...] = proj_scr[1][n_seq:n_rows, :]

    rows = pl.ds(pl.multiple_of(tile * t, t), t)
    for src, dst in zip(proj_scr, staged):
        block = src[rows, :]
        for b in range(t):
            dst[b] = block[b:b + 1, :]
    _sample_mix(*staged, hgnw_ref, sinkrow_ref, seg_ref, exp_ref, ck_ref, cv_ref, mk_ref, mv_ref, s_ref,
                staged_att, staged_hg, nk_ref, nv_ref, ns_ref)
    att_scr[rows, :] = jnp.concatenate([staged_att[b] for b in range(t)], axis=0)
    hg_scr[rows, :] = jnp.concatenate([staged_hg[b] for b in range(t)], axis=0)

    @pl.when(tile == pl.num_programs(1) - 1)
    def _():
        x = (x_scr[...] + _dot(att_scr[...].astype(BF16), wo_ref[0:ATT_WIDTH, :])
             + _dot(hg_scr[...].astype(BF16), wo_ref[ATT_WIDTH:, :]))
        x_scr[...] = x
        y_ref[...] = x


def _small_path(sinks, x0, norm_w, w_in, qnw, knw, lb_logits, bd, w_out, hg_nw, sink_rows, seg, expand,
                cache_k, cache_v, meta_k, meta_v, state):
    n_rows = x0.shape[0]
    n_seq, depth = state.shape[:2]
    t = SAMPLE_TILE
    dmat = jnp.asarray(_decay_sum_matrix(N_META), BF16)
    full = lambda a: pl.BlockSpec(a.shape, lambda l, i, s: (0,) * a.ndim)
    per_layer = lambda a: pl.BlockSpec((None,) + a.shape[1:], lambda l, i, s: (l,) + (0,) * (a.ndim - 1))
    cache = lambda a: pl.BlockSpec((t, None) + a.shape[2:], lambda l, i, s: (i, l) + (0,) * (a.ndim - 2))
    as_rows = lambda a: a[:, None, :]
    state_t = (HG_HEADS, HG_VDIM, HG_KDIM)
    widths = (ATT_WIDTH, 2 * KV_WIDTH, ATT_WIDTH, HG_WIDTH, HG_WIDTH, HG_WIDTH, HG_WIDTH, HG_WIDTH)
    operands = [x0, as_rows(norm_w), w_in, as_rows(qnw), as_rows(knw), lb_logits, bd, w_out, as_rows(hg_nw),
                as_rows(sink_rows), seg, expand, dmat, cache_k, cache_v, meta_k, meta_v, state]
    in_specs = [full(x0), per_layer(operands[1]), per_layer(w_in), per_layer(operands[3]), per_layer(operands[4]),
                full(lb_logits), full(bd), per_layer(w_out), per_layer(operands[8]), per_layer(operands[9]),
                full(seg), full(expand), full(dmat), cache(cache_k), cache(cache_v), cache(meta_k), cache(meta_v),
                cache(state)]
    out_shape = [jax.ShapeDtypeStruct((n_rows, D_MODEL), F32),
                 jax.ShapeDtypeStruct((depth, N_META, 2 * KV_WIDTH), F32),
                 jax.ShapeDtypeStruct((depth,) + state_t, F32),
                 jax.ShapeDtypeStruct(cache_k.shape, F32), jax.ShapeDtypeStruct(cache_v.shape, F32),
                 jax.ShapeDtypeStruct(state.shape, F32)]
    out_specs = [pl.BlockSpec((n_rows, D_MODEL), lambda l, i, s: (0, 0)),
                 pl.BlockSpec((None, N_META, 2 * KV_WIDTH), lambda l, i, s: (l, 0, 0)),
                 pl.BlockSpec((None,) + state_t, lambda l, i, s: (l, 0, 0, 0)),
                 cache(cache_k), cache(cache_v), cache(state)]
    scratch = ([pltpu.VMEM((n_rows, D_MODEL), F32), pltpu.VMEM((n_rows, ATT_WIDTH), F32),
                pltpu.VMEM((n_rows, HG_WIDTH), F32)]
               + [pltpu.VMEM((n_rows, w), F32) for w in widths]
               + [pltpu.VMEM((t, 1, w), F32) for w in widths]
               + [pltpu.VMEM((t, 1, ATT_WIDTH), F32), pltpu.VMEM((t, 1, HG_WIDTH), F32)])
    return pl.pallas_call(
        _small_path_kernel,
        grid_spec=pltpu.PrefetchScalarGridSpec(
            num_scalar_prefetch=1, grid=(depth, n_seq // t),
            in_specs=in_specs, out_specs=out_specs, scratch_shapes=scratch),
        out_shape=out_shape,
        compiler_params=pltpu.CompilerParams(dimension_semantics=("arbitrary", "arbitrary"),
                                             vmem_limit_bytes=VMEM_LIMIT),
        name="small_path",
    )(sinks, *operands)


def _g_major(a, axis):
    shape = a.shape
    a = a.reshape(shape[:axis] + (KV_HEADS, GQA_GROUP, HEAD_DIM) + shape[axis + 1:])
    a = jnp.swapaxes(a, axis, axis + 1)
    return a.reshape(shape)


def _constants():
    lane = np.arange(ATT_WIDTH)
    g_of, h_of = lane // KV_WIDTH, (lane % KV_WIDTH) // HEAD_DIM
    head = h_of * GQA_GROUP + g_of
    seg = (head[:, None] == np.arange(KV_WIDTH)[None, :]).astype(np.float32)
    grp = np.arange(256) // HEAD_DIM
    bd = (grp[:, None] == grp[None, :]).astype(np.float32) / HEAD_DIM
    return jnp.asarray(seg, BF16), jnp.asarray(seg.T, BF16), jnp.asarray(bd, BF16)


def kernel(x_prompt, x_sample, cache_win_k, cache_win_v, cache_meta_k, cache_meta_v, state_hgrn, meta_tokens,
           norm_w, w_in, q_norm_w, k_norm_w, attn_sinks, hg_lb_logits, hg_norm_w, w_out):
    batch, seq, _ = x_prompt.shape
    n_seq = x_sample.shape[0]
    depth = w_in.shape[0]
    w_buf = cache_win_k.shape[2]
    assert x_sample.shape[1] == 1 and w_buf == WINDOW and seq % ROW_TILE == 0 and n_seq % SAMPLE_TILE == 0

    seg, expand, bd = _constants()
    w_in_b = w_in.astype(BF16)
    w_out_b = jnp.concatenate([_g_major(w_out[:, :ATT_WIDTH], 1), w_out[:, ATT_WIDTH:]], axis=1).astype(BF16)
    qnw = jnp.tile(q_norm_w, (1, ATT_HEADS)) * (HEAD_DIM ** -0.5 * LOG2E)
    knw = jnp.tile(k_norm_w, (1, KV_HEADS))
    lb_logits = hg_lb_logits.astype(F32)
    sinks = attn_sinks.astype(F32) * LOG2E
    sink_rows = jnp.pad(sinks, ((0, 0), (0, KV_WIDTH - ATT_HEADS)))

    ck = cache_win_k.reshape(n_seq, depth, w_buf, KV_WIDTH)
    cv = cache_win_v.reshape(n_seq, depth, w_buf, KV_WIDTH)
    mk = cache_meta_k.reshape(n_seq, depth, N_META, KV_WIDTH)
    mv = cache_meta_v.reshape(n_seq, depth, N_META, KV_WIDTH)

    x_small = jnp.concatenate([x_sample.reshape(n_seq, D_MODEL), meta_tokens.astype(F32)], axis=0)
    y_small, kv_meta, s0t, new_k, new_v, new_state = _small_path(
        sinks, x_small, norm_w, w_in_b, qnw, knw, lb_logits, bd, w_out_b, hg_norm_w, sink_rows,
        seg, expand, ck, cv, mk, mv, state_hgrn)

    xp = x_prompt.reshape(batch * seq, D_MODEL)
    outs = {k: [] for k in ("wkp", "wvp", "hsp")}
    mix = None
    for l in range(depth):
        pr = _project(l, xp, norm_w[l][None], w_in_b[l], qnw[l][None], knw[l][None], lb_logits, bd,
                      row_tile=ROW_TILE, act_dtype=BF16, mix=mix)
        if mix is not None:
            xp, pr = pr[0], pr[1:]
        q, kv, sga, hq, hk, hv, g, sgh = pr
        att = _prompt_attention(sinks[l], q, kv, kv_meta[l], sga, batch, seq)
        hg, s_fin = _prompt_hgrn(hq, hk, hv, g, sgh, s0t[l], hg_norm_w[l][None], batch, seq)
        mix = (att, hg, w_out_b[l])

        kv3 = kv.reshape(batch, seq, 2 * KV_WIDTH)
        outs["wkp"].append(kv3[:, seq - w_buf:, :KV_WIDTH])
        outs["wvp"].append(kv3[:, seq - w_buf:, KV_WIDTH:])
        outs["hsp"].append(s_fin)
    xp = _out_project(mix[0], mix[1], xp, mix[2], ROW_TILE)

    stack = lambda name: jnp.stack(outs[name], axis=1)
    heads = lambda a: a.reshape(a.shape[:-1] + (KV_HEADS, HEAD_DIM))
    meta_rows = lambda a: jnp.broadcast_to(a[None], (batch,) + a.shape)
    return (xp.reshape(batch, seq, D_MODEL), y_small[:n_seq].reshape(n_seq, 1, D_MODEL),
            heads(stack("wkp")), heads(stack("wvp")),
            heads(meta_rows(kv_meta[:, :, :KV_WIDTH])), heads(meta_rows(kv_meta[:, :, KV_WIDTH:])), stack("hsp"),
            heads(new_k), heads(new_v), new_state)
```

```python
import functools

import numpy as np
import jax
import jax.numpy as jnp
from jax import lax
from jax.experimental import pallas as pl
from jax.experimental.pallas import tpu as pltpu

F32 = jnp.float32
BF16 = jnp.bfloat16

D_MODEL = 1024
N_META = 16
WINDOW = 128
HEAD_DIM = 64
ATT_WIDTH = 512
ATT_HEADS = 8
KV_HEADS = 2
GQA_GROUP = 4
KV_WIDTH = KV_HEADS * HEAD_DIM
HG_WIDTH = 512
HG_HEADS = 4
HG_KDIM = 128
HG_VDIM = 128
PROJ_WIDTH = 3328
EPS = 1e-6
NEG_BIG = -1e30
TINY = 1e-30
LOG2E = 1.4426950408889634

C_Q, C_K, C_V, C_GA, C_QH, C_FH, C_IH, C_GH = 0, 512, 640, 768, 1280, 1792, 2304, 2816

SUBLANES = 8
MXU_DEPTH = 256
HG_CHUNK = 64
ROW_TILE = 512
HG_TILE = 512
ATT_TILE = 512
SAMPLE_TILE = 8
VMEM_LIMIT = 48 * 1024 * 1024


def _dot(a, b):
    return jnp.dot(a, b, preferred_element_type=F32)


def _dot_nt(a, b):
    return lax.dot_general(a, b, (((1,), (1,)), ((), ())), preferred_element_type=F32)


def _dot_tn(a, b):
    return lax.dot_general(a, b, (((0,), (0,)), ((), ())), preferred_element_type=F32)


def _silu(x):
    return x * (1.0 / (1.0 + jnp.exp(-x)))


def _group_major(x):
    first = lax.broadcasted_iota(jnp.int32, (x.shape[0], KV_WIDTH), 1) < HEAD_DIM
    blocks = [x[:, j * KV_WIDTH:(j + 1) * KV_WIDTH] for j in range(GQA_GROUP)]
    swapped = [pltpu.roll(b, HEAD_DIM, 1) for b in blocks]
    out = []
    for g in range(GQA_GROUP):
        a, b = g // 2, GQA_GROUP // 2 + g // 2
        out.append(jnp.where(first, blocks[a], swapped[b]) if g % 2 == 0 else jnp.where(first, swapped[a], blocks[b]))
    return jnp.concatenate(out, axis=-1)


def _row_parts(n_rows, n_parts):
    step = n_rows // n_parts
    return [slice(i * step, (i + 1) * step) for i in range(n_parts)]


def _mix_proj_kernel(layer, n_parts, att_ref, hg_ref, wo_ref, x_ref, *rest):
    parts = _row_parts(x_ref.shape[0], n_parts)
    xnew_ref = rest[6]
    xs = []
    for rs in parts:
        x = (x_ref[rs, :] + _dot(att_ref[rs, :].astype(BF16), wo_ref[0:ATT_WIDTH, :])
             + _dot(hg_ref[rs, :].astype(BF16), wo_ref[ATT_WIDTH:, :]))
        xnew_ref[rs, :] = x
        xs.append(x)
    _proj_body(layer, parts, xs, *rest[:6], *rest[7:])


def _proj_kernel(layer, n_parts, x_ref, *rest):
    parts = _row_parts(x_ref.shape[0], n_parts)
    _proj_body(layer, parts, [x_ref[rs, :] for rs in parts], *rest)


def _proj_body(layer, parts, xs, nw_ref, w_ref, qnw_ref, knw_ref, lbl_ref, bd_ref,
               q_ref, kv_ref, sga_ref, hq_ref, hk_ref, hv_ref, g_ref, sgh_ref):
    nw = nw_ref[...]
    hs = []
    for x in xs:
        ms = jnp.mean(x * x, axis=-1, keepdims=True)
        hs.append((x * lax.rsqrt(ms + EPS) * nw).astype(BF16))

    def proj(h, lo, hi):
        return _dot(h, w_ref[:, lo:hi])

    pq = [[proj(h, C_Q + 256 * c, C_Q + 256 * (c + 1)) for c in range(2)] for h in hs]
    pk = [proj(h, C_K, C_V) for h in hs]
    sq = [[(p * p).astype(BF16) for p in pqs + [pks]] for pqs, pks in zip(pq, pk)]

    logits = lbl_ref[...]
    e = jnp.exp(logits - jnp.max(logits, axis=0, keepdims=True))
    p = e / jnp.sum(e, axis=0, keepdims=True)
    depth_row = lax.broadcasted_iota(jnp.int32, logits.shape, 0)
    in_range = jnp.where(depth_row >= 1, layer - depth_row, -1) >= 0
    lb = jnp.sum(jnp.where(in_range, p, 0.0), axis=0, keepdims=True)

    for rs, h in zip(parts, hs):
        kv_ref[rs, KV_WIDTH:2 * KV_WIDTH] = proj(h, C_V, C_GA)
        sga_ref[rs, :] = _group_major(_silu(proj(h, C_GA, C_QH))).astype(sga_ref.dtype)
        hq_ref[rs, :] = _silu(proj(h, C_QH, C_FH)).astype(hq_ref.dtype)

        z = proj(h, C_FH, C_IH)
        ez = jnp.exp(-jnp.abs(z))
        r = 1.0 / (1.0 + ez)
        pos = z >= 0.0
        sig_pos = jnp.where(pos, r, ez * r)
        sig_neg = jnp.where(pos, ez * r, r)
        hk_ref[rs, :] = ((1.0 - lb) * sig_neg).astype(hk_ref.dtype)
        f = lb + (1.0 - lb) * sig_pos
        g_ref[rs, :] = jnp.log(jnp.maximum(f, TINY))

        hv_ref[rs, :] = proj(h, C_IH, C_GH).astype(hv_ref.dtype)
        sgh_ref[rs, :] = _silu(proj(h, C_GH, PROJ_WIDTH)).astype(sgh_ref.dtype)

    bd = bd_ref[...]
    for rs, pqs, pks, sqs in zip(parts, pq, pk, sq):
        q = jnp.concatenate([p * lax.rsqrt(_dot(s, bd) + EPS) for p, s in zip(pqs, sqs[:2])], axis=-1)
        q_ref[rs, :] = _group_major(q * qnw_ref[...]).astype(q_ref.dtype)
        kv_ref[rs, 0:KV_WIDTH] = pks * lax.rsqrt(_dot(sqs[2], bd[:KV_WIDTH, :KV_WIDTH]) + EPS) * knw_ref[...]


def _project(layer, x, norm_w, w_in, qnw, knw, lb_logits, bd, row_tile, act_dtype, mix=None):
    n = x.shape[0]
    rows = lambda width: pl.BlockSpec((row_tile, width), lambda i: (i, 0))
    full = lambda shape: pl.BlockSpec(shape, lambda i: (0,) * len(shape))
    slab = lambda a, l: pl.BlockSpec((None,) + a.shape[1:], lambda i: (l, 0, 0))
    out = lambda width, dtype: jax.ShapeDtypeStruct((n, width), dtype)
    operands = [x, norm_w, w_in, qnw, knw, lb_logits, bd]
    in_specs = [rows(D_MODEL), full((1, D_MODEL)), slab(w_in, layer), full((1, ATT_WIDTH)),
                full((1, KV_WIDTH)), full(lb_logits.shape), full((256, 256))]
    out_specs = [rows(ATT_WIDTH), rows(2 * KV_WIDTH), rows(ATT_WIDTH), rows(HG_WIDTH), rows(HG_WIDTH),
                 rows(HG_WIDTH), rows(HG_WIDTH), rows(HG_WIDTH)]
    out_shape = [out(ATT_WIDTH, act_dtype), out(2 * KV_WIDTH, F32), out(ATT_WIDTH, act_dtype),
                 out(HG_WIDTH, act_dtype), out(HG_WIDTH, act_dtype), out(HG_WIDTH, act_dtype),
                 out(HG_WIDTH, F32), out(HG_WIDTH, act_dtype)]
    body = _proj_kernel
    if mix is not None:
        operands = list(mix) + operands
        in_specs = [rows(ATT_WIDTH), rows(HG_WIDTH), slab(mix[2], layer - 1)] + in_specs
        out_specs = [rows(D_MODEL)] + out_specs
        out_shape = [out(D_MODEL, F32)] + out_shape
        body = _mix_proj_kernel
    n_parts = 2 if row_tile % (2 * 128) == 0 else 1
    return pl.pallas_call(
        functools.partial(body, layer, n_parts),
        grid=(n // row_tile,),
        in_specs=in_specs, out_specs=out_specs, out_shape=out_shape,
        compiler_params=pltpu.CompilerParams(dimension_semantics=("arbitrary",), vmem_limit_bytes=VMEM_LIMIT),
        name="proj",
    )(*operands)


def _out_kernel(att_ref, hg_ref, x_ref, w_ref, y_ref):
    y_ref[...] = (x_ref[...] + _dot(att_ref[...].astype(BF16), w_ref[0:ATT_WIDTH, :])
                  + _dot(hg_ref[...].astype(BF16), w_ref[ATT_WIDTH:, :]))


def _out_project(layer, att, hg, x, w_out, row_tile):
    n = x.shape[0]
    rows = lambda width: pl.BlockSpec((row_tile, width), lambda i: (i, 0))
    return pl.pallas_call(
        _out_kernel,
        grid=(n // row_tile,),
        in_specs=[rows(ATT_WIDTH), rows(HG_WIDTH), rows(D_MODEL),
                  pl.BlockSpec((None, D_MODEL, D_MODEL), lambda i: (layer, 0, 0))],
        out_specs=rows(D_MODEL),
        out_shape=jax.ShapeDtypeStruct((n, D_MODEL), F32),
        compiler_params=pltpu.CompilerParams(dimension_semantics=("arbitrary",), vmem_limit_bytes=VMEM_LIMIT),
        name="out_proj",
    )(att, hg, x, w_out)


def _split_heads(k):
    first = lax.broadcasted_iota(jnp.int32, k.shape, 1) < HEAD_DIM
    return (jnp.where(first, k, 0.0).astype(BF16), jnp.where(first, 0.0, k).astype(BF16))


def _attend(q, key_sets, sink_of):
    m = q.shape[0]
    first = lax.broadcasted_iota(jnp.int32, (m, KV_WIDTH), 1) < HEAD_DIM
    blocks = []
    for g in range(GQA_GROUP):
        qg = q[:, g * KV_WIDTH:(g + 1) * KV_WIDTH]
        per_head = []
        for h in range(KV_HEADS):
            scores = []
            for k_heads, _, mask in key_sets:
                s = _dot_nt(qg, k_heads[h])
                if mask is not None:
                    s = jnp.where(mask, s, NEG_BIG)
                scores.append(s)
            sink = sink_of(h, g)
            mx = jnp.max(scores[0], axis=-1, keepdims=True)
            for s in scores[1:]:
                mx = jnp.maximum(mx, jnp.max(s, axis=-1, keepdims=True))
            mx = jnp.maximum(mx, sink)
            den = jnp.exp2(sink - mx)
            acc = jnp.zeros((m, KV_WIDTH), F32)
            for s, (_, v, _) in zip(scores, key_sets):
                p = jnp.exp2(s - mx)
                den = den + jnp.sum(p, axis=-1, keepdims=True)
                acc = acc + _dot(p.astype(BF16), v)
            per_head.append(acc * (1.0 / den))
        blocks.append(jnp.where(first, per_head[0], per_head[1]))
    return jnp.concatenate(blocks, axis=-1)


def _prompt_attn_kernel(sink_ref, q_ref, kvc_ref, kvp_ref, kvm_ref, sga_ref, o_ref):
    i = pl.program_id(1)
    n_blocks = q_ref.shape[0] // WINDOW
    n_keys = 2 * WINDOW + N_META
    kvm = kvm_ref[...]
    k_blocks = [_split_heads(kvp_ref[:, :KV_WIDTH])]
    vt_blocks = [kvp_ref[:, KV_WIDTH:].T.astype(BF16)]
    for j in range(n_blocks):
        rows = slice(j * WINDOW, (j + 1) * WINDOW)
        k_blocks.append(_split_heads(kvc_ref[rows, :KV_WIDTH]))
        vt_blocks.append(kvc_ref[rows, KV_WIDTH:].T.astype(BF16))
    k_meta = _split_heads(kvm[:, :KV_WIDTH])
    v_meta = jnp.concatenate([kvm[:, KV_WIDTH:], jnp.zeros((WINDOW - N_META, KV_WIDTH), F32)], axis=0)
    vt_meta = v_meta.T.astype(BF16)
    pad = jnp.zeros((3 * WINDOW - n_keys, WINDOW), BF16)

    key = lax.broadcasted_iota(jnp.int32, (n_keys, WINDOW), 0)
    qi = lax.broadcasted_iota(jnp.int32, (n_keys, WINDOW), 1)
    band = jnp.where(key < WINDOW, key - qi - 1, jnp.where(key < 2 * WINDOW, qi - (key - WINDOW), 0))
    visible = band >= 0
    no_prev = jnp.where(i > 0, 0, 2 * WINDOW)
    visible_first = jnp.where(key < WINDOW, band - no_prev, band) >= 0
    head0_rows = lax.broadcasted_iota(jnp.int32, (KV_WIDTH, WINDOW), 0) < HEAD_DIM
    group_lanes = [slice(g * KV_WIDTH, (g + 1) * KV_WIDTH) for g in range(GQA_GROUP)]

    s_both = []
    for j in range(n_blocks):
        rows = slice(j * WINDOW, (j + 1) * WINDOW)
        kk = jnp.concatenate([k_blocks[j][0], k_blocks[j + 1][0], k_meta[0],
                              k_blocks[j][1], k_blocks[j + 1][1], k_meta[1]], axis=0)
        s_both.append([_dot_nt(kk, q_ref[rows, lanes]) for lanes in group_lanes])
    probs = []
    for j in range(n_blocks):
        vis = visible_first if j == 0 else visible
        for g in range(GQA_GROUP):
            for h in range(KV_HEADS):
                sink = sink_ref[h * GQA_GROUP + g]
                s = jnp.where(vis, s_both[j][g][h * n_keys:(h + 1) * n_keys], NEG_BIG)
                mx = jnp.maximum(jnp.max(s, axis=0, keepdims=True), sink)
                p = jnp.exp2(s - mx)
                den = jnp.sum(p, axis=0, keepdims=True) + jnp.exp2(sink - mx)
                probs.append((jnp.concatenate([p.astype(BF16), pad], axis=0), 1.0 / den))
    outs = []
    for j in range(n_blocks):
        v_t = jnp.concatenate([vt_blocks[j], vt_blocks[j + 1], vt_meta], axis=1)
        for p_pad, inv in probs[j * ATT_HEADS:(j + 1) * ATT_HEADS]:
            outs.append(_dot(v_t, p_pad) * inv)
    for j in range(n_blocks):
        rows = slice(j * WINDOW, (j + 1) * WINDOW)
        for g, lanes in enumerate(group_lanes):
            pair = outs[j * ATT_HEADS + g * KV_HEADS:j * ATT_HEADS + (g + 1) * KV_HEADS]
            o_t = jnp.where(head0_rows, pair[0], pair[1])
            o_ref[rows, lanes] = (o_t.T * sga_ref[rows, lanes].astype(F32)).astype(o_ref.dtype)


def _prompt_attention(sinks, q, kv, kv_meta, sga, batch, seq):
    nb = seq // ATT_TILE
    per = ATT_TILE // WINDOW
    blk = lambda width: pl.BlockSpec((ATT_TILE, width), lambda b, i, s: (b * nb + i, 0))
    prev = pl.BlockSpec((WINDOW, 2 * KV_WIDTH), lambda b, i, s: ((b * nb + i) * per - jnp.minimum(i, 1), 0))
    meta = pl.BlockSpec((N_META, 2 * KV_WIDTH), lambda b, i, s: (0, 0))
    return pl.pallas_call(
        _prompt_attn_kernel,
        grid_spec=pltpu.PrefetchScalarGridSpec(
            num_scalar_prefetch=1, grid=(batch, nb),
            in_specs=[blk(ATT_WIDTH), blk(2 * KV_WIDTH), prev, meta, blk(ATT_WIDTH)],
            out_specs=blk(ATT_WIDTH)),
        out_shape=jax.ShapeDtypeStruct((batch * seq, ATT_WIDTH), BF16),
        compiler_params=pltpu.CompilerParams(dimension_semantics=("arbitrary", "arbitrary"),
                                             vmem_limit_bytes=VMEM_LIMIT),
        name="prompt_attn",
    )(sinks, q, kv, kv, kv_meta, sga)


def _level_sizes(c):
    return [c >> (i + 1) for i in range(c.bit_length() - 1)]


def _decay_sum_matrix(c):
    t = np.arange(c)[:, None]
    r = np.arange(c)[None, :]
    mats = [r <= t]
    for bs in _level_sizes(c):
        if bs < SUBLANES:
            a = (t // (2 * bs)) * (2 * bs) + bs - 1
            mats.append(((r > t) & (r <= a)) | ((r > a) & (r <= t)))
    return np.tile(np.concatenate(mats, axis=0).astype(np.float32), (1, _split_terms(c)))


def _split_terms(c):
    return min(3, MXU_DEPTH // c)


def _decay_sums(g, dmat):
    rest = g * LOG2E
    terms = []
    for _ in range(dmat.shape[1] // g.shape[0]):
        terms.append(rest.astype(BF16))
        rest = rest - terms[-1].astype(F32)
    return _dot(dmat, jnp.concatenate(terms, axis=0))


def _chunk_exponents(block, c):
    start = block(0)
    grp = lambda j: start[j * SUBLANES:(j + 1) * SUBLANES, :]
    lasts = {}

    def last(j):
        if j not in lasts:
            lasts[j] = jnp.broadcast_to(start[j * SUBLANES + SUBLANES - 1:(j + 1) * SUBLANES, :], (SUBLANES, HG_KDIM))
        return lasts[j]

    n_grp = c // SUBLANES
    end = jnp.concatenate([last(n_grp - 1) - grp(j) for j in range(n_grp)], axis=0)
    levels, fine = [], 1
    for bs in _level_sizes(c):
        if bs >= SUBLANES:
            per = bs // SUBLANES
            parts = []
            for j in range(n_grp):
                blk = j // per
                anchor = last((blk // 2) * 2 * per + per - 1)
                parts.append(grp(j) - anchor if blk % 2 == 1 else anchor - grp(j))
            levels.append(jnp.concatenate(parts, axis=0))
        else:
            levels.append(block(fine))
            fine += 1
    return start, end, levels


def _hgrn_masks(c):
    row = lax.broadcasted_iota(jnp.int32, (c, c), 0)
    col = lax.broadcasted_iota(jnp.int32, (c, c), 1)
    row_k = lax.broadcasted_iota(jnp.int32, (c, HG_KDIM), 0)
    levels = []
    for bs in _level_sizes(c):
        rb, cb = row // bs, col // bs
        pairs = ((rb % 2) * (1 - jnp.abs(cb - rb + 1))) > 0
        levels.append((bs, (row_k // bs) % 2 == 1, pairs))
    return levels, row == col


def _hgrn_chunks(items, states, masks):
    levels, diag = masks
    c = items[0][1].shape[0]

    stage1 = []
    for _, q, k, v, (_, to_end, level_ex) in items:
        pair_scores = []
        for (bs, q_side, _), lex in zip(levels, level_ex):
            if bs >= SUBLANES:
                side = jnp.concatenate(
                    [(q if (j // bs) % 2 == 1 else k)[j:j + SUBLANES] for j in range(0, c, SUBLANES)], axis=0)
            else:
                side = jnp.where(q_side, q, k)
            u = (side * jnp.exp2(lex)).astype(BF16)
            pair_scores.append(_dot_nt(u, u))
        kdec = (k * jnp.exp2(to_end)).astype(BF16)
        stage1.append((pair_scores, _dot_tn(v, kdec)))

    intra = []
    for (_, q, k, v, _), (pair_scores, _) in zip(items, stage1):
        a = jnp.where(diag, jnp.sum(q * k, axis=-1, keepdims=True), 0.0)
        for (_, _, pairs), scores in zip(levels, pair_scores):
            a = jnp.where(pairs, scores, a)
        intra.append(_dot(a.astype(BF16), v))

    outs = []
    for (head, q, _, _, (from_start, _, _)), (_, increment), o_intra in zip(items, stage1, intra):
        st = states[head]
        decay = jnp.exp2(from_start)
        outs.append(o_intra + _dot_nt((q * decay).astype(BF16), st.astype(BF16)))
        states[head] = st * decay[c - 1:c, :] + increment
    return outs


def _head_norm_gate(o, nw, gate):
    ms = jnp.mean(o * o, axis=-1, keepdims=True)
    return o * lax.rsqrt(ms + EPS) * nw * gate


def _prompt_hgrn_kernel(hq_ref, hk_ref, hv_ref, g_ref, sgh_ref, s0_ref, nw_ref, dmat_ref, o_ref, sfin_ref, st_ref):
    j = pl.program_id(1)

    @pl.when(j == 0)
    def _():
        st_ref[...] = s0_ref[...]

    masks = _hgrn_masks(HG_CHUNK)
    nw = nw_ref[...]
    dmat = dmat_ref[...]

    def block_of(sums, lanes):
        return lambda i: sums[i * HG_CHUNK:(i + 1) * HG_CHUNK, lanes]

    items, where = [], []
    for ci in range(HG_TILE // HG_CHUNK):
        rows = slice(ci * HG_CHUNK, (ci + 1) * HG_CHUNK)
        sums = _decay_sums(g_ref[rows, :], dmat)
        for h in range(HG_HEADS):
            lanes = slice(h * HG_KDIM, (h + 1) * HG_KDIM)
            items.append((h, hq_ref[rows, lanes].astype(F32), hk_ref[rows, lanes].astype(F32),
                          hv_ref[rows, lanes], _chunk_exponents(block_of(sums, lanes), HG_CHUNK)))
            where.append((rows, lanes))
    states = [st_ref[h] for h in range(HG_HEADS)]
    outs = _hgrn_chunks(items, states, masks)
    for (rows, lanes), o in zip(where, outs):
        o_ref[rows, lanes] = _head_norm_gate(o, nw, sgh_ref[rows, lanes].astype(F32)).astype(o_ref.dtype)
    for h in range(HG_HEADS):
        st_ref[h] = states[h]

    @pl.when(j == pl.num_programs(1) - 1)
    def _():
        for h in range(HG_HEADS):
            sfin_ref[h] = st_ref[h].T


def _prompt_hgrn(hq, hk, hv, g, sgh, s0t, hg_nw, batch, seq):
    nt = seq // HG_TILE
    blk = pl.BlockSpec((HG_TILE, HG_WIDTH), lambda b, j: (b * nt + j, 0))
    state_shape = (HG_HEADS, HG_VDIM, HG_KDIM)
    dmat = jnp.asarray(_decay_sum_matrix(HG_CHUNK), BF16)
    return pl.pallas_call(
        _prompt_hgrn_kernel,
        grid=(batch, nt),
        in_specs=[blk, blk, blk, blk, blk, pl.BlockSpec(state_shape, lambda b, j: (0, 0, 0)),
                  pl.BlockSpec((1, HG_VDIM), lambda b, j: (0, 0)), pl.BlockSpec(dmat.shape, lambda b, j: (0, 0))],
        out_specs=[blk, pl.BlockSpec((None,) + state_shape, lambda b, j: (b, 0, 0, 0))],
        out_shape=[jax.ShapeDtypeStruct((batch * seq, HG_WIDTH), BF16),
                   jax.ShapeDtypeStruct((batch,) + state_shape, F32)],
        scratch_shapes=[pltpu.VMEM(state_shape, F32)],
        compiler_params=pltpu.CompilerParams(dimension_semantics=("arbitrary", "arbitrary"),
                                             vmem_limit_bytes=VMEM_LIMIT),
        name="prompt_hgrn",
    )(hq, hk, hv, g, sgh, s0t, hg_nw, dmat)


def _meta_mix(sink_of, q_ref, kv_ref, sga_ref, hq_ref, hk_ref, hv_ref, g_ref, sgh_ref, nw_ref, dmat_ref,
              att_ref, hg_ref, st_ref):
    kv = kv_ref[...]
    row = lax.broadcasted_iota(jnp.int32, (N_META, N_META), 0)
    col = lax.broadcasted_iota(jnp.int32, (N_META, N_META), 1)
    att = _attend(q_ref[...].astype(BF16),
                  [(_split_heads(kv[:, :KV_WIDTH]), kv[:, KV_WIDTH:].astype(BF16), col <= row)], sink_of)
    att_ref[...] = (att * sga_ref[...]).astype(att_ref.dtype)

    sums = _decay_sums(g_ref[...], dmat_ref[...])
    masks = _hgrn_masks(N_META)
    head_lanes = [slice(h * HG_KDIM, (h + 1) * HG_KDIM) for h in range(HG_HEADS)]
    items = [(h, hq_ref[:, lanes], hk_ref[:, lanes], hv_ref[:, lanes].astype(BF16),
              _chunk_exponents((lambda lanes: lambda i: sums[i * N_META:(i + 1) * N_META, lanes])(lanes), N_META))
             for h, lanes in enumerate(head_lanes)]
    states = [jnp.zeros((HG_VDIM, HG_KDIM), F32) for _ in range(HG_HEADS)]
    outs = _hgrn_chunks(items, states, masks)
    for h, lanes in enumerate(head_lanes):
        st_ref[h] = states[h]
        hg_ref[:, lanes] = _head_norm_gate(outs[h], nw_ref[...], sgh_ref[:, lanes]).astype(hg_ref.dtype)


def _sample_mix(q_ref, kv_ref, sga_ref, hq_ref, hk_ref, hv_ref, g_ref, sgh_ref, nw_ref,
                sink_ref, seg_ref, exp_ref, ck_ref, cv_ref, mk_ref, mv_ref, s_ref,
                att_ref, hg_ref, nk_ref, nv_ref, ns_ref):
    n_keys = WINDOW + N_META + SUBLANES
    key_row = lax.broadcasted_iota(jnp.int32, (n_keys, KV_WIDTH), 0)
    visible = jnp.logical_and(key_row >= 1, key_row <= WINDOW + N_META)
    win_row = lax.broadcasted_iota(jnp.int32, (WINDOW, KV_WIDTH), 0)
    sink = sink_ref[...]
    seg = seg_ref[...]
    expand = exp_ref[...]
    nw = nw_ref[...]

    group = 4

    def per_group(i, carry):
        seqs = [i * group + j for j in range(group)]
        new_rows, scores, values = [], [], []
        for b in seqs:
            kv_new = kv_ref[b]
            k_new = kv_new[:, 0:KV_WIDTH]
            v_new = kv_new[:, KV_WIDTH:2 * KV_WIDTH]
            new_rows.append((k_new, v_new))
            ck = ck_ref[b]
            cv = cv_ref[b]
            keys = jnp.concatenate([ck, mk_ref[b], jnp.broadcast_to(k_new, (SUBLANES, KV_WIDTH))], axis=0)
            values.append(jnp.concatenate([cv, mv_ref[b], jnp.broadcast_to(v_new, (SUBLANES, KV_WIDTH))], axis=0))
            prod = jnp.concatenate([keys] * GQA_GROUP, axis=1) * q_ref[b]
            scores.append(_dot(prod.astype(BF16), seg))
            nk_ref[b] = jnp.where(win_row == WINDOW - 1, k_new, pltpu.roll(ck, WINDOW - 1, 0))
            nv_ref[b] = jnp.where(win_row == WINDOW - 1, v_new, pltpu.roll(cv, WINDOW - 1, 0))

        wides = []
        for s in scores:
            s = jnp.where(visible, s, NEG_BIG)
            mx = jnp.maximum(jnp.max(s, axis=0, keepdims=True), sink)
            p = jnp.exp2(s - mx)
            den = jnp.sum(p, axis=0, keepdims=True) + jnp.exp2(sink - mx)
            p = p * (1.0 / den)
            wides.append(_dot(p.astype(BF16), expand))

        for b, wide, vals in zip(seqs, wides, values):
            att = jnp.sum(wide * jnp.concatenate([vals] * GQA_GROUP, axis=1), axis=0, keepdims=True)
            att_ref[b] = att * sga_ref[b]

            decay_row, hq_row, hk_row, hv_row, sgh_row = jnp.exp(g_ref[b]), hq_ref[b], hk_ref[b], hv_ref[b], sgh_ref[b]
            outs = []
            for h in range(HG_HEADS):
                lanes = slice(h * HG_KDIM, (h + 1) * HG_KDIM)
                col = lambda r: jnp.broadcast_to(r[:, lanes], (HG_KDIM, HG_KDIM)).T
                s1 = col(decay_row) * s_ref[b, h] + col(hk_row) * hv_row[:, lanes]
                ns_ref[b, h] = s1
                o = jnp.sum(col(hq_row) * s1, axis=0, keepdims=True)
                outs.append(_head_norm_gate(o, nw, sgh_row[:, lanes]))
            hg_ref[b] = jnp.concatenate(outs, axis=-1)
        return carry

    lax.fori_loop(0, q_ref.shape[0] // group, per_group, 0)


def _small_path_kernel(sinks_ref, x0_ref, nw_ref, w_ref, qnw_ref, knw_ref, lbl_ref, bd_ref, wo_ref, hgnw_ref,
                       sinkrow_ref, seg_ref, exp_ref, dmat_ref, ck_ref, cv_ref, mk_ref, mv_ref, s_ref,
                       y_ref, kvm_ref, s0t_ref, nk_ref, nv_ref, ns_ref,
                       x_scr, att_scr, hg_scr, *scr):
    proj_scr, staged, staged_att, staged_hg = scr[:8], scr[8:16], scr[16], scr[17]
    layer, tile = pl.program_id(0), pl.program_id(1)
    n_rows = x_scr.shape[0]
    n_seq = n_rows - N_META
    t = ck_ref.shape[0]

    @pl.when(jnp.logical_and(layer == 0, tile == 0))
    def _():
        x_scr[...] = x0_ref[...]

    @pl.when(tile == 0)
    def _():
        _proj_body(layer, [slice(0, n_rows)], [x_scr[...]], nw_ref, w_ref, qnw_ref, knw_ref, lbl_ref, bd_ref,
                   *proj_scr)
        meta = lambda ref: ref.at[n_seq:n_rows]
        _meta_mix(lambda h, g: sinks_ref[layer, h * GQA_GROUP + g], *[meta(r) for r in proj_scr], hgnw_ref,
                  dmat_ref, meta(att_scr), meta(hg_scr), s0t_ref)
        kvm_ref[...] = proj_scr[1][n_seq:n_rows, :]

    rows = pl.ds(pl.multiple_of(tile * t, t), t)
    for src, dst in zip(proj_scr, staged):
        block = src[rows, :]
        for b in range(t):
            dst[b] = block[b:b + 1, :]
    _sample_mix(*staged, hgnw_ref, sinkrow_ref, seg_ref, exp_ref, ck_ref, cv_ref, mk_ref, mv_ref, s_ref,
                staged_att, staged_hg, nk_ref, nv_ref, ns_ref)
    att_scr[rows, :] = jnp.concatenate([staged_att[b] for b in range(t)], axis=0)
    hg_scr[rows, :] = jnp.concatenate([staged_hg[b] for b in range(t)], axis=0)

    @pl.when(tile == pl.num_programs(1) - 1)
    def _():
        x = (x_scr[...] + _dot(att_scr[...].astype(BF16), wo_ref[0:ATT_WIDTH, :])
             + _dot(hg_scr[...].astype(BF16), wo_ref[ATT_WIDTH:, :]))
        x_scr[...] = x
        y_ref[...] = x


def _small_path(sinks, x0, norm_w, w_in, qnw, knw, lb_logits, bd, w_out, hg_nw, sink_rows, seg, expand,
                cache_k, cache_v, meta_k, meta_v, state):
    n_rows = x0.shape[0]
    n_seq, depth = state.shape[:2]
    t = SAMPLE_TILE
    dmat = jnp.asarray(_decay_sum_matrix(N_META), BF16)
    full = lambda a: pl.BlockSpec(a.shape, lambda l, i, s: (0,) * a.ndim)
    per_layer = lambda a: pl.BlockSpec((None,) + a.shape[1:], lambda l, i, s: (l,) + (0,) * (a.ndim - 1))
    cache = lambda a: pl.BlockSpec((t, None) + a.shape[2:], lambda l, i, s: (i, l) + (0,) * (a.ndim - 2))
    as_rows = lambda a: a[:, None, :]
    state_t = (HG_HEADS, HG_VDIM, HG_KDIM)
    widths = (ATT_WIDTH, 2 * KV_WIDTH, ATT_WIDTH, HG_WIDTH, HG_WIDTH, HG_WIDTH, HG_WIDTH, HG_WIDTH)
    operands = [x0, as_rows(norm_w), w_in, as_rows(qnw), as_rows(knw), lb_logits, bd, w_out, as_rows(hg_nw),
                as_rows(sink_rows), seg, expand, dmat, cache_k, cache_v, meta_k, meta_v, state]
    in_specs = [full(x0), per_layer(operands[1]), per_layer(w_in), per_layer(operands[3]), per_layer(operands[4]),
                full(lb_logits), full(bd), per_layer(w_out), per_layer(operands[8]), per_layer(operands[9]),
                full(seg), full(expand), full(dmat), cache(cache_k), cache(cache_v), cache(meta_k), cache(meta_v),
                cache(state)]
    out_shape = [jax.ShapeDtypeStruct((n_rows, D_MODEL), F32),
                 jax.ShapeDtypeStruct((depth, N_META, 2 * KV_WIDTH), F32),
                 jax.ShapeDtypeStruct((depth,) + state_t, F32),
                 jax.ShapeDtypeStruct(cache_k.shape, F32), jax.ShapeDtypeStruct(cache_v.shape, F32),
                 jax.ShapeDtypeStruct(state.shape, F32)]
    out_specs = [pl.BlockSpec((n_rows, D_MODEL), lambda l, i, s: (0, 0)),
                 pl.BlockSpec((None, N_META, 2 * KV_WIDTH), lambda l, i, s: (l, 0, 0)),
                 pl.BlockSpec((None,) + state_t, lambda l, i, s: (l, 0, 0, 0)),
                 cache(cache_k), cache(cache_v), cache(state)]
    scratch = ([pltpu.VMEM((n_rows, D_MODEL), F32), pltpu.VMEM((n_rows, ATT_WIDTH), F32),
                pltpu.VMEM((n_rows, HG_WIDTH), F32)]
               + [pltpu.VMEM((n_rows, w), F32) for w in widths]
               + [pltpu.VMEM((t, 1, w), F32) for w in widths]
               + [pltpu.VMEM((t, 1, ATT_WIDTH), F32), pltpu.VMEM((t, 1, HG_WIDTH), F32)])
    return pl.pallas_call(
        _small_path_kernel,
        grid_spec=pltpu.PrefetchScalarGridSpec(
            num_scalar_prefetch=1, grid=(depth, n_seq // t),
            in_specs=in_specs, out_specs=out_specs, scratch_shapes=scratch),
        out_shape=out_shape,
        compiler_params=pltpu.CompilerParams(dimension_semantics=("arbitrary", "arbitrary"),
                                             vmem_limit_bytes=VMEM_LIMIT),
        name="small_path",
    )(sinks, *operands)


def _g_major(a, axis):
    shape = a.shape
    a = a.reshape(shape[:axis] + (KV_HEADS, GQA_GROUP, HEAD_DIM) + shape[axis + 1:])
    a = jnp.swapaxes(a, axis, axis + 1)
    return a.reshape(shape)


def _constants():
    lane = np.arange(ATT_WIDTH)
    g_of, h_of = lane // KV_WIDTH, (lane % KV_WIDTH) // HEAD_DIM
    head = h_of * GQA_GROUP + g_of
    seg = (head[:, None] == np.arange(KV_WIDTH)[None, :]).astype(np.float32)
    grp = np.arange(256) // HEAD_DIM
    bd = (grp[:, None] == grp[None, :]).astype(np.float32) / HEAD_DIM
    return jnp.asarray(seg, BF16), jnp.asarray(seg.T, BF16), jnp.asarray(bd, BF16)


def kernel(x_prompt, x_sample, cache_win_k, cache_win_v, cache_meta_k, cache_meta_v, state_hgrn, meta_tokens,
           norm_w, w_in, q_norm_w, k_norm_w, attn_sinks, hg_lb_logits, hg_norm_w, w_out):
    batch, seq, _ = x_prompt.shape
    n_seq = x_sample.shape[0]
    depth = w_in.shape[0]
    w_buf = cache_win_k.shape[2]
    assert x_sample.shape[1] == 1 and w_buf == WINDOW and seq % ROW_TILE == 0 and n_seq % SAMPLE_TILE == 0

    seg, expand, bd = _constants()
    w_in_b = w_in.astype(BF16)
    w_out_b = jnp.concatenate([_g_major(w_out[:, :ATT_WIDTH], 1), w_out[:, ATT_WIDTH:]], axis=1).astype(BF16)
    qnw = jnp.tile(q_norm_w, (1, ATT_HEADS)) * (HEAD_DIM ** -0.5 * LOG2E)
    knw = jnp.tile(k_norm_w, (1, KV_HEADS))
    lb_logits = hg_lb_logits.astype(F32)
    sinks = attn_sinks.astype(F32) * LOG2E
    sink_rows = jnp.pad(sinks, ((0, 0), (0, KV_WIDTH - ATT_HEADS)))

    ck = cache_win_k.reshape(n_seq, depth, w_buf, KV_WIDTH)
    cv = cache_win_v.reshape(n_seq, depth, w_buf, KV_WIDTH)
    mk = cache_meta_k.reshape(n_seq, depth, N_META, KV_WIDTH)
    mv = cache_meta_v.reshape(n_seq, depth, N_META, KV_WIDTH)

    x_small = jnp.concatenate([x_sample.reshape(n_seq, D_MODEL), meta_tokens.astype(F32)], axis=0)
    y_small, kv_meta, s0t, new_k, new_v, new_state = _small_path(
        sinks, x_small, norm_w, w_in_b, qnw, knw, lb_logits, bd, w_out_b, hg_norm_w, sink_rows,
        seg, expand, ck, cv, mk, mv, state_hgrn)

    xp = x_prompt.reshape(batch * seq, D_MODEL)
    outs = {k: [] for k in ("wkp", "wvp", "hsp")}
    mix = None
    for l in range(depth):
        pr = _project(l, xp, norm_w[l][None], w_in_b, qnw[l][None], knw[l][None], lb_logits, bd,
                      row_tile=ROW_TILE, act_dtype=BF16, mix=mix)
        if mix is not None:
            xp, pr = pr[0], pr[1:]
        q, kv, sga, hq, hk, hv, g, sgh = pr
        att = _prompt_attention(sinks[l], q, kv, kv_meta[l], sga, batch, seq)
        hg, s_fin = _prompt_hgrn(hq, hk, hv, g, sgh, s0t[l], hg_norm_w[l][None], batch, seq)
        mix = (att, hg, w_out_b)

        kv3 = kv.reshape(batch, seq, 2 * KV_WIDTH)
        outs["wkp"].append(kv3[:, seq - w_buf:, :KV_WIDTH])
        outs["wvp"].append(kv3[:, seq - w_buf:, KV_WIDTH:])
        outs["hsp"].append(s_fin)
    xp = _out_project(depth - 1, mix[0], mix[1], xp, mix[2], ROW_TILE)

    stack = lambda name: jnp.stack(outs[name], axis=1)
    heads = lambda a: a.reshape(a.shape[:-1] + (KV_HEADS, HEAD_DIM))
    meta_rows = lambda a: jnp.broadcast_to(a[None], (batch,) + a.shape)
    return (xp.reshape(batch, seq, D_MODEL), y_small[:n_seq].reshape(n_seq, 1, D_MODEL),
            heads(stack("wkp")), heads(stack("wvp")),
            heads(meta_rows(kv_meta[:, :, :KV_WIDTH])), heads(meta_rows(kv_meta[:, :, KV_WIDTH:])), stack("hsp"),
            heads(new_k), heads(new_v), new_state)
```

```python
import functools

import numpy as np
import jax
import jax.numpy as jnp
from jax import lax
from jax.experimental import pallas as pl
from jax.experimental.pallas import tpu as pltpu

F32 = jnp.float32
BF16 = jnp.bfloat16

D_MODEL = 1024
N_META = 16
WINDOW = 128
HEAD_DIM = 64
ATT_WIDTH = 512
ATT_HEADS = 8
KV_HEADS = 2
GQA_GROUP = 4
KV_WIDTH = KV_HEADS * HEAD_DIM
HG_WIDTH = 512
HG_HEADS = 4
HG_KDIM = 128
HG_VDIM = 128
PROJ_WIDTH = 3328
EPS = 1e-6
NEG_BIG = -1e30
TINY = 1e-30
LOG2E = 1.4426950408889634

C_Q, C_K, C_V, C_GA, C_QH, C_FH, C_IH, C_GH = 0, 512, 640, 768, 1280, 1792, 2304, 2816

SUBLANES = 8
MXU_DEPTH = 256
HG_CHUNK = 64
ROW_TILE = 512
HG_TILE = 512
ATT_TILE = 512
SAMPLE_TILE = 8
VMEM_LIMIT = 48 * 1024 * 1024


def _dot(a, b):
    return jnp.dot(a, b, preferred_element_type=F32)


def _dot_nt(a, b):
    return lax.dot_general(a, b, (((1,), (1,)), ((), ())), preferred_element_type=F32)


def _dot_tn(a, b):
    return lax.dot_general(a, b, (((0,), (0,)), ((), ())), preferred_element_type=F32)


def _silu(x):
    return x * (1.0 / (1.0 + jnp.exp(-x)))


def _group_major(x):
    first = lax.broadcasted_iota(jnp.int32, (x.shape[0], KV_WIDTH), 1) < HEAD_DIM
    blocks = [x[:, j * KV_WIDTH:(j + 1) * KV_WIDTH] for j in range(GQA_GROUP)]
    swapped = [pltpu.roll(b, HEAD_DIM, 1) for b in blocks]
    out = []
    for g in range(GQA_GROUP):
        a, b = g // 2, GQA_GROUP // 2 + g // 2
        out.append(jnp.where(first, blocks[a], swapped[b]) if g % 2 == 0 else jnp.where(first, swapped[a], blocks[b]))
    return jnp.concatenate(out, axis=-1)


def _row_parts(n_rows, n_parts):
    step = n_rows // n_parts
    return [slice(i * step, (i + 1) * step) for i in range(n_parts)]


def _mix_proj_kernel(layer, n_parts, att_ref, hg_ref, wo_ref, x_ref, *rest):
    parts = _row_parts(x_ref.shape[0], n_parts)
    xnew_ref = rest[6]
    xs = []
    for rs in parts:
        x = (x_ref[rs, :] + _dot(att_ref[rs, :].astype(BF16), wo_ref[0:ATT_WIDTH, :])
             + _dot(hg_ref[rs, :].astype(BF16), wo_ref[ATT_WIDTH:, :]))
        xnew_ref[rs, :] = x
        xs.append(x)
    _proj_body(layer, parts, xs, *rest[:6], *_projection_views(*rest[7:]))


def _proj_kernel(layer, n_parts, x_ref, *rest):
    parts = _row_parts(x_ref.shape[0], n_parts)
    _proj_body(layer, parts, [x_ref[rs, :] for rs in parts], *rest[:6], *_projection_views(*rest[6:]))


QA_Q, QA_GATE = 0, 1
HI_Q, HI_K, HI_V, HI_GATE = 0, 1, 2, 3
GKV_G = 0
GKV_KV = HG_WIDTH // (2 * KV_WIDTH)


def _projection_views(qa_ref, hgin_ref, gkv_ref):
    col = lambda ref, j, width: ref.at[:, j * width:(j + 1) * width]
    return (col(qa_ref, QA_Q, ATT_WIDTH), col(gkv_ref, GKV_KV, 2 * KV_WIDTH), col(qa_ref, QA_GATE, ATT_WIDTH),
            col(hgin_ref, HI_Q, HG_WIDTH), col(hgin_ref, HI_K, HG_WIDTH), col(hgin_ref, HI_V, HG_WIDTH),
            col(gkv_ref, GKV_G, HG_WIDTH), col(hgin_ref, HI_GATE, HG_WIDTH))


def _proj_body(layer, parts, xs, nw_ref, w_ref, qnw_ref, knw_ref, lbl_ref, bd_ref,
               q_ref, kv_ref, sga_ref, hq_ref, hk_ref, hv_ref, g_ref, sgh_ref):
    nw = nw_ref[...]
    hs = []
    for x in xs:
        ms = jnp.mean(x * x, axis=-1, keepdims=True)
        hs.append((x * lax.rsqrt(ms + EPS) * nw).astype(BF16))

    def proj(h, lo, hi):
        return _dot(h, w_ref[:, lo:hi])

    pq = [[proj(h, C_Q + 256 * c, C_Q + 256 * (c + 1)) for c in range(2)] for h in hs]
    pk = [proj(h, C_K, C_V) for h in hs]
    sq = [[(p * p).astype(BF16) for p in pqs + [pks]] for pqs, pks in zip(pq, pk)]

    logits = lbl_ref[...]
    e = jnp.exp(logits - jnp.max(logits, axis=0, keepdims=True))
    p = e / jnp.sum(e, axis=0, keepdims=True)
    depth_row = lax.broadcasted_iota(jnp.int32, logits.shape, 0)
    in_range = jnp.where(depth_row >= 1, layer - depth_row, -1) >= 0
    lb = jnp.sum(jnp.where(in_range, p, 0.0), axis=0, keepdims=True)

    for rs, h in zip(parts, hs):
        kv_ref[rs, KV_WIDTH:2 * KV_WIDTH] = proj(h, C_V, C_GA)
        sga_ref[rs, :] = _group_major(_silu(proj(h, C_GA, C_QH))).astype(sga_ref.dtype)
        hq_ref[rs, :] = _silu(proj(h, C_QH, C_FH)).astype(hq_ref.dtype)

        z = proj(h, C_FH, C_IH)
        ez = jnp.exp(-jnp.abs(z))
        r = 1.0 / (1.0 + ez)
        pos = z >= 0.0
        sig_pos = jnp.where(pos, r, ez * r)
        sig_neg = jnp.where(pos, ez * r, r)
        hk_ref[rs, :] = ((1.0 - lb) * sig_neg).astype(hk_ref.dtype)
        f = lb + (1.0 - lb) * sig_pos
        g_ref[rs, :] = jnp.log(jnp.maximum(f, TINY))

        hv_ref[rs, :] = proj(h, C_IH, C_GH).astype(hv_ref.dtype)
        sgh_ref[rs, :] = _silu(proj(h, C_GH, PROJ_WIDTH)).astype(sgh_ref.dtype)

    bd = bd_ref[...]
    for rs, pqs, pks, sqs in zip(parts, pq, pk, sq):
        q = jnp.concatenate([p * lax.rsqrt(_dot(s, bd) + EPS) for p, s in zip(pqs, sqs[:2])], axis=-1)
        q_ref[rs, :] = _group_major(q * qnw_ref[...]).astype(q_ref.dtype)
        kv_ref[rs, 0:KV_WIDTH] = pks * lax.rsqrt(_dot(sqs[2], bd[:KV_WIDTH, :KV_WIDTH]) + EPS) * knw_ref[...]


def _project(layer, x, norm_w, w_in, qnw, knw, lb_logits, bd, row_tile, mix=None):
    n = x.shape[0]
    rows = lambda width: pl.BlockSpec((row_tile, width), lambda i: (i, 0))
    full = lambda shape: pl.BlockSpec(shape, lambda i: (0,) * len(shape))
    slab = lambda a, l: pl.BlockSpec((None,) + a.shape[1:], lambda i: (l, 0, 0))
    out = lambda width, dtype: jax.ShapeDtypeStruct((n, width), dtype)
    operands = [x, norm_w, w_in, qnw, knw, lb_logits, bd]
    in_specs = [rows(D_MODEL), full((1, D_MODEL)), slab(w_in, layer), full((1, ATT_WIDTH)),
                full((1, KV_WIDTH)), full(lb_logits.shape), full((256, 256))]
    widths = (2 * ATT_WIDTH, 4 * HG_WIDTH, HG_WIDTH + 2 * KV_WIDTH)
    out_specs = [rows(w) for w in widths]
    out_shape = [out(widths[0], BF16), out(widths[1], BF16), out(widths[2], F32)]
    body = _proj_kernel
    if mix is not None:
        operands = list(mix) + operands
        in_specs = [rows(ATT_WIDTH), rows(HG_WIDTH), slab(mix[2], layer - 1)] + in_specs
        out_specs = [rows(D_MODEL)] + out_specs
        out_shape = [out(D_MODEL, F32)] + out_shape
        body = _mix_proj_kernel
    n_parts = 2 if row_tile % (2 * 128) == 0 else 1
    return pl.pallas_call(
        functools.partial(body, layer, n_parts),
        grid=(n // row_tile,),
        in_specs=in_specs, out_specs=out_specs, out_shape=out_shape,
        compiler_params=pltpu.CompilerParams(dimension_semantics=("arbitrary",), vmem_limit_bytes=VMEM_LIMIT),
        name="proj",
    )(*operands)


def _out_kernel(att_ref, hg_ref, x_ref, w_ref, y_ref):
    y_ref[...] = (x_ref[...] + _dot(att_ref[...].astype(BF16), w_ref[0:ATT_WIDTH, :])
                  + _dot(hg_ref[...].astype(BF16), w_ref[ATT_WIDTH:, :]))


def _out_project(layer, att, hg, x, w_out, row_tile):
    n = x.shape[0]
    rows = lambda width: pl.BlockSpec((row_tile, width), lambda i: (i, 0))
    return pl.pallas_call(
        _out_kernel,
        grid=(n // row_tile,),
        in_specs=[rows(ATT_WIDTH), rows(HG_WIDTH), rows(D_MODEL),
                  pl.BlockSpec((None, D_MODEL, D_MODEL), lambda i: (layer, 0, 0))],
        out_specs=rows(D_MODEL),
        out_shape=jax.ShapeDtypeStruct((n, D_MODEL), F32),
        compiler_params=pltpu.CompilerParams(dimension_semantics=("arbitrary",), vmem_limit_bytes=VMEM_LIMIT),
        name="out_proj",
    )(att, hg, x, w_out)


def _split_heads(k):
    first = lax.broadcasted_iota(jnp.int32, k.shape, 1) < HEAD_DIM
    return (jnp.where(first, k, 0.0).astype(BF16), jnp.where(first, 0.0, k).astype(BF16))


def _attend(q, key_sets, sink_of):
    m = q.shape[0]
    first = lax.broadcasted_iota(jnp.int32, (m, KV_WIDTH), 1) < HEAD_DIM
    blocks = []
    for g in range(GQA_GROUP):
        qg = q[:, g * KV_WIDTH:(g + 1) * KV_WIDTH]
        per_head = []
        for h in range(KV_HEADS):
            scores = []
            for k_heads, _, mask in key_sets:
                s = _dot_nt(qg, k_heads[h])
                if mask is not None:
                    s = jnp.where(mask, s, NEG_BIG)
                scores.append(s)
            sink = sink_of(h, g)
            mx = jnp.max(scores[0], axis=-1, keepdims=True)
            for s in scores[1:]:
                mx = jnp.maximum(mx, jnp.max(s, axis=-1, keepdims=True))
            mx = jnp.maximum(mx, sink)
            den = jnp.exp2(sink - mx)
            acc = jnp.zeros((m, KV_WIDTH), F32)
            for s, (_, v, _) in zip(scores, key_sets):
                p = jnp.exp2(s - mx)
                den = den + jnp.sum(p, axis=-1, keepdims=True)
                acc = acc + _dot(p.astype(BF16), v)
            per_head.append(acc * (1.0 / den))
        blocks.append(jnp.where(first, per_head[0], per_head[1]))
    return jnp.concatenate(blocks, axis=-1)


def _prompt_attn_kernel(sink_ref, q_ref, kvc_ref, kvp_ref, kvm_ref, sga_ref, o_ref):
    i = pl.program_id(1)
    n_blocks = q_ref.shape[0] // WINDOW
    n_keys = 2 * WINDOW + N_META
    kvm = kvm_ref[...]
    k_blocks = [_split_heads(kvp_ref[:, :KV_WIDTH])]
    vt_blocks = [kvp_ref[:, KV_WIDTH:].T.astype(BF16)]
    for j in range(n_blocks):
        rows = slice(j * WINDOW, (j + 1) * WINDOW)
        k_blocks.append(_split_heads(kvc_ref[rows, :KV_WIDTH]))
        vt_blocks.append(kvc_ref[rows, KV_WIDTH:].T.astype(BF16))
    k_meta = _split_heads(kvm[:, :KV_WIDTH])
    v_meta = jnp.concatenate([kvm[:, KV_WIDTH:], jnp.zeros((WINDOW - N_META, KV_WIDTH), F32)], axis=0)
    vt_meta = v_meta.T.astype(BF16)
    pad = jnp.zeros((3 * WINDOW - n_keys, WINDOW), BF16)

    key = lax.broadcasted_iota(jnp.int32, (n_keys, WINDOW), 0)
    qi = lax.broadcasted_iota(jnp.int32, (n_keys, WINDOW), 1)
    band = jnp.where(key < WINDOW, key - qi - 1, jnp.where(key < 2 * WINDOW, qi - (key - WINDOW), 0))
    visible = band >= 0
    no_prev = jnp.where(i > 0, 0, 2 * WINDOW)
    visible_first = jnp.where(key < WINDOW, band - no_prev, band) >= 0
    head0_rows = lax.broadcasted_iota(jnp.int32, (KV_WIDTH, WINDOW), 0) < HEAD_DIM
    group_lanes = [slice(g * KV_WIDTH, (g + 1) * KV_WIDTH) for g in range(GQA_GROUP)]

    s_both = []
    for j in range(n_blocks):
        rows = slice(j * WINDOW, (j + 1) * WINDOW)
        kk = jnp.concatenate([k_blocks[j][0], k_blocks[j + 1][0], k_meta[0],
                              k_blocks[j][1], k_blocks[j + 1][1], k_meta[1]], axis=0)
        s_both.append([_dot_nt(kk, q_ref[rows, lanes]) for lanes in group_lanes])
    probs = []
    for j in range(n_blocks):
        vis = visible_first if j == 0 else visible
        for g in range(GQA_GROUP):
            for h in range(KV_HEADS):
                sink = sink_ref[h * GQA_GROUP + g]
                s = jnp.where(vis, s_both[j][g][h * n_keys:(h + 1) * n_keys], NEG_BIG)
                mx = jnp.maximum(jnp.max(s, axis=0, keepdims=True), sink)
                p = jnp.exp2(s - mx)
                den = jnp.sum(p, axis=0, keepdims=True) + jnp.exp2(sink - mx)
                probs.append((jnp.concatenate([p.astype(BF16), pad], axis=0), 1.0 / den))
    outs = []
    for j in range(n_blocks):
        v_t = jnp.concatenate([vt_blocks[j], vt_blocks[j + 1], vt_meta], axis=1)
        for p_pad, inv in probs[j * ATT_HEADS:(j + 1) * ATT_HEADS]:
            outs.append(_dot(v_t, p_pad) * inv)
    for j in range(n_blocks):
        rows = slice(j * WINDOW, (j + 1) * WINDOW)
        for g, lanes in enumerate(group_lanes):
            pair = outs[j * ATT_HEADS + g * KV_HEADS:j * ATT_HEADS + (g + 1) * KV_HEADS]
            o_t = jnp.where(head0_rows, pair[0], pair[1])
            o_ref[rows, lanes] = (o_t.T * sga_ref[rows, lanes].astype(F32)).astype(o_ref.dtype)


def _prompt_attention(sinks, qa, gkv, kv_meta, batch, seq):
    nb = seq // ATT_TILE
    per = ATT_TILE // WINDOW
    blk = lambda width, col: pl.BlockSpec((ATT_TILE, width), lambda b, i, s: (b * nb + i, col))
    prev = pl.BlockSpec((WINDOW, 2 * KV_WIDTH), lambda b, i, s: ((b * nb + i) * per - jnp.minimum(i, 1), GKV_KV))
    meta = pl.BlockSpec((N_META, 2 * KV_WIDTH), lambda b, i, s: (0, 0))
    return pl.pallas_call(
        _prompt_attn_kernel,
        grid_spec=pltpu.PrefetchScalarGridSpec(
            num_scalar_prefetch=1, grid=(batch, nb),
            in_specs=[blk(ATT_WIDTH, QA_Q), blk(2 * KV_WIDTH, GKV_KV), prev, meta, blk(ATT_WIDTH, QA_GATE)],
            out_specs=blk(ATT_WIDTH, 0)),
        out_shape=jax.ShapeDtypeStruct((batch * seq, ATT_WIDTH), BF16),
        compiler_params=pltpu.CompilerParams(dimension_semantics=("arbitrary", "arbitrary"),
                                             vmem_limit_bytes=VMEM_LIMIT),
        name="prompt_attn",
    )(sinks, qa, gkv, gkv, kv_meta, qa)


def _level_sizes(c):
    return [c >> (i + 1) for i in range(c.bit_length() - 1)]


def _decay_sum_matrix(c):
    t = np.arange(c)[:, None]
    r = np.arange(c)[None, :]
    mats = [r <= t]
    for bs in _level_sizes(c):
        if bs < SUBLANES:
            a = (t // (2 * bs)) * (2 * bs) + bs - 1
            mats.append(((r > t) & (r <= a)) | ((r > a) & (r <= t)))
    return np.tile(np.concatenate(mats, axis=0).astype(np.float32), (1, _split_terms(c)))


def _split_terms(c):
    return min(3, MXU_DEPTH // c)


def _decay_sums(g, dmat):
    rest = g * LOG2E
    terms = []
    for _ in range(dmat.shape[1] // g.shape[0]):
        terms.append(rest.astype(BF16))
        rest = rest - terms[-1].astype(F32)
    return _dot(dmat, jnp.concatenate(terms, axis=0))


def _chunk_exponents(block, c):
    start = block(0)
    grp = lambda j: start[j * SUBLANES:(j + 1) * SUBLANES, :]
    lasts = {}

    def last(j):
        if j not in lasts:
            lasts[j] = jnp.broadcast_to(start[j * SUBLANES + SUBLANES - 1:(j + 1) * SUBLANES, :], (SUBLANES, HG_KDIM))
        return lasts[j]

    n_grp = c // SUBLANES
    end = jnp.concatenate([last(n_grp - 1) - grp(j) for j in range(n_grp)], axis=0)
    levels, fine = [], 1
    for bs in _level_sizes(c):
        if bs >= SUBLANES:
            per = bs // SUBLANES
            parts = []
            for j in range(n_grp):
                blk = j // per
                anchor = last((blk // 2) * 2 * per + per - 1)
                parts.append(grp(j) - anchor if blk % 2 == 1 else anchor - grp(j))
            levels.append(jnp.concatenate(parts, axis=0))
        else:
            levels.append(block(fine))
            fine += 1
    return start, end, levels


def _hgrn_masks(c):
    row = lax.broadcasted_iota(jnp.int32, (c, c), 0)
    col = lax.broadcasted_iota(jnp.int32, (c, c), 1)
    row_k = lax.broadcasted_iota(jnp.int32, (c, HG_KDIM), 0)
    levels = []
    for bs in _level_sizes(c):
        rb, cb = row // bs, col // bs
        pairs = ((rb % 2) * (1 - jnp.abs(cb - rb + 1))) > 0
        levels.append((bs, (row_k // bs) % 2 == 1, pairs))
    return levels, row == col


def _hgrn_chunks(items, states, masks):
    levels, diag = masks
    c = items[0][1].shape[0]

    stage1 = []
    for _, q, k, v, (_, to_end, level_ex) in items:
        pair_scores = []
        for (bs, q_side, _), lex in zip(levels, level_ex):
            if bs >= SUBLANES:
                side = jnp.concatenate(
                    [(q if (j // bs) % 2 == 1 else k)[j:j + SUBLANES] for j in range(0, c, SUBLANES)], axis=0)
            else:
                side = jnp.where(q_side, q, k)
            u = (side * jnp.exp2(lex)).astype(BF16)
            pair_scores.append(_dot_nt(u, u))
        kdec = (k * jnp.exp2(to_end)).astype(BF16)
        stage1.append((pair_scores, _dot_tn(v, kdec)))

    intra = []
    for (_, q, k, v, _), (pair_scores, _) in zip(items, stage1):
        a = jnp.where(diag, jnp.sum(q * k, axis=-1, keepdims=True), 0.0)
        for (_, _, pairs), scores in zip(levels, pair_scores):
            a = jnp.where(pairs, scores, a)
        intra.append(_dot(a.astype(BF16), v))

    outs = []
    for (head, q, _, _, (from_start, _, _)), (_, increment), o_intra in zip(items, stage1, intra):
        st = states[head]
        decay = jnp.exp2(from_start)
        outs.append(o_intra + _dot_nt((q * decay).astype(BF16), st.astype(BF16)))
        states[head] = st * decay[c - 1:c, :] + increment
    return outs


def _head_norm_gate(o, nw, gate):
    ms = jnp.mean(o * o, axis=-1, keepdims=True)
    return o * lax.rsqrt(ms + EPS) * nw * gate


def _prompt_hgrn_kernel(hq_ref, hk_ref, hv_ref, g_ref, sgh_ref, s0_ref, nw_ref, dmat_ref, o_ref, sfin_ref, st_ref):
    j = pl.program_id(1)

    @pl.when(j == 0)
    def _():
        st_ref[...] = s0_ref[...]

    masks = _hgrn_masks(HG_CHUNK)
    nw = nw_ref[...]
    dmat = dmat_ref[...]

    def block_of(sums, lanes):
        return lambda i: sums[i * HG_CHUNK:(i + 1) * HG_CHUNK, lanes]

    items, where = [], []
    for ci in range(HG_TILE // HG_CHUNK):
        rows = slice(ci * HG_CHUNK, (ci + 1) * HG_CHUNK)
        sums = _decay_sums(g_ref[rows, :], dmat)
        for h in range(HG_HEADS):
            lanes = slice(h * HG_KDIM, (h + 1) * HG_KDIM)
            items.append((h, hq_ref[rows, lanes].astype(F32), hk_ref[rows, lanes].astype(F32),
                          hv_ref[rows, lanes], _chunk_exponents(block_of(sums, lanes), HG_CHUNK)))
            where.append((rows, lanes))
    states = [st_ref[h] for h in range(HG_HEADS)]
    outs = _hgrn_chunks(items, states, masks)
    for (rows, lanes), o in zip(where, outs):
        o_ref[rows, lanes] = _head_norm_gate(o, nw, sgh_ref[rows, lanes].astype(F32)).astype(o_ref.dtype)
    for h in range(HG_HEADS):
        st_ref[h] = states[h]

    @pl.when(j == pl.num_programs(1) - 1)
    def _():
        for h in range(HG_HEADS):
            sfin_ref[h] = st_ref[h].T


def _prompt_hgrn(hgin, gkv, s0t, hg_nw, batch, seq):
    nt = seq // HG_TILE
    blk = lambda col: pl.BlockSpec((HG_TILE, HG_WIDTH), lambda b, j: (b * nt + j, col))
    state_shape = (HG_HEADS, HG_VDIM, HG_KDIM)
    dmat = jnp.asarray(_decay_sum_matrix(HG_CHUNK), BF16)
    return pl.pallas_call(
        _prompt_hgrn_kernel,
        grid=(batch, nt),
        in_specs=[blk(HI_Q), blk(HI_K), blk(HI_V), blk(GKV_G), blk(HI_GATE),
                  pl.BlockSpec(state_shape, lambda b, j: (0, 0, 0)),
                  pl.BlockSpec((1, HG_VDIM), lambda b, j: (0, 0)), pl.BlockSpec(dmat.shape, lambda b, j: (0, 0))],
        out_specs=[blk(0), pl.BlockSpec((None,) + state_shape, lambda b, j: (b, 0, 0, 0))],
        out_shape=[jax.ShapeDtypeStruct((batch * seq, HG_WIDTH), BF16),
                   jax.ShapeDtypeStruct((batch,) + state_shape, F32)],
        scratch_shapes=[pltpu.VMEM(state_shape, F32)],
        compiler_params=pltpu.CompilerParams(dimension_semantics=("arbitrary", "arbitrary"),
                                             vmem_limit_bytes=VMEM_LIMIT),
        name="prompt_hgrn",
    )(hgin, hgin, hgin, gkv, hgin, s0t, hg_nw, dmat)


def _meta_mix(sink_of, q_ref, kv_ref, sga_ref, hq_ref, hk_ref, hv_ref, g_ref, sgh_ref, nw_ref, dmat_ref,
              att_ref, hg_ref, st_ref):
    kv = kv_ref[...]
    row = lax.broadcasted_iota(jnp.int32, (N_META, N_META), 0)
    col = lax.broadcasted_iota(jnp.int32, (N_META, N_META), 1)
    att = _attend(q_ref[...].astype(BF16),
                  [(_split_heads(kv[:, :KV_WIDTH]), kv[:, KV_WIDTH:].astype(BF16), col <= row)], sink_of)
    att_ref[...] = (att * sga_ref[...]).astype(att_ref.dtype)

    sums = _decay_sums(g_ref[...], dmat_ref[...])
    masks = _hgrn_masks(N_META)
    head_lanes = [slice(h * HG_KDIM, (h + 1) * HG_KDIM) for h in range(HG_HEADS)]
    items = [(h, hq_ref[:, lanes], hk_ref[:, lanes], hv_ref[:, lanes].astype(BF16),
              _chunk_exponents((lambda lanes: lambda i: sums[i * N_META:(i + 1) * N_META, lanes])(lanes), N_META))
             for h, lanes in enumerate(head_lanes)]
    states = [jnp.zeros((HG_VDIM, HG_KDIM), F32) for _ in range(HG_HEADS)]
    outs = _hgrn_chunks(items, states, masks)
    for h, lanes in enumerate(head_lanes):
        st_ref[h] = states[h]
        hg_ref[:, lanes] = _head_norm_gate(outs[h], nw_ref[...], sgh_ref[:, lanes]).astype(hg_ref.dtype)


def _sample_mix(q_ref, kv_ref, sga_ref, hq_ref, hk_ref, hv_ref, g_ref, sgh_ref, nw_ref,
                sink_ref, seg_ref, exp_ref, ck_ref, cv_ref, mk_ref, mv_ref, s_ref,
                att_ref, hg_ref, nk_ref, nv_ref, ns_ref):
    n_keys = WINDOW + N_META + SUBLANES
    key_row = lax.broadcasted_iota(jnp.int32, (n_keys, KV_WIDTH), 0)
    visible = jnp.logical_and(key_row >= 1, key_row <= WINDOW + N_META)
    win_row = lax.broadcasted_iota(jnp.int32, (WINDOW, KV_WIDTH), 0)
    sink = sink_ref[...]
    seg = seg_ref[...]
    expand = exp_ref[...]
    nw = nw_ref[...]

    group = 4

    def per_group(i, carry):
        seqs = [i * group + j for j in range(group)]
        new_rows, scores, values = [], [], []
        for b in seqs:
            kv_new = kv_ref[b]
            k_new = kv_new[:, 0:KV_WIDTH]
            v_new = kv_new[:, KV_WIDTH:2 * KV_WIDTH]
            new_rows.append((k_new, v_new))
            ck = ck_ref[b]
            cv = cv_ref[b]
            keys = jnp.concatenate([ck, mk_ref[b], jnp.broadcast_to(k_new, (SUBLANES, KV_WIDTH))], axis=0)
            values.append(jnp.concatenate([cv, mv_ref[b], jnp.broadcast_to(v_new, (SUBLANES, KV_WIDTH))], axis=0))
            prod = jnp.concatenate([keys] * GQA_GROUP, axis=1) * q_ref[b]
            scores.append(_dot(prod.astype(BF16), seg))
            nk_ref[b] = jnp.where(win_row == WINDOW - 1, k_new, pltpu.roll(ck, WINDOW - 1, 0))
            nv_ref[b] = jnp.where(win_row == WINDOW - 1, v_new, pltpu.roll(cv, WINDOW - 1, 0))

        wides = []
        for s in scores:
            s = jnp.where(visible, s, NEG_BIG)
            mx = jnp.maximum(jnp.max(s, axis=0, keepdims=True), sink)
            p = jnp.exp2(s - mx)
            den = jnp.sum(p, axis=0, keepdims=True) + jnp.exp2(sink - mx)
            p = p * (1.0 / den)
            wides.append(_dot(p.astype(BF16), expand))

        for b, wide, vals in zip(seqs, wides, values):
            att = jnp.sum(wide * jnp.concatenate([vals] * GQA_GROUP, axis=1), axis=0, keepdims=True)
            att_ref[b] = att * sga_ref[b]

            decay_row, hq_row, hk_row, hv_row, sgh_row = jnp.exp(g_ref[b]), hq_ref[b], hk_ref[b], hv_ref[b], sgh_ref[b]
            outs = []
            for h in range(HG_HEADS):
                lanes = slice(h * HG_KDIM, (h + 1) * HG_KDIM)
                col = lambda r: jnp.broadcast_to(r[:, lanes], (HG_KDIM, HG_KDIM)).T
                s1 = col(decay_row) * s_ref[b, h] + col(hk_row) * hv_row[:, lanes]
                ns_ref[b, h] = s1
                o = jnp.sum(col(hq_row) * s1, axis=0, keepdims=True)
                outs.append(_head_norm_gate(o, nw, sgh_row[:, lanes]))
            hg_ref[b] = jnp.concatenate(outs, axis=-1)
        return carry

    lax.fori_loop(0, q_ref.shape[0] // group, per_group, 0)


def _small_path_kernel(sinks_ref, x0_ref, nw_ref, w_ref, qnw_ref, knw_ref, lbl_ref, bd_ref, wo_ref, hgnw_ref,
                       sinkrow_ref, seg_ref, exp_ref, dmat_ref, ck_ref, cv_ref, mk_ref, mv_ref, s_ref,
                       y_ref, kvm_ref, s0t_ref, nk_ref, nv_ref, ns_ref,
                       x_scr, att_scr, hg_scr, *scr):
    proj_scr, staged, staged_att, staged_hg = scr[:8], scr[8:16], scr[16], scr[17]
    layer, tile = pl.program_id(0), pl.program_id(1)
    n_rows = x_scr.shape[0]
    n_seq = n_rows - N_META
    t = ck_ref.shape[0]

    @pl.when(jnp.logical_and(layer == 0, tile == 0))
    def _():
        x_scr[...] = x0_ref[...]

    @pl.when(tile == 0)
    def _():
        _proj_body(layer, [slice(0, n_rows)], [x_scr[...]], nw_ref, w_ref, qnw_ref, knw_ref, lbl_ref, bd_ref,
                   *proj_scr)
        meta = lambda ref: ref.at[n_seq:n_rows]
        _meta_mix(lambda h, g: sinks_ref[layer, h * GQA_GROUP + g], *[meta(r) for r in proj_scr], hgnw_ref,
                  dmat_ref, meta(att_scr), meta(hg_scr), s0t_ref)
        kvm_ref[...] = proj_scr[1][n_seq:n_rows, :]

    rows = pl.ds(pl.multiple_of(tile * t, t), t)
    for src, dst in zip(proj_scr, staged):
        block = src[rows, :]
        for b in range(t):
            dst[b] = block[b:b + 1, :]
    _sample_mix(*staged, hgnw_ref, sinkrow_ref, seg_ref, exp_ref, ck_ref, cv_ref, mk_ref, mv_ref, s_ref,
                staged_att, staged_hg, nk_ref, nv_ref, ns_ref)
    att_scr[rows, :] = jnp.concatenate([staged_att[b] for b in range(t)], axis=0)
    hg_scr[rows, :] = jnp.concatenate([staged_hg[b] for b in range(t)], axis=0)

    @pl.when(tile == pl.num_programs(1) - 1)
    def _():
        x = (x_scr[...] + _dot(att_scr[...].astype(BF16), wo_ref[0:ATT_WIDTH, :])
             + _dot(hg_scr[...].astype(BF16), wo_ref[ATT_WIDTH:, :]))
        x_scr[...] = x
        y_ref[...] = x


def _small_path(sinks, x0, norm_w, w_in, qnw, knw, lb_logits, bd, w_out, hg_nw, sink_rows, seg, expand,
                cache_k, cache_v, meta_k, meta_v, state):
    n_rows = x0.shape[0]
    n_seq, depth = state.shape[:2]
    t = SAMPLE_TILE
    dmat = jnp.asarray(_decay_sum_matrix(N_META), BF16)
    full = lambda a: pl.BlockSpec(a.shape, lambda l, i, s: (0,) * a.ndim)
    per_layer = lambda a: pl.BlockSpec((None,) + a.shape[1:], lambda l, i, s: (l,) + (0,) * (a.ndim - 1))
    cache = lambda a: pl.BlockSpec((t, None) + a.shape[2:], lambda l, i, s: (i, l) + (0,) * (a.ndim - 2))
    as_rows = lambda a: a[:, None, :]
    state_t = (HG_HEADS, HG_VDIM, HG_KDIM)
    widths = (ATT_WIDTH, 2 * KV_WIDTH, ATT_WIDTH, HG_WIDTH, HG_WIDTH, HG_WIDTH, HG_WIDTH, HG_WIDTH)
    operands = [x0, as_rows(norm_w), w_in, as_rows(qnw), as_rows(knw), lb_logits, bd, w_out, as_rows(hg_nw),
                as_rows(sink_rows), seg, expand, dmat, cache_k, cache_v, meta_k, meta_v, state]
    in_specs = [full(x0), per_layer(operands[1]), per_layer(w_in), per_layer(operands[3]), per_layer(operands[4]),
                full(lb_logits), full(bd), per_layer(w_out), per_layer(operands[8]), per_layer(operands[9]),
                full(seg), full(expand), full(dmat), cache(cache_k), cache(cache_v), cache(meta_k), cache(meta_v),
                cache(state)]
    out_shape = [jax.ShapeDtypeStruct((n_rows, D_MODEL), F32),
                 jax.ShapeDtypeStruct((depth, N_META, 2 * KV_WIDTH), F32),
                 jax.ShapeDtypeStruct((depth,) + state_t, F32),
                 jax.ShapeDtypeStruct(cache_k.shape, F32), jax.ShapeDtypeStruct(cache_v.shape, F32),
                 jax.ShapeDtypeStruct(state.shape, F32)]
    out_specs = [pl.BlockSpec((n_rows, D_MODEL), lambda l, i, s: (0, 0)),
                 pl.BlockSpec((None, N_META, 2 * KV_WIDTH), lambda l, i, s: (l, 0, 0)),
                 pl.BlockSpec((None,) + state_t, lambda l, i, s: (l, 0, 0, 0)),
                 cache(cache_k), cache(cache_v), cache(state)]
    scratch = ([pltpu.VMEM((n_rows, D_MODEL), F32), pltpu.VMEM((n_rows, ATT_WIDTH), F32),
                pltpu.VMEM((n_rows, HG_WIDTH), F32)]
               + [pltpu.VMEM((n_rows, w), F32) for w in widths]
               + [pltpu.VMEM((t, 1, w), F32) for w in widths]
               + [pltpu.VMEM((t, 1, ATT_WIDTH), F32), pltpu.VMEM((t, 1, HG_WIDTH), F32)])
    return pl.pallas_call(
        _small_path_kernel,
        grid_spec=pltpu.PrefetchScalarGridSpec(
            num_scalar_prefetch=1, grid=(depth, n_seq // t),
            in_specs=in_specs, out_specs=out_specs, scratch_shapes=scratch),
        out_shape=out_shape,
        compiler_params=pltpu.CompilerParams(dimension_semantics=("arbitrary", "arbitrary"),
                                             vmem_limit_bytes=VMEM_LIMIT),
        name="small_path",
    )(sinks, *operands)


def _g_major(a, axis):
    shape = a.shape
    a = a.reshape(shape[:axis] + (KV_HEADS, GQA_GROUP, HEAD_DIM) + shape[axis + 1:])
    a = jnp.swapaxes(a, axis, axis + 1)
    return a.reshape(shape)


def _constants():
    lane = np.arange(ATT_WIDTH)
    g_of, h_of = lane // KV_WIDTH, (lane % KV_WIDTH) // HEAD_DIM
    head = h_of * GQA_GROUP + g_of
    seg = (head[:, None] == np.arange(KV_WIDTH)[None, :]).astype(np.float32)
    grp = np.arange(256) // HEAD_DIM
    bd = (grp[:, None] == grp[None, :]).astype(np.float32) / HEAD_DIM
    return jnp.asarray(seg, BF16), jnp.asarray(seg.T, BF16), jnp.asarray(bd, BF16)


def kernel(x_prompt, x_sample, cache_win_k, cache_win_v, cache_meta_k, cache_meta_v, state_hgrn, meta_tokens,
           norm_w, w_in, q_norm_w, k_norm_w, attn_sinks, hg_lb_logits, hg_norm_w, w_out):
    batch, seq, _ = x_prompt.shape
    n_seq = x_sample.shape[0]
    depth = w_in.shape[0]
    w_buf = cache_win_k.shape[2]
    assert x_sample.shape[1] == 1 and w_buf == WINDOW and seq % ROW_TILE == 0 and n_seq % SAMPLE_TILE == 0

    seg, expand, bd = _constants()
    w_in_b = w_in.astype(BF16)
    w_out_b = jnp.concatenate([_g_major(w_out[:, :ATT_WIDTH], 1), w_out[:, ATT_WIDTH:]], axis=1).astype(BF16)
    qnw = jnp.tile(q_norm_w, (1, ATT_HEADS)) * (HEAD_DIM ** -0.5 * LOG2E)
    knw = jnp.tile(k_norm_w, (1, KV_HEADS))
    lb_logits = hg_lb_logits.astype(F32)
    sinks = attn_sinks.astype(F32) * LOG2E
    sink_rows = jnp.pad(sinks, ((0, 0), (0, KV_WIDTH - ATT_HEADS)))

    ck = cache_win_k.reshape(n_seq, depth, w_buf, KV_WIDTH)
    cv = cache_win_v.reshape(n_seq, depth, w_buf, KV_WIDTH)
    mk = cache_meta_k.reshape(n_seq, depth, N_META, KV_WIDTH)
    mv = cache_meta_v.reshape(n_seq, depth, N_META, KV_WIDTH)

    x_small = jnp.concatenate([x_sample.reshape(n_seq, D_MODEL), meta_tokens.astype(F32)], axis=0)
    y_small, kv_meta, s0t, new_k, new_v, new_state = _small_path(
        sinks, x_small, norm_w, w_in_b, qnw, knw, lb_logits, bd, w_out_b, hg_norm_w, sink_rows,
        seg, expand, ck, cv, mk, mv, state_hgrn)

    xp = x_prompt.reshape(batch * seq, D_MODEL)
    outs = {k: [] for k in ("wkp", "wvp", "hsp")}
    mix = None
    for l in range(depth):
        pr = _project(l, xp, norm_w[l][None], w_in_b, qnw[l][None], knw[l][None], lb_logits, bd,
                      row_tile=ROW_TILE, mix=mix)
        if mix is not None:
            xp, pr = pr[0], pr[1:]
        qa, hgin, gkv = pr
        att = _prompt_attention(sinks[l], qa, gkv, kv_meta[l], batch, seq)
        hg, s_fin = _prompt_hgrn(hgin, gkv, s0t[l], hg_norm_w[l][None], batch, seq)
        mix = (att, hg, w_out_b)

        kv_last = gkv.reshape(batch, seq, -1)[:, seq - w_buf:, HG_WIDTH:]
        outs["wkp"].append(kv_last[:, :, :KV_WIDTH])
        outs["wvp"].append(kv_last[:, :, KV_WIDTH:])
        outs["hsp"].append(s_fin)
    xp = _out_project(depth - 1, mix[0], mix[1], xp, mix[2], ROW_TILE)

    stack = lambda name: jnp.stack(outs[name], axis=1)
    heads = lambda a: a.reshape(a.shape[:-1] + (KV_HEADS, HEAD_DIM))
    meta_rows = lambda a: jnp.broadcast_to(a[None], (batch,) + a.shape)
    return (xp.reshape(batch, seq, D_MODEL), y_small[:n_seq].reshape(n_seq, 1, D_MODEL),
            heads(stack("wkp")), heads(stack("wvp")),
            heads(meta_rows(kv_meta[:, :, :KV_WIDTH])), heads(meta_rows(kv_meta[:, :, KV_WIDTH:])), stack("hsp"),
            heads(new_k), heads(new_v), new_state)
```

```python
import functools

import numpy as np
import jax
import jax.numpy as jnp
from jax import lax
from jax.experimental import pallas as pl
from jax.experimental.pallas import tpu as pltpu

F32 = jnp.float32
BF16 = jnp.bfloat16

D_MODEL = 1024
N_META = 16
WINDOW = 128
HEAD_DIM = 64
ATT_WIDTH = 512
ATT_HEADS = 8
KV_HEADS = 2
GQA_GROUP = 4
KV_WIDTH = KV_HEADS * HEAD_DIM
HG_WIDTH = 512
HG_HEADS = 4
HG_KDIM = 128
HG_VDIM = 128
PROJ_WIDTH = 3328
EPS = 1e-6
NEG_BIG = -1e30
TINY = 1e-30
LOG2E = 1.4426950408889634

C_Q, C_K, C_V, C_GA, C_QH, C_FH, C_IH, C_GH = 0, 512, 640, 768, 1280, 1792, 2304, 2816

SUBLANES = 8
MXU_DEPTH = 256
HG_CHUNK = 64
ROW_TILE = 512
MIX_TILE = 512
SAMPLE_TILE = 8
VMEM_LIMIT = 48 * 1024 * 1024


def _dot(a, b):
    return jnp.dot(a, b, preferred_element_type=F32)


def _dot_nt(a, b):
    return lax.dot_general(a, b, (((1,), (1,)), ((), ())), preferred_element_type=F32)


def _dot_tn(a, b):
    return lax.dot_general(a, b, (((0,), (0,)), ((), ())), preferred_element_type=F32)


def _silu(x):
    return x * (1.0 / (1.0 + jnp.exp(-x)))


def _group_major(x):
    first = lax.broadcasted_iota(jnp.int32, (x.shape[0], KV_WIDTH), 1) < HEAD_DIM
    blocks = [x[:, j * KV_WIDTH:(j + 1) * KV_WIDTH] for j in range(GQA_GROUP)]
    swapped = [pltpu.roll(b, HEAD_DIM, 1) for b in blocks]
    out = []
    for g in range(GQA_GROUP):
        a, b = g // 2, GQA_GROUP // 2 + g // 2
        out.append(jnp.where(first, blocks[a], swapped[b]) if g % 2 == 0 else jnp.where(first, swapped[a], blocks[b]))
    return jnp.concatenate(out, axis=-1)


def _row_parts(n_rows, n_parts):
    step = n_rows // n_parts
    return [slice(i * step, (i + 1) * step) for i in range(n_parts)]


def _mix_proj_kernel(layer, n_parts, mix_ref, wo_ref, x_ref, *rest):
    parts = _row_parts(x_ref.shape[0], n_parts)
    xnew_ref = rest[6]
    xs = []
    for rs in parts:
        x = x_ref[rs, :] + _dot(mix_ref[rs, :], wo_ref[...])
        xnew_ref[rs, :] = x
        xs.append(x)
    _proj_body(layer, parts, xs, *rest[:6], *_projection_views(*rest[7:]))


def _proj_kernel(layer, n_parts, x_ref, *rest):
    parts = _row_parts(x_ref.shape[0], n_parts)
    _proj_body(layer, parts, [x_ref[rs, :] for rs in parts], *rest[:6], *_projection_views(*rest[6:]))


QA_Q, QA_GATE = 0, 1
HI_Q, HI_K, HI_V, HI_GATE = 0, 1, 2, 3
GKV_G = 0
GKV_KV = HG_WIDTH // (2 * KV_WIDTH)


def _projection_views(qa_ref, hgin_ref, gkv_ref):
    col = lambda ref, j, width: ref.at[:, j * width:(j + 1) * width]
    return (col(qa_ref, QA_Q, ATT_WIDTH), col(gkv_ref, GKV_KV, 2 * KV_WIDTH), col(qa_ref, QA_GATE, ATT_WIDTH),
            col(hgin_ref, HI_Q, HG_WIDTH), col(hgin_ref, HI_K, HG_WIDTH), col(hgin_ref, HI_V, HG_WIDTH),
            col(gkv_ref, GKV_G, HG_WIDTH), col(hgin_ref, HI_GATE, HG_WIDTH))


def _proj_body(layer, parts, xs, nw_ref, w_ref, qnw_ref, knw_ref, lbl_ref, bd_ref,
               q_ref, kv_ref, sga_ref, hq_ref, hk_ref, hv_ref, g_ref, sgh_ref):
    nw = nw_ref[...]
    hs = []
    for x in xs:
        ms = jnp.mean(x * x, axis=-1, keepdims=True)
        hs.append((x * lax.rsqrt(ms + EPS) * nw).astype(BF16))

    def proj(h, lo, hi):
        return _dot(h, w_ref[:, lo:hi])

    pq = [[proj(h, C_Q + 256 * c, C_Q + 256 * (c + 1)) for c in range(2)] for h in hs]
    pk = [proj(h, C_K, C_V) for h in hs]
    sq = [[(p * p).astype(BF16) for p in pqs + [pks]] for pqs, pks in zip(pq, pk)]

    logits = lbl_ref[...]
    e = jnp.exp(logits - jnp.max(logits, axis=0, keepdims=True))
    p = e / jnp.sum(e, axis=0, keepdims=True)
    depth_row = lax.broadcasted_iota(jnp.int32, logits.shape, 0)
    in_range = jnp.where(depth_row >= 1, layer - depth_row, -1) >= 0
    lb = jnp.sum(jnp.where(in_range, p, 0.0), axis=0, keepdims=True)

    for rs, h in zip(parts, hs):
        kv_ref[rs, KV_WIDTH:2 * KV_WIDTH] = proj(h, C_V, C_GA)
        sga_ref[rs, :] = _group_major(_silu(proj(h, C_GA, C_QH))).astype(sga_ref.dtype)
        hq_ref[rs, :] = _silu(proj(h, C_QH, C_FH)).astype(hq_ref.dtype)

        z = proj(h, C_FH, C_IH)
        ez = jnp.exp(-jnp.abs(z))
        r = 1.0 / (1.0 + ez)
        pos = z >= 0.0
        sig_pos = jnp.where(pos, r, ez * r)
        sig_neg = jnp.where(pos, ez * r, r)
        hk_ref[rs, :] = ((1.0 - lb) * sig_neg).astype(hk_ref.dtype)
        f = lb + (1.0 - lb) * sig_pos
        g_ref[rs, :] = jnp.log(jnp.maximum(f, TINY))

        hv_ref[rs, :] = proj(h, C_IH, C_GH).astype(hv_ref.dtype)
        sgh_ref[rs, :] = _silu(proj(h, C_GH, PROJ_WIDTH)).astype(sgh_ref.dtype)

    bd = bd_ref[...]
    for rs, pqs, pks, sqs in zip(parts, pq, pk, sq):
        q = jnp.concatenate([p * lax.rsqrt(_dot(s, bd) + EPS) for p, s in zip(pqs, sqs[:2])], axis=-1)
        q_ref[rs, :] = _group_major(q * qnw_ref[...]).astype(q_ref.dtype)
        kv_ref[rs, 0:KV_WIDTH] = pks * lax.rsqrt(_dot(sqs[2], bd[:KV_WIDTH, :KV_WIDTH]) + EPS) * knw_ref[...]


def _project(layer, x, norm_w, w_in, qnw, knw, lb_logits, bd, row_tile, mix=None):
    n = x.shape[0]
    rows = lambda width: pl.BlockSpec((row_tile, width), lambda i: (i, 0))
    full = lambda shape: pl.BlockSpec(shape, lambda i: (0,) * len(shape))
    slab = lambda a, l: pl.BlockSpec((None,) + a.shape[1:], lambda i: (l, 0, 0))
    out = lambda width, dtype: jax.ShapeDtypeStruct((n, width), dtype)
    operands = [x, norm_w, w_in, qnw, knw, lb_logits, bd]
    in_specs = [rows(D_MODEL), full((1, D_MODEL)), slab(w_in, layer), full((1, ATT_WIDTH)),
                full((1, KV_WIDTH)), full(lb_logits.shape), full((256, 256))]
    widths = (2 * ATT_WIDTH, 4 * HG_WIDTH, HG_WIDTH + 2 * KV_WIDTH)
    out_specs = [rows(w) for w in widths]
    out_shape = [out(widths[0], BF16), out(widths[1], BF16), out(widths[2], F32)]
    body = _proj_kernel
    if mix is not None:
        operands = list(mix) + operands
        in_specs = [rows(ATT_WIDTH + HG_WIDTH), slab(mix[1], layer - 1)] + in_specs
        out_specs = [rows(D_MODEL)] + out_specs
        out_shape = [out(D_MODEL, F32)] + out_shape
        body = _mix_proj_kernel
    n_parts = 2 if row_tile % (2 * 128) == 0 else 1
    return pl.pallas_call(
        functools.partial(body, layer, n_parts),
        grid=(n // row_tile,),
        in_specs=in_specs, out_specs=out_specs, out_shape=out_shape,
        compiler_params=pltpu.CompilerParams(dimension_semantics=("arbitrary",), vmem_limit_bytes=VMEM_LIMIT),
        name="proj",
    )(*operands)


def _out_kernel(mix_ref, x_ref, w_ref, y_ref):
    y_ref[...] = x_ref[...] + _dot(mix_ref[...], w_ref[...])


def _out_project(layer, mix, x, w_out, row_tile):
    n = x.shape[0]
    rows = lambda width: pl.BlockSpec((row_tile, width), lambda i: (i, 0))
    return pl.pallas_call(
        _out_kernel,
        grid=(n // row_tile,),
        in_specs=[rows(ATT_WIDTH + HG_WIDTH), rows(D_MODEL),
                  pl.BlockSpec((None, D_MODEL, D_MODEL), lambda i: (layer, 0, 0))],
        out_specs=rows(D_MODEL),
        out_shape=jax.ShapeDtypeStruct((n, D_MODEL), F32),
        compiler_params=pltpu.CompilerParams(dimension_semantics=("arbitrary",), vmem_limit_bytes=VMEM_LIMIT),
        name="out_proj",
    )(mix, x, w_out)


def _split_heads(k):
    first = lax.broadcasted_iota(jnp.int32, k.shape, 1) < HEAD_DIM
    return (jnp.where(first, k, 0.0).astype(BF16), jnp.where(first, 0.0, k).astype(BF16))


def _attend(q, key_sets, sink_of):
    m = q.shape[0]
    first = lax.broadcasted_iota(jnp.int32, (m, KV_WIDTH), 1) < HEAD_DIM
    blocks = []
    for g in range(GQA_GROUP):
        qg = q[:, g * KV_WIDTH:(g + 1) * KV_WIDTH]
        per_head = []
        for h in range(KV_HEADS):
            scores = []
            for k_heads, _, mask in key_sets:
                s = _dot_nt(qg, k_heads[h])
                if mask is not None:
                    s = jnp.where(mask, s, NEG_BIG)
                scores.append(s)
            sink = sink_of(h, g)
            mx = jnp.max(scores[0], axis=-1, keepdims=True)
            for s in scores[1:]:
                mx = jnp.maximum(mx, jnp.max(s, axis=-1, keepdims=True))
            mx = jnp.maximum(mx, sink)
            den = jnp.exp2(sink - mx)
            acc = jnp.zeros((m, KV_WIDTH), F32)
            for s, (_, v, _) in zip(scores, key_sets):
                p = jnp.exp2(s - mx)
                den = den + jnp.sum(p, axis=-1, keepdims=True)
                acc = acc + _dot(p.astype(BF16), v)
            per_head.append(acc * (1.0 / den))
        blocks.append(jnp.where(first, per_head[0], per_head[1]))
    return jnp.concatenate(blocks, axis=-1)


def _attn_stages(i, sink_ref, q_ref, kvc_ref, kvp_ref, kvm_ref, sga_ref, o_ref):
    n_blocks = q_ref.shape[0] // WINDOW
    n_keys = 2 * WINDOW + N_META
    kvm = kvm_ref[...]
    k_blocks = [_split_heads(kvp_ref[:, :KV_WIDTH])]
    vt_blocks = [kvp_ref[:, KV_WIDTH:].T.astype(BF16)]
    for j in range(n_blocks):
        rows = slice(j * WINDOW, (j + 1) * WINDOW)
        k_blocks.append(_split_heads(kvc_ref[rows, :KV_WIDTH]))
        vt_blocks.append(kvc_ref[rows, KV_WIDTH:].T.astype(BF16))
    k_meta = _split_heads(kvm[:, :KV_WIDTH])
    v_meta = jnp.concatenate([kvm[:, KV_WIDTH:], jnp.zeros((WINDOW - N_META, KV_WIDTH), F32)], axis=0)
    vt_meta = v_meta.T.astype(BF16)
    pad = jnp.zeros((3 * WINDOW - n_keys, WINDOW), BF16)

    key = lax.broadcasted_iota(jnp.int32, (n_keys, WINDOW), 0)
    qi = lax.broadcasted_iota(jnp.int32, (n_keys, WINDOW), 1)
    band = jnp.where(key < WINDOW, key - qi - 1, jnp.where(key < 2 * WINDOW, qi - (key - WINDOW), 0))
    visible = band >= 0
    no_prev = jnp.where(i > 0, 0, 2 * WINDOW)
    visible_first = jnp.where(key < WINDOW, band - no_prev, band) >= 0
    head0_rows = lax.broadcasted_iota(jnp.int32, (KV_WIDTH, WINDOW), 0) < HEAD_DIM
    group_lanes = [slice(g * KV_WIDTH, (g + 1) * KV_WIDTH) for g in range(GQA_GROUP)]

    s_both = []
    for j in range(n_blocks):
        rows = slice(j * WINDOW, (j + 1) * WINDOW)
        kk = jnp.concatenate([k_blocks[j][0], k_blocks[j + 1][0], k_meta[0],
                              k_blocks[j][1], k_blocks[j + 1][1], k_meta[1]], axis=0)
        s_both.append([_dot_nt(kk, q_ref[rows, lanes]) for lanes in group_lanes])
    yield
    probs = []
    for j in range(n_blocks):
        vis = visible_first if j == 0 else visible
        for g in range(GQA_GROUP):
            for h in range(KV_HEADS):
                sink = sink_ref[h * GQA_GROUP + g]
                s = jnp.where(vis, s_both[j][g][h * n_keys:(h + 1) * n_keys], NEG_BIG)
                mx = jnp.maximum(jnp.max(s, axis=0, keepdims=True), sink)
                p = jnp.exp2(s - mx)
                den = jnp.sum(p, axis=0, keepdims=True) + jnp.exp2(sink - mx)
                probs.append((jnp.concatenate([p.astype(BF16), pad], axis=0), 1.0 / den))
    outs = []
    for j in range(n_blocks):
        v_t = jnp.concatenate([vt_blocks[j], vt_blocks[j + 1], vt_meta], axis=1)
        for p_pad, inv in probs[j * ATT_HEADS:(j + 1) * ATT_HEADS]:
            outs.append(_dot(v_t, p_pad) * inv)
    yield
    for j in range(n_blocks):
        rows = slice(j * WINDOW, (j + 1) * WINDOW)
        for g, lanes in enumerate(group_lanes):
            pair = outs[j * ATT_HEADS + g * KV_HEADS:j * ATT_HEADS + (g + 1) * KV_HEADS]
            o_t = jnp.where(head0_rows, pair[0], pair[1])
            o_ref[rows, lanes] = (o_t.T * sga_ref[rows, lanes].astype(F32)).astype(o_ref.dtype)


def _level_sizes(c):
    return [c >> (i + 1) for i in range(c.bit_length() - 1)]


def _decay_sum_matrix(c):
    t = np.arange(c)[:, None]
    r = np.arange(c)[None, :]
    mats = [r <= t]
    for bs in _level_sizes(c):
        if bs < SUBLANES:
            a = (t // (2 * bs)) * (2 * bs) + bs - 1
            mats.append(((r > t) & (r <= a)) | ((r > a) & (r <= t)))
    return np.tile(np.concatenate(mats, axis=0).astype(np.float32), (1, _split_terms(c)))


def _split_terms(c):
    return min(3, MXU_DEPTH // c)


def _decay_sums(g, dmat):
    rest = g * LOG2E
    terms = []
    for _ in range(dmat.shape[1] // g.shape[0]):
        terms.append(rest.astype(BF16))
        rest = rest - terms[-1].astype(F32)
    return _dot(dmat, jnp.concatenate(terms, axis=0))


def _chunk_exponents(block, c):
    start = block(0)
    grp = lambda j: start[j * SUBLANES:(j + 1) * SUBLANES, :]
    lasts = {}

    def last(j):
        if j not in lasts:
            lasts[j] = jnp.broadcast_to(start[j * SUBLANES + SUBLANES - 1:(j + 1) * SUBLANES, :], (SUBLANES, HG_KDIM))
        return lasts[j]

    n_grp = c // SUBLANES
    end = jnp.concatenate([last(n_grp - 1) - grp(j) for j in range(n_grp)], axis=0)
    levels, fine = [], 1
    for bs in _level_sizes(c):
        if bs >= SUBLANES:
            per = bs // SUBLANES
            parts = []
            for j in range(n_grp):
                blk = j // per
                anchor = last((blk // 2) * 2 * per + per - 1)
                parts.append(grp(j) - anchor if blk % 2 == 1 else anchor - grp(j))
            levels.append(jnp.concatenate(parts, axis=0))
        else:
            levels.append(block(fine))
            fine += 1
    return start, end, levels


def _hgrn_masks(c):
    row = lax.broadcasted_iota(jnp.int32, (c, c), 0)
    col = lax.broadcasted_iota(jnp.int32, (c, c), 1)
    row_k = lax.broadcasted_iota(jnp.int32, (c, HG_KDIM), 0)
    levels = []
    for bs in _level_sizes(c):
        rb, cb = row // bs, col // bs
        pairs = ((rb % 2) * (1 - jnp.abs(cb - rb + 1))) > 0
        levels.append((bs, (row_k // bs) % 2 == 1, pairs))
    return levels, row == col


def _hgrn_chunks(items, states, masks):
    outs = []
    for _ in _hgrn_chunk_stages(items, states, masks, outs):
        pass
    return outs


def _hgrn_chunk_stages(items, states, masks, outs):
    levels, diag = masks
    c = items[0][1].shape[0]

    stage1 = []
    for _, q, k, v, (_, to_end, level_ex) in items:
        pair_scores = []
        for (bs, q_side, _), lex in zip(levels, level_ex):
            if bs >= SUBLANES:
                side = jnp.concatenate(
                    [(q if (j // bs) % 2 == 1 else k)[j:j + SUBLANES] for j in range(0, c, SUBLANES)], axis=0)
            else:
                side = jnp.where(q_side, q, k)
            u = (side * jnp.exp2(lex)).astype(BF16)
            pair_scores.append(_dot_nt(u, u))
        kdec = (k * jnp.exp2(to_end)).astype(BF16)
        stage1.append((pair_scores, _dot_tn(v, kdec)))
    yield

    intra = []
    for (_, q, k, v, _), (pair_scores, _) in zip(items, stage1):
        a = jnp.where(diag, jnp.sum(q * k, axis=-1, keepdims=True), 0.0)
        for (_, _, pairs), scores in zip(levels, pair_scores):
            a = jnp.where(pairs, scores, a)
        intra.append(_dot(a.astype(BF16), v))
    yield

    for (head, q, _, _, (from_start, _, _)), (_, increment), o_intra in zip(items, stage1, intra):
        st = states[head]
        decay = jnp.exp2(from_start)
        outs.append(o_intra + _dot_nt((q * decay).astype(BF16), st.astype(BF16)))
        states[head] = st * decay[c - 1:c, :] + increment


def _head_norm_gate(o, nw, gate):
    ms = jnp.mean(o * o, axis=-1, keepdims=True)
    return o * lax.rsqrt(ms + EPS) * nw * gate


def _hgrn_stages(hq_ref, hk_ref, hv_ref, g_ref, sgh_ref, nw_ref, dmat_ref, o_ref, st_ref):
    masks = _hgrn_masks(HG_CHUNK)
    dmat = dmat_ref[...]

    def block_of(sums, lanes):
        return lambda i: sums[i * HG_CHUNK:(i + 1) * HG_CHUNK, lanes]

    items, where = [], []
    for ci in range(hq_ref.shape[0] // HG_CHUNK):
        rows = slice(ci * HG_CHUNK, (ci + 1) * HG_CHUNK)
        sums = _decay_sums(g_ref[rows, :], dmat)
        for h in range(HG_HEADS):
            lanes = slice(h * HG_KDIM, (h + 1) * HG_KDIM)
            items.append((h, hq_ref[rows, lanes].astype(F32), hk_ref[rows, lanes].astype(F32),
                          hv_ref[rows, lanes], _chunk_exponents(block_of(sums, lanes), HG_CHUNK)))
            where.append((rows, lanes))
    yield
    states = [st_ref[h] for h in range(HG_HEADS)]
    outs = []
    yield from _hgrn_chunk_stages(items, states, masks, outs)
    nw = nw_ref[...]
    for (rows, lanes), o in zip(where, outs):
        o_ref[rows, lanes] = _head_norm_gate(o, nw, sgh_ref[rows, lanes].astype(F32)).astype(o_ref.dtype)
    for h in range(HG_HEADS):
        st_ref[h] = states[h]


def _mixer_kernel(sink_ref, q_ref, kvc_ref, kvp_ref, kvm_ref, sga_ref, hq_ref, hk_ref, hv_ref, g_ref, sgh_ref,
                  s0_ref, nw_ref, dmat_ref, mix_ref, sfin_ref, st_ref):
    j = pl.program_id(1)

    @pl.when(j == 0)
    def _():
        st_ref[...] = s0_ref[...]

    attention = _attn_stages(j, sink_ref, q_ref, kvc_ref, kvp_ref, kvm_ref, sga_ref, mix_ref.at[:, 0:ATT_WIDTH])
    hgrn = _hgrn_stages(hq_ref, hk_ref, hv_ref, g_ref, sgh_ref, nw_ref, dmat_ref, mix_ref.at[:, ATT_WIDTH:], st_ref)
    for stage in (hgrn, attention, hgrn, attention, hgrn, attention, hgrn):
        next(stage, None)
    for stage in (attention, hgrn):
        for _ in stage:
            pass

    @pl.when(j == pl.num_programs(1) - 1)
    def _():
        for h in range(HG_HEADS):
            sfin_ref[h] = st_ref[h].T


def _prompt_mixers(sinks, qa, hgin, gkv, kv_meta, s0t, hg_nw, batch, seq):
    nt = seq // MIX_TILE
    per = MIX_TILE // WINDOW
    blk = lambda width, col: pl.BlockSpec((MIX_TILE, width), lambda b, j, s: (b * nt + j, col))
    prev = pl.BlockSpec((WINDOW, 2 * KV_WIDTH), lambda b, j, s: ((b * nt + j) * per - jnp.minimum(j, 1), GKV_KV))
    full = lambda shape: pl.BlockSpec(shape, lambda b, j, s: (0,) * len(shape))
    state_shape = (HG_HEADS, HG_VDIM, HG_KDIM)
    dmat = jnp.asarray(_decay_sum_matrix(HG_CHUNK), BF16)
    return pl.pallas_call(
        _mixer_kernel,
        grid_spec=pltpu.PrefetchScalarGridSpec(
            num_scalar_prefetch=1, grid=(batch, nt),
            in_specs=[blk(ATT_WIDTH, QA_Q), blk(2 * KV_WIDTH, GKV_KV), prev, full((N_META, 2 * KV_WIDTH)),
                      blk(ATT_WIDTH, QA_GATE),
                      blk(HG_WIDTH, HI_Q), blk(HG_WIDTH, HI_K), blk(HG_WIDTH, HI_V), blk(HG_WIDTH, GKV_G),
                      blk(HG_WIDTH, HI_GATE), full(state_shape), full((1, HG_VDIM)), full(dmat.shape)],
            out_specs=[blk(ATT_WIDTH + HG_WIDTH, 0),
                       pl.BlockSpec((None,) + state_shape, lambda b, j, s: (b, 0, 0, 0))],
            scratch_shapes=[pltpu.VMEM(state_shape, F32)]),
        out_shape=[jax.ShapeDtypeStruct((batch * seq, ATT_WIDTH + HG_WIDTH), BF16),
                   jax.ShapeDtypeStruct((batch,) + state_shape, F32)],
        compiler_params=pltpu.CompilerParams(dimension_semantics=("arbitrary", "arbitrary"),
                                             vmem_limit_bytes=VMEM_LIMIT),
        name="mixers",
    )(sinks, qa, gkv, gkv, kv_meta, qa, hgin, hgin, hgin, gkv, hgin, s0t, hg_nw, dmat)


def _meta_mix(sink_of, q_ref, kv_ref, sga_ref, hq_ref, hk_ref, hv_ref, g_ref, sgh_ref, nw_ref, dmat_ref,
              att_ref, hg_ref, st_ref):
    kv = kv_ref[...]
    row = lax.broadcasted_iota(jnp.int32, (N_META, N_META), 0)
    col = lax.broadcasted_iota(jnp.int32, (N_META, N_META), 1)
    att = _attend(q_ref[...].astype(BF16),
                  [(_split_heads(kv[:, :KV_WIDTH]), kv[:, KV_WIDTH:].astype(BF16), col <= row)], sink_of)
    att_ref[...] = (att * sga_ref[...]).astype(att_ref.dtype)

    sums = _decay_sums(g_ref[...], dmat_ref[...])
    masks = _hgrn_masks(N_META)
    head_lanes = [slice(h * HG_KDIM, (h + 1) * HG_KDIM) for h in range(HG_HEADS)]
    items = [(h, hq_ref[:, lanes], hk_ref[:, lanes], hv_ref[:, lanes].astype(BF16),
              _chunk_exponents((lambda lanes: lambda i: sums[i * N_META:(i + 1) * N_META, lanes])(lanes), N_META))
             for h, lanes in enumerate(head_lanes)]
    states = [jnp.zeros((HG_VDIM, HG_KDIM), F32) for _ in range(HG_HEADS)]
    outs = _hgrn_chunks(items, states, masks)
    for h, lanes in enumerate(head_lanes):
        st_ref[h] = states[h]
        hg_ref[:, lanes] = _head_norm_gate(outs[h], nw_ref[...], sgh_ref[:, lanes]).astype(hg_ref.dtype)


def _sample_mix(q_ref, kv_ref, sga_ref, hq_ref, hk_ref, hv_ref, g_ref, sgh_ref, nw_ref,
                sink_ref, seg_ref, exp_ref, ck_ref, cv_ref, mk_ref, mv_ref, s_ref,
                att_ref, hg_ref, nk_ref, nv_ref, ns_ref):
    n_keys = WINDOW + N_META + SUBLANES
    key_row = lax.broadcasted_iota(jnp.int32, (n_keys, KV_WIDTH), 0)
    visible = jnp.logical_and(key_row >= 1, key_row <= WINDOW + N_META)
    win_row = lax.broadcasted_iota(jnp.int32, (WINDOW, KV_WIDTH), 0)
    sink = sink_ref[...]
    seg = seg_ref[...]
    expand = exp_ref[...]
    nw = nw_ref[...]

    group = 4

    def per_group(i, carry):
        seqs = [i * group + j for j in range(group)]
        new_rows, scores, values = [], [], []
        for b in seqs:
            kv_new = kv_ref[b]
            k_new = kv_new[:, 0:KV_WIDTH]
            v_new = kv_new[:, KV_WIDTH:2 * KV_WIDTH]
            new_rows.append((k_new, v_new))
            ck = ck_ref[b]
            cv = cv_ref[b]
            keys = jnp.concatenate([ck, mk_ref[b], jnp.broadcast_to(k_new, (SUBLANES, KV_WIDTH))], axis=0)
            values.append(jnp.concatenate([cv, mv_ref[b], jnp.broadcast_to(v_new, (SUBLANES, KV_WIDTH))], axis=0))
            prod = jnp.concatenate([keys] * GQA_GROUP, axis=1) * q_ref[b]
            scores.append(_dot(prod.astype(BF16), seg))
            nk_ref[b] = jnp.where(win_row == WINDOW - 1, k_new, pltpu.roll(ck, WINDOW - 1, 0))
            nv_ref[b] = jnp.where(win_row == WINDOW - 1, v_new, pltpu.roll(cv, WINDOW - 1, 0))

        wides = []
        for s in scores:
            s = jnp.where(visible, s, NEG_BIG)
            mx = jnp.maximum(jnp.max(s, axis=0, keepdims=True), sink)
            p = jnp.exp2(s - mx)
            den = jnp.sum(p, axis=0, keepdims=True) + jnp.exp2(sink - mx)
            p = p * (1.0 / den)
            wides.append(_dot(p.astype(BF16), expand))

        for b, wide, vals in zip(seqs, wides, values):
            att = jnp.sum(wide * jnp.concatenate([vals] * GQA_GROUP, axis=1), axis=0, keepdims=True)
            att_ref[b] = att * sga_ref[b]

            decay_row, hq_row, hk_row, hv_row, sgh_row = jnp.exp(g_ref[b]), hq_ref[b], hk_ref[b], hv_ref[b], sgh_ref[b]
            outs = []
            for h in range(HG_HEADS):
                lanes = slice(h * HG_KDIM, (h + 1) * HG_KDIM)
                col = lambda r: jnp.broadcast_to(r[:, lanes], (HG_KDIM, HG_KDIM)).T
                s1 = col(decay_row) * s_ref[b, h] + col(hk_row) * hv_row[:, lanes]
                ns_ref[b, h] = s1
                o = jnp.sum(col(hq_row) * s1, axis=0, keepdims=True)
                outs.append(_head_norm_gate(o, nw, sgh_row[:, lanes]))
            hg_ref[b] = jnp.concatenate(outs, axis=-1)
        return carry

    lax.fori_loop(0, q_ref.shape[0] // group, per_group, 0)


def _small_path_kernel(sinks_ref, x0_ref, nw_ref, w_ref, qnw_ref, knw_ref, lbl_ref, bd_ref, wo_ref, hgnw_ref,
                       sinkrow_ref, seg_ref, exp_ref, dmat_ref, ck_ref, cv_ref, mk_ref, mv_ref, s_ref,
                       y_ref, kvm_ref, s0t_ref, nk_ref, nv_ref, ns_ref,
                       x_scr, att_scr, hg_scr, *scr):
    proj_scr, staged, staged_att, staged_hg = scr[:8], scr[8:16], scr[16], scr[17]
    layer, tile = pl.program_id(0), pl.program_id(1)
    n_rows = x_scr.shape[0]
    n_seq = n_rows - N_META
    t = ck_ref.shape[0]

    @pl.when(jnp.logical_and(layer == 0, tile == 0))
    def _():
        x_scr[...] = x0_ref[...]

    @pl.when(tile == 0)
    def _():
        _proj_body(layer, [slice(0, n_rows)], [x_scr[...]], nw_ref, w_ref, qnw_ref, knw_ref, lbl_ref, bd_ref,
                   *proj_scr)
        meta = lambda ref: ref.at[n_seq:n_rows]
        _meta_mix(lambda h, g: sinks_ref[layer, h * GQA_GROUP + g], *[meta(r) for r in proj_scr], hgnw_ref,
                  dmat_ref, meta(att_scr), meta(hg_scr), s0t_ref)
        kvm_ref[...] = proj_scr[1][n_seq:n_rows, :]

    rows = pl.ds(pl.multiple_of(tile * t, t), t)
    for src, dst in zip(proj_scr, staged):
        block = src[rows, :]
        for b in range(t):
            dst[b] = block[b:b + 1, :]
    _sample_mix(*staged, hgnw_ref, sinkrow_ref, seg_ref, exp_ref, ck_ref, cv_ref, mk_ref, mv_ref, s_ref,
                staged_att, staged_hg, nk_ref, nv_ref, ns_ref)
    att_scr[rows, :] = jnp.concatenate([staged_att[b] for b in range(t)], axis=0)
    hg_scr[rows, :] = jnp.concatenate([staged_hg[b] for b in range(t)], axis=0)

    @pl.when(tile == pl.num_programs(1) - 1)
    def _():
        x = (x_scr[...] + _dot(att_scr[...].astype(BF16), wo_ref[0:ATT_WIDTH, :])
             + _dot(hg_scr[...].astype(BF16), wo_ref[ATT_WIDTH:, :]))
        x_scr[...] = x
        y_ref[...] = x


def _small_path(sinks, x0, norm_w, w_in, qnw, knw, lb_logits, bd, w_out, hg_nw, sink_rows, seg, expand,
                cache_k, cache_v, meta_k, meta_v, state):
    n_rows = x0.shape[0]
    n_seq, depth = state.shape[:2]
    t = SAMPLE_TILE
    dmat = jnp.asarray(_decay_sum_matrix(N_META), BF16)
    full = lambda a: pl.BlockSpec(a.shape, lambda l, i, s: (0,) * a.ndim)
    per_layer = lambda a: pl.BlockSpec((None,) + a.shape[1:], lambda l, i, s: (l,) + (0,) * (a.ndim - 1))
    cache = lambda a: pl.BlockSpec((t, None) + a.shape[2:], lambda l, i, s: (i, l) + (0,) * (a.ndim - 2))
    as_rows = lambda a: a[:, None, :]
    state_t = (HG_HEADS, HG_VDIM, HG_KDIM)
    widths = (ATT_WIDTH, 2 * KV_WIDTH, ATT_WIDTH, HG_WIDTH, HG_WIDTH, HG_WIDTH, HG_WIDTH, HG_WIDTH)
    operands = [x0, as_rows(norm_w), w_in, as_rows(qnw), as_rows(knw), lb_logits, bd, w_out, as_rows(hg_nw),
                as_rows(sink_rows), seg, expand, dmat, cache_k, cache_v, meta_k, meta_v, state]
    in_specs = [full(x0), per_layer(operands[1]), per_layer(w_in), per_layer(operands[3]), per_layer(operands[4]),
                full(lb_logits), full(bd), per_layer(w_out), per_layer(operands[8]), per_layer(operands[9]),
                full(seg), full(expand), full(dmat), cache(cache_k), cache(cache_v), cache(meta_k), cache(meta_v),
                cache(state)]
    out_shape = [jax.ShapeDtypeStruct((n_rows, D_MODEL), F32),
                 jax.ShapeDtypeStruct((depth, N_META, 2 * KV_WIDTH), F32),
                 jax.ShapeDtypeStruct((depth,) + state_t, F32),
                 jax.ShapeDtypeStruct(cache_k.shape, F32), jax.ShapeDtypeStruct(cache_v.shape, F32),
                 jax.ShapeDtypeStruct(state.shape, F32)]
    out_specs = [pl.BlockSpec((n_rows, D_MODEL), lambda l, i, s: (0, 0)),
                 pl.BlockSpec((None, N_META, 2 * KV_WIDTH), lambda l, i, s: (l, 0, 0)),
                 pl.BlockSpec((None,) + state_t, lambda l, i, s: (l, 0, 0, 0)),
                 cache(cache_k), cache(cache_v), cache(state)]
    scratch = ([pltpu.VMEM((n_rows, D_MODEL), F32), pltpu.VMEM((n_rows, ATT_WIDTH), F32),
                pltpu.VMEM((n_rows, HG_WIDTH), F32)]
               + [pltpu.VMEM((n_rows, w), F32) for w in widths]
               + [pltpu.VMEM((t, 1, w), F32) for w in widths]
               + [pltpu.VMEM((t, 1, ATT_WIDTH), F32), pltpu.VMEM((t, 1, HG_WIDTH), F32)])
    return pl.pallas_call(
        _small_path_kernel,
        grid_spec=pltpu.PrefetchScalarGridSpec(
            num_scalar_prefetch=1, grid=(depth, n_seq // t),
            in_specs=in_specs, out_specs=out_specs, scratch_shapes=scratch),
        out_shape=out_shape,
        compiler_params=pltpu.CompilerParams(dimension_semantics=("arbitrary", "arbitrary"),
                                             vmem_limit_bytes=VMEM_LIMIT),
        name="small_path",
    )(sinks, *operands)


def _g_major(a, axis):
    shape = a.shape
    a = a.reshape(shape[:axis] + (KV_HEADS, GQA_GROUP, HEAD_DIM) + shape[axis + 1:])
    a = jnp.swapaxes(a, axis, axis + 1)
    return a.reshape(shape)


def _constants():
    lane = np.arange(ATT_WIDTH)
    g_of, h_of = lane // KV_WIDTH, (lane % KV_WIDTH) // HEAD_DIM
    head = h_of * GQA_GROUP + g_of
    seg = (head[:, None] == np.arange(KV_WIDTH)[None, :]).astype(np.float32)
    grp = np.arange(256) // HEAD_DIM
    bd = (grp[:, None] == grp[None, :]).astype(np.float32) / HEAD_DIM
    return jnp.asarray(seg, BF16), jnp.asarray(seg.T, BF16), jnp.asarray(bd, BF16)


def kernel(x_prompt, x_sample, cache_win_k, cache_win_v, cache_meta_k, cache_meta_v, state_hgrn, meta_tokens,
           norm_w, w_in, q_norm_w, k_norm_w, attn_sinks, hg_lb_logits, hg_norm_w, w_out):
    batch, seq, _ = x_prompt.shape
    n_seq = x_sample.shape[0]
    depth = w_in.shape[0]
    w_buf = cache_win_k.shape[2]
    assert x_sample.shape[1] == 1 and w_buf == WINDOW and seq % ROW_TILE == 0 and n_seq % SAMPLE_TILE == 0

    seg, expand, bd = _constants()
    w_in_b = w_in.astype(BF16)
    w_out_b = jnp.concatenate([_g_major(w_out[:, :ATT_WIDTH], 1), w_out[:, ATT_WIDTH:]], axis=1).astype(BF16)
    qnw = jnp.tile(q_norm_w, (1, ATT_HEADS)) * (HEAD_DIM ** -0.5 * LOG2E)
    knw = jnp.tile(k_norm_w, (1, KV_HEADS))
    lb_logits = hg_lb_logits.astype(F32)
    sinks = attn_sinks.astype(F32) * LOG2E
    sink_rows = jnp.pad(sinks, ((0, 0), (0, KV_WIDTH - ATT_HEADS)))

    ck = cache_win_k.reshape(n_seq, depth, w_buf, KV_WIDTH)
    cv = cache_win_v.reshape(n_seq, depth, w_buf, KV_WIDTH)
    mk = cache_meta_k.reshape(n_seq, depth, N_META, KV_WIDTH)
    mv = cache_meta_v.reshape(n_seq, depth, N_META, KV_WIDTH)

    x_small = jnp.concatenate([x_sample.reshape(n_seq, D_MODEL), meta_tokens.astype(F32)], axis=0)
    y_small, kv_meta, s0t, new_k, new_v, new_state = _small_path(
        sinks, x_small, norm_w, w_in_b, qnw, knw, lb_logits, bd, w_out_b, hg_norm_w, sink_rows,
        seg, expand, ck, cv, mk, mv, state_hgrn)

    xp = x_prompt.reshape(batch * seq, D_MODEL)
    outs = {k: [] for k in ("wkp", "wvp", "hsp")}
    mix = None
    for l in range(depth):
        pr = _project(l, xp, norm_w[l][None], w_in_b, qnw[l][None], knw[l][None], lb_logits, bd,
                      row_tile=ROW_TILE, mix=mix)
        if mix is not None:
            xp, pr = pr[0], pr[1:]
        qa, hgin, gkv = pr
        mixed, s_fin = _prompt_mixers(sinks[l], qa, hgin, gkv, kv_meta[l], s0t[l], hg_norm_w[l][None], batch, seq)
        mix = (mixed, w_out_b)

        kv_last = gkv.reshape(batch, seq, -1)[:, seq - w_buf:, HG_WIDTH:]
        outs["wkp"].append(kv_last[:, :, :KV_WIDTH])
        outs["wvp"].append(kv_last[:, :, KV_WIDTH:])
        outs["hsp"].append(s_fin)
    xp = _out_project(depth - 1, mix[0], xp, mix[1], ROW_TILE)

    stack = lambda name: jnp.stack(outs[name], axis=1)
    heads = lambda a: a.reshape(a.shape[:-1] + (KV_HEADS, HEAD_DIM))
    meta_rows = lambda a: jnp.broadcast_to(a[None], (batch,) + a.shape)
    return (xp.reshape(batch, seq, D_MODEL), y_small[:n_seq].reshape(n_seq, 1, D_MODEL),
            heads(stack("wkp")), heads(stack("wvp")),
            heads(meta_rows(kv_meta[:, :, :KV_WIDTH])), heads(meta_rows(kv_meta[:, :, KV_WIDTH:])), stack("hsp"),
            heads(new_k), heads(new_v), new_state)
```

```python
import functools

import numpy as np
import jax
import jax.numpy as jnp
from jax import lax
from jax.experimental import pallas as pl
from jax.experimental.pallas import tpu as pltpu

F32 = jnp.float32
BF16 = jnp.bfloat16

D_MODEL = 1024
N_META = 16
WINDOW = 128
HEAD_DIM = 64
ATT_WIDTH = 512
ATT_HEADS = 8
KV_HEADS = 2
GQA_GROUP = 4
KV_WIDTH = KV_HEADS * HEAD_DIM
HG_WIDTH = 512
HG_HEADS = 4
HG_KDIM = 128
HG_VDIM = 128
PROJ_WIDTH = 3328
EPS = 1e-6
NEG_BIG = -1e30
TINY = 1e-30
LOG2E = 1.4426950408889634

C_Q, C_K, C_V, C_GA, C_QH, C_FH, C_IH, C_GH = 0, 512, 640, 768, 1280, 1792, 2304, 2816

SUBLANES = 8
MXU_DEPTH = 256
HG_CHUNK = 64
ROW_TILE = 512
MIX_TILE = 1024
SAMPLE_TILE = 8
VMEM_LIMIT = 48 * 1024 * 1024


def _dot(a, b):
    return jnp.dot(a, b, preferred_element_type=F32)


def _dot_nt(a, b):
    return lax.dot_general(a, b, (((1,), (1,)), ((), ())), preferred_element_type=F32)


def _dot_tn(a, b):
    return lax.dot_general(a, b, (((0,), (0,)), ((), ())), preferred_element_type=F32)


def _silu(x):
    return x * (1.0 / (1.0 + jnp.exp(-x)))


def _group_major(x):
    first = lax.broadcasted_iota(jnp.int32, (x.shape[0], KV_WIDTH), 1) < HEAD_DIM
    blocks = [x[:, j * KV_WIDTH:(j + 1) * KV_WIDTH] for j in range(GQA_GROUP)]
    swapped = [pltpu.roll(b, HEAD_DIM, 1) for b in blocks]
    out = []
    for g in range(GQA_GROUP):
        a, b = g // 2, GQA_GROUP // 2 + g // 2
        out.append(jnp.where(first, blocks[a], swapped[b]) if g % 2 == 0 else jnp.where(first, swapped[a], blocks[b]))
    return jnp.concatenate(out, axis=-1)


def _row_parts(n_rows, n_parts):
    step = n_rows // n_parts
    return [slice(i * step, (i + 1) * step) for i in range(n_parts)]


def _mix_proj_kernel(layer, n_parts, mix_ref, wo_ref, x_ref, *rest):
    parts = _row_parts(x_ref.shape[0], n_parts)
    xnew_ref = rest[6]
    xs = []
    for rs in parts:
        x = x_ref[rs, :] + _dot(mix_ref[rs, :], wo_ref[...])
        xnew_ref[rs, :] = x
        xs.append(x)
    _proj_body(layer, parts, xs, *rest[:6], *_projection_views(*rest[7:]))


def _proj_kernel(layer, n_parts, x_ref, *rest):
    parts = _row_parts(x_ref.shape[0], n_parts)
    _proj_body(layer, parts, [x_ref[rs, :] for rs in parts], *rest[:6], *_projection_views(*rest[6:]))


QA_Q, QA_GATE = 0, 1
HI_Q, HI_K, HI_V, HI_GATE = 0, 1, 2, 3
GKV_G = 0
GKV_KV = HG_WIDTH // (2 * KV_WIDTH)


def _projection_views(qa_ref, hgin_ref, gkv_ref):
    col = lambda ref, j, width: ref.at[:, j * width:(j + 1) * width]
    return (col(qa_ref, QA_Q, ATT_WIDTH), col(gkv_ref, GKV_KV, 2 * KV_WIDTH), col(qa_ref, QA_GATE, ATT_WIDTH),
            col(hgin_ref, HI_Q, HG_WIDTH), col(hgin_ref, HI_K, HG_WIDTH), col(hgin_ref, HI_V, HG_WIDTH),
            col(gkv_ref, GKV_G, HG_WIDTH), col(hgin_ref, HI_GATE, HG_WIDTH))


def _proj_body(layer, parts, xs, nw_ref, w_ref, qnw_ref, knw_ref, lbl_ref, bd_ref,
               q_ref, kv_ref, sga_ref, hq_ref, hk_ref, hv_ref, g_ref, sgh_ref):
    nw = nw_ref[...]
    hs = []
    for x in xs:
        ms = jnp.mean(x * x, axis=-1, keepdims=True)
        hs.append((x * lax.rsqrt(ms + EPS) * nw).astype(BF16))

    def proj(h, lo, hi):
        return _dot(h, w_ref[:, lo:hi])

    pq = [[proj(h, C_Q + 256 * c, C_Q + 256 * (c + 1)) for c in range(2)] for h in hs]
    pk = [proj(h, C_K, C_V) for h in hs]
    sq = [[(p * p).astype(BF16) for p in pqs + [pks]] for pqs, pks in zip(pq, pk)]

    logits = lbl_ref[...]
    e = jnp.exp(logits - jnp.max(logits, axis=0, keepdims=True))
    p = e / jnp.sum(e, axis=0, keepdims=True)
    depth_row = lax.broadcasted_iota(jnp.int32, logits.shape, 0)
    in_range = jnp.where(depth_row >= 1, layer - depth_row, -1) >= 0
    lb = jnp.sum(jnp.where(in_range, p, 0.0), axis=0, keepdims=True)

    for rs, h in zip(parts, hs):
        kv_ref[rs, KV_WIDTH:2 * KV_WIDTH] = proj(h, C_V, C_GA)
        sga_ref[rs, :] = _group_major(_silu(proj(h, C_GA, C_QH))).astype(sga_ref.dtype)
        hq_ref[rs, :] = _silu(proj(h, C_QH, C_FH)).astype(hq_ref.dtype)

        z = proj(h, C_FH, C_IH)
        ez = jnp.exp(-jnp.abs(z))
        r = 1.0 / (1.0 + ez)
        pos = z >= 0.0
        sig_pos = jnp.where(pos, r, ez * r)
        sig_neg = jnp.where(pos, ez * r, r)
        hk_ref[rs, :] = ((1.0 - lb) * sig_neg).astype(hk_ref.dtype)
        f = lb + (1.0 - lb) * sig_pos
        g_ref[rs, :] = jnp.log(jnp.maximum(f, TINY))

        hv_ref[rs, :] = proj(h, C_IH, C_GH).astype(hv_ref.dtype)
        sgh_ref[rs, :] = _silu(proj(h, C_GH, PROJ_WIDTH)).astype(sgh_ref.dtype)

    bd = bd_ref[...]
    for rs, pqs, pks, sqs in zip(parts, pq, pk, sq):
        q = jnp.concatenate([p * lax.rsqrt(_dot(s, bd) + EPS) for p, s in zip(pqs, sqs[:2])], axis=-1)
        q_ref[rs, :] = _group_major(q * qnw_ref[...]).astype(q_ref.dtype)
        kv_ref[rs, 0:KV_WIDTH] = pks * lax.rsqrt(_dot(sqs[2], bd[:KV_WIDTH, :KV_WIDTH]) + EPS) * knw_ref[...]


def _project(layer, x, norm_w, w_in, qnw, knw, lb_logits, bd, row_tile, mix=None):
    n = x.shape[0]
    rows = lambda width: pl.BlockSpec((row_tile, width), lambda i: (i, 0))
    full = lambda shape: pl.BlockSpec(shape, lambda i: (0,) * len(shape))
    slab = lambda a, l: pl.BlockSpec((None,) + a.shape[1:], lambda i: (l, 0, 0))
    out = lambda width, dtype: jax.ShapeDtypeStruct((n, width), dtype)
    operands = [x, norm_w, w_in, qnw, knw, lb_logits, bd]
    in_specs = [rows(D_MODEL), full((1, D_MODEL)), slab(w_in, layer), full((1, ATT_WIDTH)),
                full((1, KV_WIDTH)), full(lb_logits.shape), full((256, 256))]
    widths = (2 * ATT_WIDTH, 4 * HG_WIDTH, HG_WIDTH + 2 * KV_WIDTH)
    out_specs = [rows(w) for w in widths]
    out_shape = [out(widths[0], BF16), out(widths[1], BF16), out(widths[2], F32)]
    body = _proj_kernel
    if mix is not None:
        operands = list(mix) + operands
        in_specs = [rows(ATT_WIDTH + HG_WIDTH), slab(mix[1], layer - 1)] + in_specs
        out_specs = [rows(D_MODEL)] + out_specs
        out_shape = [out(D_MODEL, F32)] + out_shape
        body = _mix_proj_kernel
    n_parts = 4 if row_tile % (4 * 128) == 0 else 1
    return pl.pallas_call(
        functools.partial(body, layer, n_parts),
        grid=(n // row_tile,),
        in_specs=in_specs, out_specs=out_specs, out_shape=out_shape,
        compiler_params=pltpu.CompilerParams(dimension_semantics=("arbitrary",), vmem_limit_bytes=VMEM_LIMIT),
        name="proj",
    )(*operands)


def _out_kernel(mix_ref, x_ref, w_ref, y_ref):
    y_ref[...] = x_ref[...] + _dot(mix_ref[...], w_ref[...])


def _out_project(layer, mix, x, w_out, row_tile):
    n = x.shape[0]
    rows = lambda width: pl.BlockSpec((row_tile, width), lambda i: (i, 0))
    return pl.pallas_call(
        _out_kernel,
        grid=(n // row_tile,),
        in_specs=[rows(ATT_WIDTH + HG_WIDTH), rows(D_MODEL),
                  pl.BlockSpec((None, D_MODEL, D_MODEL), lambda i: (layer, 0, 0))],
        out_specs=rows(D_MODEL),
        out_shape=jax.ShapeDtypeStruct((n, D_MODEL), F32),
        compiler_params=pltpu.CompilerParams(dimension_semantics=("arbitrary",), vmem_limit_bytes=VMEM_LIMIT),
        name="out_proj",
    )(mix, x, w_out)


def _split_heads(k):
    first = lax.broadcasted_iota(jnp.int32, k.shape, 1) < HEAD_DIM
    return (jnp.where(first, k, 0.0).astype(BF16), jnp.where(first, 0.0, k).astype(BF16))


def _attend(q, key_sets, sink_of):
    m = q.shape[0]
    first = lax.broadcasted_iota(jnp.int32, (m, KV_WIDTH), 1) < HEAD_DIM
    blocks = []
    for g in range(GQA_GROUP):
        qg = q[:, g * KV_WIDTH:(g + 1) * KV_WIDTH]
        per_head = []
        for h in range(KV_HEADS):
            scores = []
            for k_heads, _, mask in key_sets:
                s = _dot_nt(qg, k_heads[h])
                if mask is not None:
                    s = jnp.where(mask, s, NEG_BIG)
                scores.append(s)
            sink = sink_of(h, g)
            mx = jnp.max(scores[0], axis=-1, keepdims=True)
            for s in scores[1:]:
                mx = jnp.maximum(mx, jnp.max(s, axis=-1, keepdims=True))
            mx = jnp.maximum(mx, sink)
            den = jnp.exp2(sink - mx)
            acc = jnp.zeros((m, KV_WIDTH), F32)
            for s, (_, v, _) in zip(scores, key_sets):
                p = jnp.exp2(s - mx)
                den = den + jnp.sum(p, axis=-1, keepdims=True)
                acc = acc + _dot(p.astype(BF16), v)
            per_head.append(acc * (1.0 / den))
        blocks.append(jnp.where(first, per_head[0], per_head[1]))
    return jnp.concatenate(blocks, axis=-1)


def _attn_stages(i, sink_ref, q_ref, kvc_ref, kvp_ref, kvm_ref, sga_ref, o_ref):
    n_blocks = q_ref.shape[0] // WINDOW
    n_keys = 2 * WINDOW + N_META
    kvm = kvm_ref[...]
    k_blocks = [_split_heads(kvp_ref[:, :KV_WIDTH])]
    vt_blocks = [kvp_ref[:, KV_WIDTH:].T.astype(BF16)]
    for j in range(n_blocks):
        rows = slice(j * WINDOW, (j + 1) * WINDOW)
        k_blocks.append(_split_heads(kvc_ref[rows, :KV_WIDTH]))
        vt_blocks.append(kvc_ref[rows, KV_WIDTH:].T.astype(BF16))
    k_meta = _split_heads(kvm[:, :KV_WIDTH])
    v_meta = jnp.concatenate([kvm[:, KV_WIDTH:], jnp.zeros((WINDOW - N_META, KV_WIDTH), F32)], axis=0)
    vt_meta = v_meta.T.astype(BF16)
    pad = jnp.zeros((3 * WINDOW - n_keys, WINDOW), BF16)

    key = lax.broadcasted_iota(jnp.int32, (n_keys, WINDOW), 0)
    qi = lax.broadcasted_iota(jnp.int32, (n_keys, WINDOW), 1)
    band = jnp.where(key < WINDOW, key - qi - 1, jnp.where(key < 2 * WINDOW, qi - (key - WINDOW), 0))
    visible = band >= 0
    no_prev = jnp.where(i > 0, 0, 2 * WINDOW)
    visible_first = jnp.where(key < WINDOW, band - no_prev, band) >= 0
    head0_rows = lax.broadcasted_iota(jnp.int32, (KV_WIDTH, WINDOW), 0) < HEAD_DIM
    group_lanes = [slice(g * KV_WIDTH, (g + 1) * KV_WIDTH) for g in range(GQA_GROUP)]

    s_both = []
    for j in range(n_blocks):
        rows = slice(j * WINDOW, (j + 1) * WINDOW)
        kk = jnp.concatenate([k_blocks[j][0], k_blocks[j + 1][0], k_meta[0],
                              k_blocks[j][1], k_blocks[j + 1][1], k_meta[1]], axis=0)
        s_both.append([_dot_nt(kk, q_ref[rows, lanes]) for lanes in group_lanes])
    yield
    probs = []
    for j in range(n_blocks):
        vis = visible_first if j == 0 else visible
        for g in range(GQA_GROUP):
            for h in range(KV_HEADS):
                sink = sink_ref[h * GQA_GROUP + g]
                s = jnp.where(vis, s_both[j][g][h * n_keys:(h + 1) * n_keys], NEG_BIG)
                mx = jnp.maximum(jnp.max(s, axis=0, keepdims=True), sink)
                p = jnp.exp2(s - mx)
                den = jnp.sum(p, axis=0, keepdims=True) + jnp.exp2(sink - mx)
                probs.append((jnp.concatenate([p.astype(BF16), pad], axis=0), 1.0 / den))
    outs = []
    for j in range(n_blocks):
        v_t = jnp.concatenate([vt_blocks[j], vt_blocks[j + 1], vt_meta], axis=1)
        for p_pad, inv in probs[j * ATT_HEADS:(j + 1) * ATT_HEADS]:
            outs.append(_dot(v_t, p_pad) * inv)
    yield
    for j in range(n_blocks):
        rows = slice(j * WINDOW, (j + 1) * WINDOW)
        for g, lanes in enumerate(group_lanes):
            pair = outs[j * ATT_HEADS + g * KV_HEADS:j * ATT_HEADS + (g + 1) * KV_HEADS]
            o_t = jnp.where(head0_rows, pair[0], pair[1])
            o_ref[rows, lanes] = (o_t.T * sga_ref[rows, lanes].astype(F32)).astype(o_ref.dtype)


def _level_sizes(c):
    return [c >> (i + 1) for i in range(c.bit_length() - 1)]


def _decay_sum_matrix(c):
    t = np.arange(c)[:, None]
    r = np.arange(c)[None, :]
    mats = [r <= t]
    for bs in _level_sizes(c):
        if bs < SUBLANES:
            a = (t // (2 * bs)) * (2 * bs) + bs - 1
            mats.append(((r > t) & (r <= a)) | ((r > a) & (r <= t)))
    return np.tile(np.concatenate(mats, axis=0).astype(np.float32), (1, _split_terms(c)))


def _split_terms(c):
    return min(3, MXU_DEPTH // c)


def _decay_sums(g, dmat):
    rest = g * LOG2E
    terms = []
    for _ in range(dmat.shape[1] // g.shape[0]):
        terms.append(rest.astype(BF16))
        rest = rest - terms[-1].astype(F32)
    return _dot(dmat, jnp.concatenate(terms, axis=0))


def _chunk_exponents(block, c):
    start = block(0)
    grp = lambda j: start[j * SUBLANES:(j + 1) * SUBLANES, :]
    lasts = {}

    def last(j):
        if j not in lasts:
            lasts[j] = jnp.broadcast_to(start[j * SUBLANES + SUBLANES - 1:(j + 1) * SUBLANES, :], (SUBLANES, HG_KDIM))
        return lasts[j]

    n_grp = c // SUBLANES
    end = jnp.concatenate([last(n_grp - 1) - grp(j) for j in range(n_grp)], axis=0)
    levels, fine = [], 1
    for bs in _level_sizes(c):
        if bs >= SUBLANES:
            per = bs // SUBLANES
            parts = []
            for j in range(n_grp):
                blk = j // per
                anchor = last((blk // 2) * 2 * per + per - 1)
                parts.append(grp(j) - anchor if blk % 2 == 1 else anchor - grp(j))
            levels.append(jnp.concatenate(parts, axis=0))
        else:
            levels.append(block(fine))
            fine += 1
    return start, end, levels


def _hgrn_masks(c):
    row = lax.broadcasted_iota(jnp.int32, (c, c), 0)
    col = lax.broadcasted_iota(jnp.int32, (c, c), 1)
    row_k = lax.broadcasted_iota(jnp.int32, (c, HG_KDIM), 0)
    levels = []
    for bs in _level_sizes(c):
        rb, cb = row // bs, col // bs
        pairs = ((rb % 2) * (1 - jnp.abs(cb - rb + 1))) > 0
        levels.append((bs, (row_k // bs) % 2 == 1, pairs))
    return levels, row == col


def _hgrn_chunks(items, states, masks):
    outs = []
    for _ in _hgrn_chunk_stages(items, states, masks, outs):
        pass
    return outs


def _hgrn_chunk_stages(items, states, masks, outs):
    levels, diag = masks
    c = items[0][1].shape[0]

    stage1 = []
    for _, q, k, v, (_, to_end, level_ex) in items:
        pair_scores = []
        for (bs, q_side, _), lex in zip(levels, level_ex):
            if bs >= SUBLANES:
                side = jnp.concatenate(
                    [(q if (j // bs) % 2 == 1 else k)[j:j + SUBLANES] for j in range(0, c, SUBLANES)], axis=0)
            else:
                side = jnp.where(q_side, q, k)
            u = (side * jnp.exp2(lex)).astype(BF16)
            pair_scores.append(_dot_nt(u, u))
        kdec = (k * jnp.exp2(to_end)).astype(BF16)
        stage1.append((pair_scores, _dot_tn(v, kdec)))
    yield

    intra = []
    for (_, q, k, v, _), (pair_scores, _) in zip(items, stage1):
        a = jnp.where(diag, jnp.sum(q * k, axis=-1, keepdims=True), 0.0)
        for (_, _, pairs), scores in zip(levels, pair_scores):
            a = jnp.where(pairs, scores, a)
        intra.append(_dot(a.astype(BF16), v))
    yield

    for (head, q, _, _, (from_start, _, _)), (_, increment), o_intra in zip(items, stage1, intra):
        st = states[head]
        decay = jnp.exp2(from_start)
        outs.append(o_intra + _dot_nt((q * decay).astype(BF16), st.astype(BF16)))
        states[head] = st * decay[c - 1:c, :] + increment


def _head_norm_gate(o, nw, gate):
    ms = jnp.mean(o * o, axis=-1, keepdims=True)
    return o * lax.rsqrt(ms + EPS) * nw * gate


def _hgrn_stages(hq_ref, hk_ref, hv_ref, g_ref, sgh_ref, nw_ref, dmat_ref, o_ref, st_ref):
    masks = _hgrn_masks(HG_CHUNK)
    dmat = dmat_ref[...]

    def block_of(sums, lanes):
        return lambda i: sums[i * HG_CHUNK:(i + 1) * HG_CHUNK, lanes]

    items, where = [], []
    for ci in range(hq_ref.shape[0] // HG_CHUNK):
        rows = slice(ci * HG_CHUNK, (ci + 1) * HG_CHUNK)
        sums = _decay_sums(g_ref[rows, :], dmat)
        for h in range(HG_HEADS):
            lanes = slice(h * HG_KDIM, (h + 1) * HG_KDIM)
            items.append((h, hq_ref[rows, lanes].astype(F32), hk_ref[rows, lanes].astype(F32),
                          hv_ref[rows, lanes], _chunk_exponents(block_of(sums, lanes), HG_CHUNK)))
            where.append((rows, lanes))
    yield
    states = [st_ref[h] for h in range(HG_HEADS)]
    outs = []
    yield from _hgrn_chunk_stages(items, states, masks, outs)
    nw = nw_ref[...]
    for (rows, lanes), o in zip(where, outs):
        o_ref[rows, lanes] = _head_norm_gate(o, nw, sgh_ref[rows, lanes].astype(F32)).astype(o_ref.dtype)
    for h in range(HG_HEADS):
        st_ref[h] = states[h]


def _mixer_kernel(sink_ref, q_ref, kvc_ref, kvp_ref, kvm_ref, sga_ref, hq_ref, hk_ref, hv_ref, g_ref, sgh_ref,
                  s0_ref, nw_ref, dmat_ref, mix_ref, sfin_ref, st_ref):
    j = pl.program_id(1)

    @pl.when(j == 0)
    def _():
        st_ref[...] = s0_ref[...]

    attention = _attn_stages(j, sink_ref, q_ref, kvc_ref, kvp_ref, kvm_ref, sga_ref, mix_ref.at[:, 0:ATT_WIDTH])
    hgrn = _hgrn_stages(hq_ref, hk_ref, hv_ref, g_ref, sgh_ref, nw_ref, dmat_ref, mix_ref.at[:, ATT_WIDTH:], st_ref)
    for stage in (hgrn, hgrn, attention, hgrn, attention, attention, hgrn):
        next(stage, None)
    for stage in (attention, hgrn):
        for _ in stage:
            pass

    @pl.when(j == pl.num_programs(1) - 1)
    def _():
        for h in range(HG_HEADS):
            sfin_ref[h] = st_ref[h].T


def _prompt_mixers(sinks, qa, hgin, gkv, kv_meta, s0t, hg_nw, batch, seq):
    nt = seq // MIX_TILE
    per = MIX_TILE // WINDOW
    blk = lambda width, col: pl.BlockSpec((MIX_TILE, width), lambda b, j, s: (b * nt + j, col))
    prev = pl.BlockSpec((WINDOW, 2 * KV_WIDTH), lambda b, j, s: ((b * nt + j) * per - jnp.minimum(j, 1), GKV_KV))
    full = lambda shape: pl.BlockSpec(shape, lambda b, j, s: (0,) * len(shape))
    state_shape = (HG_HEADS, HG_VDIM, HG_KDIM)
    dmat = jnp.asarray(_decay_sum_matrix(HG_CHUNK), BF16)
    return pl.pallas_call(
        _mixer_kernel,
        grid_spec=pltpu.PrefetchScalarGridSpec(
            num_scalar_prefetch=1, grid=(batch, nt),
            in_specs=[blk(ATT_WIDTH, QA_Q), blk(2 * KV_WIDTH, GKV_KV), prev, full((N_META, 2 * KV_WIDTH)),
                      blk(ATT_WIDTH, QA_GATE),
                      blk(HG_WIDTH, HI_Q), blk(HG_WIDTH, HI_K), blk(HG_WIDTH, HI_V), blk(HG_WIDTH, GKV_G),
                      blk(HG_WIDTH, HI_GATE), full(state_shape), full((1, HG_VDIM)), full(dmat.shape)],
            out_specs=[blk(ATT_WIDTH + HG_WIDTH, 0),
                       pl.BlockSpec((None,) + state_shape, lambda b, j, s: (b, 0, 0, 0))],
            scratch_shapes=[pltpu.VMEM(state_shape, F32)]),
        out_shape=[jax.ShapeDtypeStruct((batch * seq, ATT_WIDTH + HG_WIDTH), BF16),
                   jax.ShapeDtypeStruct((batch,) + state_shape, F32)],
        compiler_params=pltpu.CompilerParams(dimension_semantics=("arbitrary", "arbitrary"),
                                             vmem_limit_bytes=VMEM_LIMIT),
        name="mixers",
    )(sinks, qa, gkv, gkv, kv_meta, qa, hgin, hgin, hgin, gkv, hgin, s0t, hg_nw, dmat)


def _meta_mix(sink_of, q_ref, kv_ref, sga_ref, hq_ref, hk_ref, hv_ref, g_ref, sgh_ref, nw_ref, dmat_ref,
              att_ref, hg_ref, st_ref):
    kv = kv_ref[...]
    row = lax.broadcasted_iota(jnp.int32, (N_META, N_META), 0)
    col = lax.broadcasted_iota(jnp.int32, (N_META, N_META), 1)
    att = _attend(q_ref[...].astype(BF16),
                  [(_split_heads(kv[:, :KV_WIDTH]), kv[:, KV_WIDTH:].astype(BF16), col <= row)], sink_of)
    att_ref[...] = (att * sga_ref[...]).astype(att_ref.dtype)

    sums = _decay_sums(g_ref[...], dmat_ref[...])
    masks = _hgrn_masks(N_META)
    head_lanes = [slice(h * HG_KDIM, (h + 1) * HG_KDIM) for h in range(HG_HEADS)]
    items = [(h, hq_ref[:, lanes], hk_ref[:, lanes], hv_ref[:, lanes].astype(BF16),
              _chunk_exponents((lambda lanes: lambda i: sums[i * N_META:(i + 1) * N_META, lanes])(lanes), N_META))
             for h, lanes in enumerate(head_lanes)]
    states = [jnp.zeros((HG_VDIM, HG_KDIM), F32) for _ in range(HG_HEADS)]
    outs = _hgrn_chunks(items, states, masks)
    for h, lanes in enumerate(head_lanes):
        st_ref[h] = states[h]
        hg_ref[:, lanes] = _head_norm_gate(outs[h], nw_ref[...], sgh_ref[:, lanes]).astype(hg_ref.dtype)


def _sample_mix(q_ref, kv_ref, sga_ref, hq_ref, hk_ref, hv_ref, g_ref, sgh_ref, nw_ref,
                sink_ref, seg_ref, exp_ref, ck_ref, cv_ref, mk_ref, mv_ref, s_ref,
                att_ref, hg_ref, nk_ref, nv_ref, ns_ref):
    n_keys = WINDOW + N_META + SUBLANES
    key_row = lax.broadcasted_iota(jnp.int32, (n_keys, KV_WIDTH), 0)
    visible = jnp.logical_and(key_row >= 1, key_row <= WINDOW + N_META)
    win_row = lax.broadcasted_iota(jnp.int32, (WINDOW, KV_WIDTH), 0)
    sink = sink_ref[...]
    seg = seg_ref[...]
    expand = exp_ref[...]
    nw = nw_ref[...]

    group = 4

    def per_group(i, carry):
        seqs = [i * group + j for j in range(group)]
        new_rows, scores, values = [], [], []
        for b in seqs:
            kv_new = kv_ref[b]
            k_new = kv_new[:, 0:KV_WIDTH]
            v_new = kv_new[:, KV_WIDTH:2 * KV_WIDTH]
            new_rows.append((k_new, v_new))
            ck = ck_ref[b]
            cv = cv_ref[b]
            keys = jnp.concatenate([ck, mk_ref[b], jnp.broadcast_to(k_new, (SUBLANES, KV_WIDTH))], axis=0)
            values.append(jnp.concatenate([cv, mv_ref[b], jnp.broadcast_to(v_new, (SUBLANES, KV_WIDTH))], axis=0))
            prod = jnp.concatenate([keys] * GQA_GROUP, axis=1) * q_ref[b]
            scores.append(_dot(prod.astype(BF16), seg))
            nk_ref[b] = jnp.where(win_row == WINDOW - 1, k_new, pltpu.roll(ck, WINDOW - 1, 0))
            nv_ref[b] = jnp.where(win_row == WINDOW - 1, v_new, pltpu.roll(cv, WINDOW - 1, 0))

        wides = []
        for s in scores:
            s = jnp.where(visible, s, NEG_BIG)
            mx = jnp.maximum(jnp.max(s, axis=0, keepdims=True), sink)
            p = jnp.exp2(s - mx)
            den = jnp.sum(p, axis=0, keepdims=True) + jnp.exp2(sink - mx)
            p = p * (1.0 / den)
            wides.append(_dot(p.astype(BF16), expand))

        for b, wide, vals in zip(seqs, wides, values):
            att = jnp.sum(wide * jnp.concatenate([vals] * GQA_GROUP, axis=1), axis=0, keepdims=True)
            att_ref[b] = att * sga_ref[b]

            decay_row, hq_row, hk_row, hv_row, sgh_row = jnp.exp(g_ref[b]), hq_ref[b], hk_ref[b], hv_ref[b], sgh_ref[b]
            outs = []
            for h in range(HG_HEADS):
                lanes = slice(h * HG_KDIM, (h + 1) * HG_KDIM)
                col = lambda r: jnp.broadcast_to(r[:, lanes], (HG_KDIM, HG_KDIM)).T
                s1 = col(decay_row) * s_ref[b, h] + col(hk_row) * hv_row[:, lanes]
                ns_ref[b, h] = s1
                o = jnp.sum(col(hq_row) * s1, axis=0, keepdims=True)
                outs.append(_head_norm_gate(o, nw, sgh_row[:, lanes]))
            hg_ref[b] = jnp.concatenate(outs, axis=-1)
        return carry

    lax.fori_loop(0, q_ref.shape[0] // group, per_group, 0)


def _small_path_kernel(sinks_ref, x0_ref, nw_ref, w_ref, qnw_ref, knw_ref, lbl_ref, bd_ref, wo_ref, hgnw_ref,
                       sinkrow_ref, seg_ref, exp_ref, dmat_ref, ck_ref, cv_ref, mk_ref, mv_ref, s_ref,
                       y_ref, kvm_ref, s0t_ref, nk_ref, nv_ref, ns_ref,
                       x_scr, att_scr, hg_scr, *scr):
    proj_scr, staged, staged_att, staged_hg = scr[:8], scr[8:16], scr[16], scr[17]
    layer, tile = pl.program_id(0), pl.program_id(1)
    n_rows = x_scr.shape[0]
    n_seq = n_rows - N_META
    t = ck_ref.shape[0]

    @pl.when(jnp.logical_and(layer == 0, tile == 0))
    def _():
        x_scr[...] = x0_ref[...]

    @pl.when(tile == 0)
    def _():
        _proj_body(layer, [slice(0, n_rows)], [x_scr[...]], nw_ref, w_ref, qnw_ref, knw_ref, lbl_ref, bd_ref,
                   *proj_scr)
        meta = lambda ref: ref.at[n_seq:n_rows]
        _meta_mix(lambda h, g: sinks_ref[layer, h * GQA_GROUP + g], *[meta(r) for r in proj_scr], hgnw_ref,
                  dmat_ref, meta(att_scr), meta(hg_scr), s0t_ref)
        kvm_ref[...] = proj_scr[1][n_seq:n_rows, :]

    rows = pl.ds(pl.multiple_of(tile * t, t), t)
    for src, dst in zip(proj_scr, staged):
        block = src[rows, :]
        for b in range(t):
            dst[b] = block[b:b + 1, :]
    _sample_mix(*staged, hgnw_ref, sinkrow_ref, seg_ref, exp_ref, ck_ref, cv_ref, mk_ref, mv_ref, s_ref,
                staged_att, staged_hg, nk_ref, nv_ref, ns_ref)
    att_scr[rows, :] = jnp.concatenate([staged_att[b] for b in range(t)], axis=0)
    hg_scr[rows, :] = jnp.concatenate([staged_hg[b] for b in range(t)], axis=0)

    @pl.when(tile == pl.num_programs(1) - 1)
    def _():
        x = (x_scr[...] + _dot(att_scr[...].astype(BF16), wo_ref[0:ATT_WIDTH, :])
             + _dot(hg_scr[...].astype(BF16), wo_ref[ATT_WIDTH:, :]))
        x_scr[...] = x
        y_ref[...] = x


def _small_path(sinks, x0, norm_w, w_in, qnw, knw, lb_logits, bd, w_out, hg_nw, sink_rows, seg, expand,
                cache_k, cache_v, meta_k, meta_v, state):
    n_rows = x0.shape[0]
    n_seq, depth = state.shape[:2]
    t = SAMPLE_TILE
    dmat = jnp.asarray(_decay_sum_matrix(N_META), BF16)
    full = lambda a: pl.BlockSpec(a.shape, lambda l, i, s: (0,) * a.ndim)
    per_layer = lambda a: pl.BlockSpec((None,) + a.shape[1:], lambda l, i, s: (l,) + (0,) * (a.ndim - 1))
    cache = lambda a: pl.BlockSpec((t, None) + a.shape[2:], lambda l, i, s: (i, l) + (0,) * (a.ndim - 2))
    as_rows = lambda a: a[:, None, :]
    state_t = (HG_HEADS, HG_VDIM, HG_KDIM)
    widths = (ATT_WIDTH, 2 * KV_WIDTH, ATT_WIDTH, HG_WIDTH, HG_WIDTH, HG_WIDTH, HG_WIDTH, HG_WIDTH)
    operands = [x0, as_rows(norm_w), w_in, as_rows(qnw), as_rows(knw), lb_logits, bd, w_out, as_rows(hg_nw),
                as_rows(sink_rows), seg, expand, dmat, cache_k, cache_v, meta_k, meta_v, state]
    in_specs = [full(x0), per_layer(operands[1]), per_layer(w_in), per_layer(operands[3]), per_layer(operands[4]),
                full(lb_logits), full(bd), per_layer(w_out), per_layer(operands[8]), per_layer(operands[9]),
                full(seg), full(expand), full(dmat), cache(cache_k), cache(cache_v), cache(meta_k), cache(meta_v),
                cache(state)]
    out_shape = [jax.ShapeDtypeStruct((n_rows, D_MODEL), F32),
                 jax.ShapeDtypeStruct((depth, N_META, 2 * KV_WIDTH), F32),
                 jax.ShapeDtypeStruct((depth,) + state_t, F32),
                 jax.ShapeDtypeStruct(cache_k.shape, F32), jax.ShapeDtypeStruct(cache_v.shape, F32),
                 jax.ShapeDtypeStruct(state.shape, F32)]
    out_specs = [pl.BlockSpec((n_rows, D_MODEL), lambda l, i, s: (0, 0)),
                 pl.BlockSpec((None, N_META, 2 * KV_WIDTH), lambda l, i, s: (l, 0, 0)),
                 pl.BlockSpec((None,) + state_t, lambda l, i, s: (l, 0, 0, 0)),
                 cache(cache_k), cache(cache_v), cache(state)]
    scratch = ([pltpu.VMEM((n_rows, D_MODEL), F32), pltpu.VMEM((n_rows, ATT_WIDTH), F32),
                pltpu.VMEM((n_rows, HG_WIDTH), F32)]
               + [pltpu.VMEM((n_rows, w), F32) for w in widths]
               + [pltpu.VMEM((t, 1, w), F32) for w in widths]
               + [pltpu.VMEM((t, 1, ATT_WIDTH), F32), pltpu.VMEM((t, 1, HG_WIDTH), F32)])
    return pl.pallas_call(
        _small_path_kernel,
        grid_spec=pltpu.PrefetchScalarGridSpec(
            num_scalar_prefetch=1, grid=(depth, n_seq // t),
            in_specs=in_specs, out_specs=out_specs, scratch_shapes=scratch),
        out_shape=out_shape,
        compiler_params=pltpu.CompilerParams(dimension_semantics=("arbitrary", "arbitrary"),
                                             vmem_limit_bytes=VMEM_LIMIT),
        name="small_path",
    )(sinks, *operands)


def _g_major(a, axis):
    shape = a.shape
    a = a.reshape(shape[:axis] + (KV_HEADS, GQA_GROUP, HEAD_DIM) + shape[axis + 1:])
    a = jnp.swapaxes(a, axis, axis + 1)
    return a.reshape(shape)


def _constants():
    lane = np.arange(ATT_WIDTH)
    g_of, h_of = lane // KV_WIDTH, (lane % KV_WIDTH) // HEAD_DIM
    head = h_of * GQA_GROUP + g_of
    seg = (head[:, None] == np.arange(KV_WIDTH)[None, :]).astype(np.float32)
    grp = np.arange(256) // HEAD_DIM
    bd = (grp[:, None] == grp[None, :]).astype(np.float32) / HEAD_DIM
    return jnp.asarray(seg, BF16), jnp.asarray(seg.T, BF16), jnp.asarray(bd, BF16)


def kernel(x_prompt, x_sample, cache_win_k, cache_win_v, cache_meta_k, cache_meta_v, state_hgrn, meta_tokens,
           norm_w, w_in, q_norm_w, k_norm_w, attn_sinks, hg_lb_logits, hg_norm_w, w_out):
    batch, seq, _ = x_prompt.shape
    n_seq = x_sample.shape[0]
    depth = w_in.shape[0]
    w_buf = cache_win_k.shape[2]
    assert x_sample.shape[1] == 1 and w_buf == WINDOW and n_seq % SAMPLE_TILE == 0
    assert seq % ROW_TILE == 0 and seq % MIX_TILE == 0

    seg, expand, bd = _constants()
    w_in_b = w_in.astype(BF16)
    w_out_b = jnp.concatenate([_g_major(w_out[:, :ATT_WIDTH], 1), w_out[:, ATT_WIDTH:]], axis=1).astype(BF16)
    qnw = jnp.tile(q_norm_w, (1, ATT_HEADS)) * (HEAD_DIM ** -0.5 * LOG2E)
    knw = jnp.tile(k_norm_w, (1, KV_HEADS))
    lb_logits = hg_lb_logits.astype(F32)
    sinks = attn_sinks.astype(F32) * LOG2E
    sink_rows = jnp.pad(sinks, ((0, 0), (0, KV_WIDTH - ATT_HEADS)))

    ck = cache_win_k.reshape(n_seq, depth, w_buf, KV_WIDTH)
    cv = cache_win_v.reshape(n_seq, depth, w_buf, KV_WIDTH)
    mk = cache_meta_k.reshape(n_seq, depth, N_META, KV_WIDTH)
    mv = cache_meta_v.reshape(n_seq, depth, N_META, KV_WIDTH)

    x_small = jnp.concatenate([x_sample.reshape(n_seq, D_MODEL), meta_tokens.astype(F32)], axis=0)
    y_small, kv_meta, s0t, new_k, new_v, new_state = _small_path(
        sinks, x_small, norm_w, w_in_b, qnw, knw, lb_logits, bd, w_out_b, hg_norm_w, sink_rows,
        seg, expand, ck, cv, mk, mv, state_hgrn)

    xp = x_prompt.reshape(batch * seq, D_MODEL)
    outs = {k: [] for k in ("wkp", "wvp", "hsp")}
    mix = None
    for l in range(depth):
        pr = _project(l, xp, norm_w[l][None], w_in_b, qnw[l][None], knw[l][None], lb_logits, bd,
                      row_tile=ROW_TILE, mix=mix)
        if mix is not None:
            xp, pr = pr[0], pr[1:]
        qa, hgin, gkv = pr
        mixed, s_fin = _prompt_mixers(sinks[l], qa, hgin, gkv, kv_meta[l], s0t[l], hg_norm_w[l][None], batch, seq)
        mix = (mixed, w_out_b)

        kv_last = gkv.reshape(batch, seq, -1)[:, seq - w_buf:, HG_WIDTH:]
        outs["wkp"].append(kv_last[:, :, :KV_WIDTH])
        outs["wvp"].append(kv_last[:, :, KV_WIDTH:])
        outs["hsp"].append(s_fin)
    xp = _out_project(depth - 1, mix[0], xp, mix[1], MIX_TILE)

    stack = lambda name: jnp.stack(outs[name], axis=1)
    heads = lambda a: a.reshape(a.shape[:-1] + (KV_HEADS, HEAD_DIM))
    meta_rows = lambda a: jnp.broadcast_to(a[None], (batch,) + a.shape)
    return (xp.reshape(batch, seq, D_MODEL), y_small[:n_seq].reshape(n_seq, 1, D_MODEL),
            heads(stack("wkp")), heads(stack("wvp")),
            heads(meta_rows(kv_meta[:, :, :KV_WIDTH])), heads(meta_rows(kv_meta[:, :, KV_WIDTH:])), stack("hsp"),
            heads(new_k), heads(new_v), new_state)
```

```python
import functools

import numpy as np
import jax
import jax.numpy as jnp
from jax import lax
from jax.experimental import pallas as pl
from jax.experimental.pallas import tpu as pltpu

F32 = jnp.float32
BF16 = jnp.bfloat16

D_MODEL = 1024
N_META = 16
WINDOW = 128
HEAD_DIM = 64
ATT_WIDTH = 512
ATT_HEADS = 8
KV_HEADS = 2
GQA_GROUP = 4
KV_WIDTH = KV_HEADS * HEAD_DIM
HG_WIDTH = 512
HG_HEADS = 4
HG_KDIM = 128
HG_VDIM = 128
PROJ_WIDTH = 3328
EPS = 1e-6
NEG_BIG = -1e30
TINY = 1e-30
LOG2E = 1.4426950408889634

C_Q, C_K, C_V, C_GA, C_QH, C_FH, C_IH, C_GH = 0, 512, 640, 768, 1280, 1792, 2304, 2816

SUBLANES = 8
MXU_DEPTH = 256
HG_CHUNK = 64
ROW_TILE = 512
MIX_TILE = 1024
SAMPLE_TILE = 8
VMEM_LIMIT = 48 * 1024 * 1024


def _dot(a, b):
    return jnp.dot(a, b, preferred_element_type=F32)


def _dot_nt(a, b):
    return lax.dot_general(a, b, (((1,), (1,)), ((), ())), preferred_element_type=F32)


def _dot_tn(a, b):
    return lax.dot_general(a, b, (((0,), (0,)), ((), ())), preferred_element_type=F32)


def _silu(x):
    return x * (1.0 / (1.0 + jnp.exp(-x)))


def _group_major(x):
    first = lax.broadcasted_iota(jnp.int32, (x.shape[0], KV_WIDTH), 1) < HEAD_DIM
    blocks = [x[:, j * KV_WIDTH:(j + 1) * KV_WIDTH] for j in range(GQA_GROUP)]
    swapped = [pltpu.roll(b, HEAD_DIM, 1) for b in blocks]
    out = []
    for g in range(GQA_GROUP):
        a, b = g // 2, GQA_GROUP // 2 + g // 2
        out.append(jnp.where(first, blocks[a], swapped[b]) if g % 2 == 0 else jnp.where(first, swapped[a], blocks[b]))
    return jnp.concatenate(out, axis=-1)


def _row_parts(n_rows, n_parts):
    step = n_rows // n_parts
    return [slice(i * step, (i + 1) * step) for i in range(n_parts)]


def _mix_proj_kernel(layer, n_parts, mix_ref, wo_ref, x_ref, *rest):
    parts = _row_parts(x_ref.shape[0], n_parts)
    xnew_ref = rest[6]
    xs = []
    for rs in parts:
        x = x_ref[rs, :] + _dot(mix_ref[rs, :], wo_ref[...])
        xnew_ref[rs, :] = x
        xs.append(x)
    _proj_body(layer, parts, xs, *rest[:6], *_projection_views(*rest[7:]))


def _proj_kernel(layer, n_parts, x_ref, *rest):
    parts = _row_parts(x_ref.shape[0], n_parts)
    _proj_body(layer, parts, [x_ref[rs, :] for rs in parts], *rest[:6], *_projection_views(*rest[6:]))


QA_Q, QA_GATE = 0, 1
HI_Q, HI_K, HI_V, HI_GATE = 0, 1, 2, 3
GKV_G = 0
GKV_KV = HG_WIDTH // (2 * KV_WIDTH)


def _projection_views(qa_ref, hgin_ref, gkv_ref):
    col = lambda ref, j, width: ref.at[:, j * width:(j + 1) * width]
    return (col(qa_ref, QA_Q, ATT_WIDTH), col(gkv_ref, GKV_KV, 2 * KV_WIDTH), col(qa_ref, QA_GATE, ATT_WIDTH),
            col(hgin_ref, HI_Q, HG_WIDTH), col(hgin_ref, HI_K, HG_WIDTH), col(hgin_ref, HI_V, HG_WIDTH),
            col(gkv_ref, GKV_G, HG_WIDTH), col(hgin_ref, HI_GATE, HG_WIDTH))


def _proj_body(layer, parts, xs, nw_ref, w_ref, qnw_ref, knw_ref, lbl_ref, bd_ref,
               q_ref, kv_ref, sga_ref, hq_ref, hk_ref, hv_ref, g_ref, sgh_ref):
    nw = nw_ref[...]
    hs = []
    for x in xs:
        ms = jnp.mean(x * x, axis=-1, keepdims=True)
        hs.append((x * lax.rsqrt(ms + EPS) * nw).astype(BF16))

    def proj(h, lo, hi):
        return _dot(h, w_ref[:, lo:hi])

    pq = [[proj(h, C_Q + 256 * c, C_Q + 256 * (c + 1)) for c in range(2)] for h in hs]
    pk = [proj(h, C_K, C_V) for h in hs]
    sq = [[(p * p).astype(BF16) for p in pqs + [pks]] for pqs, pks in zip(pq, pk)]

    logits = lbl_ref[...]
    e = jnp.exp(logits - jnp.max(logits, axis=0, keepdims=True))
    p = e / jnp.sum(e, axis=0, keepdims=True)
    depth_row = lax.broadcasted_iota(jnp.int32, logits.shape, 0)
    in_range = jnp.where(depth_row >= 1, layer - depth_row, -1) >= 0
    lb = jnp.sum(jnp.where(in_range, p, 0.0), axis=0, keepdims=True)

    for rs, h in zip(parts, hs):
        kv_ref[rs, KV_WIDTH:2 * KV_WIDTH] = proj(h, C_V, C_GA)
        sga_ref[rs, :] = _group_major(_silu(proj(h, C_GA, C_QH))).astype(sga_ref.dtype)
        hq_ref[rs, :] = _silu(proj(h, C_QH, C_FH)).astype(hq_ref.dtype)

        z = proj(h, C_FH, C_IH)
        ez = jnp.exp(-jnp.abs(z))
        r = 1.0 / (1.0 + ez)
        pos = z >= 0.0
        sig_pos = jnp.where(pos, r, ez * r)
        sig_neg = jnp.where(pos, ez * r, r)
        hk_ref[rs, :] = ((1.0 - lb) * sig_neg).astype(hk_ref.dtype)
        f = lb + (1.0 - lb) * sig_pos
        g_ref[rs, :] = jnp.log(jnp.maximum(f, TINY))

        hv_ref[rs, :] = proj(h, C_IH, C_GH).astype(hv_ref.dtype)
        sgh_ref[rs, :] = _silu(proj(h, C_GH, PROJ_WIDTH)).astype(sgh_ref.dtype)

    bd = bd_ref[...]
    for rs, pqs, pks, sqs in zip(parts, pq, pk, sq):
        q = jnp.concatenate([p * lax.rsqrt(_dot(s, bd) + EPS) for p, s in zip(pqs, sqs[:2])], axis=-1)
        q_ref[rs, :] = _group_major(q * qnw_ref[...]).astype(q_ref.dtype)
        kv_ref[rs, 0:KV_WIDTH] = pks * lax.rsqrt(_dot(sqs[2], bd[:KV_WIDTH, :KV_WIDTH]) + EPS) * knw_ref[...]


def _project(layer, x, norm_w, w_in, qnw, knw, lb_logits, bd, row_tile, mix=None):
    n = x.shape[0]
    rows = lambda width: pl.BlockSpec((row_tile, width), lambda i: (i, 0))
    full = lambda shape: pl.BlockSpec(shape, lambda i: (0,) * len(shape))
    slab = lambda a, l: pl.BlockSpec((None,) + a.shape[1:], lambda i: (l, 0, 0))
    out = lambda width, dtype: jax.ShapeDtypeStruct((n, width), dtype)
    operands = [x, norm_w, w_in, qnw, knw, lb_logits, bd]
    in_specs = [rows(D_MODEL), full((1, D_MODEL)), slab(w_in, layer), full((1, ATT_WIDTH)),
                full((1, KV_WIDTH)), full(lb_logits.shape), full((256, 256))]
    widths = (2 * ATT_WIDTH, 4 * HG_WIDTH, HG_WIDTH + 2 * KV_WIDTH)
    out_specs = [rows(w) for w in widths]
    out_shape = [out(widths[0], BF16), out(widths[1], BF16), out(widths[2], F32)]
    body = _proj_kernel
    if mix is not None:
        operands = list(mix) + operands
        in_specs = [rows(ATT_WIDTH + HG_WIDTH), slab(mix[1], layer - 1)] + in_specs
        out_specs = [rows(D_MODEL)] + out_specs
        out_shape = [out(D_MODEL, F32)] + out_shape
        body = _mix_proj_kernel
    n_parts = 4 if row_tile % (4 * 128) == 0 else 1
    return pl.pallas_call(
        functools.partial(body, layer, n_parts),
        grid=(n // row_tile,),
        in_specs=in_specs, out_specs=out_specs, out_shape=out_shape,
        compiler_params=pltpu.CompilerParams(dimension_semantics=("arbitrary",), vmem_limit_bytes=VMEM_LIMIT),
        name="proj",
    )(*operands)


def _out_kernel(mix_ref, x_ref, w_ref, y_ref):
    y_ref[...] = x_ref[...] + _dot(mix_ref[...], w_ref[...])


def _out_project(layer, mix, x, w_out, row_tile):
    n = x.shape[0]
    rows = lambda width: pl.BlockSpec((row_tile, width), lambda i: (i, 0))
    return pl.pallas_call(
        _out_kernel,
        grid=(n // row_tile,),
        in_specs=[rows(ATT_WIDTH + HG_WIDTH), rows(D_MODEL),
                  pl.BlockSpec((None, D_MODEL, D_MODEL), lambda i: (layer, 0, 0))],
        out_specs=rows(D_MODEL),
        out_shape=jax.ShapeDtypeStruct((n, D_MODEL), F32),
        compiler_params=pltpu.CompilerParams(dimension_semantics=("arbitrary",), vmem_limit_bytes=VMEM_LIMIT),
        name="out_proj",
    )(mix, x, w_out)


def _split_heads(k):
    first = lax.broadcasted_iota(jnp.int32, k.shape, 1) < HEAD_DIM
    return (jnp.where(first, k, 0.0).astype(BF16), jnp.where(first, 0.0, k).astype(BF16))


def _attend(q, key_sets, sink_of):
    m = q.shape[0]
    first = lax.broadcasted_iota(jnp.int32, (m, KV_WIDTH), 1) < HEAD_DIM
    blocks = []
    for g in range(GQA_GROUP):
        qg = q[:, g * KV_WIDTH:(g + 1) * KV_WIDTH]
        per_head = []
        for h in range(KV_HEADS):
            scores = []
            for k_heads, _, mask in key_sets:
                s = _dot_nt(qg, k_heads[h])
                if mask is not None:
                    s = jnp.where(mask, s, NEG_BIG)
                scores.append(s)
            sink = sink_of(h, g)
            mx = jnp.max(scores[0], axis=-1, keepdims=True)
            for s in scores[1:]:
                mx = jnp.maximum(mx, jnp.max(s, axis=-1, keepdims=True))
            mx = jnp.maximum(mx, sink)
            den = jnp.exp2(sink - mx)
            acc = jnp.zeros((m, KV_WIDTH), F32)
            for s, (_, v, _) in zip(scores, key_sets):
                p = jnp.exp2(s - mx)
                den = den + jnp.sum(p, axis=-1, keepdims=True)
                acc = acc + _dot(p.astype(BF16), v)
            per_head.append(acc * (1.0 / den))
        blocks.append(jnp.where(first, per_head[0], per_head[1]))
    return jnp.concatenate(blocks, axis=-1)


def _attn_stages(i, sink_ref, q_ref, kvc_ref, kvp_ref, kvm_ref, sga_ref, o_ref):
    n_blocks = q_ref.shape[0] // WINDOW
    n_keys = 2 * WINDOW + N_META
    kvm = kvm_ref[...]
    k_blocks = [_split_heads(kvp_ref[:, :KV_WIDTH])]
    vt_blocks = [kvp_ref[:, KV_WIDTH:].T.astype(BF16)]
    for j in range(n_blocks):
        rows = slice(j * WINDOW, (j + 1) * WINDOW)
        k_blocks.append(_split_heads(kvc_ref[rows, :KV_WIDTH]))
        vt_blocks.append(kvc_ref[rows, KV_WIDTH:].T.astype(BF16))
    k_meta = _split_heads(kvm[:, :KV_WIDTH])
    v_meta = jnp.concatenate([kvm[:, KV_WIDTH:], jnp.zeros((WINDOW - N_META, KV_WIDTH), F32)], axis=0)
    vt_meta = v_meta.T.astype(BF16)
    pad = jnp.zeros((3 * WINDOW - n_keys, WINDOW), BF16)

    key = lax.broadcasted_iota(jnp.int32, (n_keys, WINDOW), 0)
    qi = lax.broadcasted_iota(jnp.int32, (n_keys, WINDOW), 1)
    band = jnp.where(key < WINDOW, key - qi - 1, jnp.where(key < 2 * WINDOW, qi - (key - WINDOW), 0))
    visible = band >= 0
    no_prev = jnp.where(i > 0, 0, 2 * WINDOW)
    visible_first = jnp.where(key < WINDOW, band - no_prev, band) >= 0
    head0_rows = lax.broadcasted_iota(jnp.int32, (KV_WIDTH, WINDOW), 0) < HEAD_DIM
    group_lanes = [slice(g * KV_WIDTH, (g + 1) * KV_WIDTH) for g in range(GQA_GROUP)]

    s_both = []
    for j in range(n_blocks):
        rows = slice(j * WINDOW, (j + 1) * WINDOW)
        kk = jnp.concatenate([k_blocks[j][0], k_blocks[j + 1][0], k_meta[0],
                              k_blocks[j][1], k_blocks[j + 1][1], k_meta[1]], axis=0)
        s_both.append([_dot_nt(kk, q_ref[rows, lanes]) for lanes in group_lanes])
    yield
    probs = []
    for j in range(n_blocks):
        vis = visible_first if j == 0 else visible
        for g in range(GQA_GROUP):
            for h in range(KV_HEADS):
                sink = sink_ref[h * GQA_GROUP + g]
                s = jnp.where(vis, s_both[j][g][h * n_keys:(h + 1) * n_keys], NEG_BIG)
                mx = jnp.maximum(jnp.max(s, axis=0, keepdims=True), sink)
                p = jnp.exp2(s - mx)
                den = jnp.sum(p, axis=0, keepdims=True) + jnp.exp2(sink - mx)
                probs.append((jnp.concatenate([p.astype(BF16), pad], axis=0), 1.0 / den))
    outs = []
    for j in range(n_blocks):
        v_t = jnp.concatenate([vt_blocks[j], vt_blocks[j + 1], vt_meta], axis=1)
        for p_pad, inv in probs[j * ATT_HEADS:(j + 1) * ATT_HEADS]:
            outs.append(_dot(v_t, p_pad) * inv)
    yield
    for j in range(n_blocks):
        rows = slice(j * WINDOW, (j + 1) * WINDOW)
        for g, lanes in enumerate(group_lanes):
            pair = outs[j * ATT_HEADS + g * KV_HEADS:j * ATT_HEADS + (g + 1) * KV_HEADS]
            o_t = jnp.where(head0_rows, pair[0], pair[1])
            o_ref[rows, lanes] = (o_t.T * sga_ref[rows, lanes].astype(F32)).astype(o_ref.dtype)


def _level_sizes(c):
    return [c >> (i + 1) for i in range(c.bit_length() - 1)]


def _decay_sum_matrix(c):
    t = np.arange(c)[:, None]
    r = np.arange(c)[None, :]
    mats = [r <= t]
    for bs in _level_sizes(c):
        if bs < SUBLANES:
            a = (t // (2 * bs)) * (2 * bs) + bs - 1
            mats.append(((r > t) & (r <= a)) | ((r > a) & (r <= t)))
    return np.tile(np.concatenate(mats, axis=0).astype(np.float32), (1, _split_terms(c)))


def _split_terms(c):
    return min(3, MXU_DEPTH // c)


def _decay_sums(g, dmat):
    rest = g * LOG2E
    terms = []
    for _ in range(dmat.shape[1] // g.shape[0]):
        terms.append(rest.astype(BF16))
        rest = rest - terms[-1].astype(F32)
    return _dot(dmat, jnp.concatenate(terms, axis=0))


def _chunk_exponents(block, c):
    start = block(0)
    grp = lambda j: start[j * SUBLANES:(j + 1) * SUBLANES, :]
    lasts = {}

    def last(j):
        if j not in lasts:
            lasts[j] = jnp.broadcast_to(start[j * SUBLANES + SUBLANES - 1:(j + 1) * SUBLANES, :], (SUBLANES, HG_KDIM))
        return lasts[j]

    n_grp = c // SUBLANES
    end = jnp.concatenate([last(n_grp - 1) - grp(j) for j in range(n_grp)], axis=0)
    levels, fine = [], 1
    for bs in _level_sizes(c):
        if bs >= SUBLANES:
            per = bs // SUBLANES
            parts = []
            for j in range(n_grp):
                blk = j // per
                anchor = last((blk // 2) * 2 * per + per - 1)
                parts.append(grp(j) - anchor if blk % 2 == 1 else anchor - grp(j))
            levels.append(jnp.concatenate(parts, axis=0))
        else:
            levels.append(block(fine))
            fine += 1
    return start, end, levels


def _hgrn_masks(c):
    row = lax.broadcasted_iota(jnp.int32, (c, c), 0)
    col = lax.broadcasted_iota(jnp.int32, (c, c), 1)
    row_k = lax.broadcasted_iota(jnp.int32, (c, HG_KDIM), 0)
    levels = []
    for bs in _level_sizes(c):
        rb, cb = row // bs, col // bs
        pairs = ((rb % 2) * (1 - jnp.abs(cb - rb + 1))) > 0
        levels.append((bs, (row_k // bs) % 2 == 1, pairs))
    return levels, row == col


def _hgrn_chunks(items, states, masks):
    outs = []
    for _ in _hgrn_chunk_stages(items, states, masks, outs):
        pass
    return outs


def _hgrn_chunk_stages(items, states, masks, outs):
    levels, diag = masks
    c = items[0][1].shape[0]

    stage1 = []
    for _, q, k, v, (_, to_end, level_ex) in items:
        pair_scores = []
        for (bs, q_side, _), lex in zip(levels, level_ex):
            if bs >= SUBLANES:
                side = jnp.concatenate(
                    [(q if (j // bs) % 2 == 1 else k)[j:j + SUBLANES] for j in range(0, c, SUBLANES)], axis=0)
            else:
                side = jnp.where(q_side, q, k)
            u = (side * jnp.exp2(lex)).astype(BF16)
            pair_scores.append(_dot_nt(u, u))
        kdec = (k * jnp.exp2(to_end)).astype(BF16)
        stage1.append((pair_scores, _dot_tn(v, kdec)))
    yield

    intra = []
    for (_, q, k, v, _), (pair_scores, _) in zip(items, stage1):
        a = jnp.where(diag, jnp.sum(q * k, axis=-1, keepdims=True), 0.0)
        for (_, _, pairs), scores in zip(levels, pair_scores):
            a = jnp.where(pairs, scores, a)
        intra.append(_dot(a.astype(BF16), v))
    yield

    for (head, q, _, _, (from_start, _, _)), (_, increment), o_intra in zip(items, stage1, intra):
        st = states[head]
        decay = jnp.exp2(from_start)
        outs.append(o_intra + _dot_nt((q * decay).astype(BF16), st.astype(BF16)))
        states[head] = st * decay[c - 1:c, :] + increment


def _head_norm_gate(o, nw, gate):
    ms = jnp.mean(o * o, axis=-1, keepdims=True)
    return o * lax.rsqrt(ms + EPS) * nw * gate


def _hgrn_stages(hq_ref, hk_ref, hv_ref, g_ref, sgh_ref, nw_ref, dmat_ref, o_ref, st_ref):
    masks = _hgrn_masks(HG_CHUNK)
    dmat = dmat_ref[...]

    def block_of(sums, lanes):
        return lambda i: sums[i * HG_CHUNK:(i + 1) * HG_CHUNK, lanes]

    items, where = [], []
    for ci in range(hq_ref.shape[0] // HG_CHUNK):
        rows = slice(ci * HG_CHUNK, (ci + 1) * HG_CHUNK)
        sums = _decay_sums(g_ref[rows, :], dmat)
        for h in range(HG_HEADS):
            lanes = slice(h * HG_KDIM, (h + 1) * HG_KDIM)
            items.append((h, hq_ref[rows, lanes].astype(F32), hk_ref[rows, lanes].astype(F32),
                          hv_ref[rows, lanes], _chunk_exponents(block_of(sums, lanes), HG_CHUNK)))
            where.append((rows, lanes))
    yield
    states = [st_ref[h] for h in range(HG_HEADS)]
    outs = []
    yield from _hgrn_chunk_stages(items, states, masks, outs)
    nw = nw_ref[...]
    for (rows, lanes), o in zip(where, outs):
        o_ref[rows, lanes] = _head_norm_gate(o, nw, sgh_ref[rows, lanes].astype(F32)).astype(o_ref.dtype)
    for h in range(HG_HEADS):
        st_ref[h] = states[h]


def _mixer_kernel(sink_ref, q_ref, kvc_ref, kvp_ref, kvm_ref, sga_ref, hq_ref, hk_ref, hv_ref, g_ref, sgh_ref,
                  s0_ref, nw_ref, dmat_ref, mix_ref, sfin_ref, st_ref):
    j = pl.program_id(1)

    @pl.when(j == 0)
    def _():
        st_ref[...] = s0_ref[...]

    attention = _attn_stages(j, sink_ref, q_ref, kvc_ref, kvp_ref, kvm_ref, sga_ref, mix_ref.at[:, 0:ATT_WIDTH])
    hgrn = _hgrn_stages(hq_ref, hk_ref, hv_ref, g_ref, sgh_ref, nw_ref, dmat_ref, mix_ref.at[:, ATT_WIDTH:], st_ref)
    for stage in (hgrn, hgrn, attention, hgrn, attention, attention, hgrn):
        next(stage, None)
    for stage in (attention, hgrn):
        for _ in stage:
            pass

    @pl.when(j == pl.num_programs(1) - 1)
    def _():
        for h in range(HG_HEADS):
            sfin_ref[h] = st_ref[h].T


def _prompt_mixers(sinks, qa, hgin, gkv, kv_meta, s0t, hg_nw, batch, seq):
    nt = seq // MIX_TILE
    per = MIX_TILE // WINDOW
    blk = lambda width, col: pl.BlockSpec((MIX_TILE, width), lambda b, j, s: (b * nt + j, col))
    prev = pl.BlockSpec((WINDOW, 2 * KV_WIDTH), lambda b, j, s: ((b * nt + j) * per - jnp.minimum(j, 1), GKV_KV))
    full = lambda shape: pl.BlockSpec(shape, lambda b, j, s: (0,) * len(shape))
    state_shape = (HG_HEADS, HG_VDIM, HG_KDIM)
    dmat = jnp.asarray(_decay_sum_matrix(HG_CHUNK), BF16)
    return pl.pallas_call(
        _mixer_kernel,
        grid_spec=pltpu.PrefetchScalarGridSpec(
            num_scalar_prefetch=1, grid=(batch, nt),
            in_specs=[blk(ATT_WIDTH, QA_Q), blk(2 * KV_WIDTH, GKV_KV), prev, full((N_META, 2 * KV_WIDTH)),
                      blk(ATT_WIDTH, QA_GATE),
                      blk(HG_WIDTH, HI_Q), blk(HG_WIDTH, HI_K), blk(HG_WIDTH, HI_V), blk(HG_WIDTH, GKV_G),
                      blk(HG_WIDTH, HI_GATE), full(state_shape), full((1, HG_VDIM)), full(dmat.shape)],
            out_specs=[blk(ATT_WIDTH + HG_WIDTH, 0),
                       pl.BlockSpec((None,) + state_shape, lambda b, j, s: (b, 0, 0, 0))],
            scratch_shapes=[pltpu.VMEM(state_shape, F32)]),
        out_shape=[jax.ShapeDtypeStruct((batch * seq, ATT_WIDTH + HG_WIDTH), BF16),
                   jax.ShapeDtypeStruct((batch,) + state_shape, F32)],
        compiler_params=pltpu.CompilerParams(dimension_semantics=("arbitrary", "arbitrary"),
                                             vmem_limit_bytes=VMEM_LIMIT),
        name="mixers",
    )(sinks, qa, gkv, gkv, kv_meta, qa, hgin, hgin, hgin, gkv, hgin, s0t, hg_nw, dmat)


def _meta_mix(sink_of, q_ref, kv_ref, sga_ref, hq_ref, hk_ref, hv_ref, g_ref, sgh_ref, nw_ref, dmat_ref,
              att_ref, hg_ref, st_ref):
    kv = kv_ref[...]
    row = lax.broadcasted_iota(jnp.int32, (N_META, N_META), 0)
    col = lax.broadcasted_iota(jnp.int32, (N_META, N_META), 1)
    att = _attend(q_ref[...].astype(BF16),
                  [(_split_heads(kv[:, :KV_WIDTH]), kv[:, KV_WIDTH:].astype(BF16), col <= row)], sink_of)
    att_ref[...] = (att * sga_ref[...]).astype(att_ref.dtype)

    sums = _decay_sums(g_ref[...], dmat_ref[...])
    masks = _hgrn_masks(N_META)
    head_lanes = [slice(h * HG_KDIM, (h + 1) * HG_KDIM) for h in range(HG_HEADS)]
    items = [(h, hq_ref[:, lanes], hk_ref[:, lanes], hv_ref[:, lanes].astype(BF16),
              _chunk_exponents((lambda lanes: lambda i: sums[i * N_META:(i + 1) * N_META, lanes])(lanes), N_META))
             for h, lanes in enumerate(head_lanes)]
    states = [jnp.zeros((HG_VDIM, HG_KDIM), F32) for _ in range(HG_HEADS)]
    outs = _hgrn_chunks(items, states, masks)
    for h, lanes in enumerate(head_lanes):
        st_ref[h] = states[h]
        hg_ref[:, lanes] = _head_norm_gate(outs[h], nw_ref[...], sgh_ref[:, lanes]).astype(hg_ref.dtype)


def _sample_mix(q_ref, kv_ref, sga_ref, hq_ref, hk_ref, hv_ref, g_ref, sgh_ref, nw_ref,
                sink_ref, seg_ref, exp_ref, ck_ref, cv_ref, mk_ref, mv_ref, s_ref,
                att_ref, hg_ref, nk_ref, nv_ref, ns_ref):
    n_keys = WINDOW + N_META + SUBLANES
    key_row = lax.broadcasted_iota(jnp.int32, (n_keys, KV_WIDTH), 0)
    visible = jnp.logical_and(key_row >= 1, key_row <= WINDOW + N_META)
    win_row = lax.broadcasted_iota(jnp.int32, (WINDOW, KV_WIDTH), 0)
    sink = sink_ref[...]
    seg = seg_ref[...]
    expand = exp_ref[...]
    nw = nw_ref[...]

    group = 8

    def per_group(i, carry):
        seqs = [i * group + j for j in range(group)]
        new_rows, scores, values = [], [], []
        for b in seqs:
            kv_new = kv_ref[b]
            k_new = kv_new[:, 0:KV_WIDTH]
            v_new = kv_new[:, KV_WIDTH:2 * KV_WIDTH]
            new_rows.append((k_new, v_new))
            ck = ck_ref[b]
            cv = cv_ref[b]
            keys = jnp.concatenate([ck, mk_ref[b], jnp.broadcast_to(k_new, (SUBLANES, KV_WIDTH))], axis=0)
            values.append(jnp.concatenate([cv, mv_ref[b], jnp.broadcast_to(v_new, (SUBLANES, KV_WIDTH))], axis=0))
            prod = jnp.concatenate([keys] * GQA_GROUP, axis=1) * q_ref[b]
            scores.append(_dot(prod.astype(BF16), seg))
            nk_ref[b] = jnp.where(win_row == WINDOW - 1, k_new, pltpu.roll(ck, WINDOW - 1, 0))
            nv_ref[b] = jnp.where(win_row == WINDOW - 1, v_new, pltpu.roll(cv, WINDOW - 1, 0))

        wides = []
        for s in scores:
            s = jnp.where(visible, s, NEG_BIG)
            mx = jnp.maximum(jnp.max(s, axis=0, keepdims=True), sink)
            p = jnp.exp2(s - mx)
            den = jnp.sum(p, axis=0, keepdims=True) + jnp.exp2(sink - mx)
            p = p * (1.0 / den)
            wides.append(_dot(p.astype(BF16), expand))

        for b, wide, vals in zip(seqs, wides, values):
            att = jnp.sum(wide * jnp.concatenate([vals] * GQA_GROUP, axis=1), axis=0, keepdims=True)
            att_ref[b] = att * sga_ref[b]

            decay_row, hq_row, hk_row, hv_row, sgh_row = jnp.exp(g_ref[b]), hq_ref[b], hk_ref[b], hv_ref[b], sgh_ref[b]
            outs = []
            for h in range(HG_HEADS):
                lanes = slice(h * HG_KDIM, (h + 1) * HG_KDIM)
                col = lambda r: jnp.broadcast_to(r[:, lanes], (HG_KDIM, HG_KDIM)).T
                s1 = col(decay_row) * s_ref[b, h] + col(hk_row) * hv_row[:, lanes]
                ns_ref[b, h] = s1
                q_rows = jnp.broadcast_to(hq_row[:, lanes], (2 * SUBLANES, HG_KDIM)).astype(BF16)
                o = _dot(q_rows, s1.astype(BF16))[0:1, :]
                outs.append(_head_norm_gate(o, nw, sgh_row[:, lanes]))
            hg_ref[b] = jnp.concatenate(outs, axis=-1)
        return carry

    lax.fori_loop(0, q_ref.shape[0] // group, per_group, 0)


def _small_path_kernel(sinks_ref, x0_ref, nw_ref, w_ref, qnw_ref, knw_ref, lbl_ref, bd_ref, wo_ref, hgnw_ref,
                       sinkrow_ref, seg_ref, exp_ref, dmat_ref, ck_ref, cv_ref, mk_ref, mv_ref, s_ref,
                       y_ref, kvm_ref, s0t_ref, nk_ref, nv_ref, ns_ref,
                       x_scr, att_scr, hg_scr, *scr):
    proj_scr, staged, staged_att, staged_hg = scr[:8], scr[8:16], scr[16], scr[17]
    layer, tile = pl.program_id(0), pl.program_id(1)
    n_rows = x_scr.shape[0]
    n_seq = n_rows - N_META
    t = ck_ref.shape[0]

    @pl.when(jnp.logical_and(layer == 0, tile == 0))
    def _():
        x_scr[...] = x0_ref[...]

    @pl.when(tile == 0)
    def _():
        _proj_body(layer, [slice(0, n_rows)], [x_scr[...]], nw_ref, w_ref, qnw_ref, knw_ref, lbl_ref, bd_ref,
                   *proj_scr)
        meta = lambda ref: ref.at[n_seq:n_rows]
        _meta_mix(lambda h, g: sinks_ref[layer, h * GQA_GROUP + g], *[meta(r) for r in proj_scr], hgnw_ref,
                  dmat_ref, meta(att_scr), meta(hg_scr), s0t_ref)
        kvm_ref[...] = proj_scr[1][n_seq:n_rows, :]

    rows = pl.ds(pl.multiple_of(tile * t, t), t)
    for src, dst in zip(proj_scr, staged):
        block = src[rows, :]
        for b in range(t):
            dst[b] = block[b:b + 1, :]
    _sample_mix(*staged, hgnw_ref, sinkrow_ref, seg_ref, exp_ref, ck_ref, cv_ref, mk_ref, mv_ref, s_ref,
                staged_att, staged_hg, nk_ref, nv_ref, ns_ref)
    att_scr[rows, :] = jnp.concatenate([staged_att[b] for b in range(t)], axis=0)
    hg_scr[rows, :] = jnp.concatenate([staged_hg[b] for b in range(t)], axis=0)

    @pl.when(tile == pl.num_programs(1) - 1)
    def _():
        x = (x_scr[...] + _dot(att_scr[...].astype(BF16), wo_ref[0:ATT_WIDTH, :])
             + _dot(hg_scr[...].astype(BF16), wo_ref[ATT_WIDTH:, :]))
        x_scr[...] = x
        y_ref[...] = x


def _small_path(sinks, x0, norm_w, w_in, qnw, knw, lb_logits, bd, w_out, hg_nw, sink_rows, seg, expand,
                cache_k, cache_v, meta_k, meta_v, state):
    n_rows = x0.shape[0]
    n_seq, depth = state.shape[:2]
    t = SAMPLE_TILE
    dmat = jnp.asarray(_decay_sum_matrix(N_META), BF16)
    full = lambda a: pl.BlockSpec(a.shape, lambda l, i, s: (0,) * a.ndim)
    per_layer = lambda a: pl.BlockSpec((None,) + a.shape[1:], lambda l, i, s: (l,) + (0,) * (a.ndim - 1))
    cache = lambda a: pl.BlockSpec((t, None) + a.shape[2:], lambda l, i, s: (i, l) + (0,) * (a.ndim - 2))
    as_rows = lambda a: a[:, None, :]
    state_t = (HG_HEADS, HG_VDIM, HG_KDIM)
    widths = (ATT_WIDTH, 2 * KV_WIDTH, ATT_WIDTH, HG_WIDTH, HG_WIDTH, HG_WIDTH, HG_WIDTH, HG_WIDTH)
    operands = [x0, as_rows(norm_w), w_in, as_rows(qnw), as_rows(knw), lb_logits, bd, w_out, as_rows(hg_nw),
                as_rows(sink_rows), seg, expand, dmat, cache_k, cache_v, meta_k, meta_v, state]
    in_specs = [full(x0), per_layer(operands[1]), per_layer(w_in), per_layer(operands[3]), per_layer(operands[4]),
                full(lb_logits), full(bd), per_layer(w_out), per_layer(operands[8]), per_layer(operands[9]),
                full(seg), full(expand), full(dmat), cache(cache_k), cache(cache_v), cache(meta_k), cache(meta_v),
                cache(state)]
    out_shape = [jax.ShapeDtypeStruct((n_rows, D_MODEL), F32),
                 jax.ShapeDtypeStruct((depth, N_META, 2 * KV_WIDTH), F32),
                 jax.ShapeDtypeStruct((depth,) + state_t, F32),
                 jax.ShapeDtypeStruct(cache_k.shape, F32), jax.ShapeDtypeStruct(cache_v.shape, F32),
                 jax.ShapeDtypeStruct(state.shape, F32)]
    out_specs = [pl.BlockSpec((n_rows, D_MODEL), lambda l, i, s: (0, 0)),
                 pl.BlockSpec((None, N_META, 2 * KV_WIDTH), lambda l, i, s: (l, 0, 0)),
                 pl.BlockSpec((None,) + state_t, lambda l, i, s: (l, 0, 0, 0)),
                 cache(cache_k), cache(cache_v), cache(state)]
    scratch = ([pltpu.VMEM((n_rows, D_MODEL), F32), pltpu.VMEM((n_rows, ATT_WIDTH), F32),
                pltpu.VMEM((n_rows, HG_WIDTH), F32)]
               + [pltpu.VMEM((n_rows, w), F32) for w in widths]
               + [pltpu.VMEM((t, 1, w), F32) for w in widths]
               + [pltpu.VMEM((t, 1, ATT_WIDTH), F32), pltpu.VMEM((t, 1, HG_WIDTH), F32)])
    return pl.pallas_call(
        _small_path_kernel,
        grid_spec=pltpu.PrefetchScalarGridSpec(
            num_scalar_prefetch=1, grid=(depth, n_seq // t),
            in_specs=in_specs, out_specs=out_specs, scratch_shapes=scratch),
        out_shape=out_shape,
        compiler_params=pltpu.CompilerParams(dimension_semantics=("arbitrary", "arbitrary"),
                                             vmem_limit_bytes=VMEM_LIMIT),
        name="small_path",
    )(sinks, *operands)


def _g_major(a, axis):
    shape = a.shape
    a = a.reshape(shape[:axis] + (KV_HEADS, GQA_GROUP, HEAD_DIM) + shape[axis + 1:])
    a = jnp.swapaxes(a, axis, axis + 1)
    return a.reshape(shape)


def _constants():
    lane = np.arange(ATT_WIDTH)
    g_of, h_of = lane // KV_WIDTH, (lane % KV_WIDTH) // HEAD_DIM
    head = h_of * GQA_GROUP + g_of
    seg = (head[:, None] == np.arange(KV_WIDTH)[None, :]).astype(np.float32)
    grp = np.arange(256) // HEAD_DIM
    bd = (grp[:, None] == grp[None, :]).astype(np.float32) / HEAD_DIM
    return jnp.asarray(seg, BF16), jnp.asarray(seg.T, BF16), jnp.asarray(bd, BF16)


def kernel(x_prompt, x_sample, cache_win_k, cache_win_v, cache_meta_k, cache_meta_v, state_hgrn, meta_tokens,
           norm_w, w_in, q_norm_w, k_norm_w, attn_sinks, hg_lb_logits, hg_norm_w, w_out):
    batch, seq, _ = x_prompt.shape
    n_seq = x_sample.shape[0]
    depth = w_in.shape[0]
    w_buf = cache_win_k.shape[2]
    assert x_sample.shape[1] == 1 and w_buf == WINDOW and n_seq % SAMPLE_TILE == 0
    assert seq % ROW_TILE == 0 and seq % MIX_TILE == 0

    seg, expand, bd = _constants()
    w_in_b = w_in.astype(BF16)
    w_out_b = jnp.concatenate([_g_major(w_out[:, :ATT_WIDTH], 1), w_out[:, ATT_WIDTH:]], axis=1).astype(BF16)
    qnw = jnp.tile(q_norm_w, (1, ATT_HEADS)) * (HEAD_DIM ** -0.5 * LOG2E)
    knw = jnp.tile(k_norm_w, (1, KV_HEADS))
    lb_logits = hg_lb_logits.astype(F32)
    sinks = attn_sinks.astype(F32) * LOG2E
    sink_rows = jnp.pad(sinks, ((0, 0), (0, KV_WIDTH - ATT_HEADS)))

    ck = cache_win_k.reshape(n_seq, depth, w_buf, KV_WIDTH)
    cv = cache_win_v.reshape(n_seq, depth, w_buf, KV_WIDTH)
    mk = cache_meta_k.reshape(n_seq, depth, N_META, KV_WIDTH)
    mv = cache_meta_v.reshape(n_seq, depth, N_META, KV_WIDTH)

    x_small = jnp.concatenate([x_sample.reshape(n_seq, D_MODEL), meta_tokens.astype(F32)], axis=0)
    y_small, kv_meta, s0t, new_k, new_v, new_state = _small_path(
        sinks, x_small, norm_w, w_in_b, qnw, knw, lb_logits, bd, w_out_b, hg_norm_w, sink_rows,
        seg, expand, ck, cv, mk, mv, state_hgrn)

    xp = x_prompt.reshape(batch * seq, D_MODEL)
    outs = {k: [] for k in ("wkp", "wvp", "hsp")}
    mix = None
    for l in range(depth):
        pr = _project(l, xp, norm_w[l][None], w_in_b, qnw[l][None], knw[l][None], lb_logits, bd,
                      row_tile=ROW_TILE, mix=mix)
        if mix is not None:
            xp, pr = pr[0], pr[1:]
        qa, hgin, gkv = pr
        mixed, s_fin = _prompt_mixers(sinks[l], qa, hgin, gkv, kv_meta[l], s0t[l], hg_norm_w[l][None], batch, seq)
        mix = (mixed, w_out_b)

        kv_last = gkv.reshape(batch, seq, -1)[:, seq - w_buf:, HG_WIDTH:]
        outs["wkp"].append(kv_last[:, :, :KV_WIDTH])
        outs["wvp"].append(kv_last[:, :, KV_WIDTH:])
        outs["hsp"].append(s_fin)
    xp = _out_project(depth - 1, mix[0], xp, mix[1], MIX_TILE)

    stack = lambda name: jnp.stack(outs[name], axis=1)
    heads = lambda a: a.reshape(a.shape[:-1] + (KV_HEADS, HEAD_DIM))
    meta_rows = lambda a: jnp.broadcast_to(a[None], (batch,) + a.shape)
    return (xp.reshape(batch, seq, D_MODEL), y_small[:n_seq].reshape(n_seq, 1, D_MODEL),
            heads(stack("wkp")), heads(stack("wvp")),
            heads(meta_rows(kv_meta[:, :, :KV_WIDTH])), heads(meta_rows(kv_meta[:, :, KV_WIDTH:])), stack("hsp"),
            heads(new_k), heads(new_v), new_state)
```

```python
import functools

import numpy as np
import jax
import jax.numpy as jnp
from jax import lax
from jax.experimental import pallas as pl
from jax.experimental.pallas import tpu as pltpu

F32 = jnp.float32
BF16 = jnp.bfloat16

D_MODEL = 1024
N_META = 16
WINDOW = 128
HEAD_DIM = 64
ATT_WIDTH = 512
ATT_HEADS = 8
KV_HEADS = 2
GQA_GROUP = 4
KV_WIDTH = KV_HEADS * HEAD_DIM
HG_WIDTH = 512
HG_HEADS = 4
HG_KDIM = 128
HG_VDIM = 128
PROJ_WIDTH = 3328
EPS = 1e-6
NEG_BIG = -1e30
TINY = 1e-30
LOG2E = 1.4426950408889634

C_Q, C_K, C_V, C_GA, C_QH, C_FH, C_IH, C_GH = 0, 512, 640, 768, 1280, 1792, 2304, 2816

SUBLANES = 8
MXU_DEPTH = 256
HG_CHUNK = 64
ROW_TILE = 512
MIX_TILE = 1024
SAMPLE_TILE = 16
VMEM_LIMIT = 48 * 1024 * 1024


def _dot(a, b):
    return jnp.dot(a, b, preferred_element_type=F32)


def _dot_nt(a, b):
    return lax.dot_general(a, b, (((1,), (1,)), ((), ())), preferred_element_type=F32)


def _dot_tn(a, b):
    return lax.dot_general(a, b, (((0,), (0,)), ((), ())), preferred_element_type=F32)


def _silu(x):
    return x * (1.0 / (1.0 + jnp.exp(-x)))


def _group_major(x):
    first = lax.broadcasted_iota(jnp.int32, (x.shape[0], KV_WIDTH), 1) < HEAD_DIM
    blocks = [x[:, j * KV_WIDTH:(j + 1) * KV_WIDTH] for j in range(GQA_GROUP)]
    swapped = [pltpu.roll(b, HEAD_DIM, 1) for b in blocks]
    out = []
    for g in range(GQA_GROUP):
        a, b = g // 2, GQA_GROUP // 2 + g // 2
        out.append(jnp.where(first, blocks[a], swapped[b]) if g % 2 == 0 else jnp.where(first, swapped[a], blocks[b]))
    return jnp.concatenate(out, axis=-1)


def _row_parts(n_rows, n_parts):
    step = n_rows // n_parts
    return [slice(i * step, (i + 1) * step) for i in range(n_parts)]


def _mix_proj_kernel(layer, n_parts, mix_ref, wo_ref, x_ref, *rest):
    parts = _row_parts(x_ref.shape[0], n_parts)
    xnew_ref = rest[6]
    xs = []
    for rs in parts:
        x = x_ref[rs, :] + _dot(mix_ref[rs, :], wo_ref[...])
        xnew_ref[rs, :] = x
        xs.append(x)
    _proj_body(layer, parts, xs, *rest[:6], *_projection_views(*rest[7:]))


def _proj_kernel(layer, n_parts, x_ref, *rest):
    parts = _row_parts(x_ref.shape[0], n_parts)
    _proj_body(layer, parts, [x_ref[rs, :] for rs in parts], *rest[:6], *_projection_views(*rest[6:]))


QA_Q, QA_GATE = 0, 1
HI_Q, HI_K, HI_V, HI_GATE = 0, 1, 2, 3
GKV_G = 0
GKV_KV = HG_WIDTH // (2 * KV_WIDTH)


def _projection_views(qa_ref, hgin_ref, gkv_ref):
    col = lambda ref, j, width: ref.at[:, j * width:(j + 1) * width]
    return (col(qa_ref, QA_Q, ATT_WIDTH), col(gkv_ref, GKV_KV, 2 * KV_WIDTH), col(qa_ref, QA_GATE, ATT_WIDTH),
            col(hgin_ref, HI_Q, HG_WIDTH), col(hgin_ref, HI_K, HG_WIDTH), col(hgin_ref, HI_V, HG_WIDTH),
            col(gkv_ref, GKV_G, HG_WIDTH), col(hgin_ref, HI_GATE, HG_WIDTH))


def _proj_body(layer, parts, xs, nw_ref, w_ref, qnw_ref, knw_ref, lbl_ref, bd_ref,
               q_ref, kv_ref, sga_ref, hq_ref, hk_ref, hv_ref, g_ref, sgh_ref):
    nw = nw_ref[...]
    hs = []
    for x in xs:
        ms = jnp.mean(x * x, axis=-1, keepdims=True)
        hs.append((x * lax.rsqrt(ms + EPS) * nw).astype(BF16))

    def proj(h, lo, hi):
        return _dot(h, w_ref[:, lo:hi])

    pq = [[proj(h, C_Q + 256 * c, C_Q + 256 * (c + 1)) for c in range(2)] for h in hs]
    pk = [proj(h, C_K, C_V) for h in hs]
    sq = [[(p * p).astype(BF16) for p in pqs + [pks]] for pqs, pks in zip(pq, pk)]

    logits = lbl_ref[...]
    e = jnp.exp(logits - jnp.max(logits, axis=0, keepdims=True))
    p = e / jnp.sum(e, axis=0, keepdims=True)
    depth_row = lax.broadcasted_iota(jnp.int32, logits.shape, 0)
    in_range = jnp.where(depth_row >= 1, layer - depth_row, -1) >= 0
    lb = jnp.sum(jnp.where(in_range, p, 0.0), axis=0, keepdims=True)

    for rs, h in zip(parts, hs):
        kv_ref[rs, KV_WIDTH:2 * KV_WIDTH] = proj(h, C_V, C_GA)
        sga_ref[rs, :] = _group_major(_silu(proj(h, C_GA, C_QH))).astype(sga_ref.dtype)
        hq_ref[rs, :] = _silu(proj(h, C_QH, C_FH)).astype(hq_ref.dtype)

        z = proj(h, C_FH, C_IH)
        ez = jnp.exp(-jnp.abs(z))
        r = 1.0 / (1.0 + ez)
        pos = z >= 0.0
        sig_pos = jnp.where(pos, r, ez * r)
        sig_neg = jnp.where(pos, ez * r, r)
        hk_ref[rs, :] = ((1.0 - lb) * sig_neg).astype(hk_ref.dtype)
        f = lb + (1.0 - lb) * sig_pos
        g_ref[rs, :] = jnp.log(jnp.maximum(f, TINY))

        hv_ref[rs, :] = proj(h, C_IH, C_GH).astype(hv_ref.dtype)
        sgh_ref[rs, :] = _silu(proj(h, C_GH, PROJ_WIDTH)).astype(sgh_ref.dtype)

    bd = bd_ref[...]
    for rs, pqs, pks, sqs in zip(parts, pq, pk, sq):
        q = jnp.concatenate([p * lax.rsqrt(_dot(s, bd) + EPS) for p, s in zip(pqs, sqs[:2])], axis=-1)
        q_ref[rs, :] = _group_major(q * qnw_ref[...]).astype(q_ref.dtype)
        kv_ref[rs, 0:KV_WIDTH] = pks * lax.rsqrt(_dot(sqs[2], bd[:KV_WIDTH, :KV_WIDTH]) + EPS) * knw_ref[...]


def _project(layer, x, norm_w, w_in, qnw, knw, lb_logits, bd, row_tile, mix=None):
    n = x.shape[0]
    rows = lambda width: pl.BlockSpec((row_tile, width), lambda i: (i, 0))
    full = lambda shape: pl.BlockSpec(shape, lambda i: (0,) * len(shape))
    slab = lambda a, l: pl.BlockSpec((None,) + a.shape[1:], lambda i: (l, 0, 0))
    out = lambda width, dtype: jax.ShapeDtypeStruct((n, width), dtype)
    operands = [x, norm_w, w_in, qnw, knw, lb_logits, bd]
    in_specs = [rows(D_MODEL), full((1, D_MODEL)), slab(w_in, layer), full((1, ATT_WIDTH)),
                full((1, KV_WIDTH)), full(lb_logits.shape), full((256, 256))]
    widths = (2 * ATT_WIDTH, 4 * HG_WIDTH, HG_WIDTH + 2 * KV_WIDTH)
    out_specs = [rows(w) for w in widths]
    out_shape = [out(widths[0], BF16), out(widths[1], BF16), out(widths[2], F32)]
    body = _proj_kernel
    if mix is not None:
        operands = list(mix) + operands
        in_specs = [rows(ATT_WIDTH + HG_WIDTH), slab(mix[1], layer - 1)] + in_specs
        out_specs = [rows(D_MODEL)] + out_specs
        out_shape = [out(D_MODEL, F32)] + out_shape
        body = _mix_proj_kernel
    n_parts = 4 if row_tile % (4 * 128) == 0 else 1
    return pl.pallas_call(
        functools.partial(body, layer, n_parts),
        grid=(n // row_tile,),
        in_specs=in_specs, out_specs=out_specs, out_shape=out_shape,
        compiler_params=pltpu.CompilerParams(dimension_semantics=("arbitrary",), vmem_limit_bytes=VMEM_LIMIT),
        name="proj",
    )(*operands)


def _out_kernel(mix_ref, x_ref, w_ref, y_ref):
    y_ref[...] = x_ref[...] + _dot(mix_ref[...], w_ref[...])


def _out_project(layer, mix, x, w_out, row_tile):
    n = x.shape[0]
    rows = lambda width: pl.BlockSpec((row_tile, width), lambda i: (i, 0))
    return pl.pallas_call(
        _out_kernel,
        grid=(n // row_tile,),
        in_specs=[rows(ATT_WIDTH + HG_WIDTH), rows(D_MODEL),
                  pl.BlockSpec((None, D_MODEL, D_MODEL), lambda i: (layer, 0, 0))],
        out_specs=rows(D_MODEL),
        out_shape=jax.ShapeDtypeStruct((n, D_MODEL), F32),
        compiler_params=pltpu.CompilerParams(dimension_semantics=("arbitrary",), vmem_limit_bytes=VMEM_LIMIT),
        name="out_proj",
    )(mix, x, w_out)


def _split_heads(k):
    first = lax.broadcasted_iota(jnp.int32, k.shape, 1) < HEAD_DIM
    return (jnp.where(first, k, 0.0).astype(BF16), jnp.where(first, 0.0, k).astype(BF16))


def _attend(q, key_sets, sink_of):
    m = q.shape[0]
    first = lax.broadcasted_iota(jnp.int32, (m, KV_WIDTH), 1) < HEAD_DIM
    blocks = []
    for g in range(GQA_GROUP):
        qg = q[:, g * KV_WIDTH:(g + 1) * KV_WIDTH]
        per_head = []
        for h in range(KV_HEADS):
            scores = []
            for k_heads, _, mask in key_sets:
                s = _dot_nt(qg, k_heads[h])
                if mask is not None:
                    s = jnp.where(mask, s, NEG_BIG)
                scores.append(s)
            sink = sink_of(h, g)
            mx = jnp.max(scores[0], axis=-1, keepdims=True)
            for s in scores[1:]:
                mx = jnp.maximum(mx, jnp.max(s, axis=-1, keepdims=True))
            mx = jnp.maximum(mx, sink)
            den = jnp.exp2(sink - mx)
            acc = jnp.zeros((m, KV_WIDTH), F32)
            for s, (_, v, _) in zip(scores, key_sets):
                p = jnp.exp2(s - mx)
                den = den + jnp.sum(p, axis=-1, keepdims=True)
                acc = acc + _dot(p.astype(BF16), v)
            per_head.append(acc * (1.0 / den))
        blocks.append(jnp.where(first, per_head[0], per_head[1]))
    return jnp.concatenate(blocks, axis=-1)


def _attn_stages(i, sink_ref, q_ref, kvc_ref, kvp_ref, kvm_ref, sga_ref, o_ref):
    n_blocks = q_ref.shape[0] // WINDOW
    n_keys = 2 * WINDOW + N_META
    kvm = kvm_ref[...]
    k_blocks = [_split_heads(kvp_ref[:, :KV_WIDTH])]
    vt_blocks = [kvp_ref[:, KV_WIDTH:].T.astype(BF16)]
    for j in range(n_blocks):
        rows = slice(j * WINDOW, (j + 1) * WINDOW)
        k_blocks.append(_split_heads(kvc_ref[rows, :KV_WIDTH]))
        vt_blocks.append(kvc_ref[rows, KV_WIDTH:].T.astype(BF16))
    k_meta = _split_heads(kvm[:, :KV_WIDTH])
    v_meta = jnp.concatenate([kvm[:, KV_WIDTH:], jnp.zeros((WINDOW - N_META, KV_WIDTH), F32)], axis=0)
    vt_meta = v_meta.T.astype(BF16)
    pad = jnp.zeros((3 * WINDOW - n_keys, WINDOW), BF16)

    key = lax.broadcasted_iota(jnp.int32, (n_keys, WINDOW), 0)
    qi = lax.broadcasted_iota(jnp.int32, (n_keys, WINDOW), 1)
    band = jnp.where(key < WINDOW, key - qi - 1, jnp.where(key < 2 * WINDOW, qi - (key - WINDOW), 0))
    visible = band >= 0
    no_prev = jnp.where(i > 0, 0, 2 * WINDOW)
    visible_first = jnp.where(key < WINDOW, band - no_prev, band) >= 0
    head0_rows = lax.broadcasted_iota(jnp.int32, (KV_WIDTH, WINDOW), 0) < HEAD_DIM
    group_lanes = [slice(g * KV_WIDTH, (g + 1) * KV_WIDTH) for g in range(GQA_GROUP)]

    s_both = []
    for j in range(n_blocks):
        rows = slice(j * WINDOW, (j + 1) * WINDOW)
        kk = jnp.concatenate([k_blocks[j][0], k_blocks[j + 1][0], k_meta[0],
                              k_blocks[j][1], k_blocks[j + 1][1], k_meta[1]], axis=0)
        s_both.append([_dot_nt(kk, q_ref[rows, lanes]) for lanes in group_lanes])
    yield
    probs = []
    for j in range(n_blocks):
        vis = visible_first if j == 0 else visible
        for g in range(GQA_GROUP):
            for h in range(KV_HEADS):
                sink = sink_ref[h * GQA_GROUP + g]
                s = jnp.where(vis, s_both[j][g][h * n_keys:(h + 1) * n_keys], NEG_BIG)
                mx = jnp.maximum(jnp.max(s, axis=0, keepdims=True), sink)
                p = jnp.exp2(s - mx)
                den = jnp.sum(p, axis=0, keepdims=True) + jnp.exp2(sink - mx)
                probs.append((jnp.concatenate([p.astype(BF16), pad], axis=0), 1.0 / den))
    outs = []
    for j in range(n_blocks):
        v_t = jnp.concatenate([vt_blocks[j], vt_blocks[j + 1], vt_meta], axis=1)
        for p_pad, inv in probs[j * ATT_HEADS:(j + 1) * ATT_HEADS]:
            outs.append(_dot(v_t, p_pad) * inv)
    yield
    for j in range(n_blocks):
        rows = slice(j * WINDOW, (j + 1) * WINDOW)
        for g, lanes in enumerate(group_lanes):
            pair = outs[j * ATT_HEADS + g * KV_HEADS:j * ATT_HEADS + (g + 1) * KV_HEADS]
            o_t = jnp.where(head0_rows, pair[0], pair[1])
            o_ref[rows, lanes] = (o_t.T * sga_ref[rows, lanes].astype(F32)).astype(o_ref.dtype)


def _level_sizes(c):
    return [c >> (i + 1) for i in range(c.bit_length() - 1)]


def _decay_sum_matrix(c):
    t = np.arange(c)[:, None]
    r = np.arange(c)[None, :]
    mats = [r <= t]
    for bs in _level_sizes(c):
        if bs < SUBLANES:
            a = (t // (2 * bs)) * (2 * bs) + bs - 1
            mats.append(((r > t) & (r <= a)) | ((r > a) & (r <= t)))
    return np.tile(np.concatenate(mats, axis=0).astype(np.float32), (1, _split_terms(c)))


def _split_terms(c):
    return min(3, MXU_DEPTH // c)


def _decay_sums(g, dmat):
    rest = g * LOG2E
    terms = []
    for _ in range(dmat.shape[1] // g.shape[0]):
        terms.append(rest.astype(BF16))
        rest = rest - terms[-1].astype(F32)
    return _dot(dmat, jnp.concatenate(terms, axis=0))


def _chunk_exponents(block, c):
    start = block(0)
    grp = lambda j: start[j * SUBLANES:(j + 1) * SUBLANES, :]
    lasts = {}

    def last(j):
        if j not in lasts:
            lasts[j] = jnp.broadcast_to(start[j * SUBLANES + SUBLANES - 1:(j + 1) * SUBLANES, :], (SUBLANES, HG_KDIM))
        return lasts[j]

    n_grp = c // SUBLANES
    end = jnp.concatenate([last(n_grp - 1) - grp(j) for j in range(n_grp)], axis=0)
    levels, fine = [], 1
    for bs in _level_sizes(c):
        if bs >= SUBLANES:
            per = bs // SUBLANES
            parts = []
            for j in range(n_grp):
                blk = j // per
                anchor = last((blk // 2) * 2 * per + per - 1)
                parts.append(grp(j) - anchor if blk % 2 == 1 else anchor - grp(j))
            levels.append(jnp.concatenate(parts, axis=0))
        else:
            levels.append(block(fine))
            fine += 1
    return start, end, levels


def _hgrn_masks(c):
    row = lax.broadcasted_iota(jnp.int32, (c, c), 0)
    col = lax.broadcasted_iota(jnp.int32, (c, c), 1)
    row_k = lax.broadcasted_iota(jnp.int32, (c, HG_KDIM), 0)
    levels = []
    for bs in _level_sizes(c):
        rb, cb = row // bs, col // bs
        pairs = ((rb % 2) * (1 - jnp.abs(cb - rb + 1))) > 0
        levels.append((bs, (row_k // bs) % 2 == 1, pairs))
    return levels, row == col


def _hgrn_chunks(items, states, masks):
    outs = []
    for _ in _hgrn_chunk_stages(items, states, masks, outs):
        pass
    return outs


def _hgrn_chunk_stages(items, states, masks, outs):
    levels, diag = masks
    c = items[0][1].shape[0]

    stage1 = []
    for _, q, k, v, (_, to_end, level_ex) in items:
        pair_scores = []
        for (bs, q_side, _), lex in zip(levels, level_ex):
            if bs >= SUBLANES:
                side = jnp.concatenate(
                    [(q if (j // bs) % 2 == 1 else k)[j:j + SUBLANES] for j in range(0, c, SUBLANES)], axis=0)
            else:
                side = jnp.where(q_side, q, k)
            u = (side * jnp.exp2(lex)).astype(BF16)
            pair_scores.append(_dot_nt(u, u))
        kdec = (k * jnp.exp2(to_end)).astype(BF16)
        stage1.append((pair_scores, _dot_tn(v, kdec)))
    yield

    intra = []
    for (_, q, k, v, _), (pair_scores, _) in zip(items, stage1):
        a = jnp.where(diag, jnp.sum(q * k, axis=-1, keepdims=True), 0.0)
        for (_, _, pairs), scores in zip(levels, pair_scores):
            a = jnp.where(pairs, scores, a)
        intra.append(_dot(a.astype(BF16), v))
    yield

    for (head, q, _, _, (from_start, _, _)), (_, increment), o_intra in zip(items, stage1, intra):
        st = states[head]
        decay = jnp.exp2(from_start)
        outs.append(o_intra + _dot_nt((q * decay).astype(BF16), st.astype(BF16)))
        states[head] = st * decay[c - 1:c, :] + increment


def _head_norm_gate(o, nw, gate):
    ms = jnp.mean(o * o, axis=-1, keepdims=True)
    return o * lax.rsqrt(ms + EPS) * nw * gate


def _hgrn_stages(hq_ref, hk_ref, hv_ref, g_ref, sgh_ref, nw_ref, dmat_ref, o_ref, st_ref):
    masks = _hgrn_masks(HG_CHUNK)
    dmat = dmat_ref[...]

    def block_of(sums, lanes):
        return lambda i: sums[i * HG_CHUNK:(i + 1) * HG_CHUNK, lanes]

    items, where = [], []
    for ci in range(hq_ref.shape[0] // HG_CHUNK):
        rows = slice(ci * HG_CHUNK, (ci + 1) * HG_CHUNK)
        sums = _decay_sums(g_ref[rows, :], dmat)
        for h in range(HG_HEADS):
            lanes = slice(h * HG_KDIM, (h + 1) * HG_KDIM)
            items.append((h, hq_ref[rows, lanes].astype(F32), hk_ref[rows, lanes].astype(F32),
                          hv_ref[rows, lanes], _chunk_exponents(block_of(sums, lanes), HG_CHUNK)))
            where.append((rows, lanes))
    yield
    states = [st_ref[h] for h in range(HG_HEADS)]
    outs = []
    yield from _hgrn_chunk_stages(items, states, masks, outs)
    nw = nw_ref[...]
    for (rows, lanes), o in zip(where, outs):
        o_ref[rows, lanes] = _head_norm_gate(o, nw, sgh_ref[rows, lanes].astype(F32)).astype(o_ref.dtype)
    for h in range(HG_HEADS):
        st_ref[h] = states[h]


def _mixer_kernel(sink_ref, q_ref, kvc_ref, kvp_ref, kvm_ref, sga_ref, hq_ref, hk_ref, hv_ref, g_ref, sgh_ref,
                  s0_ref, nw_ref, dmat_ref, mix_ref, sfin_ref, st_ref):
    j = pl.program_id(1)

    @pl.when(j == 0)
    def _():
        st_ref[...] = s0_ref[...]

    attention = _attn_stages(j, sink_ref, q_ref, kvc_ref, kvp_ref, kvm_ref, sga_ref, mix_ref.at[:, 0:ATT_WIDTH])
    hgrn = _hgrn_stages(hq_ref, hk_ref, hv_ref, g_ref, sgh_ref, nw_ref, dmat_ref, mix_ref.at[:, ATT_WIDTH:], st_ref)
    for stage in (hgrn, hgrn, attention, hgrn, attention, attention, hgrn):
        next(stage, None)
    for stage in (attention, hgrn):
        for _ in stage:
            pass

    @pl.when(j == pl.num_programs(1) - 1)
    def _():
        for h in range(HG_HEADS):
            sfin_ref[h] = st_ref[h].T


def _prompt_mixers(sinks, qa, hgin, gkv, kv_meta, s0t, hg_nw, batch, seq):
    nt = seq // MIX_TILE
    per = MIX_TILE // WINDOW
    blk = lambda width, col: pl.BlockSpec((MIX_TILE, width), lambda b, j, s: (b * nt + j, col))
    prev = pl.BlockSpec((WINDOW, 2 * KV_WIDTH), lambda b, j, s: ((b * nt + j) * per - jnp.minimum(j, 1), GKV_KV))
    full = lambda shape: pl.BlockSpec(shape, lambda b, j, s: (0,) * len(shape))
    state_shape = (HG_HEADS, HG_VDIM, HG_KDIM)
    dmat = jnp.asarray(_decay_sum_matrix(HG_CHUNK), BF16)
    return pl.pallas_call(
        _mixer_kernel,
        grid_spec=pltpu.PrefetchScalarGridSpec(
            num_scalar_prefetch=1, grid=(batch, nt),
            in_specs=[blk(ATT_WIDTH, QA_Q), blk(2 * KV_WIDTH, GKV_KV), prev, full((N_META, 2 * KV_WIDTH)),
                      blk(ATT_WIDTH, QA_GATE),
                      blk(HG_WIDTH, HI_Q), blk(HG_WIDTH, HI_K), blk(HG_WIDTH, HI_V), blk(HG_WIDTH, GKV_G),
                      blk(HG_WIDTH, HI_GATE), full(state_shape), full((1, HG_VDIM)), full(dmat.shape)],
            out_specs=[blk(ATT_WIDTH + HG_WIDTH, 0),
                       pl.BlockSpec((None,) + state_shape, lambda b, j, s: (b, 0, 0, 0))],
            scratch_shapes=[pltpu.VMEM(state_shape, F32)]),
        out_shape=[jax.ShapeDtypeStruct((batch * seq, ATT_WIDTH + HG_WIDTH), BF16),
                   jax.ShapeDtypeStruct((batch,) + state_shape, F32)],
        compiler_params=pltpu.CompilerParams(dimension_semantics=("arbitrary", "arbitrary"),
                                             vmem_limit_bytes=VMEM_LIMIT),
        name="mixers",
    )(sinks, qa, gkv, gkv, kv_meta, qa, hgin, hgin, hgin, gkv, hgin, s0t, hg_nw, dmat)


def _meta_mix(sink_of, q_ref, kv_ref, sga_ref, hq_ref, hk_ref, hv_ref, g_ref, sgh_ref, nw_ref, dmat_ref,
              att_ref, hg_ref, st_ref):
    kv = kv_ref[...]
    row = lax.broadcasted_iota(jnp.int32, (N_META, N_META), 0)
    col = lax.broadcasted_iota(jnp.int32, (N_META, N_META), 1)
    att = _attend(q_ref[...].astype(BF16),
                  [(_split_heads(kv[:, :KV_WIDTH]), kv[:, KV_WIDTH:].astype(BF16), col <= row)], sink_of)
    att_ref[...] = (att * sga_ref[...]).astype(att_ref.dtype)

    sums = _decay_sums(g_ref[...], dmat_ref[...])
    masks = _hgrn_masks(N_META)
    head_lanes = [slice(h * HG_KDIM, (h + 1) * HG_KDIM) for h in range(HG_HEADS)]
    items = [(h, hq_ref[:, lanes], hk_ref[:, lanes], hv_ref[:, lanes].astype(BF16),
              _chunk_exponents((lambda lanes: lambda i: sums[i * N_META:(i + 1) * N_META, lanes])(lanes), N_META))
             for h, lanes in enumerate(head_lanes)]
    states = [jnp.zeros((HG_VDIM, HG_KDIM), F32) for _ in range(HG_HEADS)]
    outs = _hgrn_chunks(items, states, masks)
    for h, lanes in enumerate(head_lanes):
        st_ref[h] = states[h]
        hg_ref[:, lanes] = _head_norm_gate(outs[h], nw_ref[...], sgh_ref[:, lanes]).astype(hg_ref.dtype)


def _sample_mix(q_ref, kv_ref, sga_ref, hq_ref, hk_ref, hv_ref, g_ref, sgh_ref, nw_ref,
                sink_ref, seg_ref, exp_ref, ck_ref, cv_ref, mk_ref, mv_ref, s_ref,
                att_ref, hg_ref, nk_ref, nv_ref, ns_ref):
    n_keys = WINDOW + N_META + SUBLANES
    key_row = lax.broadcasted_iota(jnp.int32, (n_keys, KV_WIDTH), 0)
    visible = jnp.logical_and(key_row >= 1, key_row <= WINDOW + N_META)
    win_row = lax.broadcasted_iota(jnp.int32, (WINDOW, KV_WIDTH), 0)
    sink = sink_ref[...]
    seg = seg_ref[...]
    expand = exp_ref[...]
    nw = nw_ref[...]

    group = 8

    def per_group(i, carry):
        seqs = [i * group + j for j in range(group)]
        new_rows, scores, values = [], [], []
        for b in seqs:
            kv_new = kv_ref[b]
            k_new = kv_new[:, 0:KV_WIDTH]
            v_new = kv_new[:, KV_WIDTH:2 * KV_WIDTH]
            new_rows.append((k_new, v_new))
            ck = ck_ref[b]
            cv = cv_ref[b]
            keys = jnp.concatenate([ck, mk_ref[b], jnp.broadcast_to(k_new, (SUBLANES, KV_WIDTH))], axis=0)
            values.append(jnp.concatenate([cv, mv_ref[b], jnp.broadcast_to(v_new, (SUBLANES, KV_WIDTH))], axis=0))
            prod = jnp.concatenate([keys] * GQA_GROUP, axis=1) * q_ref[b]
            scores.append(_dot(prod.astype(BF16), seg))
            nk_ref[b] = jnp.where(win_row == WINDOW - 1, k_new, pltpu.roll(ck, WINDOW - 1, 0))
            nv_ref[b] = jnp.where(win_row == WINDOW - 1, v_new, pltpu.roll(cv, WINDOW - 1, 0))

        wides = []
        for s in scores:
            s = jnp.where(visible, s, NEG_BIG)
            mx = jnp.maximum(jnp.max(s, axis=0, keepdims=True), sink)
            p = jnp.exp2(s - mx)
            den = jnp.sum(p, axis=0, keepdims=True) + jnp.exp2(sink - mx)
            p = p * (1.0 / den)
            wides.append(_dot(p.astype(BF16), expand))

        for b, wide, vals in zip(seqs, wides, values):
            att = jnp.sum(wide * jnp.concatenate([vals] * GQA_GROUP, axis=1), axis=0, keepdims=True)
            att_ref[b] = att * sga_ref[b]

            decay_row, hq_row, hk_row, hv_row, sgh_row = jnp.exp(g_ref[b]), hq_ref[b], hk_ref[b], hv_ref[b], sgh_ref[b]
            outs = []
            for h in range(HG_HEADS):
                lanes = slice(h * HG_KDIM, (h + 1) * HG_KDIM)
                col = lambda r: jnp.broadcast_to(r[:, lanes], (HG_KDIM, HG_KDIM)).T
                s1 = col(decay_row) * s_ref[b, h] + col(hk_row) * hv_row[:, lanes]
                ns_ref[b, h] = s1
                q_rows = jnp.broadcast_to(hq_row[:, lanes], (2 * SUBLANES, HG_KDIM)).astype(BF16)
                o = _dot(q_rows, s1.astype(BF16))[0:1, :]
                outs.append(_head_norm_gate(o, nw, sgh_row[:, lanes]))
            hg_ref[b] = jnp.concatenate(outs, axis=-1)
        return carry

    lax.fori_loop(0, q_ref.shape[0] // group, per_group, 0)


def _small_path_kernel(sinks_ref, x0_ref, nw_ref, w_ref, qnw_ref, knw_ref, lbl_ref, bd_ref, wo_ref, hgnw_ref,
                       sinkrow_ref, seg_ref, exp_ref, dmat_ref, ck_ref, cv_ref, mk_ref, mv_ref, s_ref,
                       y_ref, kvm_ref, s0t_ref, nk_ref, nv_ref, ns_ref,
                       x_scr, att_scr, hg_scr, *scr):
    proj_scr, staged, staged_att, staged_hg = scr[:8], scr[8:16], scr[16], scr[17]
    layer, tile = pl.program_id(0), pl.program_id(1)
    n_rows = x_scr.shape[0]
    n_seq = n_rows - N_META
    t = ck_ref.shape[0]

    @pl.when(jnp.logical_and(layer == 0, tile == 0))
    def _():
        x_scr[...] = x0_ref[...]

    @pl.when(tile == 0)
    def _():
        _proj_body(layer, [slice(0, n_rows)], [x_scr[...]], nw_ref, w_ref, qnw_ref, knw_ref, lbl_ref, bd_ref,
                   *proj_scr)
        meta = lambda ref: ref.at[n_seq:n_rows]
        _meta_mix(lambda h, g: sinks_ref[layer, h * GQA_GROUP + g], *[meta(r) for r in proj_scr], hgnw_ref,
                  dmat_ref, meta(att_scr), meta(hg_scr), s0t_ref)
        kvm_ref[...] = proj_scr[1][n_seq:n_rows, :]

    rows = pl.ds(pl.multiple_of(tile * t, t), t)
    for src, dst in zip(proj_scr, staged):
        block = src[rows, :]
        for b in range(t):
            dst[b] = block[b:b + 1, :]
    _sample_mix(*staged, hgnw_ref, sinkrow_ref, seg_ref, exp_ref, ck_ref, cv_ref, mk_ref, mv_ref, s_ref,
                staged_att, staged_hg, nk_ref, nv_ref, ns_ref)
    att_scr[rows, :] = jnp.concatenate([staged_att[b] for b in range(t)], axis=0)
    hg_scr[rows, :] = jnp.concatenate([staged_hg[b] for b in range(t)], axis=0)

    @pl.when(tile == pl.num_programs(1) - 1)
    def _():
        x = (x_scr[...] + _dot(att_scr[...].astype(BF16), wo_ref[0:ATT_WIDTH, :])
             + _dot(hg_scr[...].astype(BF16), wo_ref[ATT_WIDTH:, :]))
        x_scr[...] = x
        y_ref[...] = x


def _small_path(sinks, x0, norm_w, w_in, qnw, knw, lb_logits, bd, w_out, hg_nw, sink_rows, seg, expand,
                cache_k, cache_v, meta_k, meta_v, state):
    n_rows = x0.shape[0]
    n_seq, depth = state.shape[:2]
    t = SAMPLE_TILE
    dmat = jnp.asarray(_decay_sum_matrix(N_META), BF16)
    full = lambda a: pl.BlockSpec(a.shape, lambda l, i, s: (0,) * a.ndim)
    per_layer = lambda a: pl.BlockSpec((None,) + a.shape[1:], lambda l, i, s: (l,) + (0,) * (a.ndim - 1))
    cache = lambda a: pl.BlockSpec((t, None) + a.shape[2:], lambda l, i, s: (i, l) + (0,) * (a.ndim - 2))
    as_rows = lambda a: a[:, None, :]
    state_t = (HG_HEADS, HG_VDIM, HG_KDIM)
    widths = (ATT_WIDTH, 2 * KV_WIDTH, ATT_WIDTH, HG_WIDTH, HG_WIDTH, HG_WIDTH, HG_WIDTH, HG_WIDTH)
    operands = [x0, as_rows(norm_w), w_in, as_rows(qnw), as_rows(knw), lb_logits, bd, w_out, as_rows(hg_nw),
                as_rows(sink_rows), seg, expand, dmat, cache_k, cache_v, meta_k, meta_v, state]
    in_specs = [full(x0), per_layer(operands[1]), per_layer(w_in), per_layer(operands[3]), per_layer(operands[4]),
                full(lb_logits), full(bd), per_layer(w_out), per_layer(operands[8]), per_layer(operands[9]),
                full(seg), full(expand), full(dmat), cache(cache_k), cache(cache_v), cache(meta_k), cache(meta_v),
                cache(state)]
    out_shape = [jax.ShapeDtypeStruct((n_rows, D_MODEL), F32),
                 jax.ShapeDtypeStruct((depth, N_META, 2 * KV_WIDTH), F32),
                 jax.ShapeDtypeStruct((depth,) + state_t, F32),
                 jax.ShapeDtypeStruct(cache_k.shape, F32), jax.ShapeDtypeStruct(cache_v.shape, F32),
                 jax.ShapeDtypeStruct(state.shape, F32)]
    out_specs = [pl.BlockSpec((n_rows, D_MODEL), lambda l, i, s: (0, 0)),
                 pl.BlockSpec((None, N_META, 2 * KV_WIDTH), lambda l, i, s: (l, 0, 0)),
                 pl.BlockSpec((None,) + state_t, lambda l, i, s: (l, 0, 0, 0)),
                 cache(cache_k), cache(cache_v), cache(state)]
    scratch = ([pltpu.VMEM((n_rows, D_MODEL), F32), pltpu.VMEM((n_rows, ATT_WIDTH), F32),
                pltpu.VMEM((n_rows, HG_WIDTH), F32)]
               + [pltpu.VMEM((n_rows, w), F32) for w in widths]
               + [pltpu.VMEM((t, 1, w), F32) for w in widths]
               + [pltpu.VMEM((t, 1, ATT_WIDTH), F32), pltpu.VMEM((t, 1, HG_WIDTH), F32)])
    return pl.pallas_call(
        _small_path_kernel,
        grid_spec=pltpu.PrefetchScalarGridSpec(
            num_scalar_prefetch=1, grid=(depth, n_seq // t),
            in_specs=in_specs, out_specs=out_specs, scratch_shapes=scratch),
        out_shape=out_shape,
        compiler_params=pltpu.CompilerParams(dimension_semantics=("arbitrary", "arbitrary"),
                                             vmem_limit_bytes=VMEM_LIMIT),
        name="small_path",
    )(sinks, *operands)


def _g_major(a, axis):
    shape = a.shape
    a = a.reshape(shape[:axis] + (KV_HEADS, GQA_GROUP, HEAD_DIM) + shape[axis + 1:])
    a = jnp.swapaxes(a, axis, axis + 1)
    return a.reshape(shape)


def _constants():
    lane = np.arange(ATT_WIDTH)
    g_of, h_of = lane // KV_WIDTH, (lane % KV_WIDTH) // HEAD_DIM
    head = h_of * GQA_GROUP + g_of
    seg = (head[:, None] == np.arange(KV_WIDTH)[None, :]).astype(np.float32)
    grp = np.arange(256) // HEAD_DIM
    bd = (grp[:, None] == grp[None, :]).astype(np.float32) / HEAD_DIM
    return jnp.asarray(seg, BF16), jnp.asarray(seg.T, BF16), jnp.asarray(bd, BF16)


def kernel(x_prompt, x_sample, cache_win_k, cache_win_v, cache_meta_k, cache_meta_v, state_hgrn, meta_tokens,
           norm_w, w_in, q_norm_w, k_norm_w, attn_sinks, hg_lb_logits, hg_norm_w, w_out):
    batch, seq, _ = x_prompt.shape
    n_seq = x_sample.shape[0]
    depth = w_in.shape[0]
    w_buf = cache_win_k.shape[2]
    assert x_sample.shape[1] == 1 and w_buf == WINDOW and n_seq % SAMPLE_TILE == 0
    assert seq % ROW_TILE == 0 and seq % MIX_TILE == 0

    seg, expand, bd = _constants()
    w_in_b = w_in.astype(BF16)
    w_out_b = jnp.concatenate([_g_major(w_out[:, :ATT_WIDTH], 1), w_out[:, ATT_WIDTH:]], axis=1).astype(BF16)
    qnw = jnp.tile(q_norm_w, (1, ATT_HEADS)) * (HEAD_DIM ** -0.5 * LOG2E)
    knw = jnp.tile(k_norm_w, (1, KV_HEADS))
    lb_logits = hg_lb_logits.astype(F32)
    sinks = attn_sinks.astype(F32) * LOG2E
    sink_rows = jnp.pad(sinks, ((0, 0), (0, KV_WIDTH - ATT_HEADS)))

    ck = cache_win_k.reshape(n_seq, depth, w_buf, KV_WIDTH)
    cv = cache_win_v.reshape(n_seq, depth, w_buf, KV_WIDTH)
    mk = cache_meta_k.reshape(n_seq, depth, N_META, KV_WIDTH)
    mv = cache_meta_v.reshape(n_seq, depth, N_META, KV_WIDTH)

    x_small = jnp.concatenate([x_sample.reshape(n_seq, D_MODEL), meta_tokens.astype(F32)], axis=0)
    y_small, kv_meta, s0t, new_k, new_v, new_state = _small_path(
        sinks, x_small, norm_w, w_in_b, qnw, knw, lb_logits, bd, w_out_b, hg_norm_w, sink_rows,
        seg, expand, ck, cv, mk, mv, state_hgrn)

    xp = x_prompt.reshape(batch * seq, D_MODEL)
    outs = {k: [] for k in ("wkp", "wvp", "hsp")}
    mix = None
    for l in range(depth):
        pr = _project(l, xp, norm_w[l][None], w_in_b, qnw[l][None], knw[l][None], lb_logits, bd,
                      row_tile=ROW_TILE, mix=mix)
        if mix is not None:
            xp, pr = pr[0], pr[1:]
        qa, hgin, gkv = pr
        mixed, s_fin = _prompt_mixers(sinks[l], qa, hgin, gkv, kv_meta[l], s0t[l], hg_norm_w[l][None], batch, seq)
        mix = (mixed, w_out_b)

        kv_last = gkv.reshape(batch, seq, -1)[:, seq - w_buf:, HG_WIDTH:]
        outs["wkp"].append(kv_last[:, :, :KV_WIDTH])
        outs["wvp"].append(kv_last[:, :, KV_WIDTH:])
        outs["hsp"].append(s_fin)
    xp = _out_project(depth - 1, mix[0], xp, mix[1], MIX_TILE)

    stack = lambda name: jnp.stack(outs[name], axis=1)
    heads = lambda a: a.reshape(a.shape[:-1] + (KV_HEADS, HEAD_DIM))
    meta_rows = lambda a: jnp.broadcast_to(a[None], (batch,) + a.shape)
    return (xp.reshape(batch, seq, D_MODEL), y_small[:n_seq].reshape(n_seq, 1, D_MODEL),
            heads(stack("wkp")), heads(stack("wvp")),
            heads(meta_rows(kv_meta[:, :, :KV_WIDTH])), heads(meta_rows(kv_meta[:, :, KV_WIDTH:])), stack("hsp"),
            heads(new_k), heads(new_v), new_state)
```

```python
import functools

import numpy as np
import jax
import jax.numpy as jnp
from jax import lax
from jax.experimental import pallas as pl
from jax.experimental.pallas import tpu as pltpu

F32 = jnp.float32
BF16 = jnp.bfloat16

D_MODEL = 1024
N_META = 16
WINDOW = 128
HEAD_DIM = 64
ATT_WIDTH = 512
ATT_HEADS = 8
KV_HEADS = 2
GQA_GROUP = 4
KV_WIDTH = KV_HEADS * HEAD_DIM
HG_WIDTH = 512
HG_HEADS = 4
HG_KDIM = 128
HG_VDIM = 128
PROJ_WIDTH = 3328
EPS = 1e-6
NEG_BIG = -1e30
TINY = 1e-30
LOG2E = 1.4426950408889634

C_Q, C_K, C_V, C_GA, C_QH, C_FH, C_IH, C_GH = 0, 512, 640, 768, 1280, 1792, 2304, 2816

SUBLANES = 8
MXU_DEPTH = 256
HG_CHUNK = 64
ROW_TILE = 1024
MIX_TILE = 1024
SAMPLE_TILE = 16
VMEM_LIMIT = 48 * 1024 * 1024
PROJ_VMEM_LIMIT = 56 * 1024 * 1024


def _dot(a, b):
    return jnp.dot(a, b, preferred_element_type=F32)


def _dot_nt(a, b):
    return lax.dot_general(a, b, (((1,), (1,)), ((), ())), preferred_element_type=F32)


def _dot_tn(a, b):
    return lax.dot_general(a, b, (((0,), (0,)), ((), ())), preferred_element_type=F32)


def _silu(x):
    return x * (1.0 / (1.0 + jnp.exp(-x)))


def _group_major(x):
    first = lax.broadcasted_iota(jnp.int32, (x.shape[0], KV_WIDTH), 1) < HEAD_DIM
    blocks = [x[:, j * KV_WIDTH:(j + 1) * KV_WIDTH] for j in range(GQA_GROUP)]
    swapped = [pltpu.roll(b, HEAD_DIM, 1) for b in blocks]
    out = []
    for g in range(GQA_GROUP):
        a, b = g // 2, GQA_GROUP // 2 + g // 2
        out.append(jnp.where(first, blocks[a], swapped[b]) if g % 2 == 0 else jnp.where(first, swapped[a], blocks[b]))
    return jnp.concatenate(out, axis=-1)


def _row_parts(n_rows, n_parts):
    step = n_rows // n_parts
    return [slice(i * step, (i + 1) * step) for i in range(n_parts)]


def _mix_proj_kernel(layer, n_parts, mix_ref, wo_ref, x_ref, *rest):
    parts = _row_parts(x_ref.shape[0], n_parts)
    xnew_ref = rest[6]
    xs = []
    for rs in parts:
        x = x_ref[rs, :] + _dot(mix_ref[rs, :], wo_ref[...])
        xnew_ref[rs, :] = x
        xs.append(x)
    _proj_body(layer, parts, xs, *rest[:6], *_projection_views(*rest[7:]))


def _proj_kernel(layer, n_parts, x_ref, *rest):
    parts = _row_parts(x_ref.shape[0], n_parts)
    _proj_body(layer, parts, [x_ref[rs, :] for rs in parts], *rest[:6], *_projection_views(*rest[6:]))


QA_Q, QA_GATE = 0, 1
HI_Q, HI_K, HI_V, HI_GATE = 0, 1, 2, 3
GKV_G = 0
GKV_KV = HG_WIDTH // (2 * KV_WIDTH)


def _projection_views(qa_ref, hgin_ref, gkv_ref):
    col = lambda ref, j, width: ref.at[:, j * width:(j + 1) * width]
    return (col(qa_ref, QA_Q, ATT_WIDTH), col(gkv_ref, GKV_KV, 2 * KV_WIDTH), col(qa_ref, QA_GATE, ATT_WIDTH),
            col(hgin_ref, HI_Q, HG_WIDTH), col(hgin_ref, HI_K, HG_WIDTH), col(hgin_ref, HI_V, HG_WIDTH),
            col(gkv_ref, GKV_G, HG_WIDTH), col(hgin_ref, HI_GATE, HG_WIDTH))


def _proj_body(layer, parts, xs, nw_ref, w_ref, qnw_ref, knw_ref, lbl_ref, bd_ref,
               q_ref, kv_ref, sga_ref, hq_ref, hk_ref, hv_ref, g_ref, sgh_ref):
    nw = nw_ref[...]
    hs = []
    for x in xs:
        ms = jnp.mean(x * x, axis=-1, keepdims=True)
        hs.append((x * lax.rsqrt(ms + EPS) * nw).astype(BF16))

    def proj(h, lo, hi):
        return _dot(h, w_ref[:, lo:hi])

    pq = [[proj(h, C_Q + 256 * c, C_Q + 256 * (c + 1)) for c in range(2)] for h in hs]
    pk = [proj(h, C_K, C_V) for h in hs]
    sq = [[(p * p).astype(BF16) for p in pqs + [pks]] for pqs, pks in zip(pq, pk)]

    logits = lbl_ref[...]
    e = jnp.exp(logits - jnp.max(logits, axis=0, keepdims=True))
    p = e / jnp.sum(e, axis=0, keepdims=True)
    depth_row = lax.broadcasted_iota(jnp.int32, logits.shape, 0)
    in_range = jnp.where(depth_row >= 1, layer - depth_row, -1) >= 0
    lb = jnp.sum(jnp.where(in_range, p, 0.0), axis=0, keepdims=True)

    for rs, h in zip(parts, hs):
        kv_ref[rs, KV_WIDTH:2 * KV_WIDTH] = proj(h, C_V, C_GA)
        sga_ref[rs, :] = _group_major(_silu(proj(h, C_GA, C_QH))).astype(sga_ref.dtype)
        hq_ref[rs, :] = _silu(proj(h, C_QH, C_FH)).astype(hq_ref.dtype)

        z = proj(h, C_FH, C_IH)
        ez = jnp.exp(-jnp.abs(z))
        r = 1.0 / (1.0 + ez)
        pos = z >= 0.0
        sig_pos = jnp.where(pos, r, ez * r)
        sig_neg = jnp.where(pos, ez * r, r)
        hk_ref[rs, :] = ((1.0 - lb) * sig_neg).astype(hk_ref.dtype)
        f = lb + (1.0 - lb) * sig_pos
        g_ref[rs, :] = jnp.log(jnp.maximum(f, TINY))

        hv_ref[rs, :] = proj(h, C_IH, C_GH).astype(hv_ref.dtype)
        sgh_ref[rs, :] = _silu(proj(h, C_GH, PROJ_WIDTH)).astype(sgh_ref.dtype)

    bd = bd_ref[...]
    for rs, pqs, pks, sqs in zip(parts, pq, pk, sq):
        q = jnp.concatenate([p * lax.rsqrt(_dot(s, bd) + EPS) for p, s in zip(pqs, sqs[:2])], axis=-1)
        q_ref[rs, :] = _group_major(q * qnw_ref[...]).astype(q_ref.dtype)
        kv_ref[rs, 0:KV_WIDTH] = pks * lax.rsqrt(_dot(sqs[2], bd[:KV_WIDTH, :KV_WIDTH]) + EPS) * knw_ref[...]


def _project(layer, x, norm_w, w_in, qnw, knw, lb_logits, bd, row_tile, mix=None):
    n = x.shape[0]
    rows = lambda width: pl.BlockSpec((row_tile, width), lambda i: (i, 0))
    full = lambda shape: pl.BlockSpec(shape, lambda i: (0,) * len(shape))
    slab = lambda a, l: pl.BlockSpec((None,) + a.shape[1:], lambda i: (l, 0, 0), pipeline_mode=pl.Buffered(1))
    out = lambda width, dtype: jax.ShapeDtypeStruct((n, width), dtype)
    operands = [x, norm_w, w_in, qnw, knw, lb_logits, bd]
    in_specs = [rows(D_MODEL), full((1, D_MODEL)), slab(w_in, layer), full((1, ATT_WIDTH)),
                full((1, KV_WIDTH)), full(lb_logits.shape), full((256, 256))]
    widths = (2 * ATT_WIDTH, 4 * HG_WIDTH, HG_WIDTH + 2 * KV_WIDTH)
    out_specs = [rows(w) for w in widths]
    out_shape = [out(widths[0], BF16), out(widths[1], BF16), out(widths[2], F32)]
    body = _proj_kernel
    if mix is not None:
        operands = list(mix) + operands
        in_specs = [rows(ATT_WIDTH + HG_WIDTH), slab(mix[1], layer - 1)] + in_specs
        out_specs = [rows(D_MODEL)] + out_specs
        out_shape = [out(D_MODEL, F32)] + out_shape
        body = _mix_proj_kernel
    n_parts = row_tile // 128 if row_tile % 128 == 0 else 1
    return pl.pallas_call(
        functools.partial(body, layer, n_parts),
        grid=(n // row_tile,),
        in_specs=in_specs, out_specs=out_specs, out_shape=out_shape,
        compiler_params=pltpu.CompilerParams(dimension_semantics=("arbitrary",), vmem_limit_bytes=PROJ_VMEM_LIMIT),
        name="proj",
    )(*operands)


def _out_kernel(mix_ref, x_ref, w_ref, y_ref):
    y_ref[...] = x_ref[...] + _dot(mix_ref[...], w_ref[...])


def _out_project(layer, mix, x, w_out, row_tile):
    n = x.shape[0]
    rows = lambda width: pl.BlockSpec((row_tile, width), lambda i: (i, 0))
    return pl.pallas_call(
        _out_kernel,
        grid=(n // row_tile,),
        in_specs=[rows(ATT_WIDTH + HG_WIDTH), rows(D_MODEL),
                  pl.BlockSpec((None, D_MODEL, D_MODEL), lambda i: (layer, 0, 0))],
        out_specs=rows(D_MODEL),
        out_shape=jax.ShapeDtypeStruct((n, D_MODEL), F32),
        compiler_params=pltpu.CompilerParams(dimension_semantics=("arbitrary",), vmem_limit_bytes=VMEM_LIMIT),
        name="out_proj",
    )(mix, x, w_out)


def _split_heads(k):
    first = lax.broadcasted_iota(jnp.int32, k.shape, 1) < HEAD_DIM
    return (jnp.where(first, k, 0.0).astype(BF16), jnp.where(first, 0.0, k).astype(BF16))


def _attend(q, key_sets, sink_of):
    m = q.shape[0]
    first = lax.broadcasted_iota(jnp.int32, (m, KV_WIDTH), 1) < HEAD_DIM
    blocks = []
    for g in range(GQA_GROUP):
        qg = q[:, g * KV_WIDTH:(g + 1) * KV_WIDTH]
        per_head = []
        for h in range(KV_HEADS):
            scores = []
            for k_heads, _, mask in key_sets:
                s = _dot_nt(qg, k_heads[h])
                if mask is not None:
                    s = jnp.where(mask, s, NEG_BIG)
                scores.append(s)
            sink = sink_of(h, g)
            mx = jnp.max(scores[0], axis=-1, keepdims=True)
            for s in scores[1:]:
                mx = jnp.maximum(mx, jnp.max(s, axis=-1, keepdims=True))
            mx = jnp.maximum(mx, sink)
            den = jnp.exp2(sink - mx)
            acc = jnp.zeros((m, KV_WIDTH), F32)
            for s, (_, v, _) in zip(scores, key_sets):
                p = jnp.exp2(s - mx)
                den = den + jnp.sum(p, axis=-1, keepdims=True)
                acc = acc + _dot(p.astype(BF16), v)
            per_head.append(acc * (1.0 / den))
        blocks.append(jnp.where(first, per_head[0], per_head[1]))
    return jnp.concatenate(blocks, axis=-1)


def _attn_stages(i, sink_ref, q_ref, kvc_ref, kvp_ref, kvm_ref, sga_ref, o_ref):
    n_blocks = q_ref.shape[0] // WINDOW
    n_keys = 2 * WINDOW + N_META
    kvm = kvm_ref[...]
    k_blocks = [_split_heads(kvp_ref[:, :KV_WIDTH])]
    vt_blocks = [kvp_ref[:, KV_WIDTH:].T.astype(BF16)]
    for j in range(n_blocks):
        rows = slice(j * WINDOW, (j + 1) * WINDOW)
        k_blocks.append(_split_heads(kvc_ref[rows, :KV_WIDTH]))
        vt_blocks.append(kvc_ref[rows, KV_WIDTH:].T.astype(BF16))
    k_meta = _split_heads(kvm[:, :KV_WIDTH])
    v_meta = jnp.concatenate([kvm[:, KV_WIDTH:], jnp.zeros((WINDOW - N_META, KV_WIDTH), F32)], axis=0)
    vt_meta = v_meta.T.astype(BF16)
    pad = jnp.zeros((3 * WINDOW - n_keys, WINDOW), BF16)

    key = lax.broadcasted_iota(jnp.int32, (n_keys, WINDOW), 0)
    qi = lax.broadcasted_iota(jnp.int32, (n_keys, WINDOW), 1)
    band = jnp.where(key < WINDOW, key - qi - 1, jnp.where(key < 2 * WINDOW, qi - (key - WINDOW), 0))
    visible = band >= 0
    no_prev = jnp.where(i > 0, 0, 2 * WINDOW)
    visible_first = jnp.where(key < WINDOW, band - no_prev, band) >= 0
    head0_rows = lax.broadcasted_iota(jnp.int32, (KV_WIDTH, WINDOW), 0) < HEAD_DIM
    group_lanes = [slice(g * KV_WIDTH, (g + 1) * KV_WIDTH) for g in range(GQA_GROUP)]

    s_both = []
    for j in range(n_blocks):
        rows = slice(j * WINDOW, (j + 1) * WINDOW)
        kk = jnp.concatenate([k_blocks[j][0], k_blocks[j + 1][0], k_meta[0],
                              k_blocks[j][1], k_blocks[j + 1][1], k_meta[1]], axis=0)
        s_both.append([_dot_nt(kk, q_ref[rows, lanes]) for lanes in group_lanes])
    yield
    probs = []
    for j in range(n_blocks):
        vis = visible_first if j == 0 else visible
        for g in range(GQA_GROUP):
            for h in range(KV_HEADS):
                sink = sink_ref[h * GQA_GROUP + g]
                s = jnp.where(vis, s_both[j][g][h * n_keys:(h + 1) * n_keys], NEG_BIG)
                mx = jnp.maximum(jnp.max(s, axis=0, keepdims=True), sink)
                p = jnp.exp2(s - mx)
                den = jnp.sum(p, axis=0, keepdims=True) + jnp.exp2(sink - mx)
                probs.append((jnp.concatenate([p.astype(BF16), pad], axis=0), 1.0 / den))
    outs = []
    for j in range(n_blocks):
        v_t = jnp.concatenate([vt_blocks[j], vt_blocks[j + 1], vt_meta], axis=1)
        for p_pad, inv in probs[j * ATT_HEADS:(j + 1) * ATT_HEADS]:
            outs.append(_dot(v_t, p_pad) * inv)
    yield
    for j in range(n_blocks):
        rows = slice(j * WINDOW, (j + 1) * WINDOW)
        for g, lanes in enumerate(group_lanes):
            pair = outs[j * ATT_HEADS + g * KV_HEADS:j * ATT_HEADS + (g + 1) * KV_HEADS]
            o_t = jnp.where(head0_rows, pair[0], pair[1])
            o_ref[rows, lanes] = (o_t.T * sga_ref[rows, lanes].astype(F32)).astype(o_ref.dtype)


def _level_sizes(c):
    return [c >> (i + 1) for i in range(c.bit_length() - 1)]


def _decay_sum_matrix(c):
    t = np.arange(c)[:, None]
    r = np.arange(c)[None, :]
    mats = [r <= t]
    for bs in _level_sizes(c):
        if bs < SUBLANES:
            a = (t // (2 * bs)) * (2 * bs) + bs - 1
            mats.append(((r > t) & (r <= a)) | ((r > a) & (r <= t)))
    return np.tile(np.concatenate(mats, axis=0).astype(np.float32), (1, _split_terms(c)))


def _split_terms(c):
    return min(3, MXU_DEPTH // c)


def _decay_sums(g, dmat):
    rest = g * LOG2E
    terms = []
    for _ in range(dmat.shape[1] // g.shape[0]):
        terms.append(rest.astype(BF16))
        rest = rest - terms[-1].astype(F32)
    return _dot(dmat, jnp.concatenate(terms, axis=0))


def _chunk_exponents(block, c):
    start = block(0)
    grp = lambda j: start[j * SUBLANES:(j + 1) * SUBLANES, :]
    lasts = {}

    def last(j):
        if j not in lasts:
            lasts[j] = jnp.broadcast_to(start[j * SUBLANES + SUBLANES - 1:(j + 1) * SUBLANES, :], (SUBLANES, HG_KDIM))
        return lasts[j]

    n_grp = c // SUBLANES
    end = jnp.concatenate([last(n_grp - 1) - grp(j) for j in range(n_grp)], axis=0)
    levels, fine = [], 1
    for bs in _level_sizes(c):
        if bs >= SUBLANES:
            per = bs // SUBLANES
            parts = []
            for j in range(n_grp):
                blk = j // per
                anchor = last((blk // 2) * 2 * per + per - 1)
                parts.append(grp(j) - anchor if blk % 2 == 1 else anchor - grp(j))
            levels.append(jnp.concatenate(parts, axis=0))
        else:
            levels.append(block(fine))
            fine += 1
    return start, end, levels


def _hgrn_masks(c):
    row = lax.broadcasted_iota(jnp.int32, (c, c), 0)
    col = lax.broadcasted_iota(jnp.int32, (c, c), 1)
    row_k = lax.broadcasted_iota(jnp.int32, (c, HG_KDIM), 0)
    levels = []
    for bs in _level_sizes(c):
        rb, cb = row // bs, col // bs
        pairs = ((rb % 2) * (1 - jnp.abs(cb - rb + 1))) > 0
        levels.append((bs, (row_k // bs) % 2 == 1, pairs))
    return levels, row == col


def _hgrn_chunks(items, states, masks):
    outs = []
    for _ in _hgrn_chunk_stages(items, states, masks, outs):
        pass
    return outs


def _hgrn_chunk_stages(items, states, masks, outs):
    levels, diag = masks
    c = items[0][1].shape[0]

    stage1 = []
    for _, q, k, v, (_, to_end, level_ex) in items:
        pair_scores = []
        for (bs, q_side, _), lex in zip(levels, level_ex):
            if bs >= SUBLANES:
                side = jnp.concatenate(
                    [(q if (j // bs) % 2 == 1 else k)[j:j + SUBLANES] for j in range(0, c, SUBLANES)], axis=0)
            else:
                side = jnp.where(q_side, q, k)
            u = (side * jnp.exp2(lex)).astype(BF16)
            pair_scores.append(_dot_nt(u, u))
        kdec = (k * jnp.exp2(to_end)).astype(BF16)
        stage1.append((pair_scores, _dot_tn(v, kdec)))
    yield

    intra = []
    for (_, q, k, v, _), (pair_scores, _) in zip(items, stage1):
        a = jnp.where(diag, jnp.sum(q * k, axis=-1, keepdims=True), 0.0)
        for (_, _, pairs), scores in zip(levels, pair_scores):
            a = jnp.where(pairs, scores, a)
        intra.append(_dot(a.astype(BF16), v))
    yield

    for (head, q, _, _, (from_start, _, _)), (_, increment), o_intra in zip(items, stage1, intra):
        st = states[head]
        decay = jnp.exp2(from_start)
        outs.append(o_intra + _dot_nt((q * decay).astype(BF16), st.astype(BF16)))
        states[head] = st * decay[c - 1:c, :] + increment


def _head_norm_gate(o, nw, gate):
    ms = jnp.mean(o * o, axis=-1, keepdims=True)
    return o * lax.rsqrt(ms + EPS) * nw * gate


def _hgrn_stages(hq_ref, hk_ref, hv_ref, g_ref, sgh_ref, nw_ref, dmat_ref, o_ref, st_ref):
    masks = _hgrn_masks(HG_CHUNK)
    dmat = dmat_ref[...]

    def block_of(sums, lanes):
        return lambda i: sums[i * HG_CHUNK:(i + 1) * HG_CHUNK, lanes]

    items, where = [], []
    for ci in range(hq_ref.shape[0] // HG_CHUNK):
        rows = slice(ci * HG_CHUNK, (ci + 1) * HG_CHUNK)
        sums = _decay_sums(g_ref[rows, :], dmat)
        for h in range(HG_HEADS):
            lanes = slice(h * HG_KDIM, (h + 1) * HG_KDIM)
            items.append((h, hq_ref[rows, lanes].astype(F32), hk_ref[rows, lanes].astype(F32),
                          hv_ref[rows, lanes], _chunk_exponents(block_of(sums, lanes), HG_CHUNK)))
            where.append((rows, lanes))
    yield
    states = [st_ref[h] for h in range(HG_HEADS)]
    outs = []
    yield from _hgrn_chunk_stages(items, states, masks, outs)
    nw = nw_ref[...]
    for (rows, lanes), o in zip(where, outs):
        o_ref[rows, lanes] = _head_norm_gate(o, nw, sgh_ref[rows, lanes].astype(F32)).astype(o_ref.dtype)
    for h in range(HG_HEADS):
        st_ref[h] = states[h]


def _mixer_kernel(sink_ref, q_ref, kvc_ref, kvp_ref, kvm_ref, sga_ref, hq_ref, hk_ref, hv_ref, g_ref, sgh_ref,
                  s0_ref, nw_ref, dmat_ref, mix_ref, sfin_ref, st_ref):
    j = pl.program_id(1)

    @pl.when(j == 0)
    def _():
        st_ref[...] = s0_ref[...]

    attention = _attn_stages(j, sink_ref, q_ref, kvc_ref, kvp_ref, kvm_ref, sga_ref, mix_ref.at[:, 0:ATT_WIDTH])
    hgrn = _hgrn_stages(hq_ref, hk_ref, hv_ref, g_ref, sgh_ref, nw_ref, dmat_ref, mix_ref.at[:, ATT_WIDTH:], st_ref)
    for stage in (hgrn, hgrn, attention, hgrn, attention, attention, hgrn):
        next(stage, None)
    for stage in (attention, hgrn):
        for _ in stage:
            pass

    @pl.when(j == pl.num_programs(1) - 1)
    def _():
        for h in range(HG_HEADS):
            sfin_ref[h] = st_ref[h].T


def _prompt_mixers(sinks, qa, hgin, gkv, kv_meta, s0t, hg_nw, batch, seq):
    nt = seq // MIX_TILE
    per = MIX_TILE // WINDOW
    blk = lambda width, col: pl.BlockSpec((MIX_TILE, width), lambda b, j, s: (b * nt + j, col))
    prev = pl.BlockSpec((WINDOW, 2 * KV_WIDTH), lambda b, j, s: ((b * nt + j) * per - jnp.minimum(j, 1), GKV_KV))
    full = lambda shape: pl.BlockSpec(shape, lambda b, j, s: (0,) * len(shape))
    state_shape = (HG_HEADS, HG_VDIM, HG_KDIM)
    dmat = jnp.asarray(_decay_sum_matrix(HG_CHUNK), BF16)
    return pl.pallas_call(
        _mixer_kernel,
        grid_spec=pltpu.PrefetchScalarGridSpec(
            num_scalar_prefetch=1, grid=(batch, nt),
            in_specs=[blk(ATT_WIDTH, QA_Q), blk(2 * KV_WIDTH, GKV_KV), prev, full((N_META, 2 * KV_WIDTH)),
                      blk(ATT_WIDTH, QA_GATE),
                      blk(HG_WIDTH, HI_Q), blk(HG_WIDTH, HI_K), blk(HG_WIDTH, HI_V), blk(HG_WIDTH, GKV_G),
                      blk(HG_WIDTH, HI_GATE), full(state_shape), full((1, HG_VDIM)), full(dmat.shape)],
            out_specs=[blk(ATT_WIDTH + HG_WIDTH, 0),
                       pl.BlockSpec((None,) + state_shape, lambda b, j, s: (b, 0, 0, 0))],
            scratch_shapes=[pltpu.VMEM(state_shape, F32)]),
        out_shape=[jax.ShapeDtypeStruct((batch * seq, ATT_WIDTH + HG_WIDTH), BF16),
                   jax.ShapeDtypeStruct((batch,) + state_shape, F32)],
        compiler_params=pltpu.CompilerParams(dimension_semantics=("arbitrary", "arbitrary"),
                                             vmem_limit_bytes=VMEM_LIMIT),
        name="mixers",
    )(sinks, qa, gkv, gkv, kv_meta, qa, hgin, hgin, hgin, gkv, hgin, s0t, hg_nw, dmat)


def _meta_mix(sink_of, q_ref, kv_ref, sga_ref, hq_ref, hk_ref, hv_ref, g_ref, sgh_ref, nw_ref, dmat_ref,
              att_ref, hg_ref, st_ref):
    kv = kv_ref[...]
    row = lax.broadcasted_iota(jnp.int32, (N_META, N_META), 0)
    col = lax.broadcasted_iota(jnp.int32, (N_META, N_META), 1)
    att = _attend(q_ref[...].astype(BF16),
                  [(_split_heads(kv[:, :KV_WIDTH]), kv[:, KV_WIDTH:].astype(BF16), col <= row)], sink_of)
    att_ref[...] = (att * sga_ref[...]).astype(att_ref.dtype)

    sums = _decay_sums(g_ref[...], dmat_ref[...])
    masks = _hgrn_masks(N_META)
    head_lanes = [slice(h * HG_KDIM, (h + 1) * HG_KDIM) for h in range(HG_HEADS)]
    items = [(h, hq_ref[:, lanes], hk_ref[:, lanes], hv_ref[:, lanes].astype(BF16),
              _chunk_exponents((lambda lanes: lambda i: sums[i * N_META:(i + 1) * N_META, lanes])(lanes), N_META))
             for h, lanes in enumerate(head_lanes)]
    states = [jnp.zeros((HG_VDIM, HG_KDIM), F32) for _ in range(HG_HEADS)]
    outs = _hgrn_chunks(items, states, masks)
    for h, lanes in enumerate(head_lanes):
        st_ref[h] = states[h]
        hg_ref[:, lanes] = _head_norm_gate(outs[h], nw_ref[...], sgh_ref[:, lanes]).astype(hg_ref.dtype)


def _sample_mix(q_ref, kv_ref, sga_ref, hq_ref, hk_ref, hv_ref, g_ref, sgh_ref, nw_ref,
                sink_ref, seg_ref, exp_ref, ck_ref, cv_ref, mk_ref, mv_ref, s_ref,
                att_ref, hg_ref, nk_ref, nv_ref, ns_ref):
    n_keys = WINDOW + N_META + SUBLANES
    key_row = lax.broadcasted_iota(jnp.int32, (n_keys, KV_WIDTH), 0)
    visible = jnp.logical_and(key_row >= 1, key_row <= WINDOW + N_META)
    win_row = lax.broadcasted_iota(jnp.int32, (WINDOW, KV_WIDTH), 0)
    sink = sink_ref[...]
    seg = seg_ref[...]
    expand = exp_ref[...]
    nw = nw_ref[...]

    group = 8

    def per_group(i, carry):
        seqs = [i * group + j for j in range(group)]
        new_rows, scores, values = [], [], []
        for b in seqs:
            kv_new = kv_ref[b]
            k_new = kv_new[:, 0:KV_WIDTH]
            v_new = kv_new[:, KV_WIDTH:2 * KV_WIDTH]
            new_rows.append((k_new, v_new))
            ck = ck_ref[b]
            cv = cv_ref[b]
            keys = jnp.concatenate([ck, mk_ref[b], jnp.broadcast_to(k_new, (SUBLANES, KV_WIDTH))], axis=0)
            values.append(jnp.concatenate([cv, mv_ref[b], jnp.broadcast_to(v_new, (SUBLANES, KV_WIDTH))], axis=0))
            prod = jnp.concatenate([keys] * GQA_GROUP, axis=1) * q_ref[b]
            scores.append(_dot(prod.astype(BF16), seg))
            nk_ref[b] = jnp.where(win_row == WINDOW - 1, k_new, pltpu.roll(ck, WINDOW - 1, 0))
            nv_ref[b] = jnp.where(win_row == WINDOW - 1, v_new, pltpu.roll(cv, WINDOW - 1, 0))

        wides = []
        for s in scores:
            s = jnp.where(visible, s, NEG_BIG)
            mx = jnp.maximum(jnp.max(s, axis=0, keepdims=True), sink)
            p = jnp.exp2(s - mx)
            den = jnp.sum(p, axis=0, keepdims=True) + jnp.exp2(sink - mx)
            p = p * (1.0 / den)
            wides.append(_dot(p.astype(BF16), expand))

        for b, wide, vals in zip(seqs, wides, values):
            att = jnp.sum(wide * jnp.concatenate([vals] * GQA_GROUP, axis=1), axis=0, keepdims=True)
            att_ref[b] = att * sga_ref[b]

            decay_row, hq_row, hk_row, hv_row, sgh_row = jnp.exp(g_ref[b]), hq_ref[b], hk_ref[b], hv_ref[b], sgh_ref[b]
            outs = []
            for h in range(HG_HEADS):
                lanes = slice(h * HG_KDIM, (h + 1) * HG_KDIM)
                col = lambda r: jnp.broadcast_to(r[:, lanes], (HG_KDIM, HG_KDIM)).T
                s1 = col(decay_row) * s_ref[b, h] + col(hk_row) * hv_row[:, lanes]
                ns_ref[b, h] = s1
                q_rows = jnp.broadcast_to(hq_row[:, lanes], (2 * SUBLANES, HG_KDIM)).astype(BF16)
                o = _dot(q_rows, s1.astype(BF16))[0:1, :]
                outs.append(_head_norm_gate(o, nw, sgh_row[:, lanes]))
            hg_ref[b] = jnp.concatenate(outs, axis=-1)
        return carry

    lax.fori_loop(0, q_ref.shape[0] // group, per_group, 0)


def _small_path_kernel(sinks_ref, x0_ref, nw_ref, w_ref, qnw_ref, knw_ref, lbl_ref, bd_ref, wo_ref, hgnw_ref,
                       sinkrow_ref, seg_ref, exp_ref, dmat_ref, ck_ref, cv_ref, mk_ref, mv_ref, s_ref,
                       y_ref, kvm_ref, s0t_ref, nk_ref, nv_ref, ns_ref,
                       x_scr, att_scr, hg_scr, *scr):
    proj_scr, staged, staged_att, staged_hg = scr[:8], scr[8:16], scr[16], scr[17]
    layer, tile = pl.program_id(0), pl.program_id(1)
    n_rows = x_scr.shape[0]
    n_seq = n_rows - N_META
    t = ck_ref.shape[0]

    @pl.when(jnp.logical_and(layer == 0, tile == 0))
    def _():
        x_scr[...] = x0_ref[...]

    @pl.when(tile == 0)
    def _():
        _proj_body(layer, [slice(0, n_rows)], [x_scr[...]], nw_ref, w_ref, qnw_ref, knw_ref, lbl_ref, bd_ref,
                   *proj_scr)
        meta = lambda ref: ref.at[n_seq:n_rows]
        _meta_mix(lambda h, g: sinks_ref[layer, h * GQA_GROUP + g], *[meta(r) for r in proj_scr], hgnw_ref,
                  dmat_ref, meta(att_scr), meta(hg_scr), s0t_ref)
        kvm_ref[...] = proj_scr[1][n_seq:n_rows, :]

    rows = pl.ds(pl.multiple_of(tile * t, t), t)
    for src, dst in zip(proj_scr, staged):
        block = src[rows, :]
        for b in range(t):
            dst[b] = block[b:b + 1, :]
    _sample_mix(*staged, hgnw_ref, sinkrow_ref, seg_ref, exp_ref, ck_ref, cv_ref, mk_ref, mv_ref, s_ref,
                staged_att, staged_hg, nk_ref, nv_ref, ns_ref)
    att_scr[rows, :] = jnp.concatenate([staged_att[b] for b in range(t)], axis=0)
    hg_scr[rows, :] = jnp.concatenate([staged_hg[b] for b in range(t)], axis=0)

    @pl.when(tile == pl.num_programs(1) - 1)
    def _():
        x = (x_scr[...] + _dot(att_scr[...].astype(BF16), wo_ref[0:ATT_WIDTH, :])
             + _dot(hg_scr[...].astype(BF16), wo_ref[ATT_WIDTH:, :]))
        x_scr[...] = x
        y_ref[...] = x


def _small_path(sinks, x0, norm_w, w_in, qnw, knw, lb_logits, bd, w_out, hg_nw, sink_rows, seg, expand,
                cache_k, cache_v, meta_k, meta_v, state):
    n_rows = x0.shape[0]
    n_seq, depth = state.shape[:2]
    t = SAMPLE_TILE
    dmat = jnp.asarray(_decay_sum_matrix(N_META), BF16)
    full = lambda a: pl.BlockSpec(a.shape, lambda l, i, s: (0,) * a.ndim)
    per_layer = lambda a: pl.BlockSpec((None,) + a.shape[1:], lambda l, i, s: (l,) + (0,) * (a.ndim - 1))
    cache = lambda a: pl.BlockSpec((t, None) + a.shape[2:], lambda l, i, s: (i, l) + (0,) * (a.ndim - 2))
    as_rows = lambda a: a[:, None, :]
    state_t = (HG_HEADS, HG_VDIM, HG_KDIM)
    widths = (ATT_WIDTH, 2 * KV_WIDTH, ATT_WIDTH, HG_WIDTH, HG_WIDTH, HG_WIDTH, HG_WIDTH, HG_WIDTH)
    operands = [x0, as_rows(norm_w), w_in, as_rows(qnw), as_rows(knw), lb_logits, bd, w_out, as_rows(hg_nw),
                as_rows(sink_rows), seg, expand, dmat, cache_k, cache_v, meta_k, meta_v, state]
    in_specs = [full(x0), per_layer(operands[1]), per_layer(w_in), per_layer(operands[3]), per_layer(operands[4]),
                full(lb_logits), full(bd), per_layer(w_out), per_layer(operands[8]), per_layer(operands[9]),
                full(seg), full(expand), full(dmat), cache(cache_k), cache(cache_v), cache(meta_k), cache(meta_v),
                cache(state)]
    out_shape = [jax.ShapeDtypeStruct((n_rows, D_MODEL), F32),
                 jax.ShapeDtypeStruct((depth, N_META, 2 * KV_WIDTH), F32),
                 jax.ShapeDtypeStruct((depth,) + state_t, F32),
                 jax.ShapeDtypeStruct(cache_k.shape, F32), jax.ShapeDtypeStruct(cache_v.shape, F32),
                 jax.ShapeDtypeStruct(state.shape, F32)]
    out_specs = [pl.BlockSpec((n_rows, D_MODEL), lambda l, i, s: (0, 0)),
                 pl.BlockSpec((None, N_META, 2 * KV_WIDTH), lambda l, i, s: (l, 0, 0)),
                 pl.BlockSpec((None,) + state_t, lambda l, i, s: (l, 0, 0, 0)),
                 cache(cache_k), cache(cache_v), cache(state)]
    scratch = ([pltpu.VMEM((n_rows, D_MODEL), F32), pltpu.VMEM((n_rows, ATT_WIDTH), F32),
                pltpu.VMEM((n_rows, HG_WIDTH), F32)]
               + [pltpu.VMEM((n_rows, w), F32) for w in widths]
               + [pltpu.VMEM((t, 1, w), F32) for w in widths]
               + [pltpu.VMEM((t, 1, ATT_WIDTH), F32), pltpu.VMEM((t, 1, HG_WIDTH), F32)])
    return pl.pallas_call(
        _small_path_kernel,
        grid_spec=pltpu.PrefetchScalarGridSpec(
            num_scalar_prefetch=1, grid=(depth, n_seq // t),
            in_specs=in_specs, out_specs=out_specs, scratch_shapes=scratch),
        out_shape=out_shape,
        compiler_params=pltpu.CompilerParams(dimension_semantics=("arbitrary", "arbitrary"),
                                             vmem_limit_bytes=VMEM_LIMIT),
        name="small_path",
    )(sinks, *operands)


def _g_major(a, axis):
    shape = a.shape
    a = a.reshape(shape[:axis] + (KV_HEADS, GQA_GROUP, HEAD_DIM) + shape[axis + 1:])
    a = jnp.swapaxes(a, axis, axis + 1)
    return a.reshape(shape)


def _constants():
    lane = np.arange(ATT_WIDTH)
    g_of, h_of = lane // KV_WIDTH, (lane % KV_WIDTH) // HEAD_DIM
    head = h_of * GQA_GROUP + g_of
    seg = (head[:, None] == np.arange(KV_WIDTH)[None, :]).astype(np.float32)
    grp = np.arange(256) // HEAD_DIM
    bd = (grp[:, None] == grp[None, :]).astype(np.float32) / HEAD_DIM
    return jnp.asarray(seg, BF16), jnp.asarray(seg.T, BF16), jnp.asarray(bd, BF16)


def kernel(x_prompt, x_sample, cache_win_k, cache_win_v, cache_meta_k, cache_meta_v, state_hgrn, meta_tokens,
           norm_w, w_in, q_norm_w, k_norm_w, attn_sinks, hg_lb_logits, hg_norm_w, w_out):
    batch, seq, _ = x_prompt.shape
    n_seq = x_sample.shape[0]
    depth = w_in.shape[0]
    w_buf = cache_win_k.shape[2]
    assert x_sample.shape[1] == 1 and w_buf == WINDOW and n_seq % SAMPLE_TILE == 0
    assert seq % ROW_TILE == 0 and seq % MIX_TILE == 0

    seg, expand, bd = _constants()
    w_in_b = w_in.astype(BF16)
    w_out_b = jnp.concatenate([_g_major(w_out[:, :ATT_WIDTH], 1), w_out[:, ATT_WIDTH:]], axis=1).astype(BF16)
    qnw = jnp.tile(q_norm_w, (1, ATT_HEADS)) * (HEAD_DIM ** -0.5 * LOG2E)
    knw = jnp.tile(k_norm_w, (1, KV_HEADS))
    lb_logits = hg_lb_logits.astype(F32)
    sinks = attn_sinks.astype(F32) * LOG2E
    sink_rows = jnp.pad(sinks, ((0, 0), (0, KV_WIDTH - ATT_HEADS)))

    ck = cache_win_k.reshape(n_seq, depth, w_buf, KV_WIDTH)
    cv = cache_win_v.reshape(n_seq, depth, w_buf, KV_WIDTH)
    mk = cache_meta_k.reshape(n_seq, depth, N_META, KV_WIDTH)
    mv = cache_meta_v.reshape(n_seq, depth, N_META, KV_WIDTH)

    x_small = jnp.concatenate([x_sample.reshape(n_seq, D_MODEL), meta_tokens.astype(F32)], axis=0)
    y_small, kv_meta, s0t, new_k, new_v, new_state = _small_path(
        sinks, x_small, norm_w, w_in_b, qnw, knw, lb_logits, bd, w_out_b, hg_norm_w, sink_rows,
        seg, expand, ck, cv, mk, mv, state_hgrn)

    xp = x_prompt.reshape(batch * seq, D_MODEL)
    outs = {k: [] for k in ("wkp", "wvp", "hsp")}
    mix = None
    for l in range(depth):
        pr = _project(l, xp, norm_w[l][None], w_in_b, qnw[l][None], knw[l][None], lb_logits, bd,
                      row_tile=ROW_TILE, mix=mix)
        if mix is not None:
            xp, pr = pr[0], pr[1:]
        qa, hgin, gkv = pr
        mixed, s_fin = _prompt_mixers(sinks[l], qa, hgin, gkv, kv_meta[l], s0t[l], hg_norm_w[l][None], batch, seq)
        mix = (mixed, w_out_b)

        kv_last = gkv.reshape(batch, seq, -1)[:, seq - w_buf:, HG_WIDTH:]
        outs["wkp"].append(kv_last[:, :, :KV_WIDTH])
        outs["wvp"].append(kv_last[:, :, KV_WIDTH:])
        outs["hsp"].append(s_fin)
    xp = _out_project(depth - 1, mix[0], xp, mix[1], MIX_TILE)

    stack = lambda name: jnp.stack(outs[name], axis=1)
    heads = lambda a: a.reshape(a.shape[:-1] + (KV_HEADS, HEAD_DIM))
    meta_rows = lambda a: jnp.broadcast_to(a[None], (batch,) + a.shape)
    return (xp.reshape(batch, seq, D_MODEL), y_small[:n_seq].reshape(n_seq, 1, D_MODEL),
            heads(stack("wkp")), heads(stack("wvp")),
            heads(meta_rows(kv_meta[:, :, :KV_WIDTH])), heads(meta_rows(kv_meta[:, :, KV_WIDTH:])), stack("hsp"),
            heads(new_k), heads(new_v), new_state)
```

```python
import functools

import numpy as np
import jax
import jax.numpy as jnp
from jax import lax
from jax.experimental import pallas as pl
from jax.experimental.pallas import tpu as pltpu

F32 = jnp.float32
BF16 = jnp.bfloat16

D_MODEL = 1024
N_META = 16
WINDOW = 128
HEAD_DIM = 64
ATT_WIDTH = 512
ATT_HEADS = 8
KV_HEADS = 2
GQA_GROUP = 4
KV_WIDTH = KV_HEADS * HEAD_DIM
HG_WIDTH = 512
HG_HEADS = 4
HG_KDIM = 128
HG_VDIM = 128
PROJ_WIDTH = 3328
EPS = 1e-6
NEG_BIG = -1e30
TINY = 1e-30
LOG2E = 1.4426950408889634

C_Q, C_K, C_V, C_GA, C_QH, C_FH, C_IH, C_GH = 0, 512, 640, 768, 1280, 1792, 2304, 2816

SUBLANES = 8
MXU_DEPTH = 256
HG_CHUNK = 64
ROW_TILE = 1024
MIX_TILE = 1024
SAMPLE_TILE = 16
VMEM_LIMIT = 48 * 1024 * 1024
PROJ_VMEM_LIMIT = 56 * 1024 * 1024


def _dot(a, b):
    return jnp.dot(a, b, preferred_element_type=F32)


def _dot_nt(a, b):
    return lax.dot_general(a, b, (((1,), (1,)), ((), ())), preferred_element_type=F32)


def _dot_tn(a, b):
    return lax.dot_general(a, b, (((0,), (0,)), ((), ())), preferred_element_type=F32)


def _silu(x):
    return x * (1.0 / (1.0 + jnp.exp(-x)))


def _group_major(x):
    first = lax.broadcasted_iota(jnp.int32, (x.shape[0], KV_WIDTH), 1) < HEAD_DIM
    blocks = [x[:, j * KV_WIDTH:(j + 1) * KV_WIDTH] for j in range(GQA_GROUP)]
    swapped = [pltpu.roll(b, HEAD_DIM, 1) for b in blocks]
    out = []
    for g in range(GQA_GROUP):
        a, b = g // 2, GQA_GROUP // 2 + g // 2
        out.append(jnp.where(first, blocks[a], swapped[b]) if g % 2 == 0 else jnp.where(first, swapped[a], blocks[b]))
    return jnp.concatenate(out, axis=-1)


def _row_parts(n_rows, n_parts):
    step = n_rows // n_parts
    return [slice(i * step, (i + 1) * step) for i in range(n_parts)]


def _mix_proj_kernel(layer, n_parts, mix_ref, wo_ref, x_ref, *rest):
    parts = _row_parts(x_ref.shape[0], n_parts)
    xnew_ref = rest[6]
    xs = []
    for rs in parts:
        x = x_ref[rs, :] + _dot(mix_ref[rs, :], wo_ref[...])
        xnew_ref[rs, :] = x
        xs.append(x)
    _proj_body(layer, parts, xs, *rest[:6], *_projection_views(*rest[7:]))


def _proj_kernel(layer, n_parts, x_ref, *rest):
    parts = _row_parts(x_ref.shape[0], n_parts)
    _proj_body(layer, parts, [x_ref[rs, :] for rs in parts], *rest[:6], *_projection_views(*rest[6:]))


QA_Q, QA_GATE = 0, 1
HI_Q, HI_K, HI_V, HI_GATE = 0, 1, 2, 3
GKV_G = 0
GKV_KV = HG_WIDTH // (2 * KV_WIDTH)


def _projection_views(qa_ref, hgin_ref, gkv_ref):
    col = lambda ref, j, width: ref.at[:, j * width:(j + 1) * width]
    return (col(qa_ref, QA_Q, ATT_WIDTH), col(gkv_ref, GKV_KV, 2 * KV_WIDTH), col(qa_ref, QA_GATE, ATT_WIDTH),
            col(hgin_ref, HI_Q, HG_WIDTH), col(hgin_ref, HI_K, HG_WIDTH), col(hgin_ref, HI_V, HG_WIDTH),
            col(gkv_ref, GKV_G, HG_WIDTH), col(hgin_ref, HI_GATE, HG_WIDTH))


def _proj_body(layer, parts, xs, nw_ref, w_ref, qnw_ref, knw_ref, lbl_ref, bd_ref,
               q_ref, kv_ref, sga_ref, hq_ref, hk_ref, hv_ref, g_ref, sgh_ref):
    nw = nw_ref[...]
    hs = []
    for x in xs:
        ms = jnp.mean(x * x, axis=-1, keepdims=True)
        hs.append((x * lax.rsqrt(ms + EPS) * nw).astype(BF16))

    def proj(h, lo, hi):
        return _dot(h, w_ref[:, lo:hi])

    pq = [[proj(h, C_Q + 256 * c, C_Q + 256 * (c + 1)) for c in range(2)] for h in hs]
    pk = [proj(h, C_K, C_V) for h in hs]
    sq = [[(p * p).astype(BF16) for p in pqs + [pks]] for pqs, pks in zip(pq, pk)]

    logits = lbl_ref[...]
    e = jnp.exp(logits - jnp.max(logits, axis=0, keepdims=True))
    p = e / jnp.sum(e, axis=0, keepdims=True)
    depth_row = lax.broadcasted_iota(jnp.int32, logits.shape, 0)
    in_range = jnp.where(depth_row >= 1, layer - depth_row, -1) >= 0
    lb = jnp.sum(jnp.where(in_range, p, 0.0), axis=0, keepdims=True)

    for rs, h in zip(parts, hs):
        kv_ref[rs, KV_WIDTH:2 * KV_WIDTH] = proj(h, C_V, C_GA)
        sga_ref[rs, :] = _group_major(_silu(proj(h, C_GA, C_QH))).astype(sga_ref.dtype)
        hq_ref[rs, :] = _silu(proj(h, C_QH, C_FH)).astype(hq_ref.dtype)

        z = proj(h, C_FH, C_IH)
        ez = jnp.exp(-jnp.abs(z))
        r = 1.0 / (1.0 + ez)
        pos = z >= 0.0
        sig_pos = jnp.where(pos, r, ez * r)
        sig_neg = jnp.where(pos, ez * r, r)
        hk_ref[rs, :] = ((1.0 - lb) * sig_neg).astype(hk_ref.dtype)
        f = lb + (1.0 - lb) * sig_pos
        g_ref[rs, :] = jnp.log(jnp.maximum(f, TINY))

        hv_ref[rs, :] = proj(h, C_IH, C_GH).astype(hv_ref.dtype)
        sgh_ref[rs, :] = _silu(proj(h, C_GH, PROJ_WIDTH)).astype(sgh_ref.dtype)

    bd = bd_ref[...]
    for rs, pqs, pks, sqs in zip(parts, pq, pk, sq):
        q = jnp.concatenate([p * lax.rsqrt(_dot(s, bd) + EPS) for p, s in zip(pqs, sqs[:2])], axis=-1)
        q_ref[rs, :] = _group_major(q * qnw_ref[...]).astype(q_ref.dtype)
        kv_ref[rs, 0:KV_WIDTH] = pks * lax.rsqrt(_dot(sqs[2], bd[:KV_WIDTH, :KV_WIDTH]) + EPS) * knw_ref[...]


def _project(layer, x, norm_w, w_in, qnw, knw, lb_logits, bd, row_tile, mix=None):
    n = x.shape[0]
    rows = lambda width: pl.BlockSpec((row_tile, width), lambda i: (i, 0))
    full = lambda shape: pl.BlockSpec(shape, lambda i: (0,) * len(shape))
    slab = lambda a, l: pl.BlockSpec((None,) + a.shape[1:], lambda i: (l, 0, 0), pipeline_mode=pl.Buffered(1))
    out = lambda width, dtype: jax.ShapeDtypeStruct((n, width), dtype)
    operands = [x, norm_w, w_in, qnw, knw, lb_logits, bd]
    in_specs = [rows(D_MODEL), full((1, D_MODEL)), slab(w_in, layer), full((1, ATT_WIDTH)),
                full((1, KV_WIDTH)), full(lb_logits.shape), full((256, 256))]
    widths = (2 * ATT_WIDTH, 4 * HG_WIDTH, HG_WIDTH + 2 * KV_WIDTH)
    out_specs = [rows(w) for w in widths]
    out_shape = [out(widths[0], BF16), out(widths[1], BF16), out(widths[2], F32)]
    body = _proj_kernel
    if mix is not None:
        operands = list(mix) + operands
        in_specs = [rows(ATT_WIDTH + HG_WIDTH), slab(mix[1], layer - 1)] + in_specs
        out_specs = [rows(D_MODEL)] + out_specs
        out_shape = [out(D_MODEL, F32)] + out_shape
        body = _mix_proj_kernel
    n_parts = row_tile // 128 if row_tile % 128 == 0 else 1
    return pl.pallas_call(
        functools.partial(body, layer, n_parts),
        grid=(n // row_tile,),
        in_specs=in_specs, out_specs=out_specs, out_shape=out_shape,
        compiler_params=pltpu.CompilerParams(dimension_semantics=("arbitrary",), vmem_limit_bytes=PROJ_VMEM_LIMIT),
        name="proj",
    )(*operands)


def _out_kernel(mix_ref, x_ref, w_ref, y_ref):
    y_ref[...] = x_ref[...] + _dot(mix_ref[...], w_ref[...])


def _out_project(layer, mix, x, w_out, row_tile):
    n = x.shape[0]
    rows = lambda width: pl.BlockSpec((row_tile, width), lambda i: (i, 0))
    return pl.pallas_call(
        _out_kernel,
        grid=(n // row_tile,),
        in_specs=[rows(ATT_WIDTH + HG_WIDTH), rows(D_MODEL),
                  pl.BlockSpec((None, D_MODEL, D_MODEL), lambda i: (layer, 0, 0), pipeline_mode=pl.Buffered(1))],
        out_specs=rows(D_MODEL),
        out_shape=jax.ShapeDtypeStruct((n, D_MODEL), F32),
        compiler_params=pltpu.CompilerParams(dimension_semantics=("arbitrary",), vmem_limit_bytes=PROJ_VMEM_LIMIT),
        name="out_proj",
    )(mix, x, w_out)


def _split_heads(k):
    first = lax.broadcasted_iota(jnp.int32, k.shape, 1) < HEAD_DIM
    return (jnp.where(first, k, 0.0).astype(BF16), jnp.where(first, 0.0, k).astype(BF16))


def _attend(q, key_sets, sink_of):
    m = q.shape[0]
    first = lax.broadcasted_iota(jnp.int32, (m, KV_WIDTH), 1) < HEAD_DIM
    blocks = []
    for g in range(GQA_GROUP):
        qg = q[:, g * KV_WIDTH:(g + 1) * KV_WIDTH]
        per_head = []
        for h in range(KV_HEADS):
            scores = []
            for k_heads, _, mask in key_sets:
                s = _dot_nt(qg, k_heads[h])
                if mask is not None:
                    s = jnp.where(mask, s, NEG_BIG)
                scores.append(s)
            sink = sink_of(h, g)
            mx = jnp.max(scores[0], axis=-1, keepdims=True)
            for s in scores[1:]:
                mx = jnp.maximum(mx, jnp.max(s, axis=-1, keepdims=True))
            mx = jnp.maximum(mx, sink)
            den = jnp.exp2(sink - mx)
            acc = jnp.zeros((m, KV_WIDTH), F32)
            for s, (_, v, _) in zip(scores, key_sets):
                p = jnp.exp2(s - mx)
                den = den + jnp.sum(p, axis=-1, keepdims=True)
                acc = acc + _dot(p.astype(BF16), v)
            per_head.append(acc * (1.0 / den))
        blocks.append(jnp.where(first, per_head[0], per_head[1]))
    return jnp.concatenate(blocks, axis=-1)


def _attn_stages(i, sink_ref, q_ref, kvc_ref, kvp_ref, kvm_ref, sga_ref, o_ref):
    n_blocks = q_ref.shape[0] // WINDOW
    n_keys = 2 * WINDOW + N_META
    kvm = kvm_ref[...]
    k_blocks = [_split_heads(kvp_ref[:, :KV_WIDTH])]
    vt_blocks = [kvp_ref[:, KV_WIDTH:].T.astype(BF16)]
    for j in range(n_blocks):
        rows = slice(j * WINDOW, (j + 1) * WINDOW)
        k_blocks.append(_split_heads(kvc_ref[rows, :KV_WIDTH]))
        vt_blocks.append(kvc_ref[rows, KV_WIDTH:].T.astype(BF16))
    k_meta = _split_heads(kvm[:, :KV_WIDTH])
    v_meta = jnp.concatenate([kvm[:, KV_WIDTH:], jnp.zeros((WINDOW - N_META, KV_WIDTH), F32)], axis=0)
    vt_meta = v_meta.T.astype(BF16)
    pad = jnp.zeros((3 * WINDOW - n_keys, WINDOW), BF16)

    key = lax.broadcasted_iota(jnp.int32, (n_keys, WINDOW), 0)
    qi = lax.broadcasted_iota(jnp.int32, (n_keys, WINDOW), 1)
    band = jnp.where(key < WINDOW, key - qi - 1, jnp.where(key < 2 * WINDOW, qi - (key - WINDOW), 0))
    visible = band >= 0
    no_prev = jnp.where(i > 0, 0, 2 * WINDOW)
    visible_first = jnp.where(key < WINDOW, band - no_prev, band) >= 0
    head0_rows = lax.broadcasted_iota(jnp.int32, (KV_WIDTH, WINDOW), 0) < HEAD_DIM
    group_lanes = [slice(g * KV_WIDTH, (g + 1) * KV_WIDTH) for g in range(GQA_GROUP)]

    s_both = []
    for j in range(n_blocks):
        rows = slice(j * WINDOW, (j + 1) * WINDOW)
        kk = jnp.concatenate([k_blocks[j][0], k_blocks[j + 1][0], k_meta[0],
                              k_blocks[j][1], k_blocks[j + 1][1], k_meta[1]], axis=0)
        s_both.append([_dot_nt(kk, q_ref[rows, lanes]) for lanes in group_lanes])
    yield
    probs = []
    for j in range(n_blocks):
        vis = visible_first if j == 0 else visible
        for g in range(GQA_GROUP):
            for h in range(KV_HEADS):
                sink = sink_ref[h * GQA_GROUP + g]
                s = jnp.where(vis, s_both[j][g][h * n_keys:(h + 1) * n_keys], NEG_BIG)
                mx = jnp.maximum(jnp.max(s, axis=0, keepdims=True), sink)
                p = jnp.exp2(s - mx)
                den = jnp.sum(p, axis=0, keepdims=True) + jnp.exp2(sink - mx)
                probs.append((jnp.concatenate([p.astype(BF16), pad], axis=0), 1.0 / den))
    outs = []
    for j in range(n_blocks):
        v_t = jnp.concatenate([vt_blocks[j], vt_blocks[j + 1], vt_meta], axis=1)
        for p_pad, inv in probs[j * ATT_HEADS:(j + 1) * ATT_HEADS]:
            outs.append(_dot(v_t, p_pad) * inv)
    yield
    for j in range(n_blocks):
        rows = slice(j * WINDOW, (j + 1) * WINDOW)
        for g, lanes in enumerate(group_lanes):
            pair = outs[j * ATT_HEADS + g * KV_HEADS:j * ATT_HEADS + (g + 1) * KV_HEADS]
            o_t = jnp.where(head0_rows, pair[0], pair[1])
            o_ref[rows, lanes] = (o_t.T * sga_ref[rows, lanes].astype(F32)).astype(o_ref.dtype)


def _level_sizes(c):
    return [c >> (i + 1) for i in range(c.bit_length() - 1)]


def _decay_sum_matrix(c):
    t = np.arange(c)[:, None]
    r = np.arange(c)[None, :]
    mats = [r <= t]
    for bs in _level_sizes(c):
        if bs < SUBLANES:
            a = (t // (2 * bs)) * (2 * bs) + bs - 1
            mats.append(((r > t) & (r <= a)) | ((r > a) & (r <= t)))
    return np.tile(np.concatenate(mats, axis=0).astype(np.float32), (1, _split_terms(c)))


def _split_terms(c):
    return min(3, MXU_DEPTH // c)


def _decay_sums(g, dmat):
    rest = g * LOG2E
    terms = []
    for _ in range(dmat.shape[1] // g.shape[0]):
        terms.append(rest.astype(BF16))
        rest = rest - terms[-1].astype(F32)
    return _dot(dmat, jnp.concatenate(terms, axis=0))


def _chunk_exponents(block, c):
    start = block(0)
    grp = lambda j: start[j * SUBLANES:(j + 1) * SUBLANES, :]
    lasts = {}

    def last(j):
        if j not in lasts:
            lasts[j] = jnp.broadcast_to(start[j * SUBLANES + SUBLANES - 1:(j + 1) * SUBLANES, :], (SUBLANES, HG_KDIM))
        return lasts[j]

    n_grp = c // SUBLANES
    end = jnp.concatenate([last(n_grp - 1) - grp(j) for j in range(n_grp)], axis=0)
    levels, fine = [], 1
    for bs in _level_sizes(c):
        if bs >= SUBLANES:
            per = bs // SUBLANES
            parts = []
            for j in range(n_grp):
                blk = j // per
                anchor = last((blk // 2) * 2 * per + per - 1)
                parts.append(grp(j) - anchor if blk % 2 == 1 else anchor - grp(j))
            levels.append(jnp.concatenate(parts, axis=0))
        else:
            levels.append(block(fine))
            fine += 1
    return start, end, levels


def _hgrn_masks(c):
    row = lax.broadcasted_iota(jnp.int32, (c, c), 0)
    col = lax.broadcasted_iota(jnp.int32, (c, c), 1)
    row_k = lax.broadcasted_iota(jnp.int32, (c, HG_KDIM), 0)
    levels = []
    for bs in _level_sizes(c):
        rb, cb = row // bs, col // bs
        pairs = ((rb % 2) * (1 - jnp.abs(cb - rb + 1))) > 0
        levels.append((bs, (row_k // bs) % 2 == 1, pairs))
    return levels, row == col


def _hgrn_chunks(items, states, masks):
    outs = []
    for _ in _hgrn_chunk_stages(items, states, masks, outs):
        pass
    return outs


def _hgrn_chunk_stages(items, states, masks, outs):
    levels, diag = masks
    c = items[0][1].shape[0]

    stage1 = []
    for _, q, k, v, (_, to_end, level_ex) in items:
        pair_scores = []
        for (bs, q_side, _), lex in zip(levels, level_ex):
            if bs >= SUBLANES:
                side = jnp.concatenate(
                    [(q if (j // bs) % 2 == 1 else k)[j:j + SUBLANES] for j in range(0, c, SUBLANES)], axis=0)
            else:
                side = jnp.where(q_side, q, k)
            u = (side * jnp.exp2(lex)).astype(BF16)
            pair_scores.append(_dot_nt(u, u))
        kdec = (k * jnp.exp2(to_end)).astype(BF16)
        stage1.append((pair_scores, _dot_tn(v, kdec)))
    yield

    intra = []
    for (_, q, k, v, _), (pair_scores, _) in zip(items, stage1):
        a = jnp.where(diag, jnp.sum(q * k, axis=-1, keepdims=True), 0.0)
        for (_, _, pairs), scores in zip(levels, pair_scores):
            a = jnp.where(pairs, scores, a)
        intra.append(_dot(a.astype(BF16), v))
    yield

    for (head, q, _, _, (from_start, _, _)), (_, increment), o_intra in zip(items, stage1, intra):
        st = states[head]
        decay = jnp.exp2(from_start)
        outs.append(o_intra + _dot_nt((q * decay).astype(BF16), st.astype(BF16)))
        states[head] = st * decay[c - 1:c, :] + increment


def _head_norm_gate(o, nw, gate):
    ms = jnp.mean(o * o, axis=-1, keepdims=True)
    return o * lax.rsqrt(ms + EPS) * nw * gate


def _hgrn_stages(hq_ref, hk_ref, hv_ref, g_ref, sgh_ref, nw_ref, dmat_ref, o_ref, st_ref):
    masks = _hgrn_masks(HG_CHUNK)
    dmat = dmat_ref[...]

    def block_of(sums, lanes):
        return lambda i: sums[i * HG_CHUNK:(i + 1) * HG_CHUNK, lanes]

    items, where = [], []
    for ci in range(hq_ref.shape[0] // HG_CHUNK):
        rows = slice(ci * HG_CHUNK, (ci + 1) * HG_CHUNK)
        sums = _decay_sums(g_ref[rows, :], dmat)
        for h in range(HG_HEADS):
            lanes = slice(h * HG_KDIM, (h + 1) * HG_KDIM)
            items.append((h, hq_ref[rows, lanes].astype(F32), hk_ref[rows, lanes].astype(F32),
                          hv_ref[rows, lanes], _chunk_exponents(block_of(sums, lanes), HG_CHUNK)))
            where.append((rows, lanes))
    yield
    states = [st_ref[h] for h in range(HG_HEADS)]
    outs = []
    yield from _hgrn_chunk_stages(items, states, masks, outs)
    nw = nw_ref[...]
    for (rows, lanes), o in zip(where, outs):
        o_ref[rows, lanes] = _head_norm_gate(o, nw, sgh_ref[rows, lanes].astype(F32)).astype(o_ref.dtype)
    for h in range(HG_HEADS):
        st_ref[h] = states[h]


def _mixer_kernel(sink_ref, q_ref, kvc_ref, kvp_ref, kvm_ref, sga_ref, hq_ref, hk_ref, hv_ref, g_ref, sgh_ref,
                  s0_ref, nw_ref, dmat_ref, mix_ref, sfin_ref, st_ref):
    j = pl.program_id(1)

    @pl.when(j == 0)
    def _():
        st_ref[...] = s0_ref[...]

    attention = _attn_stages(j, sink_ref, q_ref, kvc_ref, kvp_ref, kvm_ref, sga_ref, mix_ref.at[:, 0:ATT_WIDTH])
    hgrn = _hgrn_stages(hq_ref, hk_ref, hv_ref, g_ref, sgh_ref, nw_ref, dmat_ref, mix_ref.at[:, ATT_WIDTH:], st_ref)
    for stage in (hgrn, hgrn, attention, hgrn, attention, attention, hgrn):
        next(stage, None)
    for stage in (attention, hgrn):
        for _ in stage:
            pass

    @pl.when(j == pl.num_programs(1) - 1)
    def _():
        for h in range(HG_HEADS):
            sfin_ref[h] = st_ref[h].T


def _prompt_mixers(sinks, qa, hgin, gkv, kv_meta, s0t, hg_nw, batch, seq):
    nt = seq // MIX_TILE
    per = MIX_TILE // WINDOW
    blk = lambda width, col: pl.BlockSpec((MIX_TILE, width), lambda b, j, s: (b * nt + j, col))
    prev = pl.BlockSpec((WINDOW, 2 * KV_WIDTH), lambda b, j, s: ((b * nt + j) * per - jnp.minimum(j, 1), GKV_KV))
    full = lambda shape: pl.BlockSpec(shape, lambda b, j, s: (0,) * len(shape))
    state_shape = (HG_HEADS, HG_VDIM, HG_KDIM)
    dmat = jnp.asarray(_decay_sum_matrix(HG_CHUNK), BF16)
    return pl.pallas_call(
        _mixer_kernel,
        grid_spec=pltpu.PrefetchScalarGridSpec(
            num_scalar_prefetch=1, grid=(batch, nt),
            in_specs=[blk(ATT_WIDTH, QA_Q), blk(2 * KV_WIDTH, GKV_KV), prev, full((N_META, 2 * KV_WIDTH)),
                      blk(ATT_WIDTH, QA_GATE),
                      blk(HG_WIDTH, HI_Q), blk(HG_WIDTH, HI_K), blk(HG_WIDTH, HI_V), blk(HG_WIDTH, GKV_G),
                      blk(HG_WIDTH, HI_GATE), full(state_shape), full((1, HG_VDIM)), full(dmat.shape)],
            out_specs=[blk(ATT_WIDTH + HG_WIDTH, 0),
                       pl.BlockSpec((None,) + state_shape, lambda b, j, s: (b, 0, 0, 0))],
            scratch_shapes=[pltpu.VMEM(state_shape, F32)]),
        out_shape=[jax.ShapeDtypeStruct((batch * seq, ATT_WIDTH + HG_WIDTH), BF16),
                   jax.ShapeDtypeStruct((batch,) + state_shape, F32)],
        compiler_params=pltpu.CompilerParams(dimension_semantics=("arbitrary", "arbitrary"),
                                             vmem_limit_bytes=VMEM_LIMIT),
        name="mixers",
    )(sinks, qa, gkv, gkv, kv_meta, qa, hgin, hgin, hgin, gkv, hgin, s0t, hg_nw, dmat)


def _meta_mix(sink_of, q_ref, kv_ref, sga_ref, hq_ref, hk_ref, hv_ref, g_ref, sgh_ref, nw_ref, dmat_ref,
              att_ref, hg_ref, st_ref):
    kv = kv_ref[...]
    row = lax.broadcasted_iota(jnp.int32, (N_META, N_META), 0)
    col = lax.broadcasted_iota(jnp.int32, (N_META, N_META), 1)
    att = _attend(q_ref[...].astype(BF16),
                  [(_split_heads(kv[:, :KV_WIDTH]), kv[:, KV_WIDTH:].astype(BF16), col <= row)], sink_of)
    att_ref[...] = (att * sga_ref[...]).astype(att_ref.dtype)

    sums = _decay_sums(g_ref[...], dmat_ref[...])
    masks = _hgrn_masks(N_META)
    head_lanes = [slice(h * HG_KDIM, (h + 1) * HG_KDIM) for h in range(HG_HEADS)]
    items = [(h, hq_ref[:, lanes], hk_ref[:, lanes], hv_ref[:, lanes].astype(BF16),
              _chunk_exponents((lambda lanes: lambda i: sums[i * N_META:(i + 1) * N_META, lanes])(lanes), N_META))
             for h, lanes in enumerate(head_lanes)]
    states = [jnp.zeros((HG_VDIM, HG_KDIM), F32) for _ in range(HG_HEADS)]
    outs = _hgrn_chunks(items, states, masks)
    for h, lanes in enumerate(head_lanes):
        st_ref[h] = states[h]
        hg_ref[:, lanes] = _head_norm_gate(outs[h], nw_ref[...], sgh_ref[:, lanes]).astype(hg_ref.dtype)


def _sample_mix(q_ref, kv_ref, sga_ref, hq_ref, hk_ref, hv_ref, g_ref, sgh_ref, nw_ref,
                sink_ref, seg_ref, exp_ref, ck_ref, cv_ref, mk_ref, mv_ref, s_ref,
                att_ref, hg_ref, nk_ref, nv_ref, ns_ref):
    n_keys = WINDOW + N_META + SUBLANES
    key_row = lax.broadcasted_iota(jnp.int32, (n_keys, KV_WIDTH), 0)
    visible = jnp.logical_and(key_row >= 1, key_row <= WINDOW + N_META)
    win_row = lax.broadcasted_iota(jnp.int32, (WINDOW, KV_WIDTH), 0)
    sink = sink_ref[...]
    seg = seg_ref[...]
    expand = exp_ref[...]
    nw = nw_ref[...]

    group = 8

    def per_group(i, carry):
        seqs = [i * group + j for j in range(group)]
        new_rows, scores, values = [], [], []
        for b in seqs:
            kv_new = kv_ref[b]
            k_new = kv_new[:, 0:KV_WIDTH]
            v_new = kv_new[:, KV_WIDTH:2 * KV_WIDTH]
            new_rows.append((k_new, v_new))
            ck = ck_ref[b]
            cv = cv_ref[b]
            keys = jnp.concatenate([ck, mk_ref[b], jnp.broadcast_to(k_new, (SUBLANES, KV_WIDTH))], axis=0)
            values.append(jnp.concatenate([cv, mv_ref[b], jnp.broadcast_to(v_new, (SUBLANES, KV_WIDTH))], axis=0))
            prod = jnp.concatenate([keys] * GQA_GROUP, axis=1) * q_ref[b]
            scores.append(_dot(prod.astype(BF16), seg))
            nk_ref[b] = jnp.where(win_row == WINDOW - 1, k_new, pltpu.roll(ck, WINDOW - 1, 0))
            nv_ref[b] = jnp.where(win_row == WINDOW - 1, v_new, pltpu.roll(cv, WINDOW - 1, 0))

        wides = []
        for s in scores:
            s = jnp.where(visible, s, NEG_BIG)
            mx = jnp.maximum(jnp.max(s, axis=0, keepdims=True), sink)
            p = jnp.exp2(s - mx)
            den = jnp.sum(p, axis=0, keepdims=True) + jnp.exp2(sink - mx)
            p = p * (1.0 / den)
            wides.append(_dot(p.astype(BF16), expand))

        for b, wide, vals in zip(seqs, wides, values):
            att = jnp.sum(wide * jnp.concatenate([vals] * GQA_GROUP, axis=1), axis=0, keepdims=True)
            att_ref[b] = att * sga_ref[b]

            decay_row, hq_row, hk_row, hv_row, sgh_row = jnp.exp(g_ref[b]), hq_ref[b], hk_ref[b], hv_ref[b], sgh_ref[b]
            outs = []
            for h in range(HG_HEADS):
                lanes = slice(h * HG_KDIM, (h + 1) * HG_KDIM)
                col = lambda r: jnp.broadcast_to(r[:, lanes], (HG_KDIM, HG_KDIM)).T
                s1 = col(decay_row) * s_ref[b, h] + col(hk_row) * hv_row[:, lanes]
                ns_ref[b, h] = s1
                q_rows = jnp.broadcast_to(hq_row[:, lanes], (2 * SUBLANES, HG_KDIM)).astype(BF16)
                o = _dot(q_rows, s1.astype(BF16))[0:1, :]
                outs.append(_head_norm_gate(o, nw, sgh_row[:, lanes]))
            hg_ref[b] = jnp.concatenate(outs, axis=-1)
        return carry

    lax.fori_loop(0, q_ref.shape[0] // group, per_group, 0)


def _small_path_kernel(sinks_ref, x0_ref, nw_ref, w_ref, qnw_ref, knw_ref, lbl_ref, bd_ref, wo_ref, hgnw_ref,
                       sinkrow_ref, seg_ref, exp_ref, dmat_ref, ck_ref, cv_ref, mk_ref, mv_ref, s_ref,
                       y_ref, kvm_ref, s0t_ref, nk_ref, nv_ref, ns_ref,
                       x_scr, att_scr, hg_scr, *scr):
    proj_scr, staged, staged_att, staged_hg = scr[:8], scr[8:16], scr[16], scr[17]
    layer, tile = pl.program_id(0), pl.program_id(1)
    n_rows = x_scr.shape[0]
    n_seq = n_rows - N_META
    t = ck_ref.shape[0]

    @pl.when(jnp.logical_and(layer == 0, tile == 0))
    def _():
        x_scr[...] = x0_ref[...]

    @pl.when(tile == 0)
    def _():
        _proj_body(layer, [slice(0, n_rows)], [x_scr[...]], nw_ref, w_ref, qnw_ref, knw_ref, lbl_ref, bd_ref,
                   *proj_scr)
        meta = lambda ref: ref.at[n_seq:n_rows]
        _meta_mix(lambda h, g: sinks_ref[layer, h * GQA_GROUP + g], *[meta(r) for r in proj_scr], hgnw_ref,
                  dmat_ref, meta(att_scr), meta(hg_scr), s0t_ref)
        kvm_ref[...] = proj_scr[1][n_seq:n_rows, :]

    rows = pl.ds(pl.multiple_of(tile * t, t), t)
    for src, dst in zip(proj_scr, staged):
        block = src[rows, :]
        for b in range(t):
            dst[b] = block[b:b + 1, :]
    _sample_mix(*staged, hgnw_ref, sinkrow_ref, seg_ref, exp_ref, ck_ref, cv_ref, mk_ref, mv_ref, s_ref,
                staged_att, staged_hg, nk_ref, nv_ref, ns_ref)
    att_scr[rows, :] = jnp.concatenate([staged_att[b] for b in range(t)], axis=0)
    hg_scr[rows, :] = jnp.concatenate([staged_hg[b] for b in range(t)], axis=0)

    @pl.when(tile == pl.num_programs(1) - 1)
    def _():
        x = (x_scr[...] + _dot(att_scr[...].astype(BF16), wo_ref[0:ATT_WIDTH, :])
             + _dot(hg_scr[...].astype(BF16), wo_ref[ATT_WIDTH:, :]))
        x_scr[...] = x
        y_ref[...] = x


def _small_path(sinks, x0, norm_w, w_in, qnw, knw, lb_logits, bd, w_out, hg_nw, sink_rows, seg, expand,
                cache_k, cache_v, meta_k, meta_v, state):
    n_rows = x0.shape[0]
    n_seq, depth = state.shape[:2]
    t = SAMPLE_TILE
    dmat = jnp.asarray(_decay_sum_matrix(N_META), BF16)
    full = lambda a: pl.BlockSpec(a.shape, lambda l, i, s: (0,) * a.ndim)
    per_layer = lambda a: pl.BlockSpec((None,) + a.shape[1:], lambda l, i, s: (l,) + (0,) * (a.ndim - 1))
    cache = lambda a: pl.BlockSpec((t, None) + a.shape[2:], lambda l, i, s: (i, l) + (0,) * (a.ndim - 2))
    as_rows = lambda a: a[:, None, :]
    state_t = (HG_HEADS, HG_VDIM, HG_KDIM)
    widths = (ATT_WIDTH, 2 * KV_WIDTH, ATT_WIDTH, HG_WIDTH, HG_WIDTH, HG_WIDTH, HG_WIDTH, HG_WIDTH)
    operands = [x0, as_rows(norm_w), w_in, as_rows(qnw), as_rows(knw), lb_logits, bd, w_out, as_rows(hg_nw),
                as_rows(sink_rows), seg, expand, dmat, cache_k, cache_v, meta_k, meta_v, state]
    in_specs = [full(x0), per_layer(operands[1]), per_layer(w_in), per_layer(operands[3]), per_layer(operands[4]),
                full(lb_logits), full(bd), per_layer(w_out), per_layer(operands[8]), per_layer(operands[9]),
                full(seg), full(expand), full(dmat), cache(cache_k), cache(cache_v), cache(meta_k), cache(meta_v),
                cache(state)]
    out_shape = [jax.ShapeDtypeStruct((n_rows, D_MODEL), F32),
                 jax.ShapeDtypeStruct((depth, N_META, 2 * KV_WIDTH), F32),
                 jax.ShapeDtypeStruct((depth,) + state_t, F32),
                 jax.ShapeDtypeStruct(cache_k.shape, F32), jax.ShapeDtypeStruct(cache_v.shape, F32),
                 jax.ShapeDtypeStruct(state.shape, F32)]
    out_specs = [pl.BlockSpec((n_rows, D_MODEL), lambda l, i, s: (0, 0)),
                 pl.BlockSpec((None, N_META, 2 * KV_WIDTH), lambda l, i, s: (l, 0, 0)),
                 pl.BlockSpec((None,) + state_t, lambda l, i, s: (l, 0, 0, 0)),
                 cache(cache_k), cache(cache_v), cache(state)]
    scratch = ([pltpu.VMEM((n_rows, D_MODEL), F32), pltpu.VMEM((n_rows, ATT_WIDTH), F32),
                pltpu.VMEM((n_rows, HG_WIDTH), F32)]
               + [pltpu.VMEM((n_rows, w), F32) for w in widths]
               + [pltpu.VMEM((t, 1, w), F32) for w in widths]
               + [pltpu.VMEM((t, 1, ATT_WIDTH), F32), pltpu.VMEM((t, 1, HG_WIDTH), F32)])
    return pl.pallas_call(
        _small_path_kernel,
        grid_spec=pltpu.PrefetchScalarGridSpec(
            num_scalar_prefetch=1, grid=(depth, n_seq // t),
            in_specs=in_specs, out_specs=out_specs, scratch_shapes=scratch),
        out_shape=out_shape,
        compiler_params=pltpu.CompilerParams(dimension_semantics=("arbitrary", "arbitrary"),
                                             vmem_limit_bytes=VMEM_LIMIT),
        name="small_path",
    )(sinks, *operands)


def _g_major(a, axis):
    shape = a.shape
    a = a.reshape(shape[:axis] + (KV_HEADS, GQA_GROUP, HEAD_DIM) + shape[axis + 1:])
    a = jnp.swapaxes(a, axis, axis + 1)
    return a.reshape(shape)


def _constants():
    lane = np.arange(ATT_WIDTH)
    g_of, h_of = lane // KV_WIDTH, (lane % KV_WIDTH) // HEAD_DIM
    head = h_of * GQA_GROUP + g_of
    seg = (head[:, None] == np.arange(KV_WIDTH)[None, :]).astype(np.float32)
    grp = np.arange(256) // HEAD_DIM
    bd = (grp[:, None] == grp[None, :]).astype(np.float32) / HEAD_DIM
    return jnp.asarray(seg, BF16), jnp.asarray(seg.T, BF16), jnp.asarray(bd, BF16)


def kernel(x_prompt, x_sample, cache_win_k, cache_win_v, cache_meta_k, cache_meta_v, state_hgrn, meta_tokens,
           norm_w, w_in, q_norm_w, k_norm_w, attn_sinks, hg_lb_logits, hg_norm_w, w_out):
    batch, seq, _ = x_prompt.shape
    n_seq = x_sample.shape[0]
    depth = w_in.shape[0]
    w_buf = cache_win_k.shape[2]
    assert x_sample.shape[1] == 1 and w_buf == WINDOW and n_seq % SAMPLE_TILE == 0
    assert seq % ROW_TILE == 0 and seq % MIX_TILE == 0

    seg, expand, bd = _constants()
    w_in_b = w_in.astype(BF16)
    w_out_b = jnp.concatenate([_g_major(w_out[:, :ATT_WIDTH], 1), w_out[:, ATT_WIDTH:]], axis=1).astype(BF16)
    qnw = jnp.tile(q_norm_w, (1, ATT_HEADS)) * (HEAD_DIM ** -0.5 * LOG2E)
    knw = jnp.tile(k_norm_w, (1, KV_HEADS))
    lb_logits = hg_lb_logits.astype(F32)
    sinks = attn_sinks.astype(F32) * LOG2E
    sink_rows = jnp.pad(sinks, ((0, 0), (0, KV_WIDTH - ATT_HEADS)))

    ck = cache_win_k.reshape(n_seq, depth, w_buf, KV_WIDTH)
    cv = cache_win_v.reshape(n_seq, depth, w_buf, KV_WIDTH)
    mk = cache_meta_k.reshape(n_seq, depth, N_META, KV_WIDTH)
    mv = cache_meta_v.reshape(n_seq, depth, N_META, KV_WIDTH)

    x_small = jnp.concatenate([x_sample.reshape(n_seq, D_MODEL), meta_tokens.astype(F32)], axis=0)
    y_small, kv_meta, s0t, new_k, new_v, new_state = _small_path(
        sinks, x_small, norm_w, w_in_b, qnw, knw, lb_logits, bd, w_out_b, hg_norm_w, sink_rows,
        seg, expand, ck, cv, mk, mv, state_hgrn)

    xp = x_prompt.reshape(batch * seq, D_MODEL)
    outs = {k: [] for k in ("wkp", "wvp", "hsp")}
    mix = None
    for l in range(depth):
        pr = _project(l, xp, norm_w[l][None], w_in_b, qnw[l][None], knw[l][None], lb_logits, bd,
                      row_tile=ROW_TILE, mix=mix)
        if mix is not None:
            xp, pr = pr[0], pr[1:]
        qa, hgin, gkv = pr
        mixed, s_fin = _prompt_mixers(sinks[l], qa, hgin, gkv, kv_meta[l], s0t[l], hg_norm_w[l][None], batch, seq)
        mix = (mixed, w_out_b)

        kv_last = gkv.reshape(batch, seq, -1)[:, seq - w_buf:, HG_WIDTH:]
        outs["wkp"].append(kv_last[:, :, :KV_WIDTH])
        outs["wvp"].append(kv_last[:, :, KV_WIDTH:])
        outs["hsp"].append(s_fin)
    xp = _out_project(depth - 1, mix[0], xp, mix[1], 2 * MIX_TILE)

    stack = lambda name: jnp.stack(outs[name], axis=1)
    heads = lambda a: a.reshape(a.shape[:-1] + (KV_HEADS, HEAD_DIM))
    meta_rows = lambda a: jnp.broadcast_to(a[None], (batch,) + a.shape)
    return (xp.reshape(batch, seq, D_MODEL), y_small[:n_seq].reshape(n_seq, 1, D_MODEL),
            heads(stack("wkp")), heads(stack("wvp")),
            heads(meta_rows(kv_meta[:, :, :KV_WIDTH])), heads(meta_rows(kv_meta[:, :, KV_WIDTH:])), stack("hsp"),
            heads(new_k), heads(new_v), new_state)
```
